```python
import math
import jax, jax.numpy as jnp
from jax import lax
import numpy as np

D_MODEL = 1024
BATCH = 8
SEQ = 2048
DEPTH = 4

GRID_W = 64
CTX_LEN = 256
HEAD_DIM = 64
A_HEADS = 4
A_KV_HEADS = 2
A_BLOCK = 128
A_WINDOW = 128
ROPE_BASE = 10000.0
SSD_HEADS = 8
SSD_HEAD_DIM = 64
SSD_INNER = SSD_HEADS * SSD_HEAD_DIM
SSD_GROUPS = 2
SSD_STATE = 64
SSD_CONV = 5
SSD_CHUNK = 128
SSD_CONV_DIM = SSD_INNER + 2 * SSD_GROUPS * SSD_STATE
NA_HEADS = 4
NA_WIN_ROWS = 8
NA_WIN_COLS = 16
A_WIDTH = A_HEADS * HEAD_DIM
NA_WIDTH = NA_HEADS * HEAD_DIM
D_MIX = A_WIDTH + SSD_INNER + NA_WIDTH
A_IN = (A_HEADS + 2 * A_KV_HEADS) * HEAD_DIM
SSD_IN = SSD_INNER + SSD_CONV_DIM + 2 * SSD_HEADS
NA_IN = 3 * NA_WIDTH
N_IN = A_IN + SSD_IN + NA_IN
MOE_GROUPS = 4
MOE_EXPERTS = 8
MOE_TOPK = 2
D_EXPERT = 256
N_MOD = 6
RMS_EPS = 1e-6
NEG_INF = -1e30

kernel_name = 'hybrid_parallel_heads_diffusion_trunk'

F32 = jnp.float32


def rmsnorm(x, g):
    xf = x.astype(F32)
    y = xf * lax.rsqrt(jnp.mean(xf * xf, axis=-1, keepdims=True) + RMS_EPS)
    return (y * g.astype(F32)).astype(x.dtype)


def modulate(h, shift, scale):
    return h * (1 + scale) + shift


def to_heads(t):
    return t.reshape(t.shape[:-1] + (-1, HEAD_DIM))


def rope_axis(xa, pos):
    d = xa.shape[-1]
    inv = 1.0 / (ROPE_BASE ** (jnp.arange(0, d, 2, dtype=F32) / d))
    ang = pos.astype(F32)[:, None] * inv[None, :]
    cos = jnp.cos(ang)[:, None, :]
    sin = jnp.sin(ang)[:, None, :]
    x1, x2 = xa[..., : d // 2], xa[..., d // 2:]
    return jnp.concatenate([x1 * cos - x2 * sin, x2 * cos + x1 * sin], axis=-1)


def rope_2d(x, rows, cols):
    xf = x.astype(F32)
    half = x.shape[-1] // 2
    out = jnp.concatenate([rope_axis(xf[..., :half], rows), rope_axis(xf[..., half:], cols)], axis=-1)
    return out.astype(x.dtype)


def dense_attn(q, k, v, sink):
    b, l, hq, hd = q.shape
    hkv = k.shape[2]
    rep = hq // hkv
    qg = q.reshape(b, l, hkv, rep, hd)
    s = jnp.einsum('blgrd,bmgd->bgrlm', qg, k).astype(F32) * hd ** -0.5
    if sink is not None:
        sk = jnp.broadcast_to(sink.astype(F32).reshape(hkv, rep, 1, 1), (b, hkv, rep, l, 1))
        s = jnp.concatenate([s, sk], axis=-1)
    p = jax.nn.softmax(s, axis=-1)[..., : k.shape[1]]
    o = jnp.einsum('bgrlm,bmgd->blgrd', p.astype(v.dtype), v)
    return o.reshape(b, l, hq * hd)


def window_gqa(q, k, v, kc, vc, sink):
    b, s, hq, hd = q.shape
    nb = s // A_BLOCK
    rep = hq // A_KV_HEADS
    qb = q.reshape(b, nb, A_BLOCK, A_KV_HEADS, rep, hd)
    pad = ((0, 0), (A_BLOCK, A_BLOCK), (0, 0), (0, 0))
    kp = jnp.pad(k, pad).reshape(b, nb + 2, A_BLOCK, A_KV_HEADS, hd)
    vp = jnp.pad(v, pad).reshape(b, nb + 2, A_BLOCK, A_KV_HEADS, hd)
    kb = jnp.concatenate([kp[:, :-2], kp[:, 1:-1], kp[:, 2:]], axis=2)
    vb = jnp.concatenate([vp[:, :-2], vp[:, 1:-1], vp[:, 2:]], axis=2)
    scale = hd ** -0.5
    s_loc = jnp.einsum('bnqgrd,bnkgd->bngrqk', qb, kb).astype(F32) * scale
    qpos = jnp.arange(nb)[:, None] * A_BLOCK + jnp.arange(A_BLOCK)[None, :]
    kpos = jnp.arange(nb)[:, None] * A_BLOCK - A_BLOCK + jnp.arange(3 * A_BLOCK)[None, :]
    valid = ((jnp.abs(qpos[:, :, None] - kpos[:, None, :]) <= A_WINDOW)
             & (kpos >= 0)[:, None, :] & (kpos < s)[:, None, :])
    s_loc = jnp.where(valid[None, :, None, None], s_loc, NEG_INF)
    s_ctx = jnp.einsum('bnqgrd,blgd->bngrql', qb, kc).astype(F32) * scale
    s_sink = jnp.broadcast_to(sink.astype(F32).reshape(1, 1, A_KV_HEADS, rep, 1, 1),
                              (b, nb, A_KV_HEADS, rep, A_BLOCK, 1))
    p = jax.nn.softmax(jnp.concatenate([s_loc, s_ctx, s_sink], axis=-1), axis=-1)
    n_loc = 3 * A_BLOCK
    p_loc = p[..., :n_loc].astype(v.dtype)
    p_ctx = p[..., n_loc:n_loc + kc.shape[1]].astype(v.dtype)
    o = (jnp.einsum('bngrqk,bnkgd->bnqgrd', p_loc, vb)
         + jnp.einsum('bngrql,blgd->bnqgrd', p_ctx, vc))
    return o.reshape(b, s, hq * hd)


def neighborhood_attn(q, k, v, kc, vc, rpb):
    b, s, h, hd = q.shape
    rows = s // GRID_W
    kh = min(NA_WIN_ROWS, rows)
    kw = NA_WIN_COLS
    r = jnp.arange(rows)
    rs = jnp.clip(r - kh // 2, 0, rows - kh)
    row_idx = rs[:, None] + jnp.arange(kh)[None, :]
    cq = jnp.arange(GRID_W)
    cs = jnp.clip(cq - kw // 2, 0, GRID_W - kw)
    kcol = jnp.arange(GRID_W)
    col_valid = (kcol[None, :] >= cs[:, None]) & (kcol[None, :] < cs[:, None] + kw)
    qg = q.reshape(b, rows, GRID_W, h, hd)
    kg = k.reshape(b, rows, GRID_W, h, hd)[:, row_idx]
    vg = v.reshape(b, rows, GRID_W, h, hd)[:, row_idx]
    scale = hd ** -0.5
    sc = jnp.einsum('brchd,brikhd->brhcik', qg, kg).astype(F32) * scale
    roff = row_idx - r[:, None] + (NA_WIN_ROWS - 1)
    coff = jnp.clip(kcol[None, :] - cq[:, None], -(kw - 1), kw - 1) + (kw - 1)
    bias = rpb.astype(F32)[:, roff[:, None, :, None], coff[None, :, None, :]]
    sc = sc + jnp.transpose(bias, (1, 0, 2, 3, 4))[None]
    sc = jnp.where(col_valid[None, None, None, :, None, :], sc, NEG_INF)
    sc = sc.reshape(b, rows, h, GRID_W, kh * GRID_W)
    s_ctx = jnp.einsum('brchd,blhd->brhcl', qg, kc).astype(F32) * scale
    p = jax.nn.softmax(jnp.concatenate([sc, s_ctx], axis=-1), axis=-1)
    n_loc = kh * GRID_W
    p_loc = p[..., :n_loc].reshape(b, rows, h, GRID_W, kh, GRID_W).astype(v.dtype)
    p_ctx = p[..., n_loc:].astype(v.dtype)
    o = (jnp.einsum('brhcik,brikhd->brchd', p_loc, vg)
         + jnp.einsum('brhcl,blhd->brchd', p_ctx, vc))
    return o.reshape(b, s, h * hd)


def conv_centred(x, w, bias):
    kk, ch = w.shape
    y = lax.conv_general_dilated(x, w.astype(x.dtype).reshape(kk, 1, ch), window_strides=(1,),
                                 padding=[(kk // 2, kk // 2)],
                                 dimension_numbers=('NWC', 'WIO', 'NWC'), feature_group_count=ch)
    return y + bias


def ssd_scan(x, dt, a, bm, cm, init_state, with_output):
    b, l, h, p = x.shape
    n = bm.shape[-1]
    nc = l // SSD_CHUNK
    q = SSD_CHUNK
    xdt = (x.astype(F32) * dt[..., None]).reshape(b, nc, q, h, p)
    bc = bm.astype(F32).reshape(b, nc, q, h, n)
    cc = cm.astype(F32).reshape(b, nc, q, h, n)
    acs = jnp.cumsum((dt * a).reshape(b, nc, q, h), axis=2)
    total = acs[:, :, -1]
    decay_to_end = jnp.exp(total[:, :, None, :] - acs)
    states = jnp.einsum('bcqhn,bcqh,bcqhp->bchpn', bc, decay_to_end, xdt)

    def step(carry, inp):
        st, tot = inp
        return carry * jnp.exp(tot)[:, :, None, None] + st, carry

    final, prev = lax.scan(step, init_state,
                           (jnp.transpose(states, (1, 0, 2, 3, 4)), jnp.transpose(total, (1, 0, 2))))
    if not with_output:
        return None, final
    prev = jnp.transpose(prev, (1, 0, 2, 3, 4))
    y_off = jnp.einsum('bcqhn,bchpn->bcqhp', cc, prev) * jnp.exp(acs)[..., None]
    seg = acs[:, :, :, None, :] - acs[:, :, None, :, :]
    lower = jnp.tril(jnp.ones((q, q), dtype=bool))
    lmat = jnp.exp(jnp.where(lower[None, None, :, :, None], seg, NEG_INF))
    scores = jnp.einsum('bcihn,bcjhn->bcijh', cc, bc) * lmat
    y_diag = jnp.einsum('bcijh,bcjhp->bcihp', scores, xdt)
    return (y_diag + y_off).reshape(b, l, h, p), final


def ssd_branch(u, conv_w, conv_b, dt_bias, a_log, d_skip, norm_g, init_f, init_b, with_output):
    b, l, _ = u.shape
    z, xbc, dt_raw = jnp.split(u, [SSD_INNER, SSD_INNER + SSD_CONV_DIM], axis=-1)
    xbc = jax.nn.silu(conv_centred(xbc, conv_w, conv_b))
    xs, bm, cm = jnp.split(xbc, [SSD_INNER, SSD_INNER + SSD_GROUPS * SSD_STATE], axis=-1)
    xs = xs.reshape(b, l, SSD_HEADS, SSD_HEAD_DIM)
    rep = SSD_HEADS // SSD_GROUPS
    bm = jnp.repeat(bm.reshape(b, l, SSD_GROUPS, SSD_STATE), rep, axis=2)
    cm = jnp.repeat(cm.reshape(b, l, SSD_GROUPS, SSD_STATE), rep, axis=2)
    dt = jax.nn.softplus(dt_raw.astype(F32).reshape(b, l, 2, SSD_HEADS) + dt_bias.astype(F32))
    a = -jnp.exp(a_log.astype(F32))
    if init_f is None:
        init_f = jnp.zeros((b, SSD_HEADS, SSD_HEAD_DIM, SSD_STATE), F32)
        init_b = jnp.zeros((b, SSD_HEADS, SSD_HEAD_DIM, SSD_STATE), F32)
    flip = lambda t: jnp.flip(t, axis=1)
    y_f, st_f = ssd_scan(xs, dt[:, :, 0], a[0], bm, cm, init_f, with_output)
    y_b, st_b = ssd_scan(flip(xs), flip(dt[:, :, 1]), a[1], flip(bm), flip(cm), init_b, with_output)
    if not with_output:
        return None, st_f, st_b
    y = y_f + flip(y_b) + d_skip.astype(F32)[:, None] * xs.astype(F32)
    y = y.reshape(b, l, SSD_INNER) * jax.nn.silu(z.astype(F32))
    return rmsnorm(y, norm_g).astype(u.dtype), st_f, st_b


def token_mixers(ux, uc, rows_pos, cols_pos, sink, conv_w, conv_b, dt_bias, a_log, d_skip,
                 norm_g, rpb, need_ctx):
    ax, bx, nx = jnp.split(ux, [A_IN, A_IN + SSD_IN], axis=-1)
    ac, bcx, nc = jnp.split(uc, [A_IN, A_IN + SSD_IN], axis=-1)
    a_split = [A_WIDTH, A_WIDTH + A_KV_HEADS * HEAD_DIM]
    qa, ka, va = [to_heads(t) for t in jnp.split(ax, a_split, axis=-1)]
    qa_c, ka_c, va_c = [to_heads(t) for t in jnp.split(ac, a_split, axis=-1)]
    qa = rope_2d(qa, rows_pos, cols_pos)
    ka = rope_2d(ka, rows_pos, cols_pos)
    oa = window_gqa(qa, ka, va, ka_c, va_c, sink)
    ob_c, st_f, st_b = ssd_branch(bcx, conv_w, conv_b, dt_bias, a_log, d_skip, norm_g, None, None, need_ctx)
    ob, _, _ = ssd_branch(bx, conv_w, conv_b, dt_bias, a_log, d_skip, norm_g, st_f, st_b, True)
    n_split = [NA_WIDTH, 2 * NA_WIDTH]
    qn, kn, vn = [to_heads(t) for t in jnp.split(nx, n_split, axis=-1)]
    qn_c, kn_c, vn_c = [to_heads(t) for t in jnp.split(nc, n_split, axis=-1)]
    on = neighborhood_attn(qn, kn, vn, kn_c, vn_c, rpb)
    o_x = jnp.concatenate([oa, ob, on], axis=-1)
    if not need_ctx:
        return o_x, None
    oa_c = dense_attn(qa_c, ka_c, va_c, sink)
    on_c = dense_attn(qn_c, kn_c, vn_c, None)
    o_c = jnp.concatenate([oa_c, ob_c, on_c], axis=-1)
    return o_x, o_c


def hier_moe(h, w_rg, b_rg, w_re, b_re, w_gate, w_up, w_down):
    n = h.shape[0]
    g_prob = jax.nn.softmax((h @ w_rg).astype(F32) + b_rg.astype(F32), axis=-1)
    g_w, g_idx = lax.top_k(g_prob, 1)
    e_logits = ((h @ w_re).astype(F32) + b_re.astype(F32)).reshape(n, MOE_GROUPS, MOE_EXPERTS)
    e_logits = jnp.take_along_axis(e_logits, g_idx[:, :, None], axis=1)[:, 0]
    top_l, top_i = lax.top_k(e_logits, MOE_TOPK)
    top_w = jax.nn.softmax(top_l, axis=-1) * g_w
    comb = jnp.einsum('nk,nke->ne', top_w, jax.nn.one_hot(top_i, MOE_EXPERTS, dtype=F32))
    out = jnp.zeros((n, h.shape[1]), F32)
    for gi in range(MOE_GROUPS):
        wsel = jnp.where(g_idx == gi, comb, 0.0)
        hid = (jax.nn.silu(jnp.einsum('nd,edf->nef', h, w_gate[gi]))
               * jnp.einsum('nd,edf->nef', h, w_up[gi]))
        out = out + jnp.einsum('nef,efd->nd', hid * wsel[..., None].astype(hid.dtype), w_down[gi])
    return out.astype(h.dtype)


def setup_inputs(seed: int = 0) -> dict:
    key = jax.random.key(seed)
    ks = jax.random.split(key, 28)

    def nrm(k, shape, scale):
        return jax.random.normal(k, shape, F32) * scale

    x = nrm(ks[0], (BATCH, SEQ, D_MODEL), 1.0)
    c = nrm(ks[1], (BATCH, D_MODEL), 1.0)
    ctx = nrm(ks[2], (BATCH, CTX_LEN, D_MODEL), 1.0)
    c_ctx = nrm(ks[3], (D_MODEL,), 1.0)
    w_mod = nrm(ks[4], (DEPTH, D_MODEL, N_MOD * D_MODEL), 0.5 * D_MODEL ** -0.5)
    b_mod = nrm(ks[5], (DEPTH, N_MOD * D_MODEL), 0.02)
    g_mix = 1.0 + nrm(ks[6], (DEPTH, D_MODEL), 0.02)
    w_in = nrm(ks[7], (DEPTH, D_MODEL, N_IN), D_MODEL ** -0.5)
    attn_sink = nrm(ks[8], (DEPTH, A_HEADS), 0.5)
    ssd_conv_w = nrm(ks[9], (DEPTH, SSD_CONV, SSD_CONV_DIM), SSD_CONV ** -0.5)
    ssd_conv_b = nrm(ks[10], (DEPTH, SSD_CONV_DIM), 0.02)
    dt0 = jnp.exp(jax.random.uniform(ks[11], (DEPTH, 2, SSD_HEADS), F32, math.log(1e-3), math.log(1e-1)))
    ssd_dt_bias = dt0 + jnp.log(-jnp.expm1(-dt0))
    ssd_a_log = jnp.log(jax.random.uniform(ks[12], (DEPTH, 2, SSD_HEADS), F32, 1.0, 16.0))
    ssd_d = 1.0 + nrm(ks[13], (DEPTH, SSD_HEADS), 0.02)
    ssd_norm_g = 1.0 + nrm(ks[14], (DEPTH, SSD_INNER), 0.02)
    na_rpb = nrm(ks[15], (DEPTH, NA_HEADS, 2 * NA_WIN_ROWS - 1, 2 * NA_WIN_COLS - 1), 0.1)
    w_out = nrm(ks[16], (DEPTH, D_MIX, D_MODEL), D_MIX ** -0.5)
    g_ffn = 1.0 + nrm(ks[17], (DEPTH, D_MODEL), 0.02)
    w_router_group = nrm(ks[18], (DEPTH, D_MODEL, MOE_GROUPS), D_MODEL ** -0.5)
    b_router_group = nrm(ks[19], (DEPTH, MOE_GROUPS), 0.01)
    w_router_expert = nrm(ks[20], (DEPTH, D_MODEL, MOE_GROUPS * MOE_EXPERTS), D_MODEL ** -0.5)
    b_router_expert = nrm(ks[21], (DEPTH, MOE_GROUPS * MOE_EXPERTS), 0.01)
    w_exp_gate = nrm(ks[22], (DEPTH, MOE_GROUPS, MOE_EXPERTS, D_MODEL, D_EXPERT), D_MODEL ** -0.5)
    w_exp_up = nrm(ks[23], (DEPTH, MOE_GROUPS, MOE_EXPERTS, D_MODEL, D_EXPERT), D_MODEL ** -0.5)
    w_exp_down = nrm(ks[24], (DEPTH, MOE_GROUPS, MOE_EXPERTS, D_EXPERT, D_MODEL), D_EXPERT ** -0.5)
    g_final = 1.0 + nrm(ks[25], (D_MODEL,), 0.02)
    return {'x': x, 'c': c, 'ctx': ctx, 'c_ctx': c_ctx, 'w_mod': w_mod, 'b_mod': b_mod,
            'g_mix': g_mix, 'w_in': w_in, 'attn_sink': attn_sink, 'ssd_conv_w': ssd_conv_w,
            'ssd_conv_b': ssd_conv_b, 'ssd_dt_bias': ssd_dt_bias, 'ssd_a_log': ssd_a_log,
            'ssd_d': ssd_d, 'ssd_norm_g': ssd_norm_g, 'na_rpb': na_rpb, 'w_out': w_out,
            'g_ffn': g_ffn, 'w_router_group': w_router_group, 'b_router_group': b_router_group,
            'w_router_expert': w_router_expert, 'b_router_expert': b_router_expert,
            'w_exp_gate': w_exp_gate, 'w_exp_up': w_exp_up, 'w_exp_down': w_exp_down,
            'g_final': g_final}


def reference(x, c, ctx, c_ctx, w_mod, b_mod, g_mix, w_in, attn_sink, ssd_conv_w, ssd_conv_b,
              ssd_dt_bias, ssd_a_log, ssd_d, ssd_norm_g, na_rpb, w_out, g_ffn, w_router_group,
              b_router_group, w_router_expert, b_router_expert, w_exp_gate, w_exp_up, w_exp_down,
              g_final):
    b, s, d = x.shape
    l_ctx = ctx.shape[1]
    t = jnp.arange(s)
    rows_pos = t // GRID_W
    cols_pos = t % GRID_W
    mod_x_in = jax.nn.silu(c)
    mod_c_in = jax.nn.silu(c_ctx)
    for layer in range(DEPTH):
        need_ctx = layer < DEPTH - 1
        mx = (mod_x_in @ w_mod[layer] + b_mod[layer]).reshape(b, N_MOD, 1, d)
        mc = (mod_c_in @ w_mod[layer] + b_mod[layer]).reshape(N_MOD, d)
        hx = modulate(rmsnorm(x, g_mix[layer]), mx[:, 0], mx[:, 1])
        hc = modulate(rmsnorm(ctx, g_mix[layer]), mc[0], mc[1])
        ux = hx @ w_in[layer]
        uc = hc @ w_in[layer]
        o_x, o_c = token_mixers(ux, uc, rows_pos, cols_pos, attn_sink[layer], ssd_conv_w[layer],
                                ssd_conv_b[layer], ssd_dt_bias[layer], ssd_a_log[layer], ssd_d[layer],
                                ssd_norm_g[layer], na_rpb[layer], need_ctx)
        x = x + mx[:, 2] * (o_x @ w_out[layer])
        tokens = modulate(rmsnorm(x, g_ffn[layer]), mx[:, 3], mx[:, 4]).reshape(b * s, d)
        if need_ctx:
            ctx = ctx + mc[2] * (o_c @ w_out[layer])
            h2c = modulate(rmsnorm(ctx, g_ffn[layer]), mc[3], mc[4]).reshape(b * l_ctx, d)
            tokens = jnp.concatenate([tokens, h2c], axis=0)
        f = hier_moe(tokens, w_router_group[layer], b_router_group[layer], w_router_expert[layer],
                     b_router_expert[layer], w_exp_gate[layer], w_exp_up[layer], w_exp_down[layer])
        x = x + mx[:, 5] * f[: b * s].reshape(b, s, d)
        if need_ctx:
            ctx = ctx + mc[5] * f[b * s:].reshape(b, l_ctx, d)
    return rmsnorm(x, g_final)
```

```python
import functools
import math

import jax
import jax.numpy as jnp
import numpy as np
from jax import lax
from jax.experimental import pallas as pl
from jax.experimental.pallas import tpu as pltpu

F32 = jnp.float32
BF16 = jnp.bfloat16

D_MODEL = 1024
SEQ = 2048
DEPTH = 4
GRID_W = 64
GRID_ROWS = SEQ // GRID_W
CTX_LEN = 256
HEAD_DIM = 64
A_HEADS = 4
A_KV_HEADS = 2
A_BLOCK = 128
ROPE_BASE = 10000.0
SSD_HEADS = 8
SSD_INNER = 512
SSD_STATE = 64
SSD_CONV = 5
SSD_CHUNK = 128
SSD_CONV_DIM = 768
NA_HEADS = 4
NA_WIN_ROWS = 8
NA_WIN_COLS = 16
A_IN = 512
SSD_IN = 1296
MOE_GROUPS = 4
MOE_EXPERTS = 8
N_EXPERTS = MOE_GROUPS * MOE_EXPERTS
D_EXPERT = 256
N_MOD = 6
RMS_EPS = 1e-6
NEG_INF = -1e30

TILE = 512
LANES = 128
C_QK, C_QKP, C_V, C_Z, C_XBC, C_NA, C_DT = 0, 384, 768, 896, 1408, 2176, 2944
N_COLS = 3200
NA_Q_ROWS = 2
NA_K_ROWS = 10
NA_BIAS_OFF = 2
NA_BIAS_N = 18


def _cparams(n_axes, vmem_mb):
    return pltpu.CompilerParams(dimension_semantics=("arbitrary",) * n_axes,
                                vmem_limit_bytes=vmem_mb << 20)


def _split3(x):
    h1 = x.astype(BF16)
    r1 = x - h1.astype(F32)
    h2 = r1.astype(BF16)
    h3 = (r1 - h2.astype(F32)).astype(BF16)
    return h1, h2, h3


def _dot(a, b):
    return jnp.dot(a, b, preferred_element_type=F32)


def _dot_nt(a, b):
    return lax.dot_general(a, b, (((1,), (1,)), ((), ())), preferred_element_type=F32)


def _dot_exact_lhs(lhs_bf16, x):
    h1, h2, h3 = _split3(x)
    return _dot(lhs_bf16, h1) + _dot(lhs_bf16, h2) + _dot(lhs_bf16, h3)


def _dot_exact_rhs(x, rhs_bf16):
    h1, h2, h3 = _split3(x)
    return _dot(h1, rhs_bf16) + _dot(h2, rhs_bf16) + _dot(h3, rhs_bf16)


def _silu(x):
    return x * jax.nn.sigmoid(x)


def _iota(shape, dim):
    return lax.broadcasted_iota(jnp.int32, shape, dim)


def _mod_kernel(c_ref, w_ref, b_ref, o_ref):
    a = _silu(c_ref[...])
    a1, a2, _ = _split3(a)
    w = w_ref[0]
    w1 = w.astype(BF16)
    w2 = (w - w1.astype(F32)).astype(BF16)
    o_ref[0] = _dot(a1, w1) + _dot(a1, w2) + _dot(a2, w1) + b_ref[0]


def _modulation(cin, w_mod, b_mod):
    nt = 1024
    return pl.pallas_call(
        _mod_kernel,
        grid=(DEPTH, N_MOD * D_MODEL // nt),
        in_specs=[pl.BlockSpec((16, D_MODEL), lambda l, j: (0, 0)),
                  pl.BlockSpec((1, D_MODEL, nt), lambda l, j: (l, 0, j)),
                  pl.BlockSpec((1, 1, nt), lambda l, j: (l, 0, j))],
        out_specs=pl.BlockSpec((1, 16, nt), lambda l, j: (l, 0, j)),
        out_shape=jax.ShapeDtypeStruct((DEPTH, 16, N_MOD * D_MODEL), F32),
        compiler_params=_cparams(2, 40),
        name="modulation",
    )(cin, w_mod, b_mod.reshape(DEPTH, 1, N_MOD * D_MODEL))


def _inproj_kernel(x_ref, mod_ref, g_ref, w_ref, cos_ref, sin_ref,
                   qk_ref, v_ref, z_ref, xbc_ref, na_ref, dt_ref):
    x = x_ref[...]
    m = mod_ref[0]
    h = x * lax.rsqrt(jnp.mean(x * x, axis=-1, keepdims=True) + RMS_EPS) * g_ref[...]
    h = h * (1.0 + m[1:2]) + m[0:1]
    hb = h.astype(BF16)

    def mm(lo, hi):
        return _dot(hb, w_ref[:, lo:hi])

    qk = mm(C_QK, C_QKP) * cos_ref[...] + mm(C_QKP, C_V) * sin_ref[...]
    qk_ref[...] = qk.astype(BF16)
    v_ref[...] = mm(C_V, C_Z).astype(BF16)
    z_ref[...] = mm(C_Z, C_XBC)
    xbc_ref[...] = mm(C_XBC, C_NA)
    na_ref[...] = mm(C_NA, C_DT).astype(BF16)
    dt_ref[...] = mm(C_DT, N_COLS)


def _tile_mod_row(i, n_lat_tiles, tiles_per_batch, n_batch):
    return jnp.where(i < n_lat_tiles, i // tiles_per_batch, n_batch)


def _inproj(xc, mod_l, g_mix, w_cat, cos_t, sin_t, n_batch):
    nt = xc.shape[0]
    n_lat_tiles = n_batch * SEQ // TILE
    tpb = SEQ // TILE
    row = lambda i: (i, 0)
    modrow = lambda i: (_tile_mod_row(i, n_lat_tiles, tpb, n_batch), 0, 0)
    posrow = lambda i: (jnp.where(i < n_lat_tiles, i % tpb, tpb), 0)
    const = lambda i: (0, 0)
    outs = [(384, BF16), (128, BF16), (512, F32), (768, F32), (768, BF16), (256, F32)]
    return pl.pallas_call(
        _inproj_kernel,
        grid=(nt // TILE,),
        in_specs=[pl.BlockSpec((TILE, D_MODEL), row),
                  pl.BlockSpec((1, N_MOD, D_MODEL), modrow),
                  pl.BlockSpec((1, D_MODEL), const),
                  pl.BlockSpec((D_MODEL, N_COLS), const),
                  pl.BlockSpec((TILE, 384), posrow),
                  pl.BlockSpec((TILE, 384), posrow)],
        out_specs=[pl.BlockSpec((TILE, w), row) for w, _ in outs],
        out_shape=[jax.ShapeDtypeStruct((nt, w), dt) for w, dt in outs],
        compiler_params=_cparams(1, 56),
        name="inproj",
    )(xc, mod_l, g_mix, w_cat, cos_t, sin_t)


def _conv_kernel(prev_ref, x_ref, next_ref, w_ref, b_ref, o_ref, *, blocks_per_seq, n_lat_blocks):
    i = pl.program_id(0)
    is_lat = i < n_lat_blocks
    first = jnp.logical_or(jnp.logical_not(is_lat), i % blocks_per_seq == 0)
    last = jnp.logical_or(jnp.logical_not(is_lat), i % blocks_per_seq == blocks_per_seq - 1)
    prev = jnp.where(first, 0.0, prev_ref[...])
    nxt = jnp.where(last, 0.0, next_ref[...])
    xw = jnp.concatenate([prev, x_ref[...], nxt], axis=0)
    rows = x_ref.shape[0]
    w = w_ref[...]
    acc = jnp.zeros(x_ref.shape, F32) + b_ref[...]
    for k in range(SSD_CONV):
        off = 8 - SSD_CONV // 2 + k
        acc = acc + xw[off:off + rows, :] * w[k:k + 1, :]
    o_ref[...] = _silu(acc)


def _conv_silu(xbc, conv_w, conv_b, n_batch):
    nt = xbc.shape[0]
    rows = 256
    bps = SEQ // rows
    n_lat_blocks = n_batch * bps
    nblk = nt // rows
    r8 = rows // 8
    kern = functools.partial(_conv_kernel, blocks_per_seq=bps, n_lat_blocks=n_lat_blocks)
    return pl.pallas_call(
        kern,
        grid=(nblk,),
        in_specs=[pl.BlockSpec((8, SSD_CONV_DIM), lambda i: (jnp.maximum(i * r8 - 1, 0), 0)),
                  pl.BlockSpec((rows, SSD_CONV_DIM), lambda i: (i, 0)),
                  pl.BlockSpec((8, SSD_CONV_DIM), lambda i: (jnp.minimum(i * r8 + r8, nt // 8 - 1), 0)),
                  pl.BlockSpec((8, SSD_CONV_DIM), lambda i: (0, 0)),
                  pl.BlockSpec((1, SSD_CONV_DIM), lambda i: (0, 0))],
        out_specs=pl.BlockSpec((rows, SSD_CONV_DIM), lambda i: (i, 0)),
        out_shape=jax.ShapeDtypeStruct((nt, SSD_CONV_DIM), F32),
        compiler_params=_cparams(1, 32),
        name="ssd_conv",
    )(xbc, xbc, xbc, conv_w, conv_b)


def _ssd_kernel(xbc_ref, dt_ref, dtb_ref, alog_ref, e512_ref, e1024_ref, y_ref, st_ref):
    d = pl.program_id(1)
    c = pl.program_id(2)

    @pl.when(c == 0)
    def _():
        st_ref[...] = jnp.zeros_like(st_ref)

    q = SSD_CHUNK
    xbc = xbc_ref[...]
    xs = xbc[:, :SSD_INNER]
    bm = xbc[:, SSD_INNER:SSD_INNER + 128]
    cm = xbc[:, SSD_INNER + 128:]
    dtr = dt_ref[...] + dtb_ref[0]
    dt = jnp.maximum(dtr, 0.0) + jnp.log1p(jnp.exp(-jnp.abs(dtr)))
    a = -jnp.exp(alog_ref[0])
    da = dt * a

    ri = _iota((q, q), 0)
    ci = _iota((q, q), 1)
    sgn = jnp.where(d == 0, 1, -1)
    tri = (ri - ci) * sgn >= 0
    trib = jnp.where(tri, 1.0, 0.0).astype(BF16)
    acs = _dot_exact_lhs(trib, da)
    acs_t = acs.T

    e512 = e512_ref[...]
    dt_e = _dot_exact_rhs(dt, e512)
    acs_e = _dot_exact_rhs(acs, e512)
    acs_e2 = _dot_exact_rhs(acs, e1024_ref[...])
    tot_e = jnp.where(d == 0, acs_e[q - 1:q, :], acs_e[0:1, :])

    xdt = xs * dt_e
    xdec = (xdt * jnp.exp(tot_e - acs_e)).astype(BF16)
    btb = bm.T.astype(BF16)
    lane = _iota((q, 128), 1)
    cm0 = jnp.where(lane < SSD_STATE, cm, 0.0).astype(BF16)
    cm1 = jnp.where(lane >= SSD_STATE, cm, 0.0).astype(BF16)
    cbs = (_dot(cm0, btb), _dot(cm1, btb))

    st = st_ref[...]
    y_off = _dot(cm.astype(BF16), st.astype(BF16)) * jnp.exp(acs_e)
    s_all = _dot(btb, xdec)
    same = (_iota((q, SSD_INNER), 0) >> 6) == (_iota((q, SSD_INNER), 1) >> 8)
    st_ref[...] = jnp.where(same, st * jnp.exp(tot_e) + s_all, 0.0)

    for pair in range(SSD_HEADS // 2):
        cb = cbs[pair // 2]
        xp = xdt[:, pair * 128:(pair + 1) * 128]
        acc = None
        for k in range(2):
            h = 2 * pair + k
            seg = acs_e2[:, h * 128:(h + 1) * 128] - acs_t[h:h + 1, :]
            lmat = jnp.exp(jnp.where(tri, seg, NEG_INF))
            g = (cb * lmat).astype(BF16)
            rhs = jnp.where((lane < 64) if k == 0 else (lane >= 64), xp, 0.0).astype(BF16)
            t = _dot(g, rhs)
            acc = t if acc is None else acc + t
        y_ref[0, :, pair * 128:(pair + 1) * 128] = acc + y_off[:, pair * 128:(pair + 1) * 128]


def _ssd_scan(xbc_act, dt_raw, dt_bias, a_log, n_batch):
    nt = xbc_act.shape[0]
    n_lat_blk = n_batch * (SEQ // SSD_CHUNK)
    lat_c = SEQ // SSD_CHUNK
    ctx_c = CTX_LEN // SSD_CHUNK
    n_steps = lat_c + ctx_c

    def blk(b, d, c):
        cc = jnp.where(d == 0, c, ctx_c - 1 - c)
        lc = jnp.where(d == 0, c - ctx_c, n_steps - 1 - c)
        return jnp.where(c < ctx_c, n_lat_blk + b * ctx_c + cc, b * lat_c + lc)

    heads = np.arange(128)
    e512 = (heads[:, None] == (np.arange(512)[None, :] // 64)).astype(np.float32)
    e1024 = (heads[:, None] == (np.arange(1024)[None, :] // 128)).astype(np.float32)
    dtb = jnp.zeros((2, 1, 128), F32).at[:, 0, :SSD_HEADS].set(dt_bias)
    alog = jnp.zeros((2, 1, 128), F32).at[:, 0, :SSD_HEADS].set(a_log)
    return pl.pallas_call(
        _ssd_kernel,
        grid=(n_batch, 2, n_steps),
        in_specs=[pl.BlockSpec((SSD_CHUNK, SSD_CONV_DIM), lambda b, d, c: (blk(b, d, c), 0)),
                  pl.BlockSpec((SSD_CHUNK, 128), lambda b, d, c: (blk(b, d, c), d)),
                  pl.BlockSpec((1, 1, 128), lambda b, d, c: (d, 0, 0)),
                  pl.BlockSpec((1, 1, 128), lambda b, d, c: (d, 0, 0)),
                  pl.BlockSpec((128, 512), lambda b, d, c: (0, 0)),
                  pl.BlockSpec((128, 1024), lambda b, d, c: (0, 0))],
        out_specs=pl.BlockSpec((1, SSD_CHUNK, SSD_INNER), lambda b, d, c: (d, blk(b, d, c), 0)),
        out_shape=jax.ShapeDtypeStruct((2, nt, SSD_INNER), F32),
        scratch_shapes=[pltpu.VMEM((128, SSD_INNER), F32)],
        compiler_params=_cparams(3, 32),
        name="ssd_scan",
    )(xbc_act, dt_raw, dtb, alog, jnp.asarray(e512, BF16), jnp.asarray(e1024, BF16))


def _softmax_pv(s_list, v_list, extra_logit=None):
    m = s_list[0].max(axis=-1, keepdims=True)
    for s in s_list[1:]:
        m = jnp.maximum(m, s.max(axis=-1, keepdims=True))
    if extra_logit is not None:
        m = jnp.maximum(m, extra_logit)
    den = None
    o = None
    for s, v in zip(s_list, v_list):
        p = jnp.exp(s - m)
        ps = p.sum(axis=-1, keepdims=True)
        den = ps if den is None else den + ps
        t = _dot(p.astype(BF16), v)
        o = t if o is None else o + t
    if extra_logit is not None:
        den = den + jnp.exp(extra_logit - m)
    return o / den


def _wattn_kernel(sink_ref, q_ref, k_ref, v_ref, kc_ref, vc_ref, o_ref):
    n = pl.program_id(1)
    nb = SEQ // A_BLOCK
    start = pl.multiple_of(jnp.clip(n - 1, 0, nb - 3) * A_BLOCK, A_BLOCK)
    q = q_ref[:, 0:256]
    kw = k_ref[pl.ds(start, 3 * A_BLOCK), 256:384]
    vw = v_ref[pl.ds(start, 3 * A_BLOCK), :]
    kc = kc_ref[:, 256:384]
    vc = vc_ref[...]
    qpos = n * A_BLOCK + (_iota((2 * A_BLOCK, 3 * A_BLOCK), 0) & (A_BLOCK - 1))
    kpos = start + _iota((2 * A_BLOCK, 3 * A_BLOCK), 1)
    valid = jnp.abs(qpos - kpos) <= A_BLOCK
    top = _iota((2 * A_BLOCK, 1), 0) < A_BLOCK
    outs = []
    for g in range(A_KV_HEADS):
        qg = jnp.concatenate([q[:, (2 * g) * 64:(2 * g + 1) * 64],
                              q[:, (2 * g + 1) * 64:(2 * g + 2) * 64]], axis=0)
        kg = kw[:, g * 64:(g + 1) * 64]
        vg = vw[:, g * 64:(g + 1) * 64]
        s_loc = jnp.where(valid, _dot_nt(qg, kg), NEG_INF)
        s_ctx = _dot_nt(qg, kc[:, g * 64:(g + 1) * 64])
        sink = jnp.where(top, sink_ref[2 * g], sink_ref[2 * g + 1])
        o = _softmax_pv([s_loc, s_ctx], [vg, vc[:, g * 64:(g + 1) * 64]], sink)
        outs += [o[:A_BLOCK], o[A_BLOCK:]]
    o_ref[...] = jnp.concatenate(outs, axis=1).astype(BF16)


def _window_attn(sink, qk, v, n_batch):
    nb = SEQ // A_BLOCK
    ctx0 = n_batch * SEQ // CTX_LEN
    return pl.pallas_call(
        _wattn_kernel,
        grid=(n_batch, nb),
        in_specs=[pl.BlockSpec(memory_space=pltpu.SMEM),
                  pl.BlockSpec((A_BLOCK, 384), lambda b, n: (b * nb + n, 0)),
                  pl.BlockSpec((SEQ, 384), lambda b, n: (b, 0)),
                  pl.BlockSpec((SEQ, 128), lambda b, n: (b, 0)),
                  pl.BlockSpec((CTX_LEN, 384), lambda b, n: (ctx0 + b, 0)),
                  pl.BlockSpec((CTX_LEN, 128), lambda b, n: (ctx0 + b, 0))],
        out_specs=pl.BlockSpec((A_BLOCK, 256), lambda b, n: (b * nb + n, 0)),
        out_shape=jax.ShapeDtypeStruct((n_batch * SEQ, 256), BF16),
        compiler_params=_cparams(2, 32),
        name="window_attn",
    )(sink, qk, qk, v, qk, v)


def _nattn_kernel(q_ref, kv_ref, c_ref, bias_ref, o_ref):
    i = pl.program_id(1)
    r0 = i * NA_Q_ROWS
    srow = jnp.clip(r0 - NA_WIN_ROWS // 2, 0, GRID_ROWS - NA_K_ROWS)
    start = pl.multiple_of(srow * GRID_W, GRID_W)
    nq = NA_Q_ROWS * GRID_W
    nk = NA_K_ROWS * GRID_W
    q = q_ref[:, 0:256]
    kw = kv_ref[pl.ds(start, nk), 256:512]
    vw = kv_ref[pl.ds(start, nk), 512:768]
    kc = c_ref[:, 256:512]
    vc = c_ref[:, 512:768]
    qrow = r0 + (_iota((nq, nk), 0) >> 6)
    krow = srow + (_iota((nq, nk), 1) >> 6)
    rs = jnp.clip(qrow - NA_WIN_ROWS // 2, 0, GRID_ROWS - NA_WIN_ROWS)
    valid = jnp.logical_and(krow >= rs, krow < rs + NA_WIN_ROWS)
    outs = []
    for h in range(NA_HEADS):
        sl = slice(h * 64, (h + 1) * 64)
        rows = []
        for qi in range(NA_Q_ROWS):
            blocks = []
            for p in range(NA_K_ROWS // 2):
                idx = srow + 2 * p - (r0 + qi) + (NA_WIN_ROWS - 1) + NA_BIAS_OFF
                blocks.append(bias_ref[h, idx])
            rows.append(jnp.concatenate(blocks, axis=1))
        bias = jnp.concatenate(rows, axis=0)
        s_loc = jnp.where(valid, _dot_nt(q[:, sl], kw[:, sl]) + bias, NEG_INF)
        s_ctx = _dot_nt(q[:, sl], kc[:, sl])
        outs.append(_softmax_pv([s_loc, s_ctx], [vw[:, sl], vc[:, sl]]))
    o_ref[...] = jnp.concatenate(outs, axis=1).astype(BF16)


def _na_bias_table(rpb):
    cq = np.arange(GRID_W)
    kcol = np.arange(GRID_W)
    cs = np.clip(cq - NA_WIN_COLS // 2, 0, GRID_W - NA_WIN_COLS)
    col_valid = (kcol[None, :] >= cs[:, None]) & (kcol[None, :] < cs[:, None] + NA_WIN_COLS)
    coff = np.clip(kcol[None, :] - cq[:, None], -(NA_WIN_COLS - 1), NA_WIN_COLS - 1) + (NA_WIN_COLS - 1)
    n_a = 2 * NA_WIN_ROWS - 1
    tm = jnp.where(col_valid[None, None], rpb.astype(F32)[:, :, coff], NEG_INF)
    neg = jnp.full((rpb.shape[0], 1, GRID_W, GRID_W), NEG_INF, F32)
    pad_lo = NA_BIAS_OFF
    pad_hi = NA_BIAS_N + 1 - pad_lo - n_a
    ext = jnp.concatenate([neg] * pad_lo + [tm] + [neg] * pad_hi, axis=1)
    return jnp.concatenate([ext[:, :NA_BIAS_N], ext[:, 1:NA_BIAS_N + 1]], axis=-1)


def _neighborhood_attn(na, bias_t, n_batch):
    steps = GRID_ROWS // NA_Q_ROWS
    nq = NA_Q_ROWS * GRID_W
    ctx0 = n_batch * SEQ // CTX_LEN
    return pl.pallas_call(
        _nattn_kernel,
        grid=(n_batch, steps),
        in_specs=[pl.BlockSpec((nq, 768), lambda b, i: (b * steps + i, 0)),
                  pl.BlockSpec((SEQ, 768), lambda b, i: (b, 0)),
                  pl.BlockSpec((CTX_LEN, 768), lambda b, i: (ctx0 + b, 0)),
                  pl.BlockSpec((NA_HEADS, NA_BIAS_N, GRID_W, 128), lambda b, i: (0, 0, 0, 0))],
        out_specs=pl.BlockSpec((nq, 256), lambda b, i: (b * steps + i, 0)),
        out_shape=jax.ShapeDtypeStruct((n_batch * SEQ, 256), BF16),
        compiler_params=_cparams(2, 40),
        name="neighborhood_attn",
    )(na, na, na, bias_t)


def _ctx_attn_kernel(sink_ref, qk_ref, v_ref, na_ref, oa_ref, on_ref):
    qk = qk_ref[...]
    v = v_ref[...]
    na = na_ref[...]
    outs = []
    for h in range(A_HEADS):
        g = h // (A_HEADS // A_KV_HEADS)
        s = _dot_nt(qk[:, h * 64:(h + 1) * 64], qk[:, 256 + g * 64:256 + (g + 1) * 64])
        sink = jnp.zeros((CTX_LEN, 1), F32) + sink_ref[h]
        outs.append(_softmax_pv([s], [v[:, g * 64:(g + 1) * 64]], sink))
    oa_ref[...] = jnp.concatenate(outs, axis=1).astype(BF16)
    outs = []
    for h in range(NA_HEADS):
        sl = slice(h * 64, (h + 1) * 64)
        s = _dot_nt(na[:, 0:256][:, sl], na[:, 256:512][:, sl])
        outs.append(_softmax_pv([s], [na[:, 512:768][:, sl]]))
    on_ref[...] = jnp.concatenate(outs, axis=1).astype(BF16)


def _ctx_attn(sink, qk, v, na, n_batch):
    ctx0 = n_batch * SEQ // CTX_LEN
    row = lambda b: (ctx0 + b, 0)
    return pl.pallas_call(
        _ctx_attn_kernel,
        grid=(n_batch,),
        in_specs=[pl.BlockSpec(memory_space=pltpu.SMEM),
                  pl.BlockSpec((CTX_LEN, 384), row),
                  pl.BlockSpec((CTX_LEN, 128), row),
                  pl.BlockSpec((CTX_LEN, 768), row)],
        out_specs=[pl.BlockSpec((CTX_LEN, 256), lambda b: (b, 0))] * 2,
        out_shape=[jax.ShapeDtypeStruct((n_batch * CTX_LEN, 256), BF16)] * 2,
        compiler_params=_cparams(1, 32),
        name="ctx_attn",
    )(sink, qk, v, na)


def _outproj_kernel(x_ref, oa_ref, y_ref, xbc_ref, z_ref, on_ref, mod_ref, dskip_ref, ng_ref,
                    w_ref, gf_ref, wr1_ref, wr2_ref, br_ref, xo_ref, tok_ref, comb_ref):
    m = mod_ref[0]
    xs = xbc_ref[:, 0:SSD_INNER]
    y = y_ref[0] + y_ref[1] + dskip_ref[...] * xs
    y = y * _silu(z_ref[...])
    ob = y * lax.rsqrt(jnp.mean(y * y, axis=-1, keepdims=True) + RMS_EPS) * ng_ref[...]
    proj = (_dot(oa_ref[...], w_ref[0:256, :]) + _dot(ob.astype(BF16), w_ref[256:768, :])
            + _dot(on_ref[...], w_ref[768:1024, :]))
    x = x_ref[...] + m[2:3] * proj
    xo_ref[...] = x
    t = x * lax.rsqrt(jnp.mean(x * x, axis=-1, keepdims=True) + RMS_EPS) * gf_ref[...]
    t = t * (1.0 + m[4:5]) + m[3:4]
    t1 = t.astype(BF16)
    tok_ref[...] = t1
    t2 = (t - t1.astype(F32)).astype(BF16)
    logits = _dot(t1, wr1_ref[...]) + _dot(t1, wr2_ref[...]) + _dot(t2, wr1_ref[...]) + br_ref[...]

    lane = _iota(logits.shape, 1)
    big = jnp.int32(1 << 20)
    is_g = jnp.logical_and(lane >= N_EXPERTS, lane < N_EXPERTS + MOE_GROUPS)
    gl = jnp.where(is_g, logits, NEG_INF)
    gmax = gl.max(axis=-1, keepdims=True)
    g_w = 1.0 / jnp.exp(gl - gmax).sum(axis=-1, keepdims=True)
    g_idx = jnp.where(gl == gmax, lane, big).min(axis=-1, keepdims=True) - N_EXPERTS
    in_grp = jnp.logical_and(lane < N_EXPERTS, (lane >> 3) == g_idx)
    el = jnp.where(in_grp, logits, NEG_INF)
    l1 = el.max(axis=-1, keepdims=True)
    i1 = jnp.where(el == l1, lane, big).min(axis=-1, keepdims=True)
    el2 = jnp.where(lane == i1, NEG_INF, el)
    l2 = el2.max(axis=-1, keepdims=True)
    i2 = jnp.where(el2 == l2, lane, big).min(axis=-1, keepdims=True)
    e2 = jnp.exp(l2 - l1)
    w1 = g_w / (1.0 + e2)
    w2 = w1 * e2
    comb_ref[...] = jnp.where(lane == i1, w1, jnp.where(lane == i2, w2, 0.0))


def _outproj(xc, oa, y2, xbc_act, z, on, mod_l, dskip, norm_g, w_out, g_ffn, wr1, wr2, br,
             n_batch, n_tiles):
    n_lat_tiles = n_batch * SEQ // TILE
    tpb = SEQ // TILE
    row = lambda i: (i, 0)
    modrow = lambda i: (_tile_mod_row(i, n_lat_tiles, tpb, n_batch), 0, 0)
    const = lambda i: (0, 0)
    return pl.pallas_call(
        _outproj_kernel,
        grid=(n_tiles,),
        in_specs=[pl.BlockSpec((TILE, D_MODEL), row),
                  pl.BlockSpec((TILE, 256), row),
                  pl.BlockSpec((2, TILE, SSD_INNER), lambda i: (0, i, 0)),
                  pl.BlockSpec((TILE, SSD_CONV_DIM), row),
                  pl.BlockSpec((TILE, SSD_INNER), row),
                  pl.BlockSpec((TILE, 256), row),
                  pl.BlockSpec((1, N_MOD, D_MODEL), modrow),
                  pl.BlockSpec((1, SSD_INNER), const),
                  pl.BlockSpec((1, SSD_INNER), const),
                  pl.BlockSpec((D_MODEL, D_MODEL), const),
                  pl.BlockSpec((1, D_MODEL), const),
                  pl.BlockSpec((D_MODEL, LANES), const),
                  pl.BlockSpec((D_MODEL, LANES), const),
                  pl.BlockSpec((1, LANES), const)],
        out_specs=[pl.BlockSpec((TILE, D_MODEL), row),
                   pl.BlockSpec((TILE, D_MODEL), row),
                   pl.BlockSpec((TILE, LANES), row)],
        out_shape=[jax.ShapeDtypeStruct((n_tiles * TILE, D_MODEL), F32),
                   jax.ShapeDtypeStruct((n_tiles * TILE, D_MODEL), BF16),
                   jax.ShapeDtypeStruct((n_tiles * TILE, LANES), F32)],
        compiler_params=_cparams(1, 48),
        name="outproj",
    )(xc, oa, y2, xbc_act, z, on, mod_l, dskip, norm_g, w_out, g_ffn, wr1, wr2, br)


def _moe_kernel(x_ref, tok_ref, comb_ref, mod_ref, wg_ref, wu_ref, wd_ref, o_ref, acc_ref):
    e = pl.program_id(1)

    @pl.when(e == 0)
    def _():
        acc_ref[...] = jnp.zeros_like(acc_ref)

    tok = tok_ref[...]
    comb = comb_ref[...]
    wsel = jnp.where(_iota(comb.shape, 1) == e, comb, 0.0).sum(axis=-1, keepdims=True)
    gate = _dot(tok, wg_ref[0].astype(BF16))
    up = _dot(tok, wu_ref[0].astype(BF16))
    hid = (_silu(gate) * up * wsel).astype(BF16)
    acc_ref[...] += _dot(hid, wd_ref[0].astype(BF16))

    @pl.when(e == N_EXPERTS - 1)
    def _():
        o_ref[...] = x_ref[...] + mod_ref[0][5:6] * acc_ref[...]


def _moe(xc, tok, comb, mod_l, w_gate, w_up, w_down, n_batch, n_tiles):
    nt = xc.shape[0]
    n_lat_tiles = n_batch * SEQ // TILE
    tpb = SEQ // TILE
    row = lambda i, e: (i, 0)
    modrow = lambda i, e: (_tile_mod_row(i, n_lat_tiles, tpb, n_batch), 0, 0)
    return pl.pallas_call(
        _moe_kernel,
        grid=(n_tiles, N_EXPERTS),
        in_specs=[pl.BlockSpec((TILE, D_MODEL), row),
                  pl.BlockSpec((TILE, D_MODEL), row),
                  pl.BlockSpec((TILE, LANES), row),
                  pl.BlockSpec((1, N_MOD, D_MODEL), modrow),
                  pl.BlockSpec((1, D_MODEL, D_EXPERT), lambda i, e: (e, 0, 0)),
                  pl.BlockSpec((1, D_MODEL, D_EXPERT), lambda i, e: (e, 0, 0)),
                  pl.BlockSpec((1, D_EXPERT, D_MODEL), lambda i, e: (e, 0, 0))],
        out_specs=pl.BlockSpec((TILE, D_MODEL), row),
        out_shape=jax.ShapeDtypeStruct((nt, D_MODEL), F32),
        scratch_shapes=[pltpu.VMEM((TILE, D_MODEL), F32)],
        compiler_params=_cparams(2, 40),
        name="moe",
    )(xc, tok, comb, mod_l, w_gate, w_up, w_down)


def _final_norm_kernel(x_ref, g_ref, o_ref):
    x = x_ref[...]
    o_ref[...] = x * lax.rsqrt(jnp.mean(x * x, axis=-1, keepdims=True) + RMS_EPS) * g_ref[...]


def _final_norm(xc, g, n_rows):
    return pl.pallas_call(
        _final_norm_kernel,
        grid=(n_rows // TILE,),
        in_specs=[pl.BlockSpec((TILE, D_MODEL), lambda i: (i, 0)),
                  pl.BlockSpec((1, D_MODEL), lambda i: (0, 0))],
        out_specs=pl.BlockSpec((TILE, D_MODEL), lambda i: (i, 0)),
        out_shape=jax.ShapeDtypeStruct((n_rows, D_MODEL), F32),
        compiler_params=_cparams(1, 32),
        name="final_norm",
    )(xc, g)


def _rope_tables():
    t = jnp.arange(SEQ)
    rows_pos = (t // GRID_W).astype(F32)
    cols_pos = (t % GRID_W).astype(F32)
    half = HEAD_DIM // 2
    inv = 1.0 / (ROPE_BASE ** (jnp.arange(0, half, 2, dtype=F32) / half))
    ang_r = rows_pos[:, None] * inv[None, :]
    ang_c = cols_pos[:, None] * inv[None, :]
    ang = jnp.concatenate([ang_r, ang_r, ang_c, ang_c], axis=1)
    cos_h, sin_h = jnp.cos(ang), jnp.sin(ang)
    scale = jnp.concatenate([jnp.full((256,), HEAD_DIM ** -0.5, F32), jnp.ones((128,), F32)])
    cos_t = jnp.tile(cos_h, (1, 6)) * scale
    sin_t = jnp.tile(sin_h, (1, 6)) * scale
    cos_t = jnp.concatenate([cos_t, jnp.broadcast_to(scale, (TILE, 384))], axis=0)
    sin_t = jnp.concatenate([sin_t, jnp.zeros((TILE, 384), F32)], axis=0)
    return cos_t, sin_t


def _rope_partner():
    j = np.arange(HEAD_DIM)
    jj = j % 32
    first = jj < 16
    partner = np.where(first, j + 16, j - 16)
    sign = np.where(first, -1.0, 1.0).astype(np.float32)
    return partner, sign


def _fused_in_weight(w_in):
    partner, sign = _rope_partner()
    wq, wk, wv = w_in[:, 0:256], w_in[:, 256:384], w_in[:, 384:512]
    o = A_IN
    wz, wxbc, wdt = w_in[:, o:o + 512], w_in[:, o + 512:o + 1280], w_in[:, o + 1280:o + 1296]
    wna = w_in[:, A_IN + SSD_IN:]

    def rot(w, heads):
        idx = (np.arange(heads)[:, None] * HEAD_DIM + partner[None, :]).reshape(-1)
        return w[:, idx] * jnp.asarray(np.tile(sign, heads))

    na_scale = jnp.concatenate([jnp.full((256,), HEAD_DIM ** -0.5, F32), jnp.ones((512,), F32)])
    pad = jnp.zeros((D_MODEL, 128 - SSD_HEADS), F32)
    cat = jnp.concatenate([wq, wk, rot(wq, A_HEADS), rot(wk, A_KV_HEADS), wv, wz, wxbc,
                           wna * na_scale, wdt[:, :SSD_HEADS], pad, wdt[:, SSD_HEADS:], pad], axis=1)
    return cat.astype(BF16)


def _router_weight(w_rg, b_rg, w_re, b_re):
    w = jnp.concatenate([w_re, w_rg, jnp.zeros((D_MODEL, LANES - N_EXPERTS - MOE_GROUPS), F32)], axis=1)
    b = jnp.concatenate([b_re, b_rg, jnp.zeros((LANES - N_EXPERTS - MOE_GROUPS,), F32)]).reshape(1, LANES)
    w1 = w.astype(BF16)
    w2 = (w - w1.astype(F32)).astype(BF16)
    return w1, w2, b


def kernel(x, c, ctx, c_ctx, w_mod, b_mod, g_mix, w_in, attn_sink, ssd_conv_w, ssd_conv_b, ssd_dt_bias, ssd_a_log, ssd_d, ssd_norm_g, na_rpb, w_out, g_ffn, w_router_group, b_router_group, w_router_expert, b_router_expert, w_exp_gate, w_exp_up, w_exp_down, g_final):
    n_batch, s, d = x.shape
    assert (s, d) == (SEQ, D_MODEL) and ctx.shape[1:] == (CTX_LEN, D_MODEL) and n_batch < 16
    n_lat = n_batch * SEQ
    n_ctx = n_batch * CTX_LEN
    assert n_ctx % TILE == 0
    n_lat_tiles = n_lat // TILE
    n_all_tiles = (n_lat + n_ctx) // TILE

    xc = jnp.concatenate([x.reshape(n_lat, d), ctx.reshape(n_ctx, d)], axis=0)
    cin = jnp.zeros((16, d), F32).at[:n_batch].set(c).at[n_batch].set(c_ctx)
    mod = _modulation(cin, w_mod, b_mod).reshape(DEPTH, 16, N_MOD, d)
    cos_t, sin_t = _rope_tables()

    for layer in range(DEPTH):
        need_ctx = layer < DEPTH - 1
        mod_l = mod[layer]
        w_cat = _fused_in_weight(w_in[layer])
        qk, v, z, xbc, na, dt_raw = _inproj(xc, mod_l, g_mix[layer].reshape(1, d), w_cat, cos_t, sin_t, n_batch)
        sink = attn_sink[layer].astype(F32)
        oa = _window_attn(sink, qk, v, n_batch)
        xbc_act = _conv_silu(xbc, jnp.zeros((8, SSD_CONV_DIM), F32).at[:SSD_CONV].set(ssd_conv_w[layer]),
                             ssd_conv_b[layer].reshape(1, SSD_CONV_DIM), n_batch)
        y2 = _ssd_scan(xbc_act, dt_raw, ssd_dt_bias[layer], ssd_a_log[layer], n_batch)
        on = _neighborhood_attn(na, _na_bias_table(na_rpb[layer]), n_batch)
        if need_ctx:
            oa_c, on_c = _ctx_attn(sink, qk, v, na, n_batch)
            oa = jnp.concatenate([oa, oa_c], axis=0)
            on = jnp.concatenate([on, on_c], axis=0)
        n_tiles = n_all_tiles if need_ctx else n_lat_tiles
        wr1, wr2, br = _router_weight(w_router_group[layer], b_router_group[layer],
                                      w_router_expert[layer], b_router_expert[layer])
        dskip = jnp.repeat(ssd_d[layer].astype(F32), SSD_INNER // SSD_HEADS).reshape(1, SSD_INNER)
        xc, tok, comb = _outproj(xc, oa, y2, xbc_act, z, on, mod_l, dskip,
                                 ssd_norm_g[layer].reshape(1, SSD_INNER), w_out[layer].astype(BF16),
                                 g_ffn[layer].reshape(1, d), wr1, wr2, br, n_batch, n_tiles)
        xc = _moe(xc, tok, comb, mod_l,
                  w_exp_gate[layer].reshape(N_EXPERTS, D_MODEL, D_EXPERT),
                  w_exp_up[layer].reshape(N_EXPERTS, D_MODEL, D_EXPERT),
                  w_exp_down[layer].reshape(N_EXPERTS, D_EXPERT, D_MODEL), n_batch, n_tiles)

    return _final_norm(xc, g_final.reshape(1, d), n_lat).reshape(n_batch, SEQ, d)
```

```python
import functools
import math

import jax
import jax.numpy as jnp
import numpy as np
from jax import lax
from jax.experimental import pallas as pl
from jax.experimental.pallas import tpu as pltpu

F32 = jnp.float32
BF16 = jnp.bfloat16

D_MODEL = 1024
SEQ = 2048
DEPTH = 4
GRID_W = 64
GRID_ROWS = SEQ // GRID_W
CTX_LEN = 256
HEAD_DIM = 64
A_HEADS = 4
A_KV_HEADS = 2
A_BLOCK = 128
ROPE_BASE = 10000.0
SSD_HEADS = 8
SSD_INNER = 512
SSD_STATE = 64
SSD_CONV = 5
SSD_CHUNK = 128
SSD_CONV_DIM = 768
NA_HEADS = 4
NA_WIN_ROWS = 8
NA_WIN_COLS = 16
A_IN = 512
SSD_IN = 1296
MOE_GROUPS = 4
MOE_EXPERTS = 8
N_EXPERTS = MOE_GROUPS * MOE_EXPERTS
D_EXPERT = 256
N_MOD = 6
RMS_EPS = 1e-6
NEG_INF = -1e30

TILE = 512
LANES = 128
C_QK, C_QKP, C_V, C_Z, C_XBC, C_NA, C_DT = 0, 384, 768, 896, 1408, 2176, 2944
N_COLS = 3200
NA_Q_ROWS = 2
NA_K_ROWS = 10
NA_BIAS_OFF = 2
NA_BIAS_N = 18


def _cparams(n_axes, vmem_mb):
    return pltpu.CompilerParams(dimension_semantics=("arbitrary",) * n_axes,
                                vmem_limit_bytes=vmem_mb << 20)


def _split3(x):
    h1 = x.astype(BF16)
    r1 = x - h1.astype(F32)
    h2 = r1.astype(BF16)
    h3 = (r1 - h2.astype(F32)).astype(BF16)
    return h1, h2, h3


def _dot(a, b):
    return jnp.dot(a, b, preferred_element_type=F32)


def _dot_nt(a, b):
    return lax.dot_general(a, b, (((1,), (1,)), ((), ())), preferred_element_type=F32)


def _dot_exact_lhs(lhs_bf16, x):
    h1, h2, h3 = _split3(x)
    return _dot(lhs_bf16, h1) + _dot(lhs_bf16, h2) + _dot(lhs_bf16, h3)


def _dot_exact_rhs(x, rhs_bf16):
    h1, h2, h3 = _split3(x)
    return _dot(h1, rhs_bf16) + _dot(h2, rhs_bf16) + _dot(h3, rhs_bf16)


def _silu(x):
    return x * jax.nn.sigmoid(x)


def _iota(shape, dim):
    return lax.broadcasted_iota(jnp.int32, shape, dim)


def _mod_kernel(c_ref, w_ref, b_ref, o_ref):
    a = _silu(c_ref[...])
    a1, a2, _ = _split3(a)
    w = w_ref[0]
    w1 = w.astype(BF16)
    w2 = (w - w1.astype(F32)).astype(BF16)
    o_ref[0] = _dot(a1, w1) + _dot(a1, w2) + _dot(a2, w1) + b_ref[0]


def _modulation(cin, w_mod, b_mod):
    nt = 1024
    return pl.pallas_call(
        _mod_kernel,
        grid=(DEPTH, N_MOD * D_MODEL // nt),
        in_specs=[pl.BlockSpec((16, D_MODEL), lambda l, j: (0, 0)),
                  pl.BlockSpec((1, D_MODEL, nt), lambda l, j: (l, 0, j)),
                  pl.BlockSpec((1, 1, nt), lambda l, j: (l, 0, j))],
        out_specs=pl.BlockSpec((1, 16, nt), lambda l, j: (l, 0, j)),
        out_shape=jax.ShapeDtypeStruct((DEPTH, 16, N_MOD * D_MODEL), F32),
        compiler_params=_cparams(2, 40),
        name="modulation",
    )(cin, w_mod, b_mod.reshape(DEPTH, 1, N_MOD * D_MODEL))


def _inproj_kernel(x_ref, mod_ref, g_ref, w_ref, cos_ref, sin_ref,
                   qk_ref, v_ref, z_ref, xbc_ref, na_ref, dt_ref):
    x = x_ref[...]
    m = mod_ref[0, 0]
    h = x * lax.rsqrt(jnp.mean(x * x, axis=-1, keepdims=True) + RMS_EPS) * g_ref[0]
    h = h * (1.0 + m[1:2]) + m[0:1]
    hb = h.astype(BF16)

    def mm(lo, hi):
        return _dot(hb, w_ref[0, :, lo:hi])

    qk = mm(C_QK, C_QKP) * cos_ref[...] + mm(C_QKP, C_V) * sin_ref[...]
    qk_ref[...] = qk.astype(BF16)
    v_ref[...] = mm(C_V, C_Z).astype(BF16)
    z_ref[...] = mm(C_Z, C_XBC)
    xbc_ref[...] = mm(C_XBC, C_NA)
    na_ref[...] = mm(C_NA, C_DT).astype(BF16)
    dt_ref[...] = mm(C_DT, N_COLS)


def _tile_mod_row(i, n_lat_tiles, tiles_per_batch, n_batch):
    return jnp.where(i < n_lat_tiles, i // tiles_per_batch, n_batch)


def _inproj(xc, mod, g_mix, w_cat, cos_t, sin_t, n_batch, layer):
    nt = xc.shape[0]
    n_lat_tiles = n_batch * SEQ // TILE
    tpb = SEQ // TILE
    row = lambda i: (i, 0)
    modrow = lambda i: (layer, _tile_mod_row(i, n_lat_tiles, tpb, n_batch), 0, 0)
    posrow = lambda i: (jnp.where(i < n_lat_tiles, i % tpb, tpb), 0)
    lay = lambda i: (layer, 0, 0)
    outs = [(384, BF16), (128, BF16), (512, F32), (768, F32), (768, BF16), (256, F32)]
    return pl.pallas_call(
        _inproj_kernel,
        grid=(nt // TILE,),
        in_specs=[pl.BlockSpec((TILE, D_MODEL), row),
                  pl.BlockSpec((1, 1, N_MOD, D_MODEL), modrow),
                  pl.BlockSpec((1, 1, D_MODEL), lay),
                  pl.BlockSpec((1, D_MODEL, N_COLS), lay),
                  pl.BlockSpec((TILE, 384), posrow),
                  pl.BlockSpec((TILE, 384), posrow)],
        out_specs=[pl.BlockSpec((TILE, w), row) for w, _ in outs],
        out_shape=[jax.ShapeDtypeStruct((nt, w), dt) for w, dt in outs],
        compiler_params=_cparams(1, 56),
        name="inproj",
    )(xc, mod, g_mix, w_cat, cos_t, sin_t)


def _conv_kernel(prev_ref, x_ref, next_ref, w_ref, b_ref, o_ref, *, blocks_per_seq, n_lat_blocks):
    i = pl.program_id(0)
    is_lat = i < n_lat_blocks
    first = jnp.logical_or(jnp.logical_not(is_lat), i % blocks_per_seq == 0)
    last = jnp.logical_or(jnp.logical_not(is_lat), i % blocks_per_seq == blocks_per_seq - 1)
    prev = jnp.where(first, 0.0, prev_ref[...])
    nxt = jnp.where(last, 0.0, next_ref[...])
    xw = jnp.concatenate([prev, x_ref[...], nxt], axis=0)
    rows = x_ref.shape[0]
    w = w_ref[...]
    acc = jnp.zeros(x_ref.shape, F32) + b_ref[...]
    for k in range(SSD_CONV):
        off = 8 - SSD_CONV // 2 + k
        acc = acc + xw[off:off + rows, :] * w[k:k + 1, :]
    o_ref[...] = _silu(acc)


def _conv_silu(xbc, conv_w, conv_b, n_batch):
    nt = xbc.shape[0]
    rows = 256
    bps = SEQ // rows
    n_lat_blocks = n_batch * bps
    nblk = nt // rows
    r8 = rows // 8
    kern = functools.partial(_conv_kernel, blocks_per_seq=bps, n_lat_blocks=n_lat_blocks)
    return pl.pallas_call(
        kern,
        grid=(nblk,),
        in_specs=[pl.BlockSpec((8, SSD_CONV_DIM), lambda i: (jnp.maximum(i * r8 - 1, 0), 0)),
                  pl.BlockSpec((rows, SSD_CONV_DIM), lambda i: (i, 0)),
                  pl.BlockSpec((8, SSD_CONV_DIM), lambda i: (jnp.minimum(i * r8 + r8, nt // 8 - 1), 0)),
                  pl.BlockSpec((8, SSD_CONV_DIM), lambda i: (0, 0)),
                  pl.BlockSpec((1, SSD_CONV_DIM), lambda i: (0, 0))],
        out_specs=pl.BlockSpec((rows, SSD_CONV_DIM), lambda i: (i, 0)),
        out_shape=jax.ShapeDtypeStruct((nt, SSD_CONV_DIM), F32),
        compiler_params=_cparams(1, 32),
        name="ssd_conv",
    )(xbc, xbc, xbc, conv_w, conv_b)


def _ssd_kernel(xbc_ref, dt_ref, dtb_ref, alog_ref, e512_ref, e1024_ref, y_ref, st_ref):
    d = pl.program_id(1)
    c = pl.program_id(2)

    @pl.when(c == 0)
    def _():
        st_ref[...] = jnp.zeros_like(st_ref)

    q = SSD_CHUNK
    xbc = xbc_ref[...]
    xs = xbc[:, :SSD_INNER]
    bm = xbc[:, SSD_INNER:SSD_INNER + 128]
    cm = xbc[:, SSD_INNER + 128:]
    dtr = dt_ref[...] + dtb_ref[0]
    dt = jnp.maximum(dtr, 0.0) + jnp.log1p(jnp.exp(-jnp.abs(dtr)))
    a = -jnp.exp(alog_ref[0])
    da = dt * a

    ri = _iota((q, q), 0)
    ci = _iota((q, q), 1)
    sgn = jnp.where(d == 0, 1, -1)
    tri = (ri - ci) * sgn >= 0
    trib = jnp.where(tri, 1.0, 0.0).astype(BF16)
    acs = _dot_exact_lhs(trib, da)
    acs_t = acs.T

    e512 = e512_ref[...]
    dt_e = _dot_exact_rhs(dt, e512)
    acs_e = _dot_exact_rhs(acs, e512)
    acs_e2 = _dot_exact_rhs(acs, e1024_ref[...])
    tot_e = jnp.where(d == 0, acs_e[q - 1:q, :], acs_e[0:1, :])

    xdt = xs * dt_e
    xdec = (xdt * jnp.exp(tot_e - acs_e)).astype(BF16)
    btb = bm.T.astype(BF16)
    lane = _iota((q, 128), 1)
    cm0 = jnp.where(lane < SSD_STATE, cm, 0.0).astype(BF16)
    cm1 = jnp.where(lane >= SSD_STATE, cm, 0.0).astype(BF16)
    cbs = (_dot(cm0, btb), _dot(cm1, btb))

    st = st_ref[...]
    y_off = _dot(cm.astype(BF16), st.astype(BF16)) * jnp.exp(acs_e)
    s_all = _dot(btb, xdec)
    same = (_iota((q, SSD_INNER), 0) >> 6) == (_iota((q, SSD_INNER), 1) >> 8)
    st_ref[...] = jnp.where(same, st * jnp.exp(tot_e) + s_all, 0.0)

    for pair in range(SSD_HEADS // 2):
        cb = cbs[pair // 2]
        xp = xdt[:, pair * 128:(pair + 1) * 128]
        acc = None
        for k in range(2):
            h = 2 * pair + k
            seg = acs_e2[:, h * 128:(h + 1) * 128] - acs_t[h:h + 1, :]
            lmat = jnp.exp(jnp.where(tri, seg, NEG_INF))
            g = (cb * lmat).astype(BF16)
            rhs = jnp.where((lane < 64) if k == 0 else (lane >= 64), xp, 0.0).astype(BF16)
            t = _dot(g, rhs)
            acc = t if acc is None else acc + t
        y_ref[0, :, pair * 128:(pair + 1) * 128] = acc + y_off[:, pair * 128:(pair + 1) * 128]


def _ssd_scan(xbc_act, dt_raw, dt_bias, a_log, n_batch):
    nt = xbc_act.shape[0]
    n_lat_blk = n_batch * (SEQ // SSD_CHUNK)
    lat_c = SEQ // SSD_CHUNK
    ctx_c = CTX_LEN // SSD_CHUNK
    n_steps = lat_c + ctx_c

    def blk(b, d, c):
        cc = jnp.where(d == 0, c, ctx_c - 1 - c)
        lc = jnp.where(d == 0, c - ctx_c, n_steps - 1 - c)
        return jnp.where(c < ctx_c, n_lat_blk + b * ctx_c + cc, b * lat_c + lc)

    heads = np.arange(128)
    e512 = (heads[:, None] == (np.arange(512)[None, :] // 64)).astype(np.float32)
    e1024 = (heads[:, None] == (np.arange(1024)[None, :] // 128)).astype(np.float32)
    dtb = jnp.zeros((2, 1, 128), F32).at[:, 0, :SSD_HEADS].set(dt_bias)
    alog = jnp.zeros((2, 1, 128), F32).at[:, 0, :SSD_HEADS].set(a_log)
    return pl.pallas_call(
        _ssd_kernel,
        grid=(n_batch, 2, n_steps),
        in_specs=[pl.BlockSpec((SSD_CHUNK, SSD_CONV_DIM), lambda b, d, c: (blk(b, d, c), 0)),
                  pl.BlockSpec((SSD_CHUNK, 128), lambda b, d, c: (blk(b, d, c), d)),
                  pl.BlockSpec((1, 1, 128), lambda b, d, c: (d, 0, 0)),
                  pl.BlockSpec((1, 1, 128), lambda b, d, c: (d, 0, 0)),
                  pl.BlockSpec((128, 512), lambda b, d, c: (0, 0)),
                  pl.BlockSpec((128, 1024), lambda b, d, c: (0, 0))],
        out_specs=pl.BlockSpec((1, SSD_CHUNK, SSD_INNER), lambda b, d, c: (d, blk(b, d, c), 0)),
        out_shape=jax.ShapeDtypeStruct((2, nt, SSD_INNER), F32),
        scratch_shapes=[pltpu.VMEM((128, SSD_INNER), F32)],
        compiler_params=_cparams(3, 32),
        name="ssd_scan",
    )(xbc_act, dt_raw, dtb, alog, jnp.asarray(e512, BF16), jnp.asarray(e1024, BF16))


def _softmax_pv(s_list, v_list, extra_logit=None):
    m = s_list[0].max(axis=-1, keepdims=True)
    for s in s_list[1:]:
        m = jnp.maximum(m, s.max(axis=-1, keepdims=True))
    if extra_logit is not None:
        m = jnp.maximum(m, extra_logit)
    den = None
    o = None
    for s, v in zip(s_list, v_list):
        p = jnp.exp(s - m)
        ps = p.sum(axis=-1, keepdims=True)
        den = ps if den is None else den + ps
        t = _dot(p.astype(BF16), v)
        o = t if o is None else o + t
    if extra_logit is not None:
        den = den + jnp.exp(extra_logit - m)
    return o / den


def _wattn_kernel(sink_ref, q_ref, k_ref, v_ref, kc_ref, vc_ref, o_ref):
    n = pl.program_id(1)
    nb = SEQ // A_BLOCK
    start = pl.multiple_of(jnp.clip(n - 1, 0, nb - 3) * A_BLOCK, A_BLOCK)
    q = q_ref[:, 0:256]
    kw = k_ref[pl.ds(start, 3 * A_BLOCK), 256:384]
    vw = v_ref[pl.ds(start, 3 * A_BLOCK), :]
    kc = kc_ref[:, 256:384]
    vc = vc_ref[...]
    qpos = n * A_BLOCK + (_iota((2 * A_BLOCK, 3 * A_BLOCK), 0) & (A_BLOCK - 1))
    kpos = start + _iota((2 * A_BLOCK, 3 * A_BLOCK), 1)
    valid = jnp.abs(qpos - kpos) <= A_BLOCK
    top = _iota((2 * A_BLOCK, 1), 0) < A_BLOCK
    outs = []
    for g in range(A_KV_HEADS):
        qg = jnp.concatenate([q[:, (2 * g) * 64:(2 * g + 1) * 64],
                              q[:, (2 * g + 1) * 64:(2 * g + 2) * 64]], axis=0)
        kg = kw[:, g * 64:(g + 1) * 64]
        vg = vw[:, g * 64:(g + 1) * 64]
        s_loc = jnp.where(valid, _dot_nt(qg, kg), NEG_INF)
        s_ctx = _dot_nt(qg, kc[:, g * 64:(g + 1) * 64])
        sink = jnp.where(top, sink_ref[2 * g], sink_ref[2 * g + 1])
        o = _softmax_pv([s_loc, s_ctx], [vg, vc[:, g * 64:(g + 1) * 64]], sink)
        outs += [o[:A_BLOCK], o[A_BLOCK:]]
    o_ref[...] = jnp.concatenate(outs, axis=1).astype(BF16)


def _window_attn(sink, qk, v, n_batch):
    nb = SEQ // A_BLOCK
    ctx0 = n_batch * SEQ // CTX_LEN
    return pl.pallas_call(
        _wattn_kernel,
        grid=(n_batch, nb),
        in_specs=[pl.BlockSpec(memory_space=pltpu.SMEM),
                  pl.BlockSpec((A_BLOCK, 384), lambda b, n: (b * nb + n, 0)),
                  pl.BlockSpec((SEQ, 384), lambda b, n: (b, 0)),
                  pl.BlockSpec((SEQ, 128), lambda b, n: (b, 0)),
                  pl.BlockSpec((CTX_LEN, 384), lambda b, n: (ctx0 + b, 0)),
                  pl.BlockSpec((CTX_LEN, 128), lambda b, n: (ctx0 + b, 0))],
        out_specs=pl.BlockSpec((A_BLOCK, 256), lambda b, n: (b * nb + n, 0)),
        out_shape=jax.ShapeDtypeStruct((n_batch * SEQ, 256), BF16),
        compiler_params=_cparams(2, 32),
        name="window_attn",
    )(sink, qk, qk, v, qk, v)


def _nattn_kernel(q_ref, kv_ref, c_ref, bias_ref, o_ref):
    i = pl.program_id(1)
    r0 = i * NA_Q_ROWS
    srow = jnp.clip(r0 - NA_WIN_ROWS // 2, 0, GRID_ROWS - NA_K_ROWS)
    start = pl.multiple_of(srow * GRID_W, GRID_W)
    nq = NA_Q_ROWS * GRID_W
    nk = NA_K_ROWS * GRID_W
    q = q_ref[:, 0:256]
    kw = kv_ref[pl.ds(start, nk), 256:512]
    vw = kv_ref[pl.ds(start, nk), 512:768]
    kc = c_ref[:, 256:512]
    vc = c_ref[:, 512:768]
    qrow = r0 + (_iota((nq, nk), 0) >> 6)
    krow = srow + (_iota((nq, nk), 1) >> 6)
    rs = jnp.clip(qrow - NA_WIN_ROWS // 2, 0, GRID_ROWS - NA_WIN_ROWS)
    valid = jnp.logical_and(krow >= rs, krow < rs + NA_WIN_ROWS)
    outs = []
    for h in range(NA_HEADS):
        sl = slice(h * 64, (h + 1) * 64)
        rows = []
        for qi in range(NA_Q_ROWS):
            blocks = []
            for p in range(NA_K_ROWS // 2):
                idx = srow + 2 * p - (r0 + qi) + (NA_WIN_ROWS - 1) + NA_BIAS_OFF
                blocks.append(bias_ref[0, h, idx])
            rows.append(jnp.concatenate(blocks, axis=1))
        bias = jnp.concatenate(rows, axis=0)
        s_loc = jnp.where(valid, _dot_nt(q[:, sl], kw[:, sl]) + bias, NEG_INF)
        s_ctx = _dot_nt(q[:, sl], kc[:, sl])
        outs.append(_softmax_pv([s_loc, s_ctx], [vw[:, sl], vc[:, sl]]))
    o_ref[...] = jnp.concatenate(outs, axis=1).astype(BF16)


def _na_bias_table(rpb):
    cq = np.arange(GRID_W)
    kcol = np.arange(GRID_W)
    cs = np.clip(cq - NA_WIN_COLS // 2, 0, GRID_W - NA_WIN_COLS)
    col_valid = (kcol[None, :] >= cs[:, None]) & (kcol[None, :] < cs[:, None] + NA_WIN_COLS)
    coff = np.clip(kcol[None, :] - cq[:, None], -(NA_WIN_COLS - 1), NA_WIN_COLS - 1) + (NA_WIN_COLS - 1)
    n_a = 2 * NA_WIN_ROWS - 1
    n_c = 2 * NA_WIN_COLS - 1
    pick = (np.arange(n_c)[:, None] == coff.reshape(1, -1)).astype(np.float32)
    tm = jnp.einsum("lhak,kn->lhan", rpb.astype(F32), jnp.asarray(pick), precision=lax.Precision.HIGHEST)
    tm = jnp.where(col_valid, tm.reshape(rpb.shape[:3] + (GRID_W, GRID_W)), NEG_INF)
    neg = jnp.full(rpb.shape[:2] + (1, GRID_W, GRID_W), NEG_INF, F32)
    pad_lo = NA_BIAS_OFF
    pad_hi = NA_BIAS_N + 1 - pad_lo - n_a
    ext = jnp.concatenate([neg] * pad_lo + [tm] + [neg] * pad_hi, axis=2)
    return jnp.concatenate([ext[:, :, :NA_BIAS_N], ext[:, :, 1:NA_BIAS_N + 1]], axis=-1)


def _neighborhood_attn(na, bias_t, n_batch, layer):
    steps = GRID_ROWS // NA_Q_ROWS
    nq = NA_Q_ROWS * GRID_W
    ctx0 = n_batch * SEQ // CTX_LEN
    return pl.pallas_call(
        _nattn_kernel,
        grid=(n_batch, steps),
        in_specs=[pl.BlockSpec((nq, 768), lambda b, i: (b * steps + i, 0)),
                  pl.BlockSpec((SEQ, 768), lambda b, i: (b, 0)),
                  pl.BlockSpec((CTX_LEN, 768), lambda b, i: (ctx0 + b, 0)),
                  pl.BlockSpec((1, NA_HEADS, NA_BIAS_N, GRID_W, 128), lambda b, i: (layer, 0, 0, 0, 0))],
        out_specs=pl.BlockSpec((nq, 256), lambda b, i: (b * steps + i, 0)),
        out_shape=jax.ShapeDtypeStruct((n_batch * SEQ, 256), BF16),
        compiler_params=_cparams(2, 40),
        name="neighborhood_attn",
    )(na, na, na, bias_t)


def _ctx_attn_kernel(sink_ref, qk_ref, v_ref, na_ref, oa_ref, on_ref):
    qk = qk_ref[...]
    v = v_ref[...]
    na = na_ref[...]
    outs = []
    for h in range(A_HEADS):
        g = h // (A_HEADS // A_KV_HEADS)
        s = _dot_nt(qk[:, h * 64:(h + 1) * 64], qk[:, 256 + g * 64:256 + (g + 1) * 64])
        sink = jnp.zeros((CTX_LEN, 1), F32) + sink_ref[h]
        outs.append(_softmax_pv([s], [v[:, g * 64:(g + 1) * 64]], sink))
    oa_ref[...] = jnp.concatenate(outs, axis=1).astype(BF16)
    outs = []
    for h in range(NA_HEADS):
        sl = slice(h * 64, (h + 1) * 64)
        s = _dot_nt(na[:, 0:256][:, sl], na[:, 256:512][:, sl])
        outs.append(_softmax_pv([s], [na[:, 512:768][:, sl]]))
    on_ref[...] = jnp.concatenate(outs, axis=1).astype(BF16)


def _ctx_attn(sink, qk, v, na, n_batch):
    ctx0 = n_batch * SEQ // CTX_LEN
    row = lambda b: (ctx0 + b, 0)
    return pl.pallas_call(
        _ctx_attn_kernel,
        grid=(n_batch,),
        in_specs=[pl.BlockSpec(memory_space=pltpu.SMEM),
                  pl.BlockSpec((CTX_LEN, 384), row),
                  pl.BlockSpec((CTX_LEN, 128), row),
                  pl.BlockSpec((CTX_LEN, 768), row)],
        out_specs=[pl.BlockSpec((CTX_LEN, 256), lambda b: (b, 0))] * 2,
        out_shape=[jax.ShapeDtypeStruct((n_batch * CTX_LEN, 256), BF16)] * 2,
        compiler_params=_cparams(1, 32),
        name="ctx_attn",
    )(sink, qk, v, na)


def _outproj_kernel(x_ref, oal_ref, oac_ref, y_ref, xbc_ref, z_ref, onl_ref, onc_ref, mod_ref, dskip_ref,
                    ng_ref, w_ref, gf_ref, wr1_ref, wr2_ref, br_ref, xo_ref, tok_ref, rt_ref, cnt_ref,
                    *, n_lat_tiles):
    is_lat = pl.program_id(0) < n_lat_tiles
    m = mod_ref[0, 0]
    xs = xbc_ref[:, 0:SSD_INNER]
    y = y_ref[0] + y_ref[1] + dskip_ref[...] * xs
    y = y * _silu(z_ref[...])
    ob = y * lax.rsqrt(jnp.mean(y * y, axis=-1, keepdims=True) + RMS_EPS) * ng_ref[...]
    oa = jnp.where(is_lat, oal_ref[...], oac_ref[...])
    on = jnp.where(is_lat, onl_ref[...], onc_ref[...])
    proj = (_dot(oa, w_ref[0, 0:256, :]) + _dot(ob.astype(BF16), w_ref[0, 256:768, :])
            + _dot(on, w_ref[0, 768:1024, :]))
    x = x_ref[...] + m[2:3] * proj
    xo_ref[...] = x
    t = x * lax.rsqrt(jnp.mean(x * x, axis=-1, keepdims=True) + RMS_EPS) * gf_ref[...]
    t = t * (1.0 + m[4:5]) + m[3:4]
    tok_ref[...] = t
    t1 = t.astype(BF16)
    t2 = (t - t1.astype(F32)).astype(BF16)
    logits = _dot(t1, wr1_ref[...]) + _dot(t1, wr2_ref[...]) + _dot(t2, wr1_ref[...]) + br_ref[...]

    lane = _iota(logits.shape, 1)
    big = jnp.int32(1 << 20)
    is_g = jnp.logical_and(lane >= N_EXPERTS, lane < N_EXPERTS + MOE_GROUPS)
    gl = jnp.where(is_g, logits, NEG_INF)
    gmax = gl.max(axis=-1, keepdims=True)
    g_w = 1.0 / jnp.exp(gl - gmax).sum(axis=-1, keepdims=True)
    g_idx = jnp.where(gl == gmax, lane, big).min(axis=-1, keepdims=True) - N_EXPERTS
    in_grp = jnp.logical_and(lane < N_EXPERTS, (lane >> 3) == g_idx)
    el = jnp.where(in_grp, logits, NEG_INF)
    l1 = el.max(axis=-1, keepdims=True)
    i1 = jnp.where(el == l1, lane, big).min(axis=-1, keepdims=True)
    el2 = jnp.where(lane == i1, NEG_INF, el)
    l2 = el2.max(axis=-1, keepdims=True)
    i2 = jnp.where(el2 == l2, lane, big).min(axis=-1, keepdims=True)
    e2 = jnp.exp(l2 - l1)
    w1 = g_w / (1.0 + e2)
    w2 = w1 * e2
    rt_ref[...] = jnp.where(lane == 0, i1.astype(F32), jnp.where(lane == 1, i2.astype(F32),
                            jnp.where(lane == 2, w1, jnp.where(lane == 3, w2, 0.0))))
    hot = jnp.logical_or(lane == i1, lane == i2)
    cnt_ref[0] = jnp.where(hot, 1.0, 0.0).sum(axis=0, keepdims=True)


def _outproj(xc, oa_l, oa_c, y2, xbc_act, z, on_l, on_c, mod, dskip, norm_g, w_out, g_ffn, wr1, wr2, br,
             n_batch, n_tiles, layer):
    n_lat_tiles = n_batch * SEQ // TILE
    tpb = SEQ // TILE
    row = lambda i: (i, 0)
    lat = lambda i: (jnp.minimum(i, n_lat_tiles - 1), 0)
    ctx = lambda i: (jnp.maximum(i - n_lat_tiles, 0), 0)
    modrow = lambda i: (layer, _tile_mod_row(i, n_lat_tiles, tpb, n_batch), 0, 0)
    const = lambda i: (0, 0)
    kern = functools.partial(_outproj_kernel, n_lat_tiles=n_lat_tiles)
    return pl.pallas_call(
        kern,
        grid=(n_tiles,),
        in_specs=[pl.BlockSpec((TILE, D_MODEL), row),
                  pl.BlockSpec((TILE, 256), lat),
                  pl.BlockSpec((TILE, 256), ctx),
                  pl.BlockSpec((2, TILE, SSD_INNER), lambda i: (0, i, 0)),
                  pl.BlockSpec((TILE, SSD_CONV_DIM), row),
                  pl.BlockSpec((TILE, SSD_INNER), row),
                  pl.BlockSpec((TILE, 256), lat),
                  pl.BlockSpec((TILE, 256), ctx),
                  pl.BlockSpec((1, 1, N_MOD, D_MODEL), modrow),
                  pl.BlockSpec((1, SSD_INNER), const),
                  pl.BlockSpec((1, SSD_INNER), const),
                  pl.BlockSpec((1, D_MODEL, D_MODEL), lambda i: (layer, 0, 0)),
                  pl.BlockSpec((1, D_MODEL), const),
                  pl.BlockSpec((D_MODEL, LANES), const),
                  pl.BlockSpec((D_MODEL, LANES), const),
                  pl.BlockSpec((1, LANES), const)],
        out_specs=[pl.BlockSpec((TILE, D_MODEL), row),
                   pl.BlockSpec((TILE, D_MODEL), row),
                   pl.BlockSpec((TILE, LANES), row),
                   pl.BlockSpec((1, 1, LANES), lambda i: (i, 0, 0))],
        out_shape=[jax.ShapeDtypeStruct((n_tiles * TILE, D_MODEL), F32),
                   jax.ShapeDtypeStruct((n_tiles * TILE, D_MODEL), F32),
                   jax.ShapeDtypeStruct((n_tiles * TILE, LANES), F32),
                   jax.ShapeDtypeStruct((n_tiles, 1, LANES), F32)],
        compiler_params=_cparams(1, 56),
        name="outproj",
    )(xc, oa_l, oa_c, y2, xbc_act, z, on_l, on_c, mod, dskip, norm_g, w_out, g_ffn, wr1, wr2, br)


MOE_TM = 256


def _moe_max_tiles(n_tokens):
    return (2 * n_tokens + N_EXPERTS * (MOE_TM - 1)) // MOE_TM


def _moe_plan(cnt, n_tiles):
    cnt = cnt[:, 0, :N_EXPERTS].astype(jnp.int32)
    tot = cnt.sum(axis=0)
    tiles_e = (tot + MOE_TM - 1) // MOE_TM
    t_end = jnp.cumsum(tiles_e)
    t_start = t_end - tiles_e
    base = (t_start * MOE_TM)[None, :] + jnp.cumsum(cnt, axis=0) - cnt
    n_used = t_end[-1]
    n_max = _moe_max_tiles(n_tiles * TILE)
    te = jnp.sum(jnp.arange(n_max)[:, None] >= t_end[None, :], axis=1)
    te = jnp.minimum(te, jnp.sum((n_used - 1) >= t_end)).astype(jnp.int32)
    tail = jnp.where(tiles_e > 0, t_end - 1, n_max).astype(jnp.int32)
    base_f = jnp.zeros((n_tiles, 1, LANES), F32).at[:, 0, :N_EXPERTS].set(base.astype(F32))
    return base_f, te, n_used.reshape(1).astype(jnp.int32), tail


def _pos_kernel(rt_ref, base_ref, pos_ref):
    rt = rt_ref[...]
    lane = _iota(rt.shape, 1)
    hot1 = lane == rt[:, 0:1].astype(jnp.int32)
    hot2 = lane == rt[:, 1:2].astype(jnp.int32)
    hot = jnp.where(jnp.logical_or(hot1, hot2), 1.0, 0.0).astype(BF16)
    strict = jnp.where(_iota((TILE, TILE), 0) > _iota((TILE, TILE), 1), 1.0, 0.0).astype(BF16)
    slot = base_ref[0] + _dot(strict, hot)
    p1 = jnp.where(hot1, slot, 0.0).sum(axis=-1, keepdims=True)
    p2 = jnp.where(hot2, slot, 0.0).sum(axis=-1, keepdims=True)
    pos_ref[...] = jnp.where(lane == 0, p1, jnp.where(lane == 1, p2, 0.0)).astype(jnp.int32)


def _positions(rt, base, n_tiles):
    return pl.pallas_call(
        _pos_kernel,
        grid=(n_tiles,),
        in_specs=[pl.BlockSpec((TILE, LANES), lambda i: (i, 0)),
                  pl.BlockSpec((1, 1, LANES), lambda i: (i, 0, 0))],
        out_specs=pl.BlockSpec((TILE, LANES), lambda i: (i, 0)),
        out_shape=jax.ShapeDtypeStruct((n_tiles * TILE, LANES), jnp.int32),
        compiler_params=_cparams(1, 32),
        name="moe_positions",
    )(rt, base)


def _dispatch_kernel(tail_ref, nu_ref, pos_ref, tok_ref, xs_ref, zbuf, zsem, sem, *, n_tiles, n_max):
    i = pl.program_id(0)

    def zero_tile(j):
        return pltpu.make_async_copy(zbuf, xs_ref.at[pl.ds(j * MOE_TM, MOE_TM), :], zsem)

    @pl.when(i == 0)
    def _():
        zbuf[...] = jnp.zeros_like(zbuf)
        for e in range(N_EXPERTS):
            @pl.when(tail_ref[e] != n_max)
            def _():
                zero_tile(tail_ref[e]).start()
        lax.fori_loop(nu_ref[0], n_max + 1, lambda j, c: (zero_tile(j).start(), c)[1], 0)
        for e in range(N_EXPERTS):
            @pl.when(tail_ref[e] != n_max)
            def _():
                zero_tile(tail_ref[e]).wait()
        lax.fori_loop(nu_ref[0], n_max + 1, lambda j, c: (zero_tile(j).wait(), c)[1], 0)

    base = i * TILE

    def body(r, carry):
        src = tok_ref.at[pl.ds(base + r, 1), :]
        pltpu.make_async_copy(src, xs_ref.at[pl.ds(pos_ref[0, 0, 2 * r], 1), :], sem).start(priority=0)
        pltpu.make_async_copy(src, xs_ref.at[pl.ds(pos_ref[0, 0, 2 * r + 1], 1), :], sem).start(priority=1)
        return carry

    lax.fori_loop(0, TILE, body, 0, unroll=8)

    @pl.when(i == n_tiles - 1)
    def _():
        for _ in range(2):
            pltpu.make_async_copy(tok_ref, xs_ref.at[pl.ds(0, n_tiles * TILE), :], sem).wait()


def _dispatch(tail, n_used, pos_s, tok, n_tiles, n_max):
    kern = functools.partial(_dispatch_kernel, n_tiles=n_tiles, n_max=n_max)
    return pl.pallas_call(
        kern,
        grid_spec=pltpu.PrefetchScalarGridSpec(
            num_scalar_prefetch=2,
            grid=(n_tiles,),
            in_specs=[pl.BlockSpec((1, 1, 2 * TILE), lambda i, tail, nu: (i, 0, 0), memory_space=pltpu.SMEM),
                      pl.BlockSpec(memory_space=pl.ANY)],
            out_specs=pl.BlockSpec(memory_space=pl.ANY),
            scratch_shapes=[pltpu.VMEM((MOE_TM, D_MODEL), F32), pltpu.SemaphoreType.DMA(()),
                            pltpu.SemaphoreType.DMA(())]),
        out_shape=jax.ShapeDtypeStruct(((n_max + 1) * MOE_TM, D_MODEL), F32),
        compiler_params=_cparams(1, 32),
        name="moe_dispatch",
    )(tail, n_used, pos_s, tok)


def _experts_kernel(te_ref, nu_ref, xs_ref, wg_ref, wu_ref, wd_ref, ys_ref):
    used = pl.program_id(0) < nu_ref[0]

    @pl.when(used)
    def _():
        x = xs_ref[...].astype(BF16)
        gate = _dot(x, wg_ref[0].astype(BF16))
        up = _dot(x, wu_ref[0].astype(BF16))
        hid = (_silu(gate) * up).astype(BF16)
        ys_ref[...] = _dot(hid, wd_ref[0].astype(BF16))

    @pl.when(jnp.logical_not(used))
    def _():
        ys_ref[...] = jnp.zeros_like(ys_ref)


def _experts(te, n_used, xs, w_gate, w_up, w_down, n_max, layer):
    rows = lambda j, te, nu: (jnp.minimum(j, nu[0] - 1), 0)
    wsel = lambda j, te, nu: (layer * N_EXPERTS + te[j], 0, 0)
    return pl.pallas_call(
        _experts_kernel,
        grid_spec=pltpu.PrefetchScalarGridSpec(
            num_scalar_prefetch=2,
            grid=(n_max,),
            in_specs=[pl.BlockSpec((MOE_TM, D_MODEL), rows),
                      pl.BlockSpec((1, D_MODEL, D_EXPERT), wsel),
                      pl.BlockSpec((1, D_MODEL, D_EXPERT), wsel),
                      pl.BlockSpec((1, D_EXPERT, D_MODEL), wsel)],
            out_specs=pl.BlockSpec((MOE_TM, D_MODEL), lambda j, te, nu: (j, 0))),
        out_shape=jax.ShapeDtypeStruct((n_max * MOE_TM, D_MODEL), F32),
        compiler_params=_cparams(1, 40),
        name="moe_experts",
    )(te, n_used, xs, w_gate, w_up, w_down)


def _combine_kernel(pos_ref, ys_ref, x_ref, rt_ref, mod_ref, o_ref, ybuf, sem):
    def body(r, carry):
        pltpu.make_async_copy(ys_ref.at[pl.ds(pos_ref[0, 0, 2 * r], 1), :],
                              ybuf.at[0, pl.ds(r, 1), :], sem).start(priority=0)
        pltpu.make_async_copy(ys_ref.at[pl.ds(pos_ref[0, 0, 2 * r + 1], 1), :],
                              ybuf.at[1, pl.ds(r, 1), :], sem).start(priority=1)
        return carry

    lax.fori_loop(0, TILE, body, 0, unroll=8)
    for k in range(2):
        pltpu.make_async_copy(ys_ref.at[pl.ds(0, TILE), :], ybuf.at[k], sem).wait()
    rt = rt_ref[...]
    f = rt[:, 2:3] * ybuf[0] + rt[:, 3:4] * ybuf[1]
    o_ref[...] = x_ref[...] + mod_ref[0, 0][5:6] * f


def _combine(pos_s, ys, xmid, rt, mod, n_batch, n_tiles, layer):
    n_lat_tiles = n_batch * SEQ // TILE
    tpb = SEQ // TILE
    row = lambda i: (i, 0)
    modrow = lambda i: (layer, _tile_mod_row(i, n_lat_tiles, tpb, n_batch), 0, 0)
    return pl.pallas_call(
        _combine_kernel,
        grid=(n_tiles,),
        in_specs=[pl.BlockSpec((1, 1, 2 * TILE), lambda i: (i, 0, 0), memory_space=pltpu.SMEM),
                  pl.BlockSpec(memory_space=pl.ANY),
                  pl.BlockSpec((TILE, D_MODEL), row),
                  pl.BlockSpec((TILE, LANES), row),
                  pl.BlockSpec((1, 1, N_MOD, D_MODEL), modrow)],
        out_specs=pl.BlockSpec((TILE, D_MODEL), row),
        out_shape=jax.ShapeDtypeStruct((n_tiles * TILE, D_MODEL), F32),
        scratch_shapes=[pltpu.VMEM((2, TILE, D_MODEL), F32), pltpu.SemaphoreType.DMA(())],
        compiler_params=_cparams(1, 40),
        name="moe_combine",
    )(pos_s, ys, xmid, rt, mod)


def _moe(xmid, tok, rt, cnt, mod, w_gate, w_up, w_down, n_batch, n_tiles, layer):
    base, te, n_used, tail = _moe_plan(cnt, n_tiles)
    n_max = _moe_max_tiles(n_tiles * TILE)
    pos = _positions(rt, base, n_tiles)
    pos_s = pos[:, :2].reshape(n_tiles, 1, 2 * TILE)
    xs = _dispatch(tail, n_used, pos_s, tok, n_tiles, n_max)
    ys = _experts(te, n_used, xs, w_gate, w_up, w_down, n_max, layer)
    return _combine(pos_s, ys, xmid, rt, mod, n_batch, n_tiles, layer)


def _final_norm_kernel(x_ref, g_ref, o_ref):
    x = x_ref[...]
    o_ref[...] = x * lax.rsqrt(jnp.mean(x * x, axis=-1, keepdims=True) + RMS_EPS) * g_ref[...]


def _final_norm(xc, g, n_rows):
    return pl.pallas_call(
        _final_norm_kernel,
        grid=(n_rows // TILE,),
        in_specs=[pl.BlockSpec((TILE, D_MODEL), lambda i: (i, 0)),
                  pl.BlockSpec((1, D_MODEL), lambda i: (0, 0))],
        out_specs=pl.BlockSpec((TILE, D_MODEL), lambda i: (i, 0)),
        out_shape=jax.ShapeDtypeStruct((n_rows, D_MODEL), F32),
        compiler_params=_cparams(1, 32),
        name="final_norm",
    )(xc, g)


def _rope_tables():
    t = jnp.arange(SEQ)
    rows_pos = (t // GRID_W).astype(F32)
    cols_pos = (t % GRID_W).astype(F32)
    half = HEAD_DIM // 2
    inv = 1.0 / (ROPE_BASE ** (jnp.arange(0, half, 2, dtype=F32) / half))
    ang_r = rows_pos[:, None] * inv[None, :]
    ang_c = cols_pos[:, None] * inv[None, :]
    ang = jnp.concatenate([ang_r, ang_r, ang_c, ang_c], axis=1)
    cos_h, sin_h = jnp.cos(ang), jnp.sin(ang)
    scale = jnp.concatenate([jnp.full((256,), HEAD_DIM ** -0.5, F32), jnp.ones((128,), F32)])
    cos_t = jnp.tile(cos_h, (1, 6)) * scale
    sin_t = jnp.tile(sin_h, (1, 6)) * scale
    cos_t = jnp.concatenate([cos_t, jnp.broadcast_to(scale, (TILE, 384))], axis=0)
    sin_t = jnp.concatenate([sin_t, jnp.zeros((TILE, 384), F32)], axis=0)
    return cos_t, sin_t


def _fused_in_weight(w_in):
    wq, wk, wv = w_in[..., 0:256], w_in[..., 256:384], w_in[..., 384:512]
    o = A_IN
    wz, wxbc, wdt = w_in[..., o:o + 512], w_in[..., o + 512:o + 1280], w_in[..., o + 1280:o + 1296]
    wna = w_in[..., A_IN + SSD_IN:]

    def rot(w):
        w4 = w.reshape(w.shape[:-1] + (w.shape[-1] // 32, 2, 16))
        return jnp.concatenate([-w4[..., 1:2, :], w4[..., 0:1, :]], axis=-2).reshape(w.shape)

    na_scale = jnp.concatenate([jnp.full((256,), HEAD_DIM ** -0.5, F32), jnp.ones((512,), F32)])
    pad = jnp.zeros(w_in.shape[:-1] + (128 - SSD_HEADS,), F32)
    cat = jnp.concatenate([wq, wk, rot(wq), rot(wk), wv, wz, wxbc,
                           wna * na_scale, wdt[..., :SSD_HEADS], pad, wdt[..., SSD_HEADS:], pad], axis=-1)
    return cat.astype(BF16)


def _router_weight(w_rg, b_rg, w_re, b_re):
    w = jnp.concatenate([w_re, w_rg, jnp.zeros((D_MODEL, LANES - N_EXPERTS - MOE_GROUPS), F32)], axis=1)
    b = jnp.concatenate([b_re, b_rg, jnp.zeros((LANES - N_EXPERTS - MOE_GROUPS,), F32)]).reshape(1, LANES)
    w1 = w.astype(BF16)
    w2 = (w - w1.astype(F32)).astype(BF16)
    return w1, w2, b


def kernel(x, c, ctx, c_ctx, w_mod, b_mod, g_mix, w_in, attn_sink, ssd_conv_w, ssd_conv_b, ssd_dt_bias, ssd_a_log, ssd_d, ssd_norm_g, na_rpb, w_out, g_ffn, w_router_group, b_router_group, w_router_expert, b_router_expert, w_exp_gate, w_exp_up, w_exp_down, g_final):
    n_batch, s, d = x.shape
    assert (s, d) == (SEQ, D_MODEL) and ctx.shape[1:] == (CTX_LEN, D_MODEL) and n_batch < 16
    n_lat = n_batch * SEQ
    n_ctx = n_batch * CTX_LEN
    assert n_ctx % TILE == 0
    n_lat_tiles = n_lat // TILE
    n_all_tiles = (n_lat + n_ctx) // TILE

    xc = jnp.concatenate([x.reshape(n_lat, d), ctx.reshape(n_ctx, d)], axis=0)
    cin = jnp.zeros((16, d), F32).at[:n_batch].set(c).at[n_batch].set(c_ctx)
    mod = _modulation(cin, w_mod, b_mod).reshape(DEPTH, 16, N_MOD, d)
    cos_t, sin_t = _rope_tables()
    w_cat = _fused_in_weight(w_in)
    w_out_b = w_out.astype(BF16)
    bias_t = _na_bias_table(na_rpb)
    g_mix3 = g_mix.reshape(DEPTH, 1, d)
    w_gate = w_exp_gate.reshape(DEPTH * N_EXPERTS, D_MODEL, D_EXPERT)
    w_up = w_exp_up.reshape(DEPTH * N_EXPERTS, D_MODEL, D_EXPERT)
    w_down = w_exp_down.reshape(DEPTH * N_EXPERTS, D_EXPERT, D_MODEL)

    for layer in range(DEPTH):
        need_ctx = layer < DEPTH - 1
        qk, v, z, xbc, na, dt_raw = _inproj(xc, mod, g_mix3, w_cat, cos_t, sin_t, n_batch, layer)
        sink = attn_sink[layer].astype(F32)
        oa = _window_attn(sink, qk, v, n_batch)
        xbc_act = _conv_silu(xbc, jnp.zeros((8, SSD_CONV_DIM), F32).at[:SSD_CONV].set(ssd_conv_w[layer]),
                             ssd_conv_b[layer].reshape(1, SSD_CONV_DIM), n_batch)
        y2 = _ssd_scan(xbc_act, dt_raw, ssd_dt_bias[layer], ssd_a_log[layer], n_batch)
        on = _neighborhood_attn(na, bias_t, n_batch, layer)
        oa_c, on_c = _ctx_attn(sink, qk, v, na, n_batch) if need_ctx else (oa, on)
        n_tiles = n_all_tiles if need_ctx else n_lat_tiles
        wr1, wr2, br = _router_weight(w_router_group[layer], b_router_group[layer],
                                      w_router_expert[layer], b_router_expert[layer])
        dskip = jnp.repeat(ssd_d[layer].astype(F32), SSD_INNER // SSD_HEADS).reshape(1, SSD_INNER)
        xmid, tok, rt, cnt = _outproj(xc, oa, oa_c, y2, xbc_act, z, on, on_c, mod, dskip,
                                      ssd_norm_g[layer].reshape(1, SSD_INNER), w_out_b,
                                      g_ffn[layer].reshape(1, d), wr1, wr2, br, n_batch, n_tiles, layer)
        xc = _moe(xmid, tok, rt, cnt, mod, w_gate, w_up, w_down, n_batch, n_tiles, layer)

    return _final_norm(xc, g_final.reshape(1, d), n_lat).reshape(n_batch, SEQ, d)
```

```python
import functools
import math

import jax
import jax.numpy as jnp
import numpy as np
from jax import lax
from jax.experimental import pallas as pl
from jax.experimental.pallas import tpu as pltpu

F32 = jnp.float32
BF16 = jnp.bfloat16

D_MODEL = 1024
SEQ = 2048
DEPTH = 4
GRID_W = 64
GRID_ROWS = SEQ // GRID_W
CTX_LEN = 256
HEAD_DIM = 64
A_HEADS = 4
A_KV_HEADS = 2
A_BLOCK = 128
ROPE_BASE = 10000.0
SSD_HEADS = 8
SSD_INNER = 512
SSD_STATE = 64
SSD_CONV = 5
SSD_CHUNK = 128
SSD_CONV_DIM = 768
NA_HEADS = 4
NA_WIN_ROWS = 8
NA_WIN_COLS = 16
A_IN = 512
SSD_IN = 1296
MOE_GROUPS = 4
MOE_EXPERTS = 8
N_EXPERTS = MOE_GROUPS * MOE_EXPERTS
D_EXPERT = 256
N_MOD = 6
RMS_EPS = 1e-6
NEG_INF = -1e30

TILE = 512
LANES = 128
C_QK, C_QKP, C_V, C_Z, C_XBC, C_NA, C_DT = 0, 384, 768, 896, 1408, 2176, 2944
N_COLS = 3200
NA_Q_ROWS = 2
NA_K_ROWS = 10
NA_BIAS_OFF = 2
NA_BIAS_N = 18


def _cparams(n_axes, vmem_mb):
    return pltpu.CompilerParams(dimension_semantics=("arbitrary",) * n_axes,
                                vmem_limit_bytes=vmem_mb << 20)


def _split3(x):
    h1 = x.astype(BF16)
    r1 = x - h1.astype(F32)
    h2 = r1.astype(BF16)
    h3 = (r1 - h2.astype(F32)).astype(BF16)
    return h1, h2, h3


def _dot(a, b):
    return jnp.dot(a, b, preferred_element_type=F32)


def _dot_nt(a, b):
    return lax.dot_general(a, b, (((1,), (1,)), ((), ())), preferred_element_type=F32)


def _dot_exact_lhs(lhs_bf16, x):
    h1, h2, h3 = _split3(x)
    return _dot(lhs_bf16, h1) + _dot(lhs_bf16, h2) + _dot(lhs_bf16, h3)


def _dot_exact_rhs(x, rhs_bf16):
    h1, h2, h3 = _split3(x)
    return _dot(h1, rhs_bf16) + _dot(h2, rhs_bf16) + _dot(h3, rhs_bf16)


def _silu(x):
    return x * jax.nn.sigmoid(x)


def _iota(shape, dim):
    return lax.broadcasted_iota(jnp.int32, shape, dim)


def _mod_kernel(c_ref, w_ref, b_ref, o_ref):
    a = _silu(c_ref[...])
    a1, a2, _ = _split3(a)
    w = w_ref[0]
    w1 = w.astype(BF16)
    w2 = (w - w1.astype(F32)).astype(BF16)
    o_ref[0] = _dot(a1, w1) + _dot(a1, w2) + _dot(a2, w1) + b_ref[0]


def _modulation(cin, w_mod, b_mod):
    nt = 1024
    return pl.pallas_call(
        _mod_kernel,
        grid=(DEPTH, N_MOD * D_MODEL // nt),
        in_specs=[pl.BlockSpec((16, D_MODEL), lambda l, j: (0, 0)),
                  pl.BlockSpec((1, D_MODEL, nt), lambda l, j: (l, 0, j)),
                  pl.BlockSpec((1, 1, nt), lambda l, j: (l, 0, j))],
        out_specs=pl.BlockSpec((1, 16, nt), lambda l, j: (l, 0, j)),
        out_shape=jax.ShapeDtypeStruct((DEPTH, 16, N_MOD * D_MODEL), F32),
        compiler_params=_cparams(2, 40),
        name="modulation",
    )(cin, w_mod, b_mod.reshape(DEPTH, 1, N_MOD * D_MODEL))


def _inproj_kernel(x_ref, mod_ref, g_ref, w_ref, cos_ref, sin_ref,
                   qk_ref, v_ref, z_ref, xbc_ref, na_ref, dt_ref):
    x = x_ref[...]
    m = mod_ref[0, 0]
    h = x * lax.rsqrt(jnp.mean(x * x, axis=-1, keepdims=True) + RMS_EPS) * g_ref[0]
    h = h * (1.0 + m[1:2]) + m[0:1]
    hb = h.astype(BF16)

    def mm(lo, hi):
        return _dot(hb, w_ref[0, :, lo:hi])

    qk = mm(C_QK, C_QKP) * cos_ref[...] + mm(C_QKP, C_V) * sin_ref[...]
    qk_ref[...] = qk.astype(BF16)
    v_ref[...] = mm(C_V, C_Z).astype(BF16)
    z_ref[...] = mm(C_Z, C_XBC)
    xbc_ref[...] = mm(C_XBC, C_NA)
    na_ref[...] = mm(C_NA, C_DT).astype(BF16)
    dt_ref[...] = mm(C_DT, N_COLS)


def _tile_mod_row(i, n_lat_tiles, tiles_per_batch, n_batch):
    return jnp.where(i < n_lat_tiles, i // tiles_per_batch, n_batch)


def _inproj(xc, mod, g_mix, w_cat, cos_t, sin_t, n_batch, layer):
    nt = xc.shape[0]
    n_lat_tiles = n_batch * SEQ // TILE
    tpb = SEQ // TILE
    row = lambda i: (i, 0)
    modrow = lambda i: (layer, _tile_mod_row(i, n_lat_tiles, tpb, n_batch), 0, 0)
    posrow = lambda i: (jnp.where(i < n_lat_tiles, i % tpb, tpb), 0)
    lay = lambda i: (layer, 0, 0)
    outs = [(384, BF16), (128, BF16), (512, F32), (768, F32), (768, BF16), (256, F32)]
    return pl.pallas_call(
        _inproj_kernel,
        grid=(nt // TILE,),
        in_specs=[pl.BlockSpec((TILE, D_MODEL), row),
                  pl.BlockSpec((1, 1, N_MOD, D_MODEL), modrow),
                  pl.BlockSpec((1, 1, D_MODEL), lay),
                  pl.BlockSpec((1, D_MODEL, N_COLS), lay),
                  pl.BlockSpec((TILE, 384), posrow),
                  pl.BlockSpec((TILE, 384), posrow)],
        out_specs=[pl.BlockSpec((TILE, w), row) for w, _ in outs],
        out_shape=[jax.ShapeDtypeStruct((nt, w), dt) for w, dt in outs],
        compiler_params=_cparams(1, 56),
        name="inproj",
    )(xc, mod, g_mix, w_cat, cos_t, sin_t)


def _conv_kernel(prev_ref, x_ref, next_ref, w_ref, b_ref, o_ref, *, blocks_per_seq, n_lat_blocks):
    i = pl.program_id(0)
    is_lat = i < n_lat_blocks
    first = jnp.logical_or(jnp.logical_not(is_lat), i % blocks_per_seq == 0)
    last = jnp.logical_or(jnp.logical_not(is_lat), i % blocks_per_seq == blocks_per_seq - 1)
    prev = jnp.where(first, 0.0, prev_ref[...])
    nxt = jnp.where(last, 0.0, next_ref[...])
    xw = jnp.concatenate([prev, x_ref[...], nxt], axis=0)
    rows = x_ref.shape[0]
    w = w_ref[...]
    acc = jnp.zeros(x_ref.shape, F32) + b_ref[...]
    for k in range(SSD_CONV):
        off = 8 - SSD_CONV // 2 + k
        acc = acc + xw[off:off + rows, :] * w[k:k + 1, :]
    o_ref[...] = _silu(acc)


def _conv_silu(xbc, conv_w, conv_b, n_batch):
    nt = xbc.shape[0]
    rows = 256
    bps = SEQ // rows
    n_lat_blocks = n_batch * bps
    nblk = nt // rows
    r8 = rows // 8
    kern = functools.partial(_conv_kernel, blocks_per_seq=bps, n_lat_blocks=n_lat_blocks)
    return pl.pallas_call(
        kern,
        grid=(nblk,),
        in_specs=[pl.BlockSpec((8, SSD_CONV_DIM), lambda i: (jnp.maximum(i * r8 - 1, 0), 0)),
                  pl.BlockSpec((rows, SSD_CONV_DIM), lambda i: (i, 0)),
                  pl.BlockSpec((8, SSD_CONV_DIM), lambda i: (jnp.minimum(i * r8 + r8, nt // 8 - 1), 0)),
                  pl.BlockSpec((8, SSD_CONV_DIM), lambda i: (0, 0)),
                  pl.BlockSpec((1, SSD_CONV_DIM), lambda i: (0, 0))],
        out_specs=pl.BlockSpec((rows, SSD_CONV_DIM), lambda i: (i, 0)),
        out_shape=jax.ShapeDtypeStruct((nt, SSD_CONV_DIM), F32),
        compiler_params=_cparams(1, 32),
        name="ssd_conv",
    )(xbc, xbc, xbc, conv_w, conv_b)


def _ssd_kernel(xbc_ref, dt_ref, dtb_ref, alog_ref, e512_ref, e1024_ref, y_ref, st_ref):
    d = pl.program_id(1)
    c = pl.program_id(2)

    @pl.when(c == 0)
    def _():
        st_ref[...] = jnp.zeros_like(st_ref)

    q = SSD_CHUNK
    xbc = xbc_ref[...]
    xs = xbc[:, :SSD_INNER]
    bm = xbc[:, SSD_INNER:SSD_INNER + 128]
    cm = xbc[:, SSD_INNER + 128:]
    dtr = dt_ref[...] + dtb_ref[0]
    dt = jnp.maximum(dtr, 0.0) + jnp.log1p(jnp.exp(-jnp.abs(dtr)))
    a = -jnp.exp(alog_ref[0])
    da = dt * a

    ri = _iota((q, q), 0)
    ci = _iota((q, q), 1)
    sgn = jnp.where(d == 0, 1, -1)
    tri = (ri - ci) * sgn >= 0
    trib = jnp.where(tri, 1.0, 0.0).astype(BF16)
    acs = _dot_exact_lhs(trib, da)
    acs_t = acs.T

    e512 = e512_ref[...]
    dt_e = _dot_exact_rhs(dt, e512)
    acs_e = _dot_exact_rhs(acs, e512)
    acs_e2 = _dot_exact_rhs(acs, e1024_ref[...])
    tot_e = jnp.where(d == 0, acs_e[q - 1:q, :], acs_e[0:1, :])

    xdt = xs * dt_e
    xdec = (xdt * jnp.exp(tot_e - acs_e)).astype(BF16)
    btb = bm.T.astype(BF16)
    lane = _iota((q, 128), 1)
    cm0 = jnp.where(lane < SSD_STATE, cm, 0.0).astype(BF16)
    cm1 = jnp.where(lane >= SSD_STATE, cm, 0.0).astype(BF16)
    cbs = (_dot(cm0, btb), _dot(cm1, btb))

    st = st_ref[...]
    y_off = _dot(cm.astype(BF16), st.astype(BF16)) * jnp.exp(acs_e)
    s_all = _dot(btb, xdec)
    same = (_iota((q, SSD_INNER), 0) >> 6) == (_iota((q, SSD_INNER), 1) >> 8)
    st_ref[...] = jnp.where(same, st * jnp.exp(tot_e) + s_all, 0.0)

    for pair in range(SSD_HEADS // 2):
        cb = cbs[pair // 2]
        xp = xdt[:, pair * 128:(pair + 1) * 128]
        acc = None
        for k in range(2):
            h = 2 * pair + k
            seg = acs_e2[:, h * 128:(h + 1) * 128] - acs_t[h:h + 1, :]
            lmat = jnp.exp(jnp.where(tri, seg, NEG_INF))
            g = (cb * lmat).astype(BF16)
            rhs = jnp.where((lane < 64) if k == 0 else (lane >= 64), xp, 0.0).astype(BF16)
            t = _dot(g, rhs)
            acc = t if acc is None else acc + t
        y_ref[0, :, pair * 128:(pair + 1) * 128] = acc + y_off[:, pair * 128:(pair + 1) * 128]


def _ssd_scan(xbc_act, dt_raw, dt_bias, a_log, n_batch):
    nt = xbc_act.shape[0]
    n_lat_blk = n_batch * (SEQ // SSD_CHUNK)
    lat_c = SEQ // SSD_CHUNK
    ctx_c = CTX_LEN // SSD_CHUNK
    n_steps = lat_c + ctx_c

    def blk(b, d, c):
        cc = jnp.where(d == 0, c, ctx_c - 1 - c)
        lc = jnp.where(d == 0, c - ctx_c, n_steps - 1 - c)
        return jnp.where(c < ctx_c, n_lat_blk + b * ctx_c + cc, b * lat_c + lc)

    heads = np.arange(128)
    e512 = (heads[:, None] == (np.arange(512)[None, :] // 64)).astype(np.float32)
    e1024 = (heads[:, None] == (np.arange(1024)[None, :] // 128)).astype(np.float32)
    dtb = jnp.zeros((2, 1, 128), F32).at[:, 0, :SSD_HEADS].set(dt_bias)
    alog = jnp.zeros((2, 1, 128), F32).at[:, 0, :SSD_HEADS].set(a_log)
    return pl.pallas_call(
        _ssd_kernel,
        grid=(n_batch, 2, n_steps),
        in_specs=[pl.BlockSpec((SSD_CHUNK, SSD_CONV_DIM), lambda b, d, c: (blk(b, d, c), 0)),
                  pl.BlockSpec((SSD_CHUNK, 128), lambda b, d, c: (blk(b, d, c), d)),
                  pl.BlockSpec((1, 1, 128), lambda b, d, c: (d, 0, 0)),
                  pl.BlockSpec((1, 1, 128), lambda b, d, c: (d, 0, 0)),
                  pl.BlockSpec((128, 512), lambda b, d, c: (0, 0)),
                  pl.BlockSpec((128, 1024), lambda b, d, c: (0, 0))],
        out_specs=pl.BlockSpec((1, SSD_CHUNK, SSD_INNER), lambda b, d, c: (d, blk(b, d, c), 0)),
        out_shape=jax.ShapeDtypeStruct((2, nt, SSD_INNER), F32),
        scratch_shapes=[pltpu.VMEM((128, SSD_INNER), F32)],
        compiler_params=_cparams(3, 32),
        name="ssd_scan",
    )(xbc_act, dt_raw, dtb, alog, jnp.asarray(e512, BF16), jnp.asarray(e1024, BF16))


def _softmax_pv(s_list, v_list, extra_logit=None):
    m = s_list[0].max(axis=-1, keepdims=True)
    for s in s_list[1:]:
        m = jnp.maximum(m, s.max(axis=-1, keepdims=True))
    if extra_logit is not None:
        m = jnp.maximum(m, extra_logit)
    den = None
    o = None
    for s, v in zip(s_list, v_list):
        p = jnp.exp(s - m)
        ps = p.sum(axis=-1, keepdims=True)
        den = ps if den is None else den + ps
        t = _dot(p.astype(BF16), v)
        o = t if o is None else o + t
    if extra_logit is not None:
        den = den + jnp.exp(extra_logit - m)
    return o / den


def _wattn_kernel(sink_ref, q_ref, k_ref, v_ref, kc_ref, vc_ref, o_ref):
    n = pl.program_id(1)
    nb = SEQ // A_BLOCK
    start = pl.multiple_of(jnp.clip(n - 1, 0, nb - 3) * A_BLOCK, A_BLOCK)
    q = q_ref[:, 0:256]
    kw = k_ref[pl.ds(start, 3 * A_BLOCK), 256:384]
    vw = v_ref[pl.ds(start, 3 * A_BLOCK), :]
    kc = kc_ref[:, 256:384]
    vc = vc_ref[...]
    qpos = n * A_BLOCK + (_iota((2 * A_BLOCK, 3 * A_BLOCK), 0) & (A_BLOCK - 1))
    kpos = start + _iota((2 * A_BLOCK, 3 * A_BLOCK), 1)
    valid = jnp.abs(qpos - kpos) <= A_BLOCK
    top = _iota((2 * A_BLOCK, 1), 0) < A_BLOCK
    outs = []
    for g in range(A_KV_HEADS):
        qg = jnp.concatenate([q[:, (2 * g) * 64:(2 * g + 1) * 64],
                              q[:, (2 * g + 1) * 64:(2 * g + 2) * 64]], axis=0)
        kg = kw[:, g * 64:(g + 1) * 64]
        vg = vw[:, g * 64:(g + 1) * 64]
        s_loc = jnp.where(valid, _dot_nt(qg, kg), NEG_INF)
        s_ctx = _dot_nt(qg, kc[:, g * 64:(g + 1) * 64])
        sink = jnp.where(top, sink_ref[2 * g], sink_ref[2 * g + 1])
        o = _softmax_pv([s_loc, s_ctx], [vg, vc[:, g * 64:(g + 1) * 64]], sink)
        outs += [o[:A_BLOCK], o[A_BLOCK:]]
    o_ref[...] = jnp.concatenate(outs, axis=1).astype(BF16)


def _window_attn(sink, qk, v, n_batch):
    nb = SEQ // A_BLOCK
    ctx0 = n_batch * SEQ // CTX_LEN
    return pl.pallas_call(
        _wattn_kernel,
        grid=(n_batch, nb),
        in_specs=[pl.BlockSpec(memory_space=pltpu.SMEM),
                  pl.BlockSpec((A_BLOCK, 384), lambda b, n: (b * nb + n, 0)),
                  pl.BlockSpec((SEQ, 384), lambda b, n: (b, 0)),
                  pl.BlockSpec((SEQ, 128), lambda b, n: (b, 0)),
                  pl.BlockSpec((CTX_LEN, 384), lambda b, n: (ctx0 + b, 0)),
                  pl.BlockSpec((CTX_LEN, 128), lambda b, n: (ctx0 + b, 0))],
        out_specs=pl.BlockSpec((A_BLOCK, 256), lambda b, n: (b * nb + n, 0)),
        out_shape=jax.ShapeDtypeStruct((n_batch * SEQ, 256), BF16),
        compiler_params=_cparams(2, 32),
        name="window_attn",
    )(sink, qk, qk, v, qk, v)


def _nattn_kernel(q_ref, kv_ref, c_ref, bias_ref, o_ref):
    i = pl.program_id(1)
    r0 = i * NA_Q_ROWS
    srow = jnp.clip(r0 - NA_WIN_ROWS // 2, 0, GRID_ROWS - NA_K_ROWS)
    start = pl.multiple_of(srow * GRID_W, GRID_W)
    nq = NA_Q_ROWS * GRID_W
    nk = NA_K_ROWS * GRID_W
    q = q_ref[:, 0:256]
    kw = kv_ref[pl.ds(start, nk), 256:512]
    vw = kv_ref[pl.ds(start, nk), 512:768]
    kc = c_ref[:, 256:512]
    vc = c_ref[:, 512:768]
    qrow = r0 + (_iota((nq, nk), 0) >> 6)
    krow = srow + (_iota((nq, nk), 1) >> 6)
    rs = jnp.clip(qrow - NA_WIN_ROWS // 2, 0, GRID_ROWS - NA_WIN_ROWS)
    valid = jnp.logical_and(krow >= rs, krow < rs + NA_WIN_ROWS)
    outs = []
    for h in range(NA_HEADS):
        sl = slice(h * 64, (h + 1) * 64)
        rows = []
        for qi in range(NA_Q_ROWS):
            blocks = []
            for p in range(NA_K_ROWS // 2):
                idx = srow + 2 * p - (r0 + qi) + (NA_WIN_ROWS - 1) + NA_BIAS_OFF
                blocks.append(bias_ref[0, h, idx])
            rows.append(jnp.concatenate(blocks, axis=1))
        bias = jnp.concatenate(rows, axis=0)
        s_loc = jnp.where(valid, _dot_nt(q[:, sl], kw[:, sl]) + bias, NEG_INF)
        s_ctx = _dot_nt(q[:, sl], kc[:, sl])
        outs.append(_softmax_pv([s_loc, s_ctx], [vw[:, sl], vc[:, sl]]))
    o_ref[...] = jnp.concatenate(outs, axis=1).astype(BF16)


def _na_bias_table(rpb):
    cq = np.arange(GRID_W)
    kcol = np.arange(GRID_W)
    cs = np.clip(cq - NA_WIN_COLS // 2, 0, GRID_W - NA_WIN_COLS)
    col_valid = (kcol[None, :] >= cs[:, None]) & (kcol[None, :] < cs[:, None] + NA_WIN_COLS)
    coff = np.clip(kcol[None, :] - cq[:, None], -(NA_WIN_COLS - 1), NA_WIN_COLS - 1) + (NA_WIN_COLS - 1)
    n_a = 2 * NA_WIN_ROWS - 1
    n_c = 2 * NA_WIN_COLS - 1
    pick = (np.arange(n_c)[:, None] == coff.reshape(1, -1)).astype(np.float32)
    tm = jnp.einsum("lhak,kn->lhan", rpb.astype(F32), jnp.asarray(pick), precision=lax.Precision.HIGHEST)
    tm = jnp.where(col_valid, tm.reshape(rpb.shape[:3] + (GRID_W, GRID_W)), NEG_INF)
    neg = jnp.full(rpb.shape[:2] + (1, GRID_W, GRID_W), NEG_INF, F32)
    pad_lo = NA_BIAS_OFF
    pad_hi = NA_BIAS_N + 1 - pad_lo - n_a
    ext = jnp.concatenate([neg] * pad_lo + [tm] + [neg] * pad_hi, axis=2)
    return jnp.concatenate([ext[:, :, :NA_BIAS_N], ext[:, :, 1:NA_BIAS_N + 1]], axis=-1)


def _neighborhood_attn(na, bias_t, n_batch, layer):
    steps = GRID_ROWS // NA_Q_ROWS
    nq = NA_Q_ROWS * GRID_W
    ctx0 = n_batch * SEQ // CTX_LEN
    return pl.pallas_call(
        _nattn_kernel,
        grid=(n_batch, steps),
        in_specs=[pl.BlockSpec((nq, 768), lambda b, i: (b * steps + i, 0)),
                  pl.BlockSpec((SEQ, 768), lambda b, i: (b, 0)),
                  pl.BlockSpec((CTX_LEN, 768), lambda b, i: (ctx0 + b, 0)),
                  pl.BlockSpec((1, NA_HEADS, NA_BIAS_N, GRID_W, 128), lambda b, i: (layer, 0, 0, 0, 0))],
        out_specs=pl.BlockSpec((nq, 256), lambda b, i: (b * steps + i, 0)),
        out_shape=jax.ShapeDtypeStruct((n_batch * SEQ, 256), BF16),
        compiler_params=_cparams(2, 40),
        name="neighborhood_attn",
    )(na, na, na, bias_t)


def _ctx_attn_kernel(sink_ref, qk_ref, v_ref, na_ref, oa_ref, on_ref):
    qk = qk_ref[...]
    v = v_ref[...]
    na = na_ref[...]
    outs = []
    for h in range(A_HEADS):
        g = h // (A_HEADS // A_KV_HEADS)
        s = _dot_nt(qk[:, h * 64:(h + 1) * 64], qk[:, 256 + g * 64:256 + (g + 1) * 64])
        sink = jnp.zeros((CTX_LEN, 1), F32) + sink_ref[h]
        outs.append(_softmax_pv([s], [v[:, g * 64:(g + 1) * 64]], sink))
    oa_ref[...] = jnp.concatenate(outs, axis=1).astype(BF16)
    outs = []
    for h in range(NA_HEADS):
        sl = slice(h * 64, (h + 1) * 64)
        s = _dot_nt(na[:, 0:256][:, sl], na[:, 256:512][:, sl])
        outs.append(_softmax_pv([s], [na[:, 512:768][:, sl]]))
    on_ref[...] = jnp.concatenate(outs, axis=1).astype(BF16)


def _ctx_attn(sink, qk, v, na, n_batch):
    ctx0 = n_batch * SEQ // CTX_LEN
    row = lambda b: (ctx0 + b, 0)
    return pl.pallas_call(
        _ctx_attn_kernel,
        grid=(n_batch,),
        in_specs=[pl.BlockSpec(memory_space=pltpu.SMEM),
                  pl.BlockSpec((CTX_LEN, 384), row),
                  pl.BlockSpec((CTX_LEN, 128), row),
                  pl.BlockSpec((CTX_LEN, 768), row)],
        out_specs=[pl.BlockSpec((CTX_LEN, 256), lambda b: (b, 0))] * 2,
        out_shape=[jax.ShapeDtypeStruct((n_batch * CTX_LEN, 256), BF16)] * 2,
        compiler_params=_cparams(1, 32),
        name="ctx_attn",
    )(sink, qk, v, na)


def _outproj_kernel(x_ref, oal_ref, oac_ref, y_ref, xbc_ref, z_ref, onl_ref, onc_ref, mod_ref, dskip_ref,
                    ng_ref, w_ref, gf_ref, wr1_ref, wr2_ref, br_ref, xo_ref, tok_ref, rt_ref, cnt_ref,
                    *, n_lat_tiles):
    is_lat = pl.program_id(0) < n_lat_tiles
    m = mod_ref[0, 0]
    xs = xbc_ref[:, 0:SSD_INNER]
    y = y_ref[0] + y_ref[1] + dskip_ref[...] * xs
    y = y * _silu(z_ref[...])
    ob = y * lax.rsqrt(jnp.mean(y * y, axis=-1, keepdims=True) + RMS_EPS) * ng_ref[...]
    oa = jnp.where(is_lat, oal_ref[...], oac_ref[...])
    on = jnp.where(is_lat, onl_ref[...], onc_ref[...])
    proj = (_dot(oa, w_ref[0, 0:256, :]) + _dot(ob.astype(BF16), w_ref[0, 256:768, :])
            + _dot(on, w_ref[0, 768:1024, :]))
    x = x_ref[...] + m[2:3] * proj
    xo_ref[...] = x
    t = x * lax.rsqrt(jnp.mean(x * x, axis=-1, keepdims=True) + RMS_EPS) * gf_ref[...]
    t = t * (1.0 + m[4:5]) + m[3:4]
    tok_ref[...] = t
    t1 = t.astype(BF16)
    t2 = (t - t1.astype(F32)).astype(BF16)
    logits = _dot(t1, wr1_ref[...]) + _dot(t1, wr2_ref[...]) + _dot(t2, wr1_ref[...]) + br_ref[...]

    lane = _iota(logits.shape, 1)
    big = jnp.int32(1 << 20)
    is_g = jnp.logical_and(lane >= N_EXPERTS, lane < N_EXPERTS + MOE_GROUPS)
    gl = jnp.where(is_g, logits, NEG_INF)
    gmax = gl.max(axis=-1, keepdims=True)
    g_w = 1.0 / jnp.exp(gl - gmax).sum(axis=-1, keepdims=True)
    g_idx = jnp.where(gl == gmax, lane, big).min(axis=-1, keepdims=True) - N_EXPERTS
    in_grp = jnp.logical_and(lane < N_EXPERTS, (lane >> 3) == g_idx)
    el = jnp.where(in_grp, logits, NEG_INF)
    l1 = el.max(axis=-1, keepdims=True)
    i1 = jnp.where(el == l1, lane, big).min(axis=-1, keepdims=True)
    el2 = jnp.where(lane == i1, NEG_INF, el)
    l2 = el2.max(axis=-1, keepdims=True)
    i2 = jnp.where(el2 == l2, lane, big).min(axis=-1, keepdims=True)
    e2 = jnp.exp(l2 - l1)
    w1 = g_w / (1.0 + e2)
    w2 = w1 * e2
    rt_ref[...] = jnp.where(lane == 0, i1.astype(F32), jnp.where(lane == 1, i2.astype(F32),
                            jnp.where(lane == 2, w1, jnp.where(lane == 3, w2, 0.0))))
    hot = jnp.logical_or(lane == i1, lane == i2)
    cnt_ref[0] = jnp.where(hot, 1.0, 0.0).sum(axis=0, keepdims=True)


def _outproj(xc, oa_l, oa_c, y2, xbc_act, z, on_l, on_c, mod, dskip, norm_g, w_out, g_ffn, wr1, wr2, br,
             n_batch, n_tiles, layer):
    n_lat_tiles = n_batch * SEQ // TILE
    tpb = SEQ // TILE
    row = lambda i: (i, 0)
    lat = lambda i: (jnp.minimum(i, n_lat_tiles - 1), 0)
    ctx = lambda i: (jnp.maximum(i - n_lat_tiles, 0), 0)
    modrow = lambda i: (layer, _tile_mod_row(i, n_lat_tiles, tpb, n_batch), 0, 0)
    const = lambda i: (0, 0)
    kern = functools.partial(_outproj_kernel, n_lat_tiles=n_lat_tiles)
    return pl.pallas_call(
        kern,
        grid=(n_tiles,),
        in_specs=[pl.BlockSpec((TILE, D_MODEL), row),
                  pl.BlockSpec((TILE, 256), lat),
                  pl.BlockSpec((TILE, 256), ctx),
                  pl.BlockSpec((2, TILE, SSD_INNER), lambda i: (0, i, 0)),
                  pl.BlockSpec((TILE, SSD_CONV_DIM), row),
                  pl.BlockSpec((TILE, SSD_INNER), row),
                  pl.BlockSpec((TILE, 256), lat),
                  pl.BlockSpec((TILE, 256), ctx),
                  pl.BlockSpec((1, 1, N_MOD, D_MODEL), modrow),
                  pl.BlockSpec((1, SSD_INNER), const),
                  pl.BlockSpec((1, SSD_INNER), const),
                  pl.BlockSpec((1, D_MODEL, D_MODEL), lambda i: (layer, 0, 0)),
                  pl.BlockSpec((1, D_MODEL), const),
                  pl.BlockSpec((D_MODEL, LANES), const),
                  pl.BlockSpec((D_MODEL, LANES), const),
                  pl.BlockSpec((1, LANES), const)],
        out_specs=[pl.BlockSpec((TILE, D_MODEL), row),
                   pl.BlockSpec((TILE, D_MODEL), row),
                   pl.BlockSpec((TILE, LANES), row),
                   pl.BlockSpec((1, 1, LANES), lambda i: (i, 0, 0))],
        out_shape=[jax.ShapeDtypeStruct((n_tiles * TILE, D_MODEL), F32),
                   jax.ShapeDtypeStruct((n_tiles * TILE, D_MODEL), F32),
                   jax.ShapeDtypeStruct((n_tiles * TILE, LANES), F32),
                   jax.ShapeDtypeStruct((n_tiles, 1, LANES), F32)],
        compiler_params=_cparams(1, 56),
        name="outproj",
    )(xc, oa_l, oa_c, y2, xbc_act, z, on_l, on_c, mod, dskip, norm_g, w_out, g_ffn, wr1, wr2, br)


MOE_TM = 256


def _moe_max_tiles(n_tokens):
    return (2 * n_tokens + N_EXPERTS * (MOE_TM - 1)) // MOE_TM


def _moe_plan(cnt, n_tiles):
    cnt = cnt[:, 0, :N_EXPERTS].astype(jnp.int32)
    tot = cnt.sum(axis=0)
    tiles_e = (tot + MOE_TM - 1) // MOE_TM
    t_end = jnp.cumsum(tiles_e)
    t_start = t_end - tiles_e
    base = (t_start * MOE_TM)[None, :] + jnp.cumsum(cnt, axis=0) - cnt
    n_used = t_end[-1]
    n_max = _moe_max_tiles(n_tiles * TILE)
    te = jnp.sum(jnp.arange(n_max)[:, None] >= t_end[None, :], axis=1)
    te = jnp.minimum(te, jnp.sum((n_used - 1) >= t_end)).astype(jnp.int32)
    tail = jnp.where(tiles_e > 0, t_end - 1, n_max).astype(jnp.int32)
    base_f = jnp.zeros((n_tiles, 1, LANES), F32).at[:, 0, :N_EXPERTS].set(base.astype(F32))
    return base_f, te, n_used.reshape(1).astype(jnp.int32), tail


def _pos_kernel(rt_ref, base_ref, pos_ref):
    rt = rt_ref[...]
    lane = _iota(rt.shape, 1)
    hot1 = lane == rt[:, 0:1].astype(jnp.int32)
    hot2 = lane == rt[:, 1:2].astype(jnp.int32)
    hot = jnp.where(jnp.logical_or(hot1, hot2), 1.0, 0.0).astype(BF16)
    strict = jnp.where(_iota((TILE, TILE), 0) > _iota((TILE, TILE), 1), 1.0, 0.0).astype(BF16)
    slot = base_ref[0] + _dot(strict, hot)
    p1 = jnp.where(hot1, slot, 0.0).sum(axis=-1, keepdims=True)
    p2 = jnp.where(hot2, slot, 0.0).sum(axis=-1, keepdims=True)
    pos_ref[...] = jnp.where(lane == 0, p1, jnp.where(lane == 1, p2, 0.0)).astype(jnp.int32)


def _positions(rt, base, n_tiles):
    return pl.pallas_call(
        _pos_kernel,
        grid=(n_tiles,),
        in_specs=[pl.BlockSpec((TILE, LANES), lambda i: (i, 0)),
                  pl.BlockSpec((1, 1, LANES), lambda i: (i, 0, 0))],
        out_specs=pl.BlockSpec((TILE, LANES), lambda i: (i, 0)),
        out_shape=jax.ShapeDtypeStruct((n_tiles * TILE, LANES), jnp.int32),
        compiler_params=_cparams(1, 32),
        name="moe_positions",
    )(rt, base)


def _dispatch_kernel(tail_ref, nu_ref, pos_ref, tok_ref, xs_ref, zbuf, zsem, sem, *, n_max):
    i = pl.program_id(0)

    def zero_tile(j):
        return pltpu.make_async_copy(zbuf, xs_ref.at[pl.ds(j * MOE_TM, MOE_TM), :], zsem)

    @pl.when(i == 0)
    def _():
        zbuf[...] = jnp.zeros_like(zbuf)
        for e in range(N_EXPERTS):
            @pl.when(tail_ref[e] != n_max)
            def _():
                zero_tile(tail_ref[e]).start()
        lax.fori_loop(nu_ref[0], n_max + 1, lambda j, c: (zero_tile(j).start(), c)[1], 0)
        for e in range(N_EXPERTS):
            @pl.when(tail_ref[e] != n_max)
            def _():
                zero_tile(tail_ref[e]).wait()
        lax.fori_loop(nu_ref[0], n_max + 1, lambda j, c: (zero_tile(j).wait(), c)[1], 0)

    def body(r, carry):
        src = tok_ref.at[pl.ds(r, 1), :]
        pltpu.make_async_copy(src, xs_ref.at[pl.ds(pos_ref[0, 0, 2 * r], 1), :], sem).start(priority=0)
        pltpu.make_async_copy(src, xs_ref.at[pl.ds(pos_ref[0, 0, 2 * r + 1], 1), :], sem).start(priority=1)
        return carry

    lax.fori_loop(0, TILE, body, 0, unroll=8)
    for _ in range(2):
        pltpu.make_async_copy(tok_ref, xs_ref.at[pl.ds(0, TILE), :], sem).wait()


def _dispatch(tail, n_used, pos_s, tok, n_tiles, n_max):
    kern = functools.partial(_dispatch_kernel, n_max=n_max)
    return pl.pallas_call(
        kern,
        grid_spec=pltpu.PrefetchScalarGridSpec(
            num_scalar_prefetch=2,
            grid=(n_tiles,),
            in_specs=[pl.BlockSpec((1, 1, 2 * TILE), lambda i, tail, nu: (i, 0, 0), memory_space=pltpu.SMEM),
                      pl.BlockSpec((TILE, D_MODEL), lambda i, tail, nu: (i, 0))],
            out_specs=pl.BlockSpec(memory_space=pl.ANY),
            scratch_shapes=[pltpu.VMEM((MOE_TM, D_MODEL), F32), pltpu.SemaphoreType.DMA(()),
                            pltpu.SemaphoreType.DMA(())]),
        out_shape=jax.ShapeDtypeStruct(((n_max + 1) * MOE_TM, D_MODEL), F32),
        compiler_params=_cparams(1, 32),
        name="moe_dispatch",
    )(tail, n_used, pos_s, tok)


def _experts_kernel(te_ref, nu_ref, xs_ref, wg_ref, wu_ref, wd_ref, ys_ref):
    used = pl.program_id(0) < nu_ref[0]

    @pl.when(used)
    def _():
        x = xs_ref[...].astype(BF16)
        gate = _dot(x, wg_ref[0].astype(BF16))
        up = _dot(x, wu_ref[0].astype(BF16))
        hid = (_silu(gate) * up).astype(BF16)
        ys_ref[...] = _dot(hid, wd_ref[0].astype(BF16))

    @pl.when(jnp.logical_not(used))
    def _():
        ys_ref[...] = jnp.zeros_like(ys_ref)


def _experts(te, n_used, xs, w_gate, w_up, w_down, n_max, layer):
    rows = lambda j, te, nu: (jnp.minimum(j, nu[0] - 1), 0)
    wsel = lambda j, te, nu: (layer * N_EXPERTS + te[j], 0, 0)
    return pl.pallas_call(
        _experts_kernel,
        grid_spec=pltpu.PrefetchScalarGridSpec(
            num_scalar_prefetch=2,
            grid=(n_max,),
            in_specs=[pl.BlockSpec((MOE_TM, D_MODEL), rows),
                      pl.BlockSpec((1, D_MODEL, D_EXPERT), wsel),
                      pl.BlockSpec((1, D_MODEL, D_EXPERT), wsel),
                      pl.BlockSpec((1, D_EXPERT, D_MODEL), wsel)],
            out_specs=pl.BlockSpec((MOE_TM, D_MODEL), lambda j, te, nu: (j, 0))),
        out_shape=jax.ShapeDtypeStruct((n_max * MOE_TM, D_MODEL), F32),
        compiler_params=_cparams(1, 40),
        name="moe_experts",
    )(te, n_used, xs, w_gate, w_up, w_down)


def _combine_kernel(pos_ref, ys_ref, x_ref, rt_ref, mod_ref, o_ref, ybuf, sem):
    def body(r, carry):
        pltpu.make_async_copy(ys_ref.at[pl.ds(pos_ref[0, 0, 2 * r], 1), :],
                              ybuf.at[0, pl.ds(r, 1), :], sem).start(priority=0)
        pltpu.make_async_copy(ys_ref.at[pl.ds(pos_ref[0, 0, 2 * r + 1], 1), :],
                              ybuf.at[1, pl.ds(r, 1), :], sem).start(priority=1)
        return carry

    lax.fori_loop(0, TILE, body, 0, unroll=8)
    for k in range(2):
        pltpu.make_async_copy(ys_ref.at[pl.ds(0, TILE), :], ybuf.at[k], sem).wait()
    rt = rt_ref[...]
    f = rt[:, 2:3] * ybuf[0] + rt[:, 3:4] * ybuf[1]
    o_ref[...] = x_ref[...] + mod_ref[0, 0][5:6] * f


def _combine(pos_s, ys, xmid, rt, mod, n_batch, n_tiles, layer):
    n_lat_tiles = n_batch * SEQ // TILE
    tpb = SEQ // TILE
    row = lambda i: (i, 0)
    modrow = lambda i: (layer, _tile_mod_row(i, n_lat_tiles, tpb, n_batch), 0, 0)
    return pl.pallas_call(
        _combine_kernel,
        grid=(n_tiles,),
        in_specs=[pl.BlockSpec((1, 1, 2 * TILE), lambda i: (i, 0, 0), memory_space=pltpu.SMEM),
                  pl.BlockSpec(memory_space=pl.ANY),
                  pl.BlockSpec((TILE, D_MODEL), row),
                  pl.BlockSpec((TILE, LANES), row),
                  pl.BlockSpec((1, 1, N_MOD, D_MODEL), modrow)],
        out_specs=pl.BlockSpec((TILE, D_MODEL), row),
        out_shape=jax.ShapeDtypeStruct((n_tiles * TILE, D_MODEL), F32),
        scratch_shapes=[pltpu.VMEM((2, TILE, D_MODEL), F32), pltpu.SemaphoreType.DMA(())],
        compiler_params=_cparams(1, 40),
        name="moe_combine",
    )(pos_s, ys, xmid, rt, mod)


def _moe(xmid, tok, rt, cnt, mod, w_gate, w_up, w_down, n_batch, n_tiles, layer):
    base, te, n_used, tail = _moe_plan(cnt, n_tiles)
    n_max = _moe_max_tiles(n_tiles * TILE)
    pos = _positions(rt, base, n_tiles)
    pos_s = pos[:, :2].reshape(n_tiles, 1, 2 * TILE)
    xs = _dispatch(tail, n_used, pos_s, tok, n_tiles, n_max)
    ys = _experts(te, n_used, xs, w_gate, w_up, w_down, n_max, layer)
    return _combine(pos_s, ys, xmid, rt, mod, n_batch, n_tiles, layer)


def _final_norm_kernel(x_ref, g_ref, o_ref):
    x = x_ref[...]
    o_ref[...] = x * lax.rsqrt(jnp.mean(x * x, axis=-1, keepdims=True) + RMS_EPS) * g_ref[...]


def _final_norm(xc, g, n_rows):
    return pl.pallas_call(
        _final_norm_kernel,
        grid=(n_rows // TILE,),
        in_specs=[pl.BlockSpec((TILE, D_MODEL), lambda i: (i, 0)),
                  pl.BlockSpec((1, D_MODEL), lambda i: (0, 0))],
        out_specs=pl.BlockSpec((TILE, D_MODEL), lambda i: (i, 0)),
        out_shape=jax.ShapeDtypeStruct((n_rows, D_MODEL), F32),
        compiler_params=_cparams(1, 32),
        name="final_norm",
    )(xc, g)


def _rope_tables():
    t = jnp.arange(SEQ)
    rows_pos = (t // GRID_W).astype(F32)
    cols_pos = (t % GRID_W).astype(F32)
    half = HEAD_DIM // 2
    inv = 1.0 / (ROPE_BASE ** (jnp.arange(0, half, 2, dtype=F32) / half))
    ang_r = rows_pos[:, None] * inv[None, :]
    ang_c = cols_pos[:, None] * inv[None, :]
    ang = jnp.concatenate([ang_r, ang_r, ang_c, ang_c], axis=1)
    cos_h, sin_h = jnp.cos(ang), jnp.sin(ang)
    scale = jnp.concatenate([jnp.full((256,), HEAD_DIM ** -0.5, F32), jnp.ones((128,), F32)])
    cos_t = jnp.tile(cos_h, (1, 6)) * scale
    sin_t = jnp.tile(sin_h, (1, 6)) * scale
    cos_t = jnp.concatenate([cos_t, jnp.broadcast_to(scale, (TILE, 384))], axis=0)
    sin_t = jnp.concatenate([sin_t, jnp.zeros((TILE, 384), F32)], axis=0)
    return cos_t, sin_t


def _fused_in_weight(w_in):
    wq, wk, wv = w_in[..., 0:256], w_in[..., 256:384], w_in[..., 384:512]
    o = A_IN
    wz, wxbc, wdt = w_in[..., o:o + 512], w_in[..., o + 512:o + 1280], w_in[..., o + 1280:o + 1296]
    wna = w_in[..., A_IN + SSD_IN:]

    def rot(w):
        w4 = w.reshape(w.shape[:-1] + (w.shape[-1] // 32, 2, 16))
        return jnp.concatenate([-w4[..., 1:2, :], w4[..., 0:1, :]], axis=-2).reshape(w.shape)

    na_scale = jnp.concatenate([jnp.full((256,), HEAD_DIM ** -0.5, F32), jnp.ones((512,), F32)])
    pad = jnp.zeros(w_in.shape[:-1] + (128 - SSD_HEADS,), F32)
    cat = jnp.concatenate([wq, wk, rot(wq), rot(wk), wv, wz, wxbc,
                           wna * na_scale, wdt[..., :SSD_HEADS], pad, wdt[..., SSD_HEADS:], pad], axis=-1)
    return cat.astype(BF16)


def _router_weight(w_rg, b_rg, w_re, b_re):
    w = jnp.concatenate([w_re, w_rg, jnp.zeros((D_MODEL, LANES - N_EXPERTS - MOE_GROUPS), F32)], axis=1)
    b = jnp.concatenate([b_re, b_rg, jnp.zeros((LANES - N_EXPERTS - MOE_GROUPS,), F32)]).reshape(1, LANES)
    w1 = w.astype(BF16)
    w2 = (w - w1.astype(F32)).astype(BF16)
    return w1, w2, b


def kernel(x, c, ctx, c_ctx, w_mod, b_mod, g_mix, w_in, attn_sink, ssd_conv_w, ssd_conv_b, ssd_dt_bias, ssd_a_log, ssd_d, ssd_norm_g, na_rpb, w_out, g_ffn, w_router_group, b_router_group, w_router_expert, b_router_expert, w_exp_gate, w_exp_up, w_exp_down, g_final):
    n_batch, s, d = x.shape
    assert (s, d) == (SEQ, D_MODEL) and ctx.shape[1:] == (CTX_LEN, D_MODEL) and n_batch < 16
    n_lat = n_batch * SEQ
    n_ctx = n_batch * CTX_LEN
    assert n_ctx % TILE == 0
    n_lat_tiles = n_lat // TILE
    n_all_tiles = (n_lat + n_ctx) // TILE

    xc = jnp.concatenate([x.reshape(n_lat, d), ctx.reshape(n_ctx, d)], axis=0)
    cin = jnp.zeros((16, d), F32).at[:n_batch].set(c).at[n_batch].set(c_ctx)
    mod = _modulation(cin, w_mod, b_mod).reshape(DEPTH, 16, N_MOD, d)
    cos_t, sin_t = _rope_tables()
    w_cat = _fused_in_weight(w_in)
    w_out_b = w_out.astype(BF16)
    bias_t = _na_bias_table(na_rpb)
    g_mix3 = g_mix.reshape(DEPTH, 1, d)
    w_gate = w_exp_gate.reshape(DEPTH * N_EXPERTS, D_MODEL, D_EXPERT)
    w_up = w_exp_up.reshape(DEPTH * N_EXPERTS, D_MODEL, D_EXPERT)
    w_down = w_exp_down.reshape(DEPTH * N_EXPERTS, D_EXPERT, D_MODEL)

    for layer in range(DEPTH):
        need_ctx = layer < DEPTH - 1
        qk, v, z, xbc, na, dt_raw = _inproj(xc, mod, g_mix3, w_cat, cos_t, sin_t, n_batch, layer)
        sink = attn_sink[layer].astype(F32)
        oa = _window_attn(sink, qk, v, n_batch)
        xbc_act = _conv_silu(xbc, jnp.zeros((8, SSD_CONV_DIM), F32).at[:SSD_CONV].set(ssd_conv_w[layer]),
                             ssd_conv_b[layer].reshape(1, SSD_CONV_DIM), n_batch)
        y2 = _ssd_scan(xbc_act, dt_raw, ssd_dt_bias[layer], ssd_a_log[layer], n_batch)
        on = _neighborhood_attn(na, bias_t, n_batch, layer)
        oa_c, on_c = _ctx_attn(sink, qk, v, na, n_batch) if need_ctx else (oa, on)
        n_tiles = n_all_tiles if need_ctx else n_lat_tiles
        wr1, wr2, br = _router_weight(w_router_group[layer], b_router_group[layer],
                                      w_router_expert[layer], b_router_expert[layer])
        dskip = jnp.repeat(ssd_d[layer].astype(F32), SSD_INNER // SSD_HEADS).reshape(1, SSD_INNER)
        xmid, tok, rt, cnt = _outproj(xc, oa, oa_c, y2, xbc_act, z, on, on_c, mod, dskip,
                                      ssd_norm_g[layer].reshape(1, SSD_INNER), w_out_b,
                                      g_ffn[layer].reshape(1, d), wr1, wr2, br, n_batch, n_tiles, layer)
        xc = _moe(xmid, tok, rt, cnt, mod, w_gate, w_up, w_down, n_batch, n_tiles, layer)

    return _final_norm(xc, g_final.reshape(1, d), n_lat).reshape(n_batch, SEQ, d)
```

```python
import functools
import math

import jax
import jax.numpy as jnp
import numpy as np
from jax import lax
from jax.experimental import pallas as pl
from jax.experimental.pallas import tpu as pltpu

F32 = jnp.float32
BF16 = jnp.bfloat16

D_MODEL = 1024
SEQ = 2048
DEPTH = 4
GRID_W = 64
GRID_ROWS = SEQ // GRID_W
CTX_LEN = 256
HEAD_DIM = 64
A_HEADS = 4
A_KV_HEADS = 2
A_BLOCK = 128
ROPE_BASE = 10000.0
SSD_HEADS = 8
SSD_INNER = 512
SSD_STATE = 64
SSD_CONV = 5
SSD_CHUNK = 128
SSD_CONV_DIM = 768
NA_HEADS = 4
NA_WIN_ROWS = 8
NA_WIN_COLS = 16
A_IN = 512
SSD_IN = 1296
MOE_GROUPS = 4
MOE_EXPERTS = 8
N_EXPERTS = MOE_GROUPS * MOE_EXPERTS
D_EXPERT = 256
N_MOD = 6
RMS_EPS = 1e-6
NEG_INF = -1e30

TILE = 512
LANES = 128
C_QK, C_QKP, C_V, C_Z, C_XBC, C_NA, C_DT = 0, 384, 768, 896, 1408, 2176, 2944
N_COLS = 3200
NA_Q_ROWS = 2
NA_K_ROWS = 10
NA_BIAS_OFF = 2
NA_BIAS_N = 18


def _cparams(n_axes, vmem_mb):
    return pltpu.CompilerParams(dimension_semantics=("arbitrary",) * n_axes,
                                vmem_limit_bytes=vmem_mb << 20)


def _split3(x):
    h1 = x.astype(BF16)
    r1 = x - h1.astype(F32)
    h2 = r1.astype(BF16)
    h3 = (r1 - h2.astype(F32)).astype(BF16)
    return h1, h2, h3


def _dot(a, b):
    return jnp.dot(a, b, preferred_element_type=F32)


def _dot_nt(a, b):
    return lax.dot_general(a, b, (((1,), (1,)), ((), ())), preferred_element_type=F32)


def _dot_exact_lhs(lhs_bf16, x):
    return _dot(jnp.concatenate([lhs_bf16] * 3, axis=1), jnp.concatenate(_split3(x), axis=0))


def _dot_exact_rhs(x, rhs3_bf16):
    return _dot(jnp.concatenate(_split3(x), axis=1), rhs3_bf16)


def _silu(x):
    return x * jax.nn.sigmoid(x)


ROW_SLABS = D_MODEL // LANES


def _store_row_tiles(ref, x):
    n = x.shape[0]
    for s in range(ROW_SLABS):
        ref[pl.ds(s, n, stride=ROW_SLABS), :] = x[:, s * LANES:(s + 1) * LANES]


def _load_row_tiles(ref, n):
    return jnp.concatenate([ref[pl.ds(s, n, stride=ROW_SLABS), :] for s in range(ROW_SLABS)], axis=1)


def _iota(shape, dim):
    return lax.broadcasted_iota(jnp.int32, shape, dim)


def _mod_kernel(c_ref, w_ref, b_ref, o_ref):
    a = _silu(c_ref[...])
    a1, a2, _ = _split3(a)
    w = w_ref[0]
    w1 = w.astype(BF16)
    w2 = (w - w1.astype(F32)).astype(BF16)
    o_ref[0] = _dot(a1, w1) + _dot(a1, w2) + _dot(a2, w1) + b_ref[0]


def _modulation(cin, w_mod, b_mod):
    nt = 1024
    return pl.pallas_call(
        _mod_kernel,
        grid=(DEPTH, N_MOD * D_MODEL // nt),
        in_specs=[pl.BlockSpec((16, D_MODEL), lambda l, j: (0, 0)),
                  pl.BlockSpec((1, D_MODEL, nt), lambda l, j: (l, 0, j)),
                  pl.BlockSpec((1, 1, nt), lambda l, j: (l, 0, j))],
        out_specs=pl.BlockSpec((1, 16, nt), lambda l, j: (l, 0, j)),
        out_shape=jax.ShapeDtypeStruct((DEPTH, 16, N_MOD * D_MODEL), F32),
        compiler_params=_cparams(2, 40),
        name="modulation",
    )(cin, w_mod, b_mod.reshape(DEPTH, 1, N_MOD * D_MODEL))


def _inproj_kernel(x_ref, mod_ref, g_ref, w_ref, cos_ref, sin_ref,
                   qk_ref, v_ref, z_ref, xbc_ref, na_ref, dt_ref):
    x = x_ref[...]
    m = mod_ref[0, 0]
    h = x * lax.rsqrt(jnp.mean(x * x, axis=-1, keepdims=True) + RMS_EPS) * g_ref[0]
    h = h * (1.0 + m[1:2]) + m[0:1]
    hb = h.astype(BF16)

    def mm(lo, hi):
        return _dot(hb, w_ref[0, :, lo:hi])

    qk = mm(C_QK, C_QKP) * cos_ref[...] + mm(C_QKP, C_V) * sin_ref[...]
    qk_ref[...] = qk.astype(BF16)
    v_ref[...] = mm(C_V, C_Z).astype(BF16)
    z_ref[...] = mm(C_Z, C_XBC)
    xbc_ref[...] = mm(C_XBC, C_NA)
    na_ref[...] = mm(C_NA, C_DT).astype(BF16)
    dt_ref[...] = mm(C_DT, N_COLS)


def _tile_mod_row(i, n_lat_tiles, tiles_per_batch, n_batch):
    return jnp.where(i < n_lat_tiles, i // tiles_per_batch, n_batch)


def _inproj(xc, mod, g_mix, w_cat, cos_t, sin_t, n_batch, layer):
    nt = xc.shape[0]
    n_lat_tiles = n_batch * SEQ // TILE
    tpb = SEQ // TILE
    row = lambda i: (i, 0)
    modrow = lambda i: (layer, _tile_mod_row(i, n_lat_tiles, tpb, n_batch), 0, 0)
    posrow = lambda i: (jnp.where(i < n_lat_tiles, i % tpb, tpb), 0)
    lay = lambda i: (layer, 0, 0)
    outs = [(384, BF16), (128, BF16), (512, F32), (768, F32), (768, BF16), (256, F32)]
    return pl.pallas_call(
        _inproj_kernel,
        grid=(nt // TILE,),
        in_specs=[pl.BlockSpec((TILE, D_MODEL), row),
                  pl.BlockSpec((1, 1, N_MOD, D_MODEL), modrow),
                  pl.BlockSpec((1, 1, D_MODEL), lay),
                  pl.BlockSpec((1, D_MODEL, N_COLS), lay),
                  pl.BlockSpec((TILE, 384), posrow),
                  pl.BlockSpec((TILE, 384), posrow)],
        out_specs=[pl.BlockSpec((TILE, w), row) for w, _ in outs],
        out_shape=[jax.ShapeDtypeStruct((nt, w), dt) for w, dt in outs],
        compiler_params=_cparams(1, 56),
        name="inproj",
    )(xc, mod, g_mix, w_cat, cos_t, sin_t)


def _conv_kernel(prev_ref, x_ref, next_ref, w_ref, b_ref, o_ref, *, blocks_per_seq, n_lat_blocks):
    i = pl.program_id(0)
    is_lat = i < n_lat_blocks
    first = jnp.logical_or(jnp.logical_not(is_lat), i % blocks_per_seq == 0)
    last = jnp.logical_or(jnp.logical_not(is_lat), i % blocks_per_seq == blocks_per_seq - 1)
    prev = jnp.where(first, 0.0, prev_ref[...])
    nxt = jnp.where(last, 0.0, next_ref[...])
    xw = jnp.concatenate([prev, x_ref[...], nxt], axis=0)
    rows = x_ref.shape[0]
    w = w_ref[...]
    acc = jnp.zeros(x_ref.shape, F32) + b_ref[...]
    for k in range(SSD_CONV):
        off = 8 - SSD_CONV // 2 + k
        acc = acc + xw[off:off + rows, :] * w[k:k + 1, :]
    o_ref[...] = _silu(acc)


def _conv_silu(xbc, conv_w, conv_b, n_batch):
    nt = xbc.shape[0]
    rows = 256
    bps = SEQ // rows
    n_lat_blocks = n_batch * bps
    nblk = nt // rows
    r8 = rows // 8
    kern = functools.partial(_conv_kernel, blocks_per_seq=bps, n_lat_blocks=n_lat_blocks)
    return pl.pallas_call(
        kern,
        grid=(nblk,),
        in_specs=[pl.BlockSpec((8, SSD_CONV_DIM), lambda i: (jnp.maximum(i * r8 - 1, 0), 0)),
                  pl.BlockSpec((rows, SSD_CONV_DIM), lambda i: (i, 0)),
                  pl.BlockSpec((8, SSD_CONV_DIM), lambda i: (jnp.minimum(i * r8 + r8, nt // 8 - 1), 0)),
                  pl.BlockSpec((8, SSD_CONV_DIM), lambda i: (0, 0)),
                  pl.BlockSpec((1, SSD_CONV_DIM), lambda i: (0, 0))],
        out_specs=pl.BlockSpec((rows, SSD_CONV_DIM), lambda i: (i, 0)),
        out_shape=jax.ShapeDtypeStruct((nt, SSD_CONV_DIM), F32),
        compiler_params=_cparams(1, 32),
        name="ssd_conv",
    )(xbc, xbc, xbc, conv_w, conv_b)


def _ssd_kernel(xbc_ref, dt_ref, dtb_ref, alog_ref, e512_ref, e1024_ref, y_ref, st_ref):
    d = pl.program_id(1)
    c = pl.program_id(2)

    @pl.when(c == 0)
    def _():
        st_ref[...] = jnp.zeros_like(st_ref)

    q = SSD_CHUNK
    xbc = xbc_ref[...]
    xs = xbc[:, :SSD_INNER]
    bm = xbc[:, SSD_INNER:SSD_INNER + 128]
    cm = xbc[:, SSD_INNER + 128:]
    dtr = dt_ref[...] + dtb_ref[0]
    dt = jnp.maximum(dtr, 0.0) + jnp.log1p(jnp.exp(-jnp.abs(dtr)))
    a = -jnp.exp(alog_ref[0])
    da = dt * a

    ri = _iota((q, q), 0)
    ci = _iota((q, q), 1)
    sgn = jnp.where(d == 0, 1, -1)
    tri = (ri - ci) * sgn >= 0
    trib = jnp.where(tri, 1.0, 0.0).astype(BF16)
    acs = _dot_exact_lhs(trib, da)
    acs_t = acs.T

    both_e = _dot_exact_rhs(jnp.concatenate([dt, acs], axis=0), e512_ref[...])
    dt_e = both_e[:q]
    acs_e = both_e[q:]
    acs_e2 = _dot_exact_rhs(acs, e1024_ref[...])
    tot_e = jnp.where(d == 0, acs_e[q - 1:q, :], acs_e[0:1, :])

    xdt = xs * dt_e
    xdec = (xdt * jnp.exp(tot_e - acs_e)).astype(BF16)
    btb = bm.T.astype(BF16)
    lane = _iota((q, 128), 1)
    cm0 = jnp.where(lane < SSD_STATE, cm, 0.0).astype(BF16)
    cm1 = jnp.where(lane >= SSD_STATE, cm, 0.0).astype(BF16)
    cbs = (_dot(cm0, btb), _dot(cm1, btb))

    st = st_ref[...]
    y_off = _dot(cm.astype(BF16), st.astype(BF16)) * jnp.exp(acs_e)
    s_all = _dot(btb, xdec)
    same = (_iota((q, SSD_INNER), 0) >> 6) == (_iota((q, SSD_INNER), 1) >> 8)
    st_ref[...] = jnp.where(same, st * jnp.exp(tot_e) + s_all, 0.0)

    for pair in range(SSD_HEADS // 2):
        cb = cbs[pair // 2]
        xp = xdt[:, pair * 128:(pair + 1) * 128]
        acc = None
        for k in range(2):
            h = 2 * pair + k
            seg = acs_e2[:, h * 128:(h + 1) * 128] - acs_t[h:h + 1, :]
            lmat = jnp.exp(jnp.where(tri, seg, NEG_INF))
            g = (cb * lmat).astype(BF16)
            rhs = jnp.where((lane < 64) if k == 0 else (lane >= 64), xp, 0.0).astype(BF16)
            t = _dot(g, rhs)
            acc = t if acc is None else acc + t
        y_ref[0, :, pair * 128:(pair + 1) * 128] = acc + y_off[:, pair * 128:(pair + 1) * 128]


def _ssd_scan(xbc_act, dt_raw, dt_bias, a_log, n_batch):
    nt = xbc_act.shape[0]
    n_lat_blk = n_batch * (SEQ // SSD_CHUNK)
    lat_c = SEQ // SSD_CHUNK
    ctx_c = CTX_LEN // SSD_CHUNK
    n_steps = lat_c + ctx_c

    def blk(b, d, c):
        cc = jnp.where(d == 0, c, ctx_c - 1 - c)
        lc = jnp.where(d == 0, c - ctx_c, n_steps - 1 - c)
        return jnp.where(c < ctx_c, n_lat_blk + b * ctx_c + cc, b * lat_c + lc)

    heads = np.arange(128)
    e512 = (heads[:, None] == (np.arange(512)[None, :] // 64)).astype(np.float32)
    e1024 = (heads[:, None] == (np.arange(1024)[None, :] // 128)).astype(np.float32)
    dtb = jnp.zeros((2, 1, 128), F32).at[:, 0, :SSD_HEADS].set(dt_bias)
    alog = jnp.zeros((2, 1, 128), F32).at[:, 0, :SSD_HEADS].set(a_log)
    return pl.pallas_call(
        _ssd_kernel,
        grid=(n_batch, 2, n_steps),
        in_specs=[pl.BlockSpec((SSD_CHUNK, SSD_CONV_DIM), lambda b, d, c: (blk(b, d, c), 0)),
                  pl.BlockSpec((SSD_CHUNK, 128), lambda b, d, c: (blk(b, d, c), d)),
                  pl.BlockSpec((1, 1, 128), lambda b, d, c: (d, 0, 0)),
                  pl.BlockSpec((1, 1, 128), lambda b, d, c: (d, 0, 0)),
                  pl.BlockSpec((384, 512), lambda b, d, c: (0, 0)),
                  pl.BlockSpec((384, 1024), lambda b, d, c: (0, 0))],
        out_specs=pl.BlockSpec((1, SSD_CHUNK, SSD_INNER), lambda b, d, c: (d, blk(b, d, c), 0)),
        out_shape=jax.ShapeDtypeStruct((2, nt, SSD_INNER), F32),
        scratch_shapes=[pltpu.VMEM((128, SSD_INNER), F32)],
        compiler_params=_cparams(3, 32),
        name="ssd_scan",
    )(xbc_act, dt_raw, dtb, alog, jnp.asarray(np.tile(e512, (3, 1)), BF16),
      jnp.asarray(np.tile(e1024, (3, 1)), BF16))


def _softmax_pv(s_list, v_list, extra_logit=None):
    m = s_list[0].max(axis=-1, keepdims=True)
    for s in s_list[1:]:
        m = jnp.maximum(m, s.max(axis=-1, keepdims=True))
    if extra_logit is not None:
        m = jnp.maximum(m, extra_logit)
    den = None
    o = None
    for s, v in zip(s_list, v_list):
        p = jnp.exp(s - m)
        ps = p.sum(axis=-1, keepdims=True)
        den = ps if den is None else den + ps
        t = _dot(p.astype(BF16), v)
        o = t if o is None else o + t
    if extra_logit is not None:
        den = den + jnp.exp(extra_logit - m)
    return o / den


def _wattn_kernel(sink_ref, q_ref, k_ref, v_ref, kc_ref, vc_ref, o_ref):
    n = pl.program_id(1)
    nb = SEQ // A_BLOCK
    start = pl.multiple_of(jnp.clip(n - 1, 0, nb - 3) * A_BLOCK, A_BLOCK)
    q = q_ref[:, 0:256]
    kw = k_ref[pl.ds(start, 3 * A_BLOCK), 256:384]
    vw = v_ref[pl.ds(start, 3 * A_BLOCK), :]
    kc = kc_ref[:, 256:384]
    vc = vc_ref[...]
    qpos = n * A_BLOCK + (_iota((2 * A_BLOCK, 3 * A_BLOCK), 0) & (A_BLOCK - 1))
    kpos = start + _iota((2 * A_BLOCK, 3 * A_BLOCK), 1)
    valid = jnp.abs(qpos - kpos) <= A_BLOCK
    top = _iota((2 * A_BLOCK, 1), 0) < A_BLOCK
    outs = []
    for g in range(A_KV_HEADS):
        qg = jnp.concatenate([q[:, (2 * g) * 64:(2 * g + 1) * 64],
                              q[:, (2 * g + 1) * 64:(2 * g + 2) * 64]], axis=0)
        kg = kw[:, g * 64:(g + 1) * 64]
        vg = vw[:, g * 64:(g + 1) * 64]
        s_loc = jnp.where(valid, _dot_nt(qg, kg), NEG_INF)
        s_ctx = _dot_nt(qg, kc[:, g * 64:(g + 1) * 64])
        sink = jnp.where(top, sink_ref[2 * g], sink_ref[2 * g + 1])
        o = _softmax_pv([s_loc, s_ctx], [vg, vc[:, g * 64:(g + 1) * 64]], sink)
        outs += [o[:A_BLOCK], o[A_BLOCK:]]
    o_ref[...] = jnp.concatenate(outs, axis=1).astype(BF16)


def _window_attn(sink, qk, v, n_batch):
    nb = SEQ // A_BLOCK
    ctx0 = n_batch * SEQ // CTX_LEN
    return pl.pallas_call(
        _wattn_kernel,
        grid=(n_batch, nb),
        in_specs=[pl.BlockSpec(memory_space=pltpu.SMEM),
                  pl.BlockSpec((A_BLOCK, 384), lambda b, n: (b * nb + n, 0)),
                  pl.BlockSpec((SEQ, 384), lambda b, n: (b, 0)),
                  pl.BlockSpec((SEQ, 128), lambda b, n: (b, 0)),
                  pl.BlockSpec((CTX_LEN, 384), lambda b, n: (ctx0 + b, 0)),
                  pl.BlockSpec((CTX_LEN, 128), lambda b, n: (ctx0 + b, 0))],
        out_specs=pl.BlockSpec((A_BLOCK, 256), lambda b, n: (b * nb + n, 0)),
        out_shape=jax.ShapeDtypeStruct((n_batch * SEQ, 256), BF16),
        compiler_params=_cparams(2, 32),
        name="window_attn",
    )(sink, qk, qk, v, qk, v)


def _nattn_kernel(q_ref, kv_ref, c_ref, bias_ref, o_ref):
    i = pl.program_id(1)
    r0 = i * NA_Q_ROWS
    srow = jnp.clip(r0 - NA_WIN_ROWS // 2, 0, GRID_ROWS - NA_K_ROWS)
    start = pl.multiple_of(srow * GRID_W, GRID_W)
    nq = NA_Q_ROWS * GRID_W
    nk = NA_K_ROWS * GRID_W
    q = q_ref[:, 0:256]
    kw = kv_ref[pl.ds(start, nk), 256:512]
    vw = kv_ref[pl.ds(start, nk), 512:768]
    kc = c_ref[:, 256:512]
    vc = c_ref[:, 512:768]
    qrow = r0 + (_iota((nq, nk), 0) >> 6)
    krow = srow + (_iota((nq, nk), 1) >> 6)
    rs = jnp.clip(qrow - NA_WIN_ROWS // 2, 0, GRID_ROWS - NA_WIN_ROWS)
    valid = jnp.logical_and(krow >= rs, krow < rs + NA_WIN_ROWS)
    outs = []
    for h in range(NA_HEADS):
        sl = slice(h * 64, (h + 1) * 64)
        rows = []
        for qi in range(NA_Q_ROWS):
            blocks = []
            for p in range(NA_K_ROWS // 2):
                idx = srow + 2 * p - (r0 + qi) + (NA_WIN_ROWS - 1) + NA_BIAS_OFF
                blocks.append(bias_ref[0, h, idx])
            rows.append(jnp.concatenate(blocks, axis=1))
        bias = jnp.concatenate(rows, axis=0)
        s_loc = jnp.where(valid, _dot_nt(q[:, sl], kw[:, sl]) + bias, NEG_INF)
        s_ctx = _dot_nt(q[:, sl], kc[:, sl])
        outs.append(_softmax_pv([s_loc, s_ctx], [vw[:, sl], vc[:, sl]]))
    o_ref[...] = jnp.concatenate(outs, axis=1).astype(BF16)


def _na_bias_table(rpb):
    cq = np.arange(GRID_W)
    kcol = np.arange(GRID_W)
    cs = np.clip(cq - NA_WIN_COLS // 2, 0, GRID_W - NA_WIN_COLS)
    col_valid = (kcol[None, :] >= cs[:, None]) & (kcol[None, :] < cs[:, None] + NA_WIN_COLS)
    coff = np.clip(kcol[None, :] - cq[:, None], -(NA_WIN_COLS - 1), NA_WIN_COLS - 1) + (NA_WIN_COLS - 1)
    n_a = 2 * NA_WIN_ROWS - 1
    n_c = 2 * NA_WIN_COLS - 1
    pick = (np.arange(n_c)[:, None] == coff.reshape(1, -1)).astype(np.float32)
    tm = jnp.einsum("lhak,kn->lhan", rpb.astype(F32), jnp.asarray(pick), precision=lax.Precision.HIGHEST)
    tm = jnp.where(col_valid, tm.reshape(rpb.shape[:3] + (GRID_W, GRID_W)), NEG_INF)
    neg = jnp.full(rpb.shape[:2] + (1, GRID_W, GRID_W), NEG_INF, F32)
    pad_lo = NA_BIAS_OFF
    pad_hi = NA_BIAS_N + 1 - pad_lo - n_a
    ext = jnp.concatenate([neg] * pad_lo + [tm] + [neg] * pad_hi, axis=2)
    return jnp.concatenate([ext[:, :, :NA_BIAS_N], ext[:, :, 1:NA_BIAS_N + 1]], axis=-1)


def _neighborhood_attn(na, bias_t, n_batch, layer):
    steps = GRID_ROWS // NA_Q_ROWS
    nq = NA_Q_ROWS * GRID_W
    ctx0 = n_batch * SEQ // CTX_LEN
    return pl.pallas_call(
        _nattn_kernel,
        grid=(n_batch, steps),
        in_specs=[pl.BlockSpec((nq, 768), lambda b, i: (b * steps + i, 0)),
                  pl.BlockSpec((SEQ, 768), lambda b, i: (b, 0)),
                  pl.BlockSpec((CTX_LEN, 768), lambda b, i: (ctx0 + b, 0)),
                  pl.BlockSpec((1, NA_HEADS, NA_BIAS_N, GRID_W, 128), lambda b, i: (layer, 0, 0, 0, 0))],
        out_specs=pl.BlockSpec((nq, 256), lambda b, i: (b * steps + i, 0)),
        out_shape=jax.ShapeDtypeStruct((n_batch * SEQ, 256), BF16),
        compiler_params=_cparams(2, 40),
        name="neighborhood_attn",
    )(na, na, na, bias_t)


def _ctx_attn_kernel(sink_ref, qk_ref, v_ref, na_ref, oa_ref, on_ref):
    qk = qk_ref[...]
    v = v_ref[...]
    na = na_ref[...]
    outs = []
    for h in range(A_HEADS):
        g = h // (A_HEADS // A_KV_HEADS)
        s = _dot_nt(qk[:, h * 64:(h + 1) * 64], qk[:, 256 + g * 64:256 + (g + 1) * 64])
        sink = jnp.zeros((CTX_LEN, 1), F32) + sink_ref[h]
        outs.append(_softmax_pv([s], [v[:, g * 64:(g + 1) * 64]], sink))
    oa_ref[...] = jnp.concatenate(outs, axis=1).astype(BF16)
    outs = []
    for h in range(NA_HEADS):
        sl = slice(h * 64, (h + 1) * 64)
        s = _dot_nt(na[:, 0:256][:, sl], na[:, 256:512][:, sl])
        outs.append(_softmax_pv([s], [na[:, 512:768][:, sl]]))
    on_ref[...] = jnp.concatenate(outs, axis=1).astype(BF16)


def _ctx_attn(sink, qk, v, na, n_batch):
    ctx0 = n_batch * SEQ // CTX_LEN
    row = lambda b: (ctx0 + b, 0)
    return pl.pallas_call(
        _ctx_attn_kernel,
        grid=(n_batch,),
        in_specs=[pl.BlockSpec(memory_space=pltpu.SMEM),
                  pl.BlockSpec((CTX_LEN, 384), row),
                  pl.BlockSpec((CTX_LEN, 128), row),
                  pl.BlockSpec((CTX_LEN, 768), row)],
        out_specs=[pl.BlockSpec((CTX_LEN, 256), lambda b: (b, 0))] * 2,
        out_shape=[jax.ShapeDtypeStruct((n_batch * CTX_LEN, 256), BF16)] * 2,
        compiler_params=_cparams(1, 32),
        name="ctx_attn",
    )(sink, qk, v, na)


def _outproj_kernel(x_ref, oal_ref, oac_ref, y_ref, xbc_ref, z_ref, onl_ref, onc_ref, mod_ref, dskip_ref,
                    ng_ref, w_ref, gf_ref, wr1_ref, wr2_ref, br_ref, xo_ref, tok_ref, rt_ref, cnt_ref,
                    *, n_lat_tiles):
    is_lat = pl.program_id(0) < n_lat_tiles
    m = mod_ref[0, 0]
    xs = xbc_ref[:, 0:SSD_INNER]
    y = y_ref[0] + y_ref[1] + dskip_ref[...] * xs
    y = y * _silu(z_ref[...])
    ob = y * lax.rsqrt(jnp.mean(y * y, axis=-1, keepdims=True) + RMS_EPS) * ng_ref[...]
    oa = jnp.where(is_lat, oal_ref[...], oac_ref[...])
    on = jnp.where(is_lat, onl_ref[...], onc_ref[...])
    proj = (_dot(oa, w_ref[0, 0:256, :]) + _dot(ob.astype(BF16), w_ref[0, 256:768, :])
            + _dot(on, w_ref[0, 768:1024, :]))
    x = x_ref[...] + m[2:3] * proj
    xo_ref[...] = x
    t = x * lax.rsqrt(jnp.mean(x * x, axis=-1, keepdims=True) + RMS_EPS) * gf_ref[...]
    t = t * (1.0 + m[4:5]) + m[3:4]
    _store_row_tiles(tok_ref, t)
    t1 = t.astype(BF16)
    t2 = (t - t1.astype(F32)).astype(BF16)
    logits = _dot(t1, wr1_ref[...]) + _dot(t1, wr2_ref[...]) + _dot(t2, wr1_ref[...]) + br_ref[...]

    lane = _iota(logits.shape, 1)
    big = jnp.int32(1 << 20)
    is_g = jnp.logical_and(lane >= N_EXPERTS, lane < N_EXPERTS + MOE_GROUPS)
    gl = jnp.where(is_g, logits, NEG_INF)
    gmax = gl.max(axis=-1, keepdims=True)
    g_w = 1.0 / jnp.exp(gl - gmax).sum(axis=-1, keepdims=True)
    g_idx = jnp.where(gl == gmax, lane, big).min(axis=-1, keepdims=True) - N_EXPERTS
    in_grp = jnp.logical_and(lane < N_EXPERTS, (lane >> 3) == g_idx)
    el = jnp.where(in_grp, logits, NEG_INF)
    l1 = el.max(axis=-1, keepdims=True)
    i1 = jnp.where(el == l1, lane, big).min(axis=-1, keepdims=True)
    el2 = jnp.where(lane == i1, NEG_INF, el)
    l2 = el2.max(axis=-1, keepdims=True)
    i2 = jnp.where(el2 == l2, lane, big).min(axis=-1, keepdims=True)
    e2 = jnp.exp(l2 - l1)
    w1 = g_w / (1.0 + e2)
    w2 = w1 * e2
    rt_ref[...] = jnp.where(lane == 0, i1.astype(F32), jnp.where(lane == 1, i2.astype(F32),
                            jnp.where(lane == 2, w1, jnp.where(lane == 3, w2, 0.0))))
    hot = jnp.logical_or(lane == i1, lane == i2)
    cnt_ref[0] = jnp.where(hot, 1.0, 0.0).sum(axis=0, keepdims=True)


def _outproj(xc, oa_l, oa_c, y2, xbc_act, z, on_l, on_c, mod, dskip, norm_g, w_out, g_ffn, wr1, wr2, br,
             n_batch, n_tiles, layer):
    n_lat_tiles = n_batch * SEQ // TILE
    tpb = SEQ // TILE
    row = lambda i: (i, 0)
    lat = lambda i: (jnp.minimum(i, n_lat_tiles - 1), 0)
    ctx = lambda i: (jnp.maximum(i - n_lat_tiles, 0), 0)
    modrow = lambda i: (layer, _tile_mod_row(i, n_lat_tiles, tpb, n_batch), 0, 0)
    const = lambda i: (0, 0)
    kern = functools.partial(_outproj_kernel, n_lat_tiles=n_lat_tiles)
    return pl.pallas_call(
        kern,
        grid=(n_tiles,),
        in_specs=[pl.BlockSpec((TILE, D_MODEL), row),
                  pl.BlockSpec((TILE, 256), lat),
                  pl.BlockSpec((TILE, 256), ctx),
                  pl.BlockSpec((2, TILE, SSD_INNER), lambda i: (0, i, 0)),
                  pl.BlockSpec((TILE, SSD_CONV_DIM), row),
                  pl.BlockSpec((TILE, SSD_INNER), row),
                  pl.BlockSpec((TILE, 256), lat),
                  pl.BlockSpec((TILE, 256), ctx),
                  pl.BlockSpec((1, 1, N_MOD, D_MODEL), modrow),
                  pl.BlockSpec((1, SSD_INNER), const),
                  pl.BlockSpec((1, SSD_INNER), const),
                  pl.BlockSpec((1, D_MODEL, D_MODEL), lambda i: (layer, 0, 0)),
                  pl.BlockSpec((1, D_MODEL), const),
                  pl.BlockSpec((D_MODEL, LANES), const),
                  pl.BlockSpec((D_MODEL, LANES), const),
                  pl.BlockSpec((1, LANES), const)],
        out_specs=[pl.BlockSpec((TILE, D_MODEL), row),
                   pl.BlockSpec((TILE * ROW_SLABS, LANES), row),
                   pl.BlockSpec((TILE, LANES), row),
                   pl.BlockSpec((1, 1, LANES), lambda i: (i, 0, 0))],
        out_shape=[jax.ShapeDtypeStruct((n_tiles * TILE, D_MODEL), F32),
                   jax.ShapeDtypeStruct((n_tiles * TILE * ROW_SLABS, LANES), F32),
                   jax.ShapeDtypeStruct((n_tiles * TILE, LANES), F32),
                   jax.ShapeDtypeStruct((n_tiles, 1, LANES), F32)],
        compiler_params=_cparams(1, 56),
        name="outproj",
    )(xc, oa_l, oa_c, y2, xbc_act, z, on_l, on_c, mod, dskip, norm_g, w_out, g_ffn, wr1, wr2, br)


MOE_TM = 512


def _moe_max_tiles(n_tokens):
    return (2 * n_tokens + N_EXPERTS * (MOE_TM - 1)) // MOE_TM


def _moe_plan(cnt, n_tiles):
    cnt = cnt[:, 0, :N_EXPERTS].astype(jnp.int32)
    tot = cnt.sum(axis=0)
    tiles_e = (tot + MOE_TM - 1) // MOE_TM
    t_end = jnp.cumsum(tiles_e)
    t_start = t_end - tiles_e
    base = (t_start * MOE_TM)[None, :] + jnp.cumsum(cnt, axis=0) - cnt
    n_used = t_end[-1]
    n_max = _moe_max_tiles(n_tiles * TILE)
    te = jnp.sum(jnp.arange(n_max)[:, None] >= t_end[None, :], axis=1)
    te = jnp.minimum(te, jnp.sum((n_used - 1) >= t_end)).astype(jnp.int32)
    tail = jnp.where(tiles_e > 0, t_end - 1, n_max).astype(jnp.int32)
    base_f = jnp.zeros((n_tiles, 1, LANES), F32).at[:, 0, :N_EXPERTS].set(base.astype(F32))
    return base_f, te, n_used.reshape(1).astype(jnp.int32), tail


def _pos_kernel(rt_ref, base_ref, pos_ref):
    rt = rt_ref[...]
    lane = _iota(rt.shape, 1)
    hot1 = lane == rt[:, 0:1].astype(jnp.int32)
    hot2 = lane == rt[:, 1:2].astype(jnp.int32)
    hot = jnp.where(jnp.logical_or(hot1, hot2), 1.0, 0.0).astype(BF16)
    strict = jnp.where(_iota((TILE, TILE), 0) > _iota((TILE, TILE), 1), 1.0, 0.0).astype(BF16)
    slot = base_ref[0] + _dot(strict, hot)
    p1 = jnp.where(hot1, slot, 0.0).sum(axis=-1, keepdims=True)
    p2 = jnp.where(hot2, slot, 0.0).sum(axis=-1, keepdims=True)
    pos_ref[...] = jnp.where(lane == 0, p1, jnp.where(lane == 1, p2, 0.0)).astype(jnp.int32)


def _positions(rt, base, n_tiles):
    return pl.pallas_call(
        _pos_kernel,
        grid=(n_tiles,),
        in_specs=[pl.BlockSpec((TILE, LANES), lambda i: (i, 0)),
                  pl.BlockSpec((1, 1, LANES), lambda i: (i, 0, 0))],
        out_specs=pl.BlockSpec((TILE, LANES), lambda i: (i, 0)),
        out_shape=jax.ShapeDtypeStruct((n_tiles * TILE, LANES), jnp.int32),
        compiler_params=_cparams(1, 32),
        name="moe_positions",
    )(rt, base)


def _dispatch_kernel(tail_ref, nu_ref, pos_ref, tok_ref, xs_ref, zbuf, zsem, sem, *, n_max):
    i = pl.program_id(0)

    def zero_tile(j):
        start = pl.multiple_of(j * (MOE_TM * ROW_SLABS), MOE_TM * ROW_SLABS)
        return pltpu.make_async_copy(zbuf, xs_ref.at[pl.ds(start, MOE_TM * ROW_SLABS), :], zsem)

    def row(ref, r):
        return ref.at[pl.ds(pl.multiple_of(r * ROW_SLABS, ROW_SLABS), ROW_SLABS), :]

    @pl.when(i == 0)
    def _():
        zbuf[...] = jnp.zeros_like(zbuf)
        for e in range(N_EXPERTS):
            @pl.when(tail_ref[e] != n_max)
            def _():
                zero_tile(tail_ref[e]).start()
        lax.fori_loop(nu_ref[0], n_max + 1, lambda j, c: (zero_tile(j).start(), c)[1], 0)
        for e in range(N_EXPERTS):
            @pl.when(tail_ref[e] != n_max)
            def _():
                zero_tile(tail_ref[e]).wait()
        lax.fori_loop(nu_ref[0], n_max + 1, lambda j, c: (zero_tile(j).wait(), c)[1], 0)

    def body(r, carry):
        src = row(tok_ref, r)
        pltpu.make_async_copy(src, row(xs_ref, pos_ref[0, 0, 2 * r]), sem).start(priority=0)
        pltpu.make_async_copy(src, row(xs_ref, pos_ref[0, 0, 2 * r + 1]), sem).start(priority=1)
        return carry

    lax.fori_loop(0, TILE, body, 0, unroll=8)
    for _ in range(2):
        pltpu.make_async_copy(tok_ref, xs_ref.at[pl.ds(0, TILE * ROW_SLABS), :], sem).wait()


def _dispatch(tail, n_used, pos_s, tok, n_tiles, n_max):
    kern = functools.partial(_dispatch_kernel, n_max=n_max)
    return pl.pallas_call(
        kern,
        grid_spec=pltpu.PrefetchScalarGridSpec(
            num_scalar_prefetch=2,
            grid=(n_tiles,),
            in_specs=[pl.BlockSpec((1, 1, 2 * TILE), lambda i, tail, nu: (i, 0, 0), memory_space=pltpu.SMEM),
                      pl.BlockSpec((TILE * ROW_SLABS, LANES), lambda i, tail, nu: (i, 0))],
            out_specs=pl.BlockSpec(memory_space=pl.ANY),
            scratch_shapes=[pltpu.VMEM((MOE_TM * ROW_SLABS, LANES), F32), pltpu.SemaphoreType.DMA(()),
                            pltpu.SemaphoreType.DMA(())]),
        out_shape=jax.ShapeDtypeStruct(((n_max + 1) * MOE_TM * ROW_SLABS, LANES), F32),
        compiler_params=_cparams(1, 32),
        name="moe_dispatch",
    )(tail, n_used, pos_s, tok)


def _experts_kernel(te_ref, nu_ref, xs_ref, wg_ref, wu_ref, wd_ref, ys_ref):
    used = pl.program_id(0) < nu_ref[0]

    @pl.when(used)
    def _():
        x = _load_row_tiles(xs_ref, MOE_TM).astype(BF16)
        gate = _dot(x, wg_ref[0].astype(BF16))
        up = _dot(x, wu_ref[0].astype(BF16))
        hid = (_silu(gate) * up).astype(BF16)
        _store_row_tiles(ys_ref, _dot(hid, wd_ref[0].astype(BF16)))

    @pl.when(jnp.logical_not(used))
    def _():
        ys_ref[...] = jnp.zeros_like(ys_ref)


def _experts(te, n_used, xs, w_gate, w_up, w_down, n_max, layer):
    rows = lambda j, te, nu: (jnp.minimum(j, nu[0] - 1), 0)
    wsel = lambda j, te, nu: (layer * N_EXPERTS + te[j], 0, 0)
    return pl.pallas_call(
        _experts_kernel,
        grid_spec=pltpu.PrefetchScalarGridSpec(
            num_scalar_prefetch=2,
            grid=(n_max,),
            in_specs=[pl.BlockSpec((MOE_TM * ROW_SLABS, LANES), rows),
                      pl.BlockSpec((1, D_MODEL, D_EXPERT), wsel),
                      pl.BlockSpec((1, D_MODEL, D_EXPERT), wsel),
                      pl.BlockSpec((1, D_EXPERT, D_MODEL), wsel)],
            out_specs=pl.BlockSpec((MOE_TM * ROW_SLABS, LANES), lambda j, te, nu: (j, 0))),
        out_shape=jax.ShapeDtypeStruct((n_max * MOE_TM * ROW_SLABS, LANES), F32),
        compiler_params=_cparams(1, 48),
        name="moe_experts",
    )(te, n_used, xs, w_gate, w_up, w_down)


def _combine_kernel(pos_ref, ys_ref, x_ref, rt_ref, mod_ref, o_ref, ybuf0, ybuf1, sem):
    def row(ref, r):
        return ref.at[pl.ds(pl.multiple_of(r * ROW_SLABS, ROW_SLABS), ROW_SLABS), :]

    def body(r, carry):
        pltpu.make_async_copy(row(ys_ref, pos_ref[0, 0, 2 * r]), row(ybuf0, r), sem).start(priority=0)
        pltpu.make_async_copy(row(ys_ref, pos_ref[0, 0, 2 * r + 1]), row(ybuf1, r), sem).start(priority=1)
        return carry

    lax.fori_loop(0, TILE, body, 0, unroll=8)
    for buf in (ybuf0, ybuf1):
        pltpu.make_async_copy(ys_ref.at[pl.ds(0, TILE * ROW_SLABS), :], buf, sem).wait()
    rt = rt_ref[...]
    f = rt[:, 2:3] * _load_row_tiles(ybuf0, TILE) + rt[:, 3:4] * _load_row_tiles(ybuf1, TILE)
    o_ref[...] = x_ref[...] + mod_ref[0, 0][5:6] * f


def _combine(pos_s, ys, xmid, rt, mod, n_batch, n_tiles, layer):
    n_lat_tiles = n_batch * SEQ // TILE
    tpb = SEQ // TILE
    row = lambda i: (i, 0)
    modrow = lambda i: (layer, _tile_mod_row(i, n_lat_tiles, tpb, n_batch), 0, 0)
    return pl.pallas_call(
        _combine_kernel,
        grid=(n_tiles,),
        in_specs=[pl.BlockSpec((1, 1, 2 * TILE), lambda i: (i, 0, 0), memory_space=pltpu.SMEM),
                  pl.BlockSpec(memory_space=pl.ANY),
                  pl.BlockSpec((TILE, D_MODEL), row),
                  pl.BlockSpec((TILE, LANES), row),
                  pl.BlockSpec((1, 1, N_MOD, D_MODEL), modrow)],
        out_specs=pl.BlockSpec((TILE, D_MODEL), row),
        out_shape=jax.ShapeDtypeStruct((n_tiles * TILE, D_MODEL), F32),
        scratch_shapes=[pltpu.VMEM((TILE * ROW_SLABS, LANES), F32), pltpu.VMEM((TILE * ROW_SLABS, LANES), F32),
                        pltpu.SemaphoreType.DMA(())],
        compiler_params=_cparams(1, 40),
        name="moe_combine",
    )(pos_s, ys, xmid, rt, mod)


def _moe(xmid, tok, rt, cnt, mod, w_gate, w_up, w_down, n_batch, n_tiles, layer):
    base, te, n_used, tail = _moe_plan(cnt, n_tiles)
    n_max = _moe_max_tiles(n_tiles * TILE)
    pos = _positions(rt, base, n_tiles)
    pos_s = pos[:, :2].reshape(n_tiles, 1, 2 * TILE)
    xs = _dispatch(tail, n_used, pos_s, tok, n_tiles, n_max)
    ys = _experts(te, n_used, xs, w_gate, w_up, w_down, n_max, layer)
    return _combine(pos_s, ys, xmid, rt, mod, n_batch, n_tiles, layer)


def _final_norm_kernel(x_ref, g_ref, o_ref):
    x = x_ref[...]
    o_ref[...] = x * lax.rsqrt(jnp.mean(x * x, axis=-1, keepdims=True) + RMS_EPS) * g_ref[...]


def _final_norm(xc, g, n_rows):
    return pl.pallas_call(
        _final_norm_kernel,
        grid=(n_rows // TILE,),
        in_specs=[pl.BlockSpec((TILE, D_MODEL), lambda i: (i, 0)),
                  pl.BlockSpec((1, D_MODEL), lambda i: (0, 0))],
        out_specs=pl.BlockSpec((TILE, D_MODEL), lambda i: (i, 0)),
        out_shape=jax.ShapeDtypeStruct((n_rows, D_MODEL), F32),
        compiler_params=_cparams(1, 32),
        name="final_norm",
    )(xc, g)


def _rope_tables():
    t = jnp.arange(SEQ)
    rows_pos = (t // GRID_W).astype(F32)
    cols_pos = (t % GRID_W).astype(F32)
    half = HEAD_DIM // 2
    inv = 1.0 / (ROPE_BASE ** (jnp.arange(0, half, 2, dtype=F32) / half))
    ang_r = rows_pos[:, None] * inv[None, :]
    ang_c = cols_pos[:, None] * inv[None, :]
    ang = jnp.concatenate([ang_r, ang_r, ang_c, ang_c], axis=1)
    cos_h, sin_h = jnp.cos(ang), jnp.sin(ang)
    scale = jnp.concatenate([jnp.full((256,), HEAD_DIM ** -0.5, F32), jnp.ones((128,), F32)])
    cos_t = jnp.tile(cos_h, (1, 6)) * scale
    sin_t = jnp.tile(sin_h, (1, 6)) * scale
    cos_t = jnp.concatenate([cos_t, jnp.broadcast_to(scale, (TILE, 384))], axis=0)
    sin_t = jnp.concatenate([sin_t, jnp.zeros((TILE, 384), F32)], axis=0)
    return cos_t, sin_t


def _fused_in_weight(w_in):
    wq, wk, wv = w_in[..., 0:256], w_in[..., 256:384], w_in[..., 384:512]
    o = A_IN
    wz, wxbc, wdt = w_in[..., o:o + 512], w_in[..., o + 512:o + 1280], w_in[..., o + 1280:o + 1296]
    wna = w_in[..., A_IN + SSD_IN:]

    def rot(w):
        w4 = w.reshape(w.shape[:-1] + (w.shape[-1] // 32, 2, 16))
        return jnp.concatenate([-w4[..., 1:2, :], w4[..., 0:1, :]], axis=-2).reshape(w.shape)

    na_scale = jnp.concatenate([jnp.full((256,), HEAD_DIM ** -0.5, F32), jnp.ones((512,), F32)])
    pad = jnp.zeros(w_in.shape[:-1] + (128 - SSD_HEADS,), F32)
    cat = jnp.concatenate([wq, wk, rot(wq), rot(wk), wv, wz, wxbc,
                           wna * na_scale, wdt[..., :SSD_HEADS], pad, wdt[..., SSD_HEADS:], pad], axis=-1)
    return cat.astype(BF16)


def _router_weight(w_rg, b_rg, w_re, b_re):
    w = jnp.concatenate([w_re, w_rg, jnp.zeros((D_MODEL, LANES - N_EXPERTS - MOE_GROUPS), F32)], axis=1)
    b = jnp.concatenate([b_re, b_rg, jnp.zeros((LANES - N_EXPERTS - MOE_GROUPS,), F32)]).reshape(1, LANES)
    w1 = w.astype(BF16)
    w2 = (w - w1.astype(F32)).astype(BF16)
    return w1, w2, b


def kernel(x, c, ctx, c_ctx, w_mod, b_mod, g_mix, w_in, attn_sink, ssd_conv_w, ssd_conv_b, ssd_dt_bias, ssd_a_log, ssd_d, ssd_norm_g, na_rpb, w_out, g_ffn, w_router_group, b_router_group, w_router_expert, b_router_expert, w_exp_gate, w_exp_up, w_exp_down, g_final):
    n_batch, s, d = x.shape
    assert (s, d) == (SEQ, D_MODEL) and ctx.shape[1:] == (CTX_LEN, D_MODEL) and n_batch < 16
    n_lat = n_batch * SEQ
    n_ctx = n_batch * CTX_LEN
    assert n_ctx % TILE == 0
    n_lat_tiles = n_lat // TILE
    n_all_tiles = (n_lat + n_ctx) // TILE

    xc = jnp.concatenate([x.reshape(n_lat, d), ctx.reshape(n_ctx, d)], axis=0)
    cin = jnp.zeros((16, d), F32).at[:n_batch].set(c).at[n_batch].set(c_ctx)
    mod = _modulation(cin, w_mod, b_mod).reshape(DEPTH, 16, N_MOD, d)
    cos_t, sin_t = _rope_tables()
    w_cat = _fused_in_weight(w_in)
    w_out_b = w_out.astype(BF16)
    bias_t = _na_bias_table(na_rpb)
    g_mix3 = g_mix.reshape(DEPTH, 1, d)
    w_gate = w_exp_gate.reshape(DEPTH * N_EXPERTS, D_MODEL, D_EXPERT)
    w_up = w_exp_up.reshape(DEPTH * N_EXPERTS, D_MODEL, D_EXPERT)
    w_down = w_exp_down.reshape(DEPTH * N_EXPERTS, D_EXPERT, D_MODEL)

    for layer in range(DEPTH):
        need_ctx = layer < DEPTH - 1
        qk, v, z, xbc, na, dt_raw = _inproj(xc, mod, g_mix3, w_cat, cos_t, sin_t, n_batch, layer)
        sink = attn_sink[layer].astype(F32)
        oa = _window_attn(sink, qk, v, n_batch)
        xbc_act = _conv_silu(xbc, jnp.zeros((8, SSD_CONV_DIM), F32).at[:SSD_CONV].set(ssd_conv_w[layer]),
                             ssd_conv_b[layer].reshape(1, SSD_CONV_DIM), n_batch)
        y2 = _ssd_scan(xbc_act, dt_raw, ssd_dt_bias[layer], ssd_a_log[layer], n_batch)
        on = _neighborhood_attn(na, bias_t, n_batch, layer)
        oa_c, on_c = _ctx_attn(sink, qk, v, na, n_batch) if need_ctx else (oa, on)
        n_tiles = n_all_tiles if need_ctx else n_lat_tiles
        wr1, wr2, br = _router_weight(w_router_group[layer], b_router_group[layer],
                                      w_router_expert[layer], b_router_expert[layer])
        dskip = jnp.repeat(ssd_d[layer].astype(F32), SSD_INNER // SSD_HEADS).reshape(1, SSD_INNER)
        xmid, tok, rt, cnt = _outproj(xc, oa, oa_c, y2, xbc_act, z, on, on_c, mod, dskip,
                                      ssd_norm_g[layer].reshape(1, SSD_INNER), w_out_b,
                                      g_ffn[layer].reshape(1, d), wr1, wr2, br, n_batch, n_tiles, layer)
        xc = _moe(xmid, tok, rt, cnt, mod, w_gate, w_up, w_down, n_batch, n_tiles, layer)

    return _final_norm(xc, g_final.reshape(1, d), n_lat).reshape(n_batch, SEQ, d)
```

```python
import functools
import math

import jax
import jax.numpy as jnp
import numpy as np
from jax import lax
from jax.experimental import pallas as pl
from jax.experimental.pallas import tpu as pltpu

F32 = jnp.float32
BF16 = jnp.bfloat16

D_MODEL = 1024
SEQ = 2048
DEPTH = 4
GRID_W = 64
GRID_ROWS = SEQ // GRID_W
CTX_LEN = 256
HEAD_DIM = 64
A_HEADS = 4
A_KV_HEADS = 2
A_BLOCK = 128
ROPE_BASE = 10000.0
SSD_HEADS = 8
SSD_INNER = 512
SSD_STATE = 64
SSD_CONV = 5
SSD_CHUNK = 128
SSD_CONV_DIM = 768
NA_HEADS = 4
NA_WIN_ROWS = 8
NA_WIN_COLS = 16
A_IN = 512
SSD_IN = 1296
MOE_GROUPS = 4
MOE_EXPERTS = 8
N_EXPERTS = MOE_GROUPS * MOE_EXPERTS
D_EXPERT = 256
N_MOD = 6
RMS_EPS = 1e-6
NEG_INF = -1e30

TILE = 512
LANES = 128
C_QK, C_QKP, C_V, C_Z, C_XBC, C_NA, C_DT = 0, 384, 768, 896, 1408, 2176, 2944
N_COLS = 3200
NA_Q_ROWS = 4
NA_K_ROWS = NA_WIN_ROWS + NA_Q_ROWS
NA_BIAS_OFF = NA_Q_ROWS
NA_BIAS_N = NA_BIAS_OFF + (NA_K_ROWS - 2) + (NA_WIN_ROWS - 1) + 1


def _cparams(n_axes, vmem_mb):
    return pltpu.CompilerParams(dimension_semantics=("arbitrary",) * n_axes,
                                vmem_limit_bytes=vmem_mb << 20)


def _split3(x):
    h1 = x.astype(BF16)
    r1 = x - h1.astype(F32)
    h2 = r1.astype(BF16)
    h3 = (r1 - h2.astype(F32)).astype(BF16)
    return h1, h2, h3


def _dot(a, b):
    return jnp.dot(a, b, preferred_element_type=F32)


def _dot_nt(a, b):
    return lax.dot_general(a, b, (((1,), (1,)), ((), ())), preferred_element_type=F32)


def _dot_exact_lhs(lhs_bf16, x):
    return _dot(jnp.concatenate([lhs_bf16] * 3, axis=1), jnp.concatenate(_split3(x), axis=0))


def _dot_exact_rhs(x, rhs3_bf16):
    return _dot(jnp.concatenate(_split3(x), axis=1), rhs3_bf16)


def _silu(x):
    return x * jax.nn.sigmoid(x)


ROW_SLABS = D_MODEL // LANES


def _store_row_tiles(ref, x):
    n = x.shape[0]
    for s in range(ROW_SLABS):
        ref[pl.ds(s, n, stride=ROW_SLABS), :] = x[:, s * LANES:(s + 1) * LANES]


def _load_row_tiles(ref, n):
    return jnp.concatenate([ref[pl.ds(s, n, stride=ROW_SLABS), :] for s in range(ROW_SLABS)], axis=1)


def _iota(shape, dim):
    return lax.broadcasted_iota(jnp.int32, shape, dim)


def _mod_kernel(c_ref, w_ref, b_ref, o_ref):
    a = _silu(c_ref[...])
    a1, a2, _ = _split3(a)
    w = w_ref[0]
    w1 = w.astype(BF16)
    w2 = (w - w1.astype(F32)).astype(BF16)
    o_ref[0] = _dot(a1, w1) + _dot(a1, w2) + _dot(a2, w1) + b_ref[0]


def _modulation(cin, w_mod, b_mod):
    nt = 1024
    return pl.pallas_call(
        _mod_kernel,
        grid=(DEPTH, N_MOD * D_MODEL // nt),
        in_specs=[pl.BlockSpec((16, D_MODEL), lambda l, j: (0, 0)),
                  pl.BlockSpec((1, D_MODEL, nt), lambda l, j: (l, 0, j)),
                  pl.BlockSpec((1, 1, nt), lambda l, j: (l, 0, j))],
        out_specs=pl.BlockSpec((1, 16, nt), lambda l, j: (l, 0, j)),
        out_shape=jax.ShapeDtypeStruct((DEPTH, 16, N_MOD * D_MODEL), F32),
        compiler_params=_cparams(2, 40),
        name="modulation",
    )(cin, w_mod, b_mod.reshape(DEPTH, 1, N_MOD * D_MODEL))


def _inproj_kernel(x_ref, mod_ref, g_ref, w_ref, cos_ref, sin_ref,
                   qk_ref, v_ref, z_ref, xbc_ref, na_ref, dt_ref):
    x = x_ref[...]
    m = mod_ref[0, 0]
    h = x * lax.rsqrt(jnp.mean(x * x, axis=-1, keepdims=True) + RMS_EPS) * g_ref[0]
    h = h * (1.0 + m[1:2]) + m[0:1]
    hb = h.astype(BF16)

    def mm(lo, hi):
        return _dot(hb, w_ref[0, :, lo:hi])

    qk = mm(C_QK, C_QKP) * cos_ref[...] + mm(C_QKP, C_V) * sin_ref[...]
    qk_ref[...] = qk.astype(BF16)
    v_ref[...] = mm(C_V, C_Z).astype(BF16)
    z_ref[...] = mm(C_Z, C_XBC)
    xbc_ref[...] = mm(C_XBC, C_NA)
    na_ref[...] = mm(C_NA, C_DT).astype(BF16)
    dt_ref[...] = mm(C_DT, N_COLS)


def _tile_mod_row(i, n_lat_tiles, tiles_per_batch, n_batch):
    return jnp.where(i < n_lat_tiles, i // tiles_per_batch, n_batch)


def _inproj(xc, mod, g_mix, w_cat, cos_t, sin_t, n_batch, layer):
    nt = xc.shape[0]
    n_lat_tiles = n_batch * SEQ // TILE
    tpb = SEQ // TILE
    row = lambda i: (i, 0)
    modrow = lambda i: (layer, _tile_mod_row(i, n_lat_tiles, tpb, n_batch), 0, 0)
    posrow = lambda i: (jnp.where(i < n_lat_tiles, i % tpb, tpb), 0)
    lay = lambda i: (layer, 0, 0)
    outs = [(384, BF16), (128, BF16), (512, F32), (768, F32), (768, BF16), (256, F32)]
    return pl.pallas_call(
        _inproj_kernel,
        grid=(nt // TILE,),
        in_specs=[pl.BlockSpec((TILE, D_MODEL), row),
                  pl.BlockSpec((1, 1, N_MOD, D_MODEL), modrow),
                  pl.BlockSpec((1, 1, D_MODEL), lay),
                  pl.BlockSpec((1, D_MODEL, N_COLS), lay),
                  pl.BlockSpec((TILE, 384), posrow),
                  pl.BlockSpec((TILE, 384), posrow)],
        out_specs=[pl.BlockSpec((TILE, w), row) for w, _ in outs],
        out_shape=[jax.ShapeDtypeStruct((nt, w), dt) for w, dt in outs],
        compiler_params=_cparams(1, 56),
        name="inproj",
    )(xc, mod, g_mix, w_cat, cos_t, sin_t)


def _conv_kernel(prev_ref, x_ref, next_ref, w_ref, b_ref, o_ref, *, blocks_per_seq, n_lat_blocks):
    i = pl.program_id(0)
    is_lat = i < n_lat_blocks
    first = jnp.logical_or(jnp.logical_not(is_lat), i % blocks_per_seq == 0)
    last = jnp.logical_or(jnp.logical_not(is_lat), i % blocks_per_seq == blocks_per_seq - 1)
    prev = jnp.where(first, 0.0, prev_ref[...])
    nxt = jnp.where(last, 0.0, next_ref[...])
    xw = jnp.concatenate([prev, x_ref[...], nxt], axis=0)
    rows = x_ref.shape[0]
    w = w_ref[...]
    acc = jnp.zeros(x_ref.shape, F32) + b_ref[...]
    for k in range(SSD_CONV):
        off = 8 - SSD_CONV // 2 + k
        acc = acc + xw[off:off + rows, :] * w[k:k + 1, :]
    o_ref[...] = _silu(acc)


def _conv_silu(xbc, conv_w, conv_b, n_batch):
    nt = xbc.shape[0]
    rows = 256
    bps = SEQ // rows
    n_lat_blocks = n_batch * bps
    nblk = nt // rows
    r8 = rows // 8
    kern = functools.partial(_conv_kernel, blocks_per_seq=bps, n_lat_blocks=n_lat_blocks)
    return pl.pallas_call(
        kern,
        grid=(nblk,),
        in_specs=[pl.BlockSpec((8, SSD_CONV_DIM), lambda i: (jnp.maximum(i * r8 - 1, 0), 0)),
                  pl.BlockSpec((rows, SSD_CONV_DIM), lambda i: (i, 0)),
                  pl.BlockSpec((8, SSD_CONV_DIM), lambda i: (jnp.minimum(i * r8 + r8, nt // 8 - 1), 0)),
                  pl.BlockSpec((8, SSD_CONV_DIM), lambda i: (0, 0)),
                  pl.BlockSpec((1, SSD_CONV_DIM), lambda i: (0, 0))],
        out_specs=pl.BlockSpec((rows, SSD_CONV_DIM), lambda i: (i, 0)),
        out_shape=jax.ShapeDtypeStruct((nt, SSD_CONV_DIM), F32),
        compiler_params=_cparams(1, 32),
        name="ssd_conv",
    )(xbc, xbc, xbc, conv_w, conv_b)


def _ssd_kernel(xf_ref, dtf_ref, xb_ref, dtb_ref, bias_ref, alog_ref, e512_ref, e1024_ref,
                yf_ref, yb_ref, stf_ref, stb_ref):
    @pl.when(pl.program_id(1) == 0)
    def _():
        stf_ref[...] = jnp.zeros_like(stf_ref)
        stb_ref[...] = jnp.zeros_like(stb_ref)

    _ssd_chunk(0, xf_ref, dtf_ref, bias_ref[0], alog_ref[0], e512_ref, e1024_ref, yf_ref, stf_ref)
    _ssd_chunk(1, xb_ref, dtb_ref, bias_ref[1], alog_ref[1], e512_ref, e1024_ref, yb_ref, stb_ref)


def _ssd_chunk(d, xbc_ref, dt_ref, dt_bias, a_log, e512_ref, e1024_ref, y_ref, st_ref):
    q = SSD_CHUNK
    xbc = xbc_ref[...]
    xs = xbc[:, :SSD_INNER]
    bm = xbc[:, SSD_INNER:SSD_INNER + 128]
    cm = xbc[:, SSD_INNER + 128:]
    dtr = dt_ref[...] + dt_bias
    dt = jnp.maximum(dtr, 0.0) + jnp.log1p(jnp.exp(-jnp.abs(dtr)))
    a = -jnp.exp(a_log)
    da = dt * a

    ri = _iota((q, q), 0)
    ci = _iota((q, q), 1)
    tri = (ri >= ci) if d == 0 else (ri <= ci)
    trib = jnp.where(tri, 1.0, 0.0).astype(BF16)
    acs = _dot_exact_lhs(trib, da)
    acs_t = acs.T

    both_e = _dot_exact_rhs(jnp.concatenate([dt, acs], axis=0), e512_ref[...])
    dt_e = both_e[:q]
    acs_e = both_e[q:]
    acs_e2 = _dot_exact_rhs(acs, e1024_ref[...])
    tot_e = acs_e[q - 1:q, :] if d == 0 else acs_e[0:1, :]

    xdt = xs * dt_e
    xdec = (xdt * jnp.exp(tot_e - acs_e)).astype(BF16)
    btb = bm.T.astype(BF16)
    lane = _iota((q, 128), 1)
    cm0 = jnp.where(lane < SSD_STATE, cm, 0.0).astype(BF16)
    cm1 = jnp.where(lane >= SSD_STATE, cm, 0.0).astype(BF16)
    cbs = (_dot(cm0, btb), _dot(cm1, btb))

    st = st_ref[...]
    y_off = _dot(cm.astype(BF16), st.astype(BF16)) * jnp.exp(acs_e)
    s_all = _dot(btb, xdec)
    same = (_iota((q, SSD_INNER), 0) >> 6) == (_iota((q, SSD_INNER), 1) >> 8)
    st_ref[...] = jnp.where(same, st * jnp.exp(tot_e) + s_all, 0.0)

    for pair in range(SSD_HEADS // 2):
        cb = cbs[pair // 2]
        xp = xdt[:, pair * 128:(pair + 1) * 128]
        acc = None
        for k in range(2):
            h = 2 * pair + k
            seg = acs_e2[:, h * 128:(h + 1) * 128] - acs_t[h:h + 1, :]
            lmat = jnp.exp(jnp.where(tri, seg, NEG_INF))
            g = (cb * lmat).astype(BF16)
            rhs = jnp.where((lane < 64) if k == 0 else (lane >= 64), xp, 0.0).astype(BF16)
            t = _dot(g, rhs)
            acc = t if acc is None else acc + t
        y_ref[:, pair * 128:(pair + 1) * 128] = acc + y_off[:, pair * 128:(pair + 1) * 128]


def _ssd_scan(xbc_act, dt_raw, dt_bias, a_log, n_batch):
    nt = xbc_act.shape[0]
    n_lat_blk = n_batch * (SEQ // SSD_CHUNK)
    lat_c = SEQ // SSD_CHUNK
    ctx_c = CTX_LEN // SSD_CHUNK
    n_steps = lat_c + ctx_c

    def blk_f(b, c):
        return jnp.where(c < ctx_c, n_lat_blk + b * ctx_c + c, b * lat_c + c - ctx_c)

    def blk_b(b, c):
        return jnp.where(c < ctx_c, n_lat_blk + b * ctx_c + ctx_c - 1 - c, b * lat_c + n_steps - 1 - c)

    heads = np.arange(128)
    e512 = (heads[:, None] == (np.arange(512)[None, :] // 64)).astype(np.float32)
    e1024 = (heads[:, None] == (np.arange(1024)[None, :] // 128)).astype(np.float32)
    dtb = jnp.zeros((2, 1, 128), F32).at[:, 0, :SSD_HEADS].set(dt_bias)
    alog = jnp.zeros((2, 1, 128), F32).at[:, 0, :SSD_HEADS].set(a_log)
    const3 = lambda b, c: (0, 0, 0)
    const2 = lambda b, c: (0, 0)
    return pl.pallas_call(
        _ssd_kernel,
        grid=(n_batch, n_steps),
        in_specs=[pl.BlockSpec((SSD_CHUNK, SSD_CONV_DIM), lambda b, c: (blk_f(b, c), 0)),
                  pl.BlockSpec((SSD_CHUNK, 128), lambda b, c: (blk_f(b, c), 0)),
                  pl.BlockSpec((SSD_CHUNK, SSD_CONV_DIM), lambda b, c: (blk_b(b, c), 0)),
                  pl.BlockSpec((SSD_CHUNK, 128), lambda b, c: (blk_b(b, c), 1)),
                  pl.BlockSpec((2, 1, 128), const3),
                  pl.BlockSpec((2, 1, 128), const3),
                  pl.BlockSpec((384, 512), const2),
                  pl.BlockSpec((384, 1024), const2)],
        out_specs=[pl.BlockSpec((SSD_CHUNK, SSD_INNER), lambda b, c: (blk_f(b, c), 0)),
                   pl.BlockSpec((SSD_CHUNK, SSD_INNER), lambda b, c: (blk_b(b, c), 0))],
        out_shape=[jax.ShapeDtypeStruct((nt, SSD_INNER), F32)] * 2,
        scratch_shapes=[pltpu.VMEM((128, SSD_INNER), F32)] * 2,
        compiler_params=_cparams(2, 32),
        name="ssd_scan",
    )(xbc_act, dt_raw, xbc_act, dt_raw, dtb, alog, jnp.asarray(np.tile(e512, (3, 1)), BF16),
      jnp.asarray(np.tile(e1024, (3, 1)), BF16))


def _softmax_pv(s_list, v_list, extra_logit=None):
    m = s_list[0].max(axis=-1, keepdims=True)
    for s in s_list[1:]:
        m = jnp.maximum(m, s.max(axis=-1, keepdims=True))
    if extra_logit is not None:
        m = jnp.maximum(m, extra_logit)
    den = None
    o = None
    for s, v in zip(s_list, v_list):
        p = jnp.exp(s - m)
        ps = p.sum(axis=-1, keepdims=True)
        den = ps if den is None else den + ps
        t = _dot(p.astype(BF16), v)
        o = t if o is None else o + t
    if extra_logit is not None:
        den = den + jnp.exp(extra_logit - m)
    return o / den


def _wattn_kernel(sink_ref, q_ref, k_ref, v_ref, kc_ref, vc_ref, o_ref):
    n = pl.program_id(1)
    nb = SEQ // A_BLOCK
    start = pl.multiple_of(jnp.clip(n - 1, 0, nb - 3) * A_BLOCK, A_BLOCK)
    q = q_ref[:, 0:256]
    kw = k_ref[pl.ds(start, 3 * A_BLOCK), 256:384]
    vw = v_ref[pl.ds(start, 3 * A_BLOCK), :]
    kc = kc_ref[:, 256:384]
    vc = vc_ref[...]
    qpos = n * A_BLOCK + (_iota((2 * A_BLOCK, 3 * A_BLOCK), 0) & (A_BLOCK - 1))
    kpos = start + _iota((2 * A_BLOCK, 3 * A_BLOCK), 1)
    valid = jnp.abs(qpos - kpos) <= A_BLOCK
    top = _iota((2 * A_BLOCK, 1), 0) < A_BLOCK
    outs = []
    for g in range(A_KV_HEADS):
        qg = jnp.concatenate([q[:, (2 * g) * 64:(2 * g + 1) * 64],
                              q[:, (2 * g + 1) * 64:(2 * g + 2) * 64]], axis=0)
        kg = kw[:, g * 64:(g + 1) * 64]
        vg = vw[:, g * 64:(g + 1) * 64]
        s_loc = jnp.where(valid, _dot_nt(qg, kg), NEG_INF)
        s_ctx = _dot_nt(qg, kc[:, g * 64:(g + 1) * 64])
        sink = jnp.where(top, sink_ref[2 * g], sink_ref[2 * g + 1])
        o = _softmax_pv([s_loc, s_ctx], [vg, vc[:, g * 64:(g + 1) * 64]], sink)
        outs += [o[:A_BLOCK], o[A_BLOCK:]]
    o_ref[...] = jnp.concatenate(outs, axis=1).astype(BF16)


def _window_attn(sink, qk, v, n_batch):
    nb = SEQ // A_BLOCK
    ctx0 = n_batch * SEQ // CTX_LEN
    return pl.pallas_call(
        _wattn_kernel,
        grid=(n_batch, nb),
        in_specs=[pl.BlockSpec(memory_space=pltpu.SMEM),
                  pl.BlockSpec((A_BLOCK, 384), lambda b, n: (b * nb + n, 0)),
                  pl.BlockSpec((SEQ, 384), lambda b, n: (b, 0)),
                  pl.BlockSpec((SEQ, 128), lambda b, n: (b, 0)),
                  pl.BlockSpec((CTX_LEN, 384), lambda b, n: (ctx0 + b, 0)),
                  pl.BlockSpec((CTX_LEN, 128), lambda b, n: (ctx0 + b, 0))],
        out_specs=pl.BlockSpec((A_BLOCK, 256), lambda b, n: (b * nb + n, 0)),
        out_shape=jax.ShapeDtypeStruct((n_batch * SEQ, 256), BF16),
        compiler_params=_cparams(2, 32),
        name="window_attn",
    )(sink, qk, qk, v, qk, v)


def _nattn_kernel(q_ref, kv_ref, c_ref, bias_ref, o_ref):
    i = pl.program_id(1)
    r0 = i * NA_Q_ROWS
    srow = jnp.clip(r0 - NA_WIN_ROWS // 2, 0, GRID_ROWS - NA_K_ROWS)
    start = pl.multiple_of(srow * GRID_W, GRID_W)
    nq = NA_Q_ROWS * GRID_W
    nk = NA_K_ROWS * GRID_W
    q = q_ref[:, 0:256]
    kw = kv_ref[pl.ds(start, nk), 256:512]
    vw = kv_ref[pl.ds(start, nk), 512:768]
    kc = c_ref[:, 256:512]
    vc = c_ref[:, 512:768]
    qrow = r0 + (_iota((nq, nk), 0) >> 6)
    krow = srow + (_iota((nq, nk), 1) >> 6)
    rs = jnp.clip(qrow - NA_WIN_ROWS // 2, 0, GRID_ROWS - NA_WIN_ROWS)
    valid = jnp.logical_and(krow >= rs, krow < rs + NA_WIN_ROWS)
    outs = []
    for h in range(NA_HEADS):
        sl = slice(h * 64, (h + 1) * 64)
        rows = []
        for qi in range(NA_Q_ROWS):
            blocks = []
            for p in range(NA_K_ROWS // 2):
                idx = srow + 2 * p - (r0 + qi) + (NA_WIN_ROWS - 1) + NA_BIAS_OFF
                blocks.append(bias_ref[0, h, idx])
            rows.append(jnp.concatenate(blocks, axis=1))
        bias = jnp.concatenate(rows, axis=0)
        s_loc = jnp.where(valid, _dot_nt(q[:, sl], kw[:, sl]) + bias, NEG_INF)
        s_ctx = _dot_nt(q[:, sl], kc[:, sl])
        outs.append(_softmax_pv([s_loc, s_ctx], [vw[:, sl], vc[:, sl]]))
    o_ref[...] = jnp.concatenate(outs, axis=1).astype(BF16)


def _na_bias_table(rpb):
    cq = np.arange(GRID_W)
    kcol = np.arange(GRID_W)
    cs = np.clip(cq - NA_WIN_COLS // 2, 0, GRID_W - NA_WIN_COLS)
    col_valid = (kcol[None, :] >= cs[:, None]) & (kcol[None, :] < cs[:, None] + NA_WIN_COLS)
    coff = np.clip(kcol[None, :] - cq[:, None], -(NA_WIN_COLS - 1), NA_WIN_COLS - 1) + (NA_WIN_COLS - 1)
    n_a = 2 * NA_WIN_ROWS - 1
    n_c = 2 * NA_WIN_COLS - 1
    pick = (np.arange(n_c)[:, None] == coff.reshape(1, -1)).astype(np.float32)
    tm = jnp.einsum("lhak,kn->lhan", rpb.astype(F32), jnp.asarray(pick), precision=lax.Precision.HIGHEST)
    tm = jnp.where(col_valid, tm.reshape(rpb.shape[:3] + (GRID_W, GRID_W)), NEG_INF)
    neg = jnp.full(rpb.shape[:2] + (1, GRID_W, GRID_W), NEG_INF, F32)
    pad_lo = NA_BIAS_OFF
    pad_hi = NA_BIAS_N + 1 - pad_lo - n_a
    ext = jnp.concatenate([neg] * pad_lo + [tm] + [neg] * pad_hi, axis=2)
    return jnp.concatenate([ext[:, :, :NA_BIAS_N], ext[:, :, 1:NA_BIAS_N + 1]], axis=-1)


def _neighborhood_attn(na, bias_t, n_batch, layer):
    steps = GRID_ROWS // NA_Q_ROWS
    nq = NA_Q_ROWS * GRID_W
    ctx0 = n_batch * SEQ // CTX_LEN
    return pl.pallas_call(
        _nattn_kernel,
        grid=(n_batch, steps),
        in_specs=[pl.BlockSpec((nq, 768), lambda b, i: (b * steps + i, 0)),
                  pl.BlockSpec((SEQ, 768), lambda b, i: (b, 0)),
                  pl.BlockSpec((CTX_LEN, 768), lambda b, i: (ctx0 + b, 0)),
                  pl.BlockSpec((1, NA_HEADS, NA_BIAS_N, GRID_W, 128), lambda b, i: (layer, 0, 0, 0, 0))],
        out_specs=pl.BlockSpec((nq, 256), lambda b, i: (b * steps + i, 0)),
        out_shape=jax.ShapeDtypeStruct((n_batch * SEQ, 256), BF16),
        compiler_params=_cparams(2, 40),
        name="neighborhood_attn",
    )(na, na, na, bias_t)


def _ctx_attn_kernel(sink_ref, qk_ref, v_ref, na_ref, oa_ref, on_ref):
    qk = qk_ref[...]
    v = v_ref[...]
    na = na_ref[...]
    outs = []
    for h in range(A_HEADS):
        g = h // (A_HEADS // A_KV_HEADS)
        s = _dot_nt(qk[:, h * 64:(h + 1) * 64], qk[:, 256 + g * 64:256 + (g + 1) * 64])
        sink = jnp.zeros((CTX_LEN, 1), F32) + sink_ref[h]
        outs.append(_softmax_pv([s], [v[:, g * 64:(g + 1) * 64]], sink))
    oa_ref[...] = jnp.concatenate(outs, axis=1).astype(BF16)
    outs = []
    for h in range(NA_HEADS):
        sl = slice(h * 64, (h + 1) * 64)
        s = _dot_nt(na[:, 0:256][:, sl], na[:, 256:512][:, sl])
        outs.append(_softmax_pv([s], [na[:, 512:768][:, sl]]))
    on_ref[...] = jnp.concatenate(outs, axis=1).astype(BF16)


def _ctx_attn(sink, qk, v, na, n_batch):
    ctx0 = n_batch * SEQ // CTX_LEN
    row = lambda b: (ctx0 + b, 0)
    return pl.pallas_call(
        _ctx_attn_kernel,
        grid=(n_batch,),
        in_specs=[pl.BlockSpec(memory_space=pltpu.SMEM),
                  pl.BlockSpec((CTX_LEN, 384), row),
                  pl.BlockSpec((CTX_LEN, 128), row),
                  pl.BlockSpec((CTX_LEN, 768), row)],
        out_specs=[pl.BlockSpec((CTX_LEN, 256), lambda b: (b, 0))] * 2,
        out_shape=[jax.ShapeDtypeStruct((n_batch * CTX_LEN, 256), BF16)] * 2,
        compiler_params=_cparams(1, 32),
        name="ctx_attn",
    )(sink, qk, v, na)


def _outproj_kernel(x_ref, oal_ref, oac_ref, yf_ref, yb_ref, xbc_ref, z_ref, onl_ref, onc_ref, mod_ref,
                    dskip_ref, ng_ref, w_ref, gf_ref, wr1_ref, wr2_ref, br_ref, xo_ref, tok_ref, rt_ref,
                    cnt_ref, *, n_lat_tiles):
    is_lat = pl.program_id(0) < n_lat_tiles
    m = mod_ref[0, 0]
    xs = xbc_ref[:, 0:SSD_INNER]
    y = yf_ref[...] + yb_ref[...] + dskip_ref[...] * xs
    y = y * _silu(z_ref[...])
    ob = y * lax.rsqrt(jnp.mean(y * y, axis=-1, keepdims=True) + RMS_EPS) * ng_ref[...]
    oa = jnp.where(is_lat, oal_ref[...], oac_ref[...])
    on = jnp.where(is_lat, onl_ref[...], onc_ref[...])
    proj = (_dot(oa, w_ref[0, 0:256, :]) + _dot(ob.astype(BF16), w_ref[0, 256:768, :])
            + _dot(on, w_ref[0, 768:1024, :]))
    x = x_ref[...] + m[2:3] * proj
    xo_ref[...] = x
    t = x * lax.rsqrt(jnp.mean(x * x, axis=-1, keepdims=True) + RMS_EPS) * gf_ref[...]
    t = t * (1.0 + m[4:5]) + m[3:4]
    _store_row_tiles(tok_ref, t)
    t1 = t.astype(BF16)
    t2 = (t - t1.astype(F32)).astype(BF16)
    logits = _dot(t1, wr1_ref[...]) + _dot(t1, wr2_ref[...]) + _dot(t2, wr1_ref[...]) + br_ref[...]

    lane = _iota(logits.shape, 1)
    big = jnp.int32(1 << 20)
    is_g = jnp.logical_and(lane >= N_EXPERTS, lane < N_EXPERTS + MOE_GROUPS)
    gl = jnp.where(is_g, logits, NEG_INF)
    gmax = gl.max(axis=-1, keepdims=True)
    g_w = 1.0 / jnp.exp(gl - gmax).sum(axis=-1, keepdims=True)
    g_idx = jnp.where(gl == gmax, lane, big).min(axis=-1, keepdims=True) - N_EXPERTS
    in_grp = jnp.logical_and(lane < N_EXPERTS, (lane >> 3) == g_idx)
    el = jnp.where(in_grp, logits, NEG_INF)
    l1 = el.max(axis=-1, keepdims=True)
    i1 = jnp.where(el == l1, lane, big).min(axis=-1, keepdims=True)
    el2 = jnp.where(lane == i1, NEG_INF, el)
    l2 = el2.max(axis=-1, keepdims=True)
    i2 = jnp.where(el2 == l2, lane, big).min(axis=-1, keepdims=True)
    e2 = jnp.exp(l2 - l1)
    w1 = g_w / (1.0 + e2)
    w2 = w1 * e2
    rt_ref[...] = jnp.where(lane == 0, i1.astype(F32), jnp.where(lane == 1, i2.astype(F32),
                            jnp.where(lane == 2, w1, jnp.where(lane == 3, w2, 0.0))))
    hot = jnp.logical_or(lane == i1, lane == i2)
    cnt_ref[0] = jnp.where(hot, 1.0, 0.0).sum(axis=0, keepdims=True)


def _outproj(xc, oa_l, oa_c, y_f, y_b, xbc_act, z, on_l, on_c, mod, dskip, norm_g, w_out, g_ffn, wr1, wr2, br,
             n_batch, n_tiles, layer):
    n_lat_tiles = n_batch * SEQ // TILE
    tpb = SEQ // TILE
    row = lambda i: (i, 0)
    lat = lambda i: (jnp.minimum(i, n_lat_tiles - 1), 0)
    ctx = lambda i: (jnp.maximum(i - n_lat_tiles, 0), 0)
    modrow = lambda i: (layer, _tile_mod_row(i, n_lat_tiles, tpb, n_batch), 0, 0)
    const = lambda i: (0, 0)
    kern = functools.partial(_outproj_kernel, n_lat_tiles=n_lat_tiles)
    return pl.pallas_call(
        kern,
        grid=(n_tiles,),
        in_specs=[pl.BlockSpec((TILE, D_MODEL), row),
                  pl.BlockSpec((TILE, 256), lat),
                  pl.BlockSpec((TILE, 256), ctx),
                  pl.BlockSpec((TILE, SSD_INNER), row),
                  pl.BlockSpec((TILE, SSD_INNER), row),
                  pl.BlockSpec((TILE, SSD_CONV_DIM), row),
                  pl.BlockSpec((TILE, SSD_INNER), row),
                  pl.BlockSpec((TILE, 256), lat),
                  pl.BlockSpec((TILE, 256), ctx),
                  pl.BlockSpec((1, 1, N_MOD, D_MODEL), modrow),
                  pl.BlockSpec((1, SSD_INNER), const),
                  pl.BlockSpec((1, SSD_INNER), const),
                  pl.BlockSpec((1, D_MODEL, D_MODEL), lambda i: (layer, 0, 0)),
                  pl.BlockSpec((1, D_MODEL), const),
                  pl.BlockSpec((D_MODEL, LANES), const),
                  pl.BlockSpec((D_MODEL, LANES), const),
                  pl.BlockSpec((1, LANES), const)],
        out_specs=[pl.BlockSpec((TILE, D_MODEL), row),
                   pl.BlockSpec((TILE * ROW_SLABS, LANES), row),
                   pl.BlockSpec((TILE, LANES), row),
                   pl.BlockSpec((1, 1, LANES), lambda i: (i, 0, 0))],
        out_shape=[jax.ShapeDtypeStruct((n_tiles * TILE, D_MODEL), F32),
                   jax.ShapeDtypeStruct((n_tiles * TILE * ROW_SLABS, LANES), F32),
                   jax.ShapeDtypeStruct((n_tiles * TILE, LANES), F32),
                   jax.ShapeDtypeStruct((n_tiles, 1, LANES), F32)],
        compiler_params=_cparams(1, 56),
        name="outproj",
    )(xc, oa_l, oa_c, y_f, y_b, xbc_act, z, on_l, on_c, mod, dskip, norm_g, w_out, g_ffn, wr1, wr2, br)


MOE_TM = 512


def _moe_max_tiles(n_tokens):
    return (2 * n_tokens + N_EXPERTS * (MOE_TM - 1)) // MOE_TM


def _moe_plan(cnt, n_tiles):
    cnt = cnt[:, 0, :N_EXPERTS].astype(jnp.int32)
    tot = cnt.sum(axis=0)
    tiles_e = (tot + MOE_TM - 1) // MOE_TM
    t_end = jnp.cumsum(tiles_e)
    t_start = t_end - tiles_e
    base = (t_start * MOE_TM)[None, :] + jnp.cumsum(cnt, axis=0) - cnt
    n_used = t_end[-1]
    n_max = _moe_max_tiles(n_tiles * TILE)
    te = jnp.sum(jnp.arange(n_max)[:, None] >= t_end[None, :], axis=1)
    te = jnp.minimum(te, jnp.sum((n_used - 1) >= t_end)).astype(jnp.int32)
    tail = jnp.where(tiles_e > 0, t_end - 1, n_max).astype(jnp.int32)
    base_f = jnp.zeros((n_tiles, 1, LANES), F32).at[:, 0, :N_EXPERTS].set(base.astype(F32))
    return base_f, te, n_used.reshape(1).astype(jnp.int32), tail


def _pos_kernel(rt_ref, base_ref, pos_ref):
    rt = rt_ref[...]
    lane = _iota(rt.shape, 1)
    hot1 = lane == rt[:, 0:1].astype(jnp.int32)
    hot2 = lane == rt[:, 1:2].astype(jnp.int32)
    hot = jnp.where(jnp.logical_or(hot1, hot2), 1.0, 0.0).astype(BF16)
    strict = jnp.where(_iota((TILE, TILE), 0) > _iota((TILE, TILE), 1), 1.0, 0.0).astype(BF16)
    slot = base_ref[0] + _dot(strict, hot)
    p1 = jnp.where(hot1, slot, 0.0).sum(axis=-1, keepdims=True)
    p2 = jnp.where(hot2, slot, 0.0).sum(axis=-1, keepdims=True)
    pos_ref[...] = jnp.where(lane == 0, p1, jnp.where(lane == 1, p2, 0.0)).astype(jnp.int32)


def _positions(rt, base, n_tiles):
    return pl.pallas_call(
        _pos_kernel,
        grid=(n_tiles,),
        in_specs=[pl.BlockSpec((TILE, LANES), lambda i: (i, 0)),
                  pl.BlockSpec((1, 1, LANES), lambda i: (i, 0, 0))],
        out_specs=pl.BlockSpec((TILE, LANES), lambda i: (i, 0)),
        out_shape=jax.ShapeDtypeStruct((n_tiles * TILE, LANES), jnp.int32),
        compiler_params=_cparams(1, 32),
        name="moe_positions",
    )(rt, base)


def _dispatch_kernel(tail_ref, nu_ref, pos_ref, tok_ref, xs_ref, zbuf, zsem, sem, *, n_max):
    i = pl.program_id(0)

    def zero_tile(j):
        start = pl.multiple_of(j * (MOE_TM * ROW_SLABS), MOE_TM * ROW_SLABS)
        return pltpu.make_async_copy(zbuf, xs_ref.at[pl.ds(start, MOE_TM * ROW_SLABS), :], zsem)

    def row(ref, r):
        return ref.at[pl.ds(pl.multiple_of(r * ROW_SLABS, ROW_SLABS), ROW_SLABS), :]

    @pl.when(i == 0)
    def _():
        zbuf[...] = jnp.zeros_like(zbuf)
        for e in range(N_EXPERTS):
            @pl.when(tail_ref[e] != n_max)
            def _():
                zero_tile(tail_ref[e]).start()
        lax.fori_loop(nu_ref[0], n_max + 1, lambda j, c: (zero_tile(j).start(), c)[1], 0)
        for e in range(N_EXPERTS):
            @pl.when(tail_ref[e] != n_max)
            def _():
                zero_tile(tail_ref[e]).wait()
        lax.fori_loop(nu_ref[0], n_max + 1, lambda j, c: (zero_tile(j).wait(), c)[1], 0)

    def body(r, carry):
        src = row(tok_ref, r)
        pltpu.make_async_copy(src, row(xs_ref, pos_ref[0, 0, 2 * r]), sem).start(priority=0)
        pltpu.make_async_copy(src, row(xs_ref, pos_ref[0, 0, 2 * r + 1]), sem).start(priority=1)
        return carry

    lax.fori_loop(0, TILE, body, 0, unroll=8)
    for _ in range(2):
        pltpu.make_async_copy(tok_ref, xs_ref.at[pl.ds(0, TILE * ROW_SLABS), :], sem).wait()


def _dispatch(tail, n_used, pos_s, tok, n_tiles, n_max):
    kern = functools.partial(_dispatch_kernel, n_max=n_max)
    return pl.pallas_call(
        kern,
        grid_spec=pltpu.PrefetchScalarGridSpec(
            num_scalar_prefetch=2,
            grid=(n_tiles,),
            in_specs=[pl.BlockSpec((1, 1, 2 * TILE), lambda i, tail, nu: (i, 0, 0), memory_space=pltpu.SMEM),
                      pl.BlockSpec((TILE * ROW_SLABS, LANES), lambda i, tail, nu: (i, 0))],
            out_specs=pl.BlockSpec(memory_space=pl.ANY),
            scratch_shapes=[pltpu.VMEM((MOE_TM * ROW_SLABS, LANES), F32), pltpu.SemaphoreType.DMA(()),
                            pltpu.SemaphoreType.DMA(())]),
        out_shape=jax.ShapeDtypeStruct(((n_max + 1) * MOE_TM * ROW_SLABS, LANES), F32),
        compiler_params=_cparams(1, 32),
        name="moe_dispatch",
    )(tail, n_used, pos_s, tok)


def _experts_kernel(te_ref, nu_ref, xs_ref, wg_ref, wu_ref, wd_ref, ys_ref):
    used = pl.program_id(0) < nu_ref[0]

    @pl.when(used)
    def _():
        x = _load_row_tiles(xs_ref, MOE_TM).astype(BF16)
        gate = _dot(x, wg_ref[0].astype(BF16))
        up = _dot(x, wu_ref[0].astype(BF16))
        hid = (_silu(gate) * up).astype(BF16)
        _store_row_tiles(ys_ref, _dot(hid, wd_ref[0].astype(BF16)))

    @pl.when(jnp.logical_not(used))
    def _():
        ys_ref[...] = jnp.zeros_like(ys_ref)


def _experts(te, n_used, xs, w_gate, w_up, w_down, n_max, layer):
    rows = lambda j, te, nu: (jnp.minimum(j, nu[0] - 1), 0)
    wsel = lambda j, te, nu: (layer * N_EXPERTS + te[j], 0, 0)
    return pl.pallas_call(
        _experts_kernel,
        grid_spec=pltpu.PrefetchScalarGridSpec(
            num_scalar_prefetch=2,
            grid=(n_max,),
            in_specs=[pl.BlockSpec((MOE_TM * ROW_SLABS, LANES), rows),
                      pl.BlockSpec((1, D_MODEL, D_EXPERT), wsel),
                      pl.BlockSpec((1, D_MODEL, D_EXPERT), wsel),
                      pl.BlockSpec((1, D_EXPERT, D_MODEL), wsel)],
            out_specs=pl.BlockSpec((MOE_TM * ROW_SLABS, LANES), lambda j, te, nu: (j, 0))),
        out_shape=jax.ShapeDtypeStruct((n_max * MOE_TM * ROW_SLABS, LANES), F32),
        compiler_params=_cparams(1, 48),
        name="moe_experts",
    )(te, n_used, xs, w_gate, w_up, w_down)


def _combine_kernel(pos_ref, ys_ref, x_ref, rt_ref, mod_ref, o_ref, ybuf0, ybuf1, sem):
    def row(ref, r):
        return ref.at[pl.ds(pl.multiple_of(r * ROW_SLABS, ROW_SLABS), ROW_SLABS), :]

    def body(r, carry):
        pltpu.make_async_copy(row(ys_ref, pos_ref[0, 0, 2 * r]), row(ybuf0, r), sem).start(priority=0)
        pltpu.make_async_copy(row(ys_ref, pos_ref[0, 0, 2 * r + 1]), row(ybuf1, r), sem).start(priority=1)
        return carry

    lax.fori_loop(0, TILE, body, 0, unroll=8)
    for buf in (ybuf0, ybuf1):
        pltpu.make_async_copy(ys_ref.at[pl.ds(0, TILE * ROW_SLABS), :], buf, sem).wait()
    rt = rt_ref[...]
    f = rt[:, 2:3] * _load_row_tiles(ybuf0, TILE) + rt[:, 3:4] * _load_row_tiles(ybuf1, TILE)
    o_ref[...] = x_ref[...] + mod_ref[0, 0][5:6] * f


def _combine(pos_s, ys, xmid, rt, mod, n_batch, n_tiles, layer):
    n_lat_tiles = n_batch * SEQ // TILE
    tpb = SEQ // TILE
    row = lambda i: (i, 0)
    modrow = lambda i: (layer, _tile_mod_row(i, n_lat_tiles, tpb, n_batch), 0, 0)
    return pl.pallas_call(
        _combine_kernel,
        grid=(n_tiles,),
        in_specs=[pl.BlockSpec((1, 1, 2 * TILE), lambda i: (i, 0, 0), memory_space=pltpu.SMEM),
                  pl.BlockSpec(memory_space=pl.ANY),
                  pl.BlockSpec((TILE, D_MODEL), row),
                  pl.BlockSpec((TILE, LANES), row),
                  pl.BlockSpec((1, 1, N_MOD, D_MODEL), modrow)],
        out_specs=pl.BlockSpec((TILE, D_MODEL), row),
        out_shape=jax.ShapeDtypeStruct((n_tiles * TILE, D_MODEL), F32),
        scratch_shapes=[pltpu.VMEM((TILE * ROW_SLABS, LANES), F32), pltpu.VMEM((TILE * ROW_SLABS, LANES), F32),
                        pltpu.SemaphoreType.DMA(())],
        compiler_params=_cparams(1, 40),
        name="moe_combine",
    )(pos_s, ys, xmid, rt, mod)


def _moe(xmid, tok, rt, cnt, mod, w_gate, w_up, w_down, n_batch, n_tiles, layer):
    base, te, n_used, tail = _moe_plan(cnt, n_tiles)
    n_max = _moe_max_tiles(n_tiles * TILE)
    pos = _positions(rt, base, n_tiles)
    pos_s = pos[:, :2].reshape(n_tiles, 1, 2 * TILE)
    xs = _dispatch(tail, n_used, pos_s, tok, n_tiles, n_max)
    ys = _experts(te, n_used, xs, w_gate, w_up, w_down, n_max, layer)
    return _combine(pos_s, ys, xmid, rt, mod, n_batch, n_tiles, layer)


def _final_norm_kernel(x_ref, g_ref, o_ref):
    x = x_ref[...]
    o_ref[...] = x * lax.rsqrt(jnp.mean(x * x, axis=-1, keepdims=True) + RMS_EPS) * g_ref[...]


def _final_norm(xc, g, n_rows):
    return pl.pallas_call(
        _final_norm_kernel,
        grid=(n_rows // TILE,),
        in_specs=[pl.BlockSpec((TILE, D_MODEL), lambda i: (i, 0)),
                  pl.BlockSpec((1, D_MODEL), lambda i: (0, 0))],
        out_specs=pl.BlockSpec((TILE, D_MODEL), lambda i: (i, 0)),
        out_shape=jax.ShapeDtypeStruct((n_rows, D_MODEL), F32),
        compiler_params=_cparams(1, 32),
        name="final_norm",
    )(xc, g)


def _rope_tables():
    t = jnp.arange(SEQ)
    rows_pos = (t // GRID_W).astype(F32)
    cols_pos = (t % GRID_W).astype(F32)
    half = HEAD_DIM // 2
    inv = 1.0 / (ROPE_BASE ** (jnp.arange(0, half, 2, dtype=F32) / half))
    ang_r = rows_pos[:, None] * inv[None, :]
    ang_c = cols_pos[:, None] * inv[None, :]
    ang = jnp.concatenate([ang_r, ang_r, ang_c, ang_c], axis=1)
    cos_h, sin_h = jnp.cos(ang), jnp.sin(ang)
    scale = jnp.concatenate([jnp.full((256,), HEAD_DIM ** -0.5, F32), jnp.ones((128,), F32)])
    cos_t = jnp.tile(cos_h, (1, 6)) * scale
    sin_t = jnp.tile(sin_h, (1, 6)) * scale
    cos_t = jnp.concatenate([cos_t, jnp.broadcast_to(scale, (TILE, 384))], axis=0)
    sin_t = jnp.concatenate([sin_t, jnp.zeros((TILE, 384), F32)], axis=0)
    return cos_t, sin_t


def _fused_in_weight(w_in):
    wq, wk, wv = w_in[..., 0:256], w_in[..., 256:384], w_in[..., 384:512]
    o = A_IN
    wz, wxbc, wdt = w_in[..., o:o + 512], w_in[..., o + 512:o + 1280], w_in[..., o + 1280:o + 1296]
    wna = w_in[..., A_IN + SSD_IN:]

    def rot(w):
        w4 = w.reshape(w.shape[:-1] + (w.shape[-1] // 32, 2, 16))
        return jnp.concatenate([-w4[..., 1:2, :], w4[..., 0:1, :]], axis=-2).reshape(w.shape)

    na_scale = jnp.concatenate([jnp.full((256,), HEAD_DIM ** -0.5, F32), jnp.ones((512,), F32)])
    pad = jnp.zeros(w_in.shape[:-1] + (128 - SSD_HEADS,), F32)
    cat = jnp.concatenate([wq, wk, rot(wq), rot(wk), wv, wz, wxbc,
                           wna * na_scale, wdt[..., :SSD_HEADS], pad, wdt[..., SSD_HEADS:], pad], axis=-1)
    return cat.astype(BF16)


def _router_weight(w_rg, b_rg, w_re, b_re):
    w = jnp.concatenate([w_re, w_rg, jnp.zeros((D_MODEL, LANES - N_EXPERTS - MOE_GROUPS), F32)], axis=1)
    b = jnp.concatenate([b_re, b_rg, jnp.zeros((LANES - N_EXPERTS - MOE_GROUPS,), F32)]).reshape(1, LANES)
    w1 = w.astype(BF16)
    w2 = (w - w1.astype(F32)).astype(BF16)
    return w1, w2, b


def kernel(x, c, ctx, c_ctx, w_mod, b_mod, g_mix, w_in, attn_sink, ssd_conv_w, ssd_conv_b, ssd_dt_bias, ssd_a_log, ssd_d, ssd_norm_g, na_rpb, w_out, g_ffn, w_router_group, b_router_group, w_router_expert, b_router_expert, w_exp_gate, w_exp_up, w_exp_down, g_final):
    n_batch, s, d = x.shape
    assert (s, d) == (SEQ, D_MODEL) and ctx.shape[1:] == (CTX_LEN, D_MODEL) and n_batch < 16
    n_lat = n_batch * SEQ
    n_ctx = n_batch * CTX_LEN
    assert n_ctx % TILE == 0
    n_lat_tiles = n_lat // TILE
    n_all_tiles = (n_lat + n_ctx) // TILE

    xc = jnp.concatenate([x.reshape(n_lat, d), ctx.reshape(n_ctx, d)], axis=0)
    cin = jnp.zeros((16, d), F32).at[:n_batch].set(c).at[n_batch].set(c_ctx)
    mod = _modulation(cin, w_mod, b_mod).reshape(DEPTH, 16, N_MOD, d)
    cos_t, sin_t = _rope_tables()
    w_cat = _fused_in_weight(w_in)
    w_out_b = w_out.astype(BF16)
    bias_t = _na_bias_table(na_rpb)
    g_mix3 = g_mix.reshape(DEPTH, 1, d)
    w_gate = w_exp_gate.reshape(DEPTH * N_EXPERTS, D_MODEL, D_EXPERT)
    w_up = w_exp_up.reshape(DEPTH * N_EXPERTS, D_MODEL, D_EXPERT)
    w_down = w_exp_down.reshape(DEPTH * N_EXPERTS, D_EXPERT, D_MODEL)

    for layer in range(DEPTH):
        need_ctx = layer < DEPTH - 1
        qk, v, z, xbc, na, dt_raw = _inproj(xc, mod, g_mix3, w_cat, cos_t, sin_t, n_batch, layer)
        sink = attn_sink[layer].astype(F32)
        oa = _window_attn(sink, qk, v, n_batch)
        xbc_act = _conv_silu(xbc, jnp.zeros((8, SSD_CONV_DIM), F32).at[:SSD_CONV].set(ssd_conv_w[layer]),
                             ssd_conv_b[layer].reshape(1, SSD_CONV_DIM), n_batch)
        y_f, y_b = _ssd_scan(xbc_act, dt_raw, ssd_dt_bias[layer], ssd_a_log[layer], n_batch)
        on = _neighborhood_attn(na, bias_t, n_batch, layer)
        oa_c, on_c = _ctx_attn(sink, qk, v, na, n_batch) if need_ctx else (oa, on)
        n_tiles = n_all_tiles if need_ctx else n_lat_tiles
        wr1, wr2, br = _router_weight(w_router_group[layer], b_router_group[layer],
                                      w_router_expert[layer], b_router_expert[layer])
        dskip = jnp.repeat(ssd_d[layer].astype(F32), SSD_INNER // SSD_HEADS).reshape(1, SSD_INNER)
        xmid, tok, rt, cnt = _outproj(xc, oa, oa_c, y_f, y_b, xbc_act, z, on, on_c, mod, dskip,
                                      ssd_norm_g[layer].reshape(1, SSD_INNER), w_out_b,
                                      g_ffn[layer].reshape(1, d), wr1, wr2, br, n_batch, n_tiles, layer)
        xc = _moe(xmid, tok, rt, cnt, mod, w_gate, w_up, w_down, n_batch, n_tiles, layer)

    return _final_norm(xc, g_final.reshape(1, d), n_lat).reshape(n_batch, SEQ, d)
```

```python
import functools
import math

import jax
import jax.numpy as jnp
import numpy as np
from jax import lax
from jax.experimental import pallas as pl
from jax.experimental.pallas import tpu as pltpu

F32 = jnp.float32
BF16 = jnp.bfloat16

D_MODEL = 1024
SEQ = 2048
DEPTH = 4
GRID_W = 64
GRID_ROWS = SEQ // GRID_W
CTX_LEN = 256
HEAD_DIM = 64
A_HEADS = 4
A_KV_HEADS = 2
A_BLOCK = 128
ROPE_BASE = 10000.0
SSD_HEADS = 8
SSD_INNER = 512
SSD_STATE = 64
SSD_CONV = 5
SSD_CHUNK = 128
SSD_CONV_DIM = 768
NA_HEADS = 4
NA_WIN_ROWS = 8
NA_WIN_COLS = 16
A_IN = 512
SSD_IN = 1296
MOE_GROUPS = 4
MOE_EXPERTS = 8
N_EXPERTS = MOE_GROUPS * MOE_EXPERTS
D_EXPERT = 256
N_MOD = 6
RMS_EPS = 1e-6
NEG_INF = -1e30

TILE = 512
LANES = 128
C_QK, C_QKP, C_V, C_Z, C_XBC, C_NA, C_DT = 0, 384, 768, 896, 1408, 2176, 2944
N_COLS = 3200
NA_Q_ROWS = 4
NA_K_ROWS = NA_WIN_ROWS + NA_Q_ROWS
NA_BIAS_OFF = NA_Q_ROWS
NA_BIAS_N = NA_BIAS_OFF + (NA_K_ROWS - 2) + (NA_WIN_ROWS - 1) + 1


def _cparams(n_axes, vmem_mb):
    return pltpu.CompilerParams(dimension_semantics=("arbitrary",) * n_axes,
                                vmem_limit_bytes=vmem_mb << 20)


def _split3(x):
    h1 = x.astype(BF16)
    r1 = x - h1.astype(F32)
    h2 = r1.astype(BF16)
    h3 = (r1 - h2.astype(F32)).astype(BF16)
    return h1, h2, h3


def _dot(a, b):
    return jnp.dot(a, b, preferred_element_type=F32)


def _dot_nt(a, b):
    return lax.dot_general(a, b, (((1,), (1,)), ((), ())), preferred_element_type=F32)


def _dot_exact_lhs(lhs_bf16, x):
    return _dot(jnp.concatenate([lhs_bf16] * 3, axis=1), jnp.concatenate(_split3(x), axis=0))


def _dot_exact_rhs(x, rhs3_bf16):
    return _dot(jnp.concatenate(_split3(x), axis=1), rhs3_bf16)


def _silu(x):
    return x * jax.nn.sigmoid(x)


ROW_SLABS = D_MODEL // LANES


def _store_row_tiles(ref, x):
    n = x.shape[0]
    for s in range(ROW_SLABS):
        ref[pl.ds(s, n, stride=ROW_SLABS), :] = x[:, s * LANES:(s + 1) * LANES]


def _load_row_tiles(ref, n):
    return jnp.concatenate([ref[pl.ds(s, n, stride=ROW_SLABS), :] for s in range(ROW_SLABS)], axis=1)


def _iota(shape, dim):
    return lax.broadcasted_iota(jnp.int32, shape, dim)


def _mod_kernel(c_ref, w_ref, b_ref, o_ref):
    a = _silu(c_ref[...])
    a1, a2, _ = _split3(a)
    w = w_ref[0]
    w1 = w.astype(BF16)
    w2 = (w - w1.astype(F32)).astype(BF16)
    o_ref[0] = _dot(a1, w1) + _dot(a1, w2) + _dot(a2, w1) + b_ref[0]


def _modulation(cin, w_mod, b_mod):
    nt = 1024
    return pl.pallas_call(
        _mod_kernel,
        grid=(DEPTH, N_MOD * D_MODEL // nt),
        in_specs=[pl.BlockSpec((16, D_MODEL), lambda l, j: (0, 0)),
                  pl.BlockSpec((1, D_MODEL, nt), lambda l, j: (l, 0, j)),
                  pl.BlockSpec((1, 1, nt), lambda l, j: (l, 0, j))],
        out_specs=pl.BlockSpec((1, 16, nt), lambda l, j: (l, 0, j)),
        out_shape=jax.ShapeDtypeStruct((DEPTH, 16, N_MOD * D_MODEL), F32),
        compiler_params=_cparams(2, 40),
        name="modulation",
    )(cin, w_mod, b_mod.reshape(DEPTH, 1, N_MOD * D_MODEL))


def _inproj_kernel(prev_ref, x_ref, next_ref, mod_ref, g_ref, w_ref, cos_ref, sin_ref, cw_ref, cb_ref,
                   qk_ref, v_ref, z_ref, xbc_ref, na_ref, dt_ref, *, n_lat_tiles, tiles_per_batch):
    xe = jnp.concatenate([prev_ref[...], x_ref[...], next_ref[...]], axis=0)
    m = mod_ref[0, 0]
    he = xe * lax.rsqrt(jnp.mean(xe * xe, axis=-1, keepdims=True) + RMS_EPS) * g_ref[0]
    hbe = (he * (1.0 + m[1:2]) + m[0:1]).astype(BF16)
    hb = hbe[8:8 + TILE]

    def mm(lo, hi):
        return _dot(hb, w_ref[0, :, lo:hi])

    qk = mm(C_QK, C_QKP) * cos_ref[...] + mm(C_QKP, C_V) * sin_ref[...]
    qk_ref[...] = qk.astype(BF16)
    v_ref[...] = mm(C_V, C_Z).astype(BF16)
    z_ref[...] = mm(C_Z, C_XBC)
    na_ref[...] = mm(C_NA, C_DT).astype(BF16)
    dt_ref[...] = mm(C_DT, N_COLS)

    i = pl.program_id(0)
    is_lat = i < n_lat_tiles
    r = _iota((TILE, 1), 0)
    pos = jnp.where(is_lat, (i % tiles_per_batch) * TILE + r, r & (CTX_LEN - 1))
    seq_len = jnp.where(is_lat, SEQ, CTX_LEN)
    xbc_e = _dot(hbe, w_ref[0, :, C_XBC:C_NA])
    cw = cw_ref[...]
    acc = jnp.zeros((TILE, SSD_CONV_DIM), F32) + cb_ref[...]
    for k in range(SSD_CONV):
        off = k - SSD_CONV // 2
        tap = xbc_e[8 + off:8 + off + TILE, :]
        if off != 0:
            tap = jnp.where(jnp.logical_and(pos + off >= 0, pos + off < seq_len), tap, 0.0)
        acc = acc + tap * cw[k:k + 1, :]
    xbc_ref[...] = _silu(acc)


def _tile_mod_row(i, n_lat_tiles, tiles_per_batch, n_batch):
    return jnp.where(i < n_lat_tiles, i // tiles_per_batch, n_batch)


def _inproj(xc, mod, g_mix, w_cat, cos_t, sin_t, conv_w, conv_b, n_batch, layer):
    nt = xc.shape[0]
    n_lat_tiles = n_batch * SEQ // TILE
    tpb = SEQ // TILE
    t8 = TILE // 8
    row = lambda i: (i, 0)
    modrow = lambda i: (layer, _tile_mod_row(i, n_lat_tiles, tpb, n_batch), 0, 0)
    posrow = lambda i: (jnp.where(i < n_lat_tiles, i % tpb, tpb), 0)
    lay = lambda i: (layer, 0, 0)
    const = lambda i: (0, 0)
    outs = [(384, BF16), (128, BF16), (512, F32), (768, F32), (768, BF16), (256, F32)]
    kern = functools.partial(_inproj_kernel, n_lat_tiles=n_lat_tiles, tiles_per_batch=tpb)
    return pl.pallas_call(
        kern,
        grid=(nt // TILE,),
        in_specs=[pl.BlockSpec((8, D_MODEL), lambda i: (jnp.maximum(i * t8 - 1, 0), 0)),
                  pl.BlockSpec((TILE, D_MODEL), row),
                  pl.BlockSpec((8, D_MODEL), lambda i: (jnp.minimum(i * t8 + t8, nt // 8 - 1), 0)),
                  pl.BlockSpec((1, 1, N_MOD, D_MODEL), modrow),
                  pl.BlockSpec((1, 1, D_MODEL), lay),
                  pl.BlockSpec((1, D_MODEL, N_COLS), lay),
                  pl.BlockSpec((TILE, 384), posrow),
                  pl.BlockSpec((TILE, 384), posrow),
                  pl.BlockSpec((8, SSD_CONV_DIM), const),
                  pl.BlockSpec((1, SSD_CONV_DIM), const)],
        out_specs=[pl.BlockSpec((TILE, w), row) for w, _ in outs],
        out_shape=[jax.ShapeDtypeStruct((nt, w), dt) for w, dt in outs],
        compiler_params=_cparams(1, 56),
        name="inproj",
    )(xc, xc, xc, mod, g_mix, w_cat, cos_t, sin_t, conv_w, conv_b)


def _ssd_kernel(xf_ref, dtf_ref, xb_ref, dtb_ref, bias_ref, alog_ref, e512_ref, e1024_ref,
                yf_ref, yb_ref, stf_ref, stb_ref):
    @pl.when(pl.program_id(1) == 0)
    def _():
        stf_ref[...] = jnp.zeros_like(stf_ref)
        stb_ref[...] = jnp.zeros_like(stb_ref)

    _ssd_chunk(0, xf_ref, dtf_ref, bias_ref[0], alog_ref[0], e512_ref, e1024_ref, yf_ref, stf_ref)
    _ssd_chunk(1, xb_ref, dtb_ref, bias_ref[1], alog_ref[1], e512_ref, e1024_ref, yb_ref, stb_ref)


def _ssd_chunk(d, xbc_ref, dt_ref, dt_bias, a_log, e512_ref, e1024_ref, y_ref, st_ref):
    q = SSD_CHUNK
    xbc = xbc_ref[...]
    xs = xbc[:, :SSD_INNER]
    bm = xbc[:, SSD_INNER:SSD_INNER + 128]
    cm = xbc[:, SSD_INNER + 128:]
    dtr = dt_ref[...] + dt_bias
    dt = jnp.maximum(dtr, 0.0) + jnp.log1p(jnp.exp(-jnp.abs(dtr)))
    a = -jnp.exp(a_log)
    da = dt * a

    ri = _iota((q, q), 0)
    ci = _iota((q, q), 1)
    tri = (ri >= ci) if d == 0 else (ri <= ci)
    trib = jnp.where(tri, 1.0, 0.0).astype(BF16)
    acs = _dot_exact_lhs(trib, da)
    acs_t = acs.T

    both_e = _dot_exact_rhs(jnp.concatenate([dt, acs], axis=0), e512_ref[...])
    dt_e = both_e[:q]
    acs_e = both_e[q:]
    acs_e2 = _dot_exact_rhs(acs, e1024_ref[...])
    tot_e = acs_e[q - 1:q, :] if d == 0 else acs_e[0:1, :]

    xdt = xs * dt_e
    xdec = (xdt * jnp.exp(tot_e - acs_e)).astype(BF16)
    btb = bm.T.astype(BF16)
    lane = _iota((q, 128), 1)
    cm0 = jnp.where(lane < SSD_STATE, cm, 0.0).astype(BF16)
    cm1 = jnp.where(lane >= SSD_STATE, cm, 0.0).astype(BF16)
    cbs = (_dot(cm0, btb), _dot(cm1, btb))

    st = st_ref[...]
    y_off = _dot(cm.astype(BF16), st.astype(BF16)) * jnp.exp(acs_e)
    s_all = _dot(btb, xdec)
    same = (_iota((q, SSD_INNER), 0) >> 6) == (_iota((q, SSD_INNER), 1) >> 8)
    st_ref[...] = jnp.where(same, st * jnp.exp(tot_e) + s_all, 0.0)

    for pair in range(SSD_HEADS // 2):
        cb = cbs[pair // 2]
        xp = xdt[:, pair * 128:(pair + 1) * 128]
        acc = None
        for k in range(2):
            h = 2 * pair + k
            seg = acs_e2[:, h * 128:(h + 1) * 128] - acs_t[h:h + 1, :]
            lmat = jnp.exp(jnp.where(tri, seg, NEG_INF))
            g = (cb * lmat).astype(BF16)
            rhs = jnp.where((lane < 64) if k == 0 else (lane >= 64), xp, 0.0).astype(BF16)
            t = _dot(g, rhs)
            acc = t if acc is None else acc + t
        y_ref[:, pair * 128:(pair + 1) * 128] = acc + y_off[:, pair * 128:(pair + 1) * 128]


def _ssd_scan(xbc_act, dt_raw, dt_bias, a_log, n_batch):
    nt = xbc_act.shape[0]
    n_lat_blk = n_batch * (SEQ // SSD_CHUNK)
    lat_c = SEQ // SSD_CHUNK
    ctx_c = CTX_LEN // SSD_CHUNK
    n_steps = lat_c + ctx_c

    def blk_f(b, c):
        return jnp.where(c < ctx_c, n_lat_blk + b * ctx_c + c, b * lat_c + c - ctx_c)

    def blk_b(b, c):
        return jnp.where(c < ctx_c, n_lat_blk + b * ctx_c + ctx_c - 1 - c, b * lat_c + n_steps - 1 - c)

    heads = np.arange(128)
    e512 = (heads[:, None] == (np.arange(512)[None, :] // 64)).astype(np.float32)
    e1024 = (heads[:, None] == (np.arange(1024)[None, :] // 128)).astype(np.float32)
    dtb = jnp.zeros((2, 1, 128), F32).at[:, 0, :SSD_HEADS].set(dt_bias)
    alog = jnp.zeros((2, 1, 128), F32).at[:, 0, :SSD_HEADS].set(a_log)
    const3 = lambda b, c: (0, 0, 0)
    const2 = lambda b, c: (0, 0)
    return pl.pallas_call(
        _ssd_kernel,
        grid=(n_batch, n_steps),
        in_specs=[pl.BlockSpec((SSD_CHUNK, SSD_CONV_DIM), lambda b, c: (blk_f(b, c), 0)),
                  pl.BlockSpec((SSD_CHUNK, 128), lambda b, c: (blk_f(b, c), 0)),
                  pl.BlockSpec((SSD_CHUNK, SSD_CONV_DIM), lambda b, c: (blk_b(b, c), 0)),
                  pl.BlockSpec((SSD_CHUNK, 128), lambda b, c: (blk_b(b, c), 1)),
                  pl.BlockSpec((2, 1, 128), const3),
                  pl.BlockSpec((2, 1, 128), const3),
                  pl.BlockSpec((384, 512), const2),
                  pl.BlockSpec((384, 1024), const2)],
        out_specs=[pl.BlockSpec((SSD_CHUNK, SSD_INNER), lambda b, c: (blk_f(b, c), 0)),
                   pl.BlockSpec((SSD_CHUNK, SSD_INNER), lambda b, c: (blk_b(b, c), 0))],
        out_shape=[jax.ShapeDtypeStruct((nt, SSD_INNER), F32)] * 2,
        scratch_shapes=[pltpu.VMEM((128, SSD_INNER), F32)] * 2,
        compiler_params=_cparams(2, 32),
        name="ssd_scan",
    )(xbc_act, dt_raw, xbc_act, dt_raw, dtb, alog, jnp.asarray(np.tile(e512, (3, 1)), BF16),
      jnp.asarray(np.tile(e1024, (3, 1)), BF16))


def _softmax_pv(s_list, v_list, extra_logit=None):
    m = s_list[0].max(axis=-1, keepdims=True)
    for s in s_list[1:]:
        m = jnp.maximum(m, s.max(axis=-1, keepdims=True))
    if extra_logit is not None:
        m = jnp.maximum(m, extra_logit)
    den = None
    o = None
    for s, v in zip(s_list, v_list):
        p = jnp.exp(s - m)
        ps = p.sum(axis=-1, keepdims=True)
        den = ps if den is None else den + ps
        t = _dot(p.astype(BF16), v)
        o = t if o is None else o + t
    if extra_logit is not None:
        den = den + jnp.exp(extra_logit - m)
    return o / den


A_QB = 2
A_KB = A_QB + 2


def _wattn_kernel(sink_ref, q_ref, k_ref, v_ref, kc_ref, vc_ref, o_ref):
    n = pl.program_id(1) * A_QB
    nb = SEQ // A_BLOCK
    nq = A_QB * A_BLOCK
    nk = A_KB * A_BLOCK
    start = pl.multiple_of(jnp.clip(n - 1, 0, nb - A_KB) * A_BLOCK, A_BLOCK)
    q = q_ref[:, 0:256]
    kw = k_ref[pl.ds(start, nk), 256:384]
    vw = v_ref[pl.ds(start, nk), :]
    kc = kc_ref[:, 256:384]
    vc = vc_ref[...]
    qrow = _iota((2 * nq, nk), 0)
    qpos = n * A_BLOCK + jnp.where(qrow < nq, qrow, qrow - nq)
    kpos = start + _iota((2 * nq, nk), 1)
    valid = jnp.abs(qpos - kpos) <= A_BLOCK
    top = _iota((2 * nq, 1), 0) < nq
    outs = []
    for g in range(A_KV_HEADS):
        qg = jnp.concatenate([q[:, (2 * g) * 64:(2 * g + 1) * 64],
                              q[:, (2 * g + 1) * 64:(2 * g + 2) * 64]], axis=0)
        kg = kw[:, g * 64:(g + 1) * 64]
        vg = vw[:, g * 64:(g + 1) * 64]
        s_loc = jnp.where(valid, _dot_nt(qg, kg), NEG_INF)
        s_ctx = _dot_nt(qg, kc[:, g * 64:(g + 1) * 64])
        sink = jnp.where(top, sink_ref[2 * g], sink_ref[2 * g + 1])
        o = _softmax_pv([s_loc, s_ctx], [vg, vc[:, g * 64:(g + 1) * 64]], sink)
        outs += [o[:nq], o[nq:]]
    o_ref[...] = jnp.concatenate(outs, axis=1).astype(BF16)


def _window_attn(sink, qk, v, n_batch):
    steps = SEQ // (A_QB * A_BLOCK)
    nq = A_QB * A_BLOCK
    ctx0 = n_batch * SEQ // CTX_LEN
    return pl.pallas_call(
        _wattn_kernel,
        grid=(n_batch, steps),
        in_specs=[pl.BlockSpec(memory_space=pltpu.SMEM),
                  pl.BlockSpec((nq, 384), lambda b, n: (b * steps + n, 0)),
                  pl.BlockSpec((SEQ, 384), lambda b, n: (b, 0)),
                  pl.BlockSpec((SEQ, 128), lambda b, n: (b, 0)),
                  pl.BlockSpec((CTX_LEN, 384), lambda b, n: (ctx0 + b, 0)),
                  pl.BlockSpec((CTX_LEN, 128), lambda b, n: (ctx0 + b, 0))],
        out_specs=pl.BlockSpec((nq, 256), lambda b, n: (b * steps + n, 0)),
        out_shape=jax.ShapeDtypeStruct((n_batch * SEQ, 256), BF16),
        compiler_params=_cparams(2, 32),
        name="window_attn",
    )(sink, qk, qk, v, qk, v)


def _nattn_kernel(q_ref, kv_ref, c_ref, bias_ref, o_ref):
    i = pl.program_id(1)
    r0 = i * NA_Q_ROWS
    srow = jnp.clip(r0 - NA_WIN_ROWS // 2, 0, GRID_ROWS - NA_K_ROWS)
    start = pl.multiple_of(srow * GRID_W, GRID_W)
    nq = NA_Q_ROWS * GRID_W
    nk = NA_K_ROWS * GRID_W
    q = q_ref[:, 0:256]
    kw = kv_ref[pl.ds(start, nk), 256:512]
    vw = kv_ref[pl.ds(start, nk), 512:768]
    kc = c_ref[:, 256:512]
    vc = c_ref[:, 512:768]
    qrow = r0 + (_iota((nq, nk), 0) >> 6)
    krow = srow + (_iota((nq, nk), 1) >> 6)
    rs = jnp.clip(qrow - NA_WIN_ROWS // 2, 0, GRID_ROWS - NA_WIN_ROWS)
    valid = jnp.logical_and(krow >= rs, krow < rs + NA_WIN_ROWS)
    outs = []
    for h in range(NA_HEADS):
        sl = slice(h * 64, (h + 1) * 64)
        rows = []
        for qi in range(NA_Q_ROWS):
            blocks = []
            for p in range(NA_K_ROWS // 2):
                idx = srow + 2 * p - (r0 + qi) + (NA_WIN_ROWS - 1) + NA_BIAS_OFF
                blocks.append(bias_ref[0, h, idx])
            rows.append(jnp.concatenate(blocks, axis=1))
        bias = jnp.concatenate(rows, axis=0)
        s_loc = jnp.where(valid, _dot_nt(q[:, sl], kw[:, sl]) + bias, NEG_INF)
        s_ctx = _dot_nt(q[:, sl], kc[:, sl])
        outs.append(_softmax_pv([s_loc, s_ctx], [vw[:, sl], vc[:, sl]]))
    o_ref[...] = jnp.concatenate(outs, axis=1).astype(BF16)


def _na_bias_table(rpb):
    cq = np.arange(GRID_W)
    kcol = np.arange(GRID_W)
    cs = np.clip(cq - NA_WIN_COLS // 2, 0, GRID_W - NA_WIN_COLS)
    col_valid = (kcol[None, :] >= cs[:, None]) & (kcol[None, :] < cs[:, None] + NA_WIN_COLS)
    coff = np.clip(kcol[None, :] - cq[:, None], -(NA_WIN_COLS - 1), NA_WIN_COLS - 1) + (NA_WIN_COLS - 1)
    n_a = 2 * NA_WIN_ROWS - 1
    n_c = 2 * NA_WIN_COLS - 1
    pick = (np.arange(n_c)[:, None] == coff.reshape(1, -1)).astype(np.float32)
    tm = jnp.einsum("lhak,kn->lhan", rpb.astype(F32), jnp.asarray(pick), precision=lax.Precision.HIGHEST)
    tm = jnp.where(col_valid, tm.reshape(rpb.shape[:3] + (GRID_W, GRID_W)), NEG_INF)
    neg = jnp.full(rpb.shape[:2] + (1, GRID_W, GRID_W), NEG_INF, F32)
    pad_lo = NA_BIAS_OFF
    pad_hi = NA_BIAS_N + 1 - pad_lo - n_a
    ext = jnp.concatenate([neg] * pad_lo + [tm] + [neg] * pad_hi, axis=2)
    return jnp.concatenate([ext[:, :, :NA_BIAS_N], ext[:, :, 1:NA_BIAS_N + 1]], axis=-1)


def _neighborhood_attn(na, bias_t, n_batch, layer):
    steps = GRID_ROWS // NA_Q_ROWS
    nq = NA_Q_ROWS * GRID_W
    ctx0 = n_batch * SEQ // CTX_LEN
    return pl.pallas_call(
        _nattn_kernel,
        grid=(n_batch, steps),
        in_specs=[pl.BlockSpec((nq, 768), lambda b, i: (b * steps + i, 0)),
                  pl.BlockSpec((SEQ, 768), lambda b, i: (b, 0)),
                  pl.BlockSpec((CTX_LEN, 768), lambda b, i: (ctx0 + b, 0)),
                  pl.BlockSpec((1, NA_HEADS, NA_BIAS_N, GRID_W, 128), lambda b, i: (layer, 0, 0, 0, 0))],
        out_specs=pl.BlockSpec((nq, 256), lambda b, i: (b * steps + i, 0)),
        out_shape=jax.ShapeDtypeStruct((n_batch * SEQ, 256), BF16),
        compiler_params=_cparams(2, 40),
        name="neighborhood_attn",
    )(na, na, na, bias_t)


def _ctx_attn_kernel(sink_ref, qk_ref, v_ref, na_ref, oa_ref, on_ref):
    qk = qk_ref[...]
    v = v_ref[...]
    na = na_ref[...]
    outs = []
    for h in range(A_HEADS):
        g = h // (A_HEADS // A_KV_HEADS)
        s = _dot_nt(qk[:, h * 64:(h + 1) * 64], qk[:, 256 + g * 64:256 + (g + 1) * 64])
        sink = jnp.zeros((CTX_LEN, 1), F32) + sink_ref[h]
        outs.append(_softmax_pv([s], [v[:, g * 64:(g + 1) * 64]], sink))
    oa_ref[...] = jnp.concatenate(outs, axis=1).astype(BF16)
    outs = []
    for h in range(NA_HEADS):
        sl = slice(h * 64, (h + 1) * 64)
        s = _dot_nt(na[:, 0:256][:, sl], na[:, 256:512][:, sl])
        outs.append(_softmax_pv([s], [na[:, 512:768][:, sl]]))
    on_ref[...] = jnp.concatenate(outs, axis=1).astype(BF16)


def _ctx_attn(sink, qk, v, na, n_batch):
    ctx0 = n_batch * SEQ // CTX_LEN
    row = lambda b: (ctx0 + b, 0)
    return pl.pallas_call(
        _ctx_attn_kernel,
        grid=(n_batch,),
        in_specs=[pl.BlockSpec(memory_space=pltpu.SMEM),
                  pl.BlockSpec((CTX_LEN, 384), row),
                  pl.BlockSpec((CTX_LEN, 128), row),
                  pl.BlockSpec((CTX_LEN, 768), row)],
        out_specs=[pl.BlockSpec((CTX_LEN, 256), lambda b: (b, 0))] * 2,
        out_shape=[jax.ShapeDtypeStruct((n_batch * CTX_LEN, 256), BF16)] * 2,
        compiler_params=_cparams(1, 32),
        name="ctx_attn",
    )(sink, qk, v, na)


def _outproj_kernel(x_ref, oal_ref, oac_ref, yf_ref, yb_ref, xbc_ref, z_ref, onl_ref, onc_ref, mod_ref,
                    dskip_ref, ng_ref, w_ref, gf_ref, wr1_ref, wr2_ref, br_ref, xo_ref, tok_ref, rt_ref,
                    cnt_ref, *, n_lat_tiles):
    is_lat = pl.program_id(0) < n_lat_tiles
    m = mod_ref[0, 0]
    xs = xbc_ref[...]
    y = yf_ref[...] + yb_ref[...] + dskip_ref[...] * xs
    y = y * _silu(z_ref[...])
    ob = y * lax.rsqrt(jnp.mean(y * y, axis=-1, keepdims=True) + RMS_EPS) * ng_ref[...]
    oa = jnp.where(is_lat, oal_ref[...], oac_ref[...])
    on = jnp.where(is_lat, onl_ref[...], onc_ref[...])
    proj = (_dot(oa, w_ref[0, 0:256, :]) + _dot(ob.astype(BF16), w_ref[0, 256:768, :])
            + _dot(on, w_ref[0, 768:1024, :]))
    x = x_ref[...] + m[2:3] * proj
    xo_ref[...] = x
    t = x * lax.rsqrt(jnp.mean(x * x, axis=-1, keepdims=True) + RMS_EPS) * gf_ref[...]
    t = t * (1.0 + m[4:5]) + m[3:4]
    _store_row_tiles(tok_ref, t)
    t1 = t.astype(BF16)
    t2 = (t - t1.astype(F32)).astype(BF16)
    logits = _dot(t1, wr1_ref[...]) + _dot(t1, wr2_ref[...]) + _dot(t2, wr1_ref[...]) + br_ref[...]

    lane = _iota(logits.shape, 1)
    big = jnp.int32(1 << 20)
    is_g = jnp.logical_and(lane >= N_EXPERTS, lane < N_EXPERTS + MOE_GROUPS)
    gl = jnp.where(is_g, logits, NEG_INF)
    gmax = gl.max(axis=-1, keepdims=True)
    g_w = 1.0 / jnp.exp(gl - gmax).sum(axis=-1, keepdims=True)
    g_idx = jnp.where(gl == gmax, lane, big).min(axis=-1, keepdims=True) - N_EXPERTS
    in_grp = jnp.logical_and(lane < N_EXPERTS, (lane >> 3) == g_idx)
    el = jnp.where(in_grp, logits, NEG_INF)
    l1 = el.max(axis=-1, keepdims=True)
    i1 = jnp.where(el == l1, lane, big).min(axis=-1, keepdims=True)
    el2 = jnp.where(lane == i1, NEG_INF, el)
    l2 = el2.max(axis=-1, keepdims=True)
    i2 = jnp.where(el2 == l2, lane, big).min(axis=-1, keepdims=True)
    e2 = jnp.exp(l2 - l1)
    w1 = g_w / (1.0 + e2)
    w2 = w1 * e2
    rt_ref[...] = jnp.where(lane == 0, i1.astype(F32), jnp.where(lane == 1, i2.astype(F32),
                            jnp.where(lane == 2, w1, jnp.where(lane == 3, w2, 0.0))))
    hot = jnp.logical_or(lane == i1, lane == i2)
    cnt_ref[0] = jnp.where(hot, 1.0, 0.0).sum(axis=0, keepdims=True)


def _outproj(xc, oa_l, oa_c, y_f, y_b, xbc_act, z, on_l, on_c, mod, dskip, norm_g, w_out, g_ffn, wr1, wr2, br,
             n_batch, n_tiles, layer):
    n_lat_tiles = n_batch * SEQ // TILE
    tpb = SEQ // TILE
    row = lambda i: (i, 0)
    lat = lambda i: (jnp.minimum(i, n_lat_tiles - 1), 0)
    ctx = lambda i: (jnp.maximum(i - n_lat_tiles, 0), 0)
    modrow = lambda i: (layer, _tile_mod_row(i, n_lat_tiles, tpb, n_batch), 0, 0)
    const = lambda i: (0, 0)
    kern = functools.partial(_outproj_kernel, n_lat_tiles=n_lat_tiles)
    return pl.pallas_call(
        kern,
        grid=(n_tiles,),
        in_specs=[pl.BlockSpec((TILE, D_MODEL), row),
                  pl.BlockSpec((TILE, 256), lat),
                  pl.BlockSpec((TILE, 256), ctx),
                  pl.BlockSpec((TILE, SSD_INNER), row),
                  pl.BlockSpec((TILE, SSD_INNER), row),
                  pl.BlockSpec((TILE, SSD_INNER), row),
                  pl.BlockSpec((TILE, SSD_INNER), row),
                  pl.BlockSpec((TILE, 256), lat),
                  pl.BlockSpec((TILE, 256), ctx),
                  pl.BlockSpec((1, 1, N_MOD, D_MODEL), modrow),
                  pl.BlockSpec((1, SSD_INNER), const),
                  pl.BlockSpec((1, SSD_INNER), const),
                  pl.BlockSpec((1, D_MODEL, D_MODEL), lambda i: (layer, 0, 0)),
                  pl.BlockSpec((1, D_MODEL), const),
                  pl.BlockSpec((D_MODEL, LANES), const),
                  pl.BlockSpec((D_MODEL, LANES), const),
                  pl.BlockSpec((1, LANES), const)],
        out_specs=[pl.BlockSpec((TILE, D_MODEL), row),
                   pl.BlockSpec((TILE * ROW_SLABS, LANES), row),
                   pl.BlockSpec((TILE, LANES), row),
                   pl.BlockSpec((1, 1, LANES), lambda i: (i, 0, 0))],
        out_shape=[jax.ShapeDtypeStruct((n_tiles * TILE, D_MODEL), F32),
                   jax.ShapeDtypeStruct((n_tiles * TILE * ROW_SLABS, LANES), F32),
                   jax.ShapeDtypeStruct((n_tiles * TILE, LANES), F32),
                   jax.ShapeDtypeStruct((n_tiles, 1, LANES), F32)],
        compiler_params=_cparams(1, 56),
        name="outproj",
    )(xc, oa_l, oa_c, y_f, y_b, xbc_act, z, on_l, on_c, mod, dskip, norm_g, w_out, g_ffn, wr1, wr2, br)


MOE_TM = 512


def _moe_max_tiles(n_tokens):
    return (2 * n_tokens + N_EXPERTS * (MOE_TM - 1)) // MOE_TM


def _moe_plan(cnt, n_tiles):
    cnt = cnt[:, 0, :N_EXPERTS].astype(jnp.int32)
    tot = cnt.sum(axis=0)
    tiles_e = (tot + MOE_TM - 1) // MOE_TM
    t_end = jnp.cumsum(tiles_e)
    t_start = t_end - tiles_e
    base = (t_start * MOE_TM)[None, :] + jnp.cumsum(cnt, axis=0) - cnt
    n_used = t_end[-1]
    n_max = _moe_max_tiles(n_tiles * TILE)
    te = jnp.sum(jnp.arange(n_max)[:, None] >= t_end[None, :], axis=1)
    te = jnp.minimum(te, jnp.sum((n_used - 1) >= t_end)).astype(jnp.int32)
    tail = jnp.where(tiles_e > 0, t_end - 1, n_max).astype(jnp.int32)
    base_f = jnp.zeros((n_tiles, 1, LANES), F32).at[:, 0, :N_EXPERTS].set(base.astype(F32))
    return base_f, te, n_used.reshape(1).astype(jnp.int32), tail


def _pos_kernel(rt_ref, base_ref, pos_ref):
    rt = rt_ref[...]
    lane = _iota(rt.shape, 1)
    hot1 = lane == rt[:, 0:1].astype(jnp.int32)
    hot2 = lane == rt[:, 1:2].astype(jnp.int32)
    hot = jnp.where(jnp.logical_or(hot1, hot2), 1.0, 0.0).astype(BF16)
    strict = jnp.where(_iota((TILE, TILE), 0) > _iota((TILE, TILE), 1), 1.0, 0.0).astype(BF16)
    slot = base_ref[0] + _dot(strict, hot)
    p1 = jnp.where(hot1, slot, 0.0).sum(axis=-1, keepdims=True)
    p2 = jnp.where(hot2, slot, 0.0).sum(axis=-1, keepdims=True)
    pos_ref[...] = jnp.where(lane == 0, p1, jnp.where(lane == 1, p2, 0.0)).astype(jnp.int32)


def _positions(rt, base, n_tiles):
    return pl.pallas_call(
        _pos_kernel,
        grid=(n_tiles,),
        in_specs=[pl.BlockSpec((TILE, LANES), lambda i: (i, 0)),
                  pl.BlockSpec((1, 1, LANES), lambda i: (i, 0, 0))],
        out_specs=pl.BlockSpec((TILE, LANES), lambda i: (i, 0)),
        out_shape=jax.ShapeDtypeStruct((n_tiles * TILE, LANES), jnp.int32),
        compiler_params=_cparams(1, 32),
        name="moe_positions",
    )(rt, base)


def _dispatch_kernel(tail_ref, nu_ref, pos_ref, tok_ref, xs_ref, zbuf, zsem, sem, *, n_max):
    i = pl.program_id(0)

    def zero_tile(j):
        start = pl.multiple_of(j * (MOE_TM * ROW_SLABS), MOE_TM * ROW_SLABS)
        return pltpu.make_async_copy(zbuf, xs_ref.at[pl.ds(start, MOE_TM * ROW_SLABS), :], zsem)

    def row(ref, r):
        return ref.at[pl.ds(pl.multiple_of(r * ROW_SLABS, ROW_SLABS), ROW_SLABS), :]

    @pl.when(i == 0)
    def _():
        zbuf[...] = jnp.zeros_like(zbuf)
        for e in range(N_EXPERTS):
            @pl.when(tail_ref[e] != n_max)
            def _():
                zero_tile(tail_ref[e]).start()
        lax.fori_loop(nu_ref[0], n_max + 1, lambda j, c: (zero_tile(j).start(), c)[1], 0)
        for e in range(N_EXPERTS):
            @pl.when(tail_ref[e] != n_max)
            def _():
                zero_tile(tail_ref[e]).wait()
        lax.fori_loop(nu_ref[0], n_max + 1, lambda j, c: (zero_tile(j).wait(), c)[1], 0)

    def body(r, carry):
        src = row(tok_ref, r)
        pltpu.make_async_copy(src, row(xs_ref, pos_ref[0, 0, 2 * r]), sem).start(priority=0)
        pltpu.make_async_copy(src, row(xs_ref, pos_ref[0, 0, 2 * r + 1]), sem).start(priority=1)
        return carry

    lax.fori_loop(0, TILE, body, 0, unroll=8)
    for _ in range(2):
        pltpu.make_async_copy(tok_ref, xs_ref.at[pl.ds(0, TILE * ROW_SLABS), :], sem).wait()


def _dispatch(tail, n_used, pos_s, tok, n_tiles, n_max):
    kern = functools.partial(_dispatch_kernel, n_max=n_max)
    return pl.pallas_call(
        kern,
        grid_spec=pltpu.PrefetchScalarGridSpec(
            num_scalar_prefetch=2,
            grid=(n_tiles,),
            in_specs=[pl.BlockSpec((1, 1, 2 * TILE), lambda i, tail, nu: (i, 0, 0), memory_space=pltpu.SMEM),
                      pl.BlockSpec((TILE * ROW_SLABS, LANES), lambda i, tail, nu: (i, 0))],
            out_specs=pl.BlockSpec(memory_space=pl.ANY),
            scratch_shapes=[pltpu.VMEM((MOE_TM * ROW_SLABS, LANES), F32), pltpu.SemaphoreType.DMA(()),
                            pltpu.SemaphoreType.DMA(())]),
        out_shape=jax.ShapeDtypeStruct(((n_max + 1) * MOE_TM * ROW_SLABS, LANES), F32),
        compiler_params=_cparams(1, 32),
        name="moe_dispatch",
    )(tail, n_used, pos_s, tok)


def _experts_kernel(te_ref, nu_ref, xs_ref, wg_ref, wu_ref, wd_ref, ys_ref):
    used = pl.program_id(0) < nu_ref[0]

    @pl.when(used)
    def _():
        x = _load_row_tiles(xs_ref, MOE_TM).astype(BF16)
        gate = _dot(x, wg_ref[0].astype(BF16))
        up = _dot(x, wu_ref[0].astype(BF16))
        hid = (_silu(gate) * up).astype(BF16)
        _store_row_tiles(ys_ref, _dot(hid, wd_ref[0].astype(BF16)))

    @pl.when(jnp.logical_not(used))
    def _():
        ys_ref[...] = jnp.zeros_like(ys_ref)


def _experts(te, n_used, xs, w_gate, w_up, w_down, n_max, layer):
    rows = lambda j, te, nu: (jnp.minimum(j, nu[0] - 1), 0)
    wsel = lambda j, te, nu: (layer * N_EXPERTS + te[j], 0, 0)
    return pl.pallas_call(
        _experts_kernel,
        grid_spec=pltpu.PrefetchScalarGridSpec(
            num_scalar_prefetch=2,
            grid=(n_max,),
            in_specs=[pl.BlockSpec((MOE_TM * ROW_SLABS, LANES), rows),
                      pl.BlockSpec((1, D_MODEL, D_EXPERT), wsel),
                      pl.BlockSpec((1, D_MODEL, D_EXPERT), wsel),
                      pl.BlockSpec((1, D_EXPERT, D_MODEL), wsel)],
            out_specs=pl.BlockSpec((MOE_TM * ROW_SLABS, LANES), lambda j, te, nu: (j, 0))),
        out_shape=jax.ShapeDtypeStruct((n_max * MOE_TM * ROW_SLABS, LANES), F32),
        compiler_params=_cparams(1, 48),
        name="moe_experts",
    )(te, n_used, xs, w_gate, w_up, w_down)


def _combine_kernel(pos_ref, ys_ref, x_ref, rt_ref, mod_ref, gfin_ref, o_ref, ybuf0, ybuf1, sem, *, final):
    def row(ref, r):
        return ref.at[pl.ds(pl.multiple_of(r * ROW_SLABS, ROW_SLABS), ROW_SLABS), :]

    def body(r, carry):
        pltpu.make_async_copy(row(ys_ref, pos_ref[0, 0, 2 * r]), row(ybuf0, r), sem).start(priority=0)
        pltpu.make_async_copy(row(ys_ref, pos_ref[0, 0, 2 * r + 1]), row(ybuf1, r), sem).start(priority=1)
        return carry

    lax.fori_loop(0, TILE, body, 0, unroll=8)
    for buf in (ybuf0, ybuf1):
        pltpu.make_async_copy(ys_ref.at[pl.ds(0, TILE * ROW_SLABS), :], buf, sem).wait()
    rt = rt_ref[...]
    f = rt[:, 2:3] * _load_row_tiles(ybuf0, TILE) + rt[:, 3:4] * _load_row_tiles(ybuf1, TILE)
    x = x_ref[...] + mod_ref[0, 0][5:6] * f
    if final:
        x = x * lax.rsqrt(jnp.mean(x * x, axis=-1, keepdims=True) + RMS_EPS) * gfin_ref[...]
    o_ref[...] = x


def _combine(pos_s, ys, xmid, rt, mod, g_final, n_batch, n_tiles, layer):
    n_lat_tiles = n_batch * SEQ // TILE
    tpb = SEQ // TILE
    row = lambda i: (i, 0)
    modrow = lambda i: (layer, _tile_mod_row(i, n_lat_tiles, tpb, n_batch), 0, 0)
    return pl.pallas_call(
        functools.partial(_combine_kernel, final=(layer == DEPTH - 1)),
        grid=(n_tiles,),
        in_specs=[pl.BlockSpec((1, 1, 2 * TILE), lambda i: (i, 0, 0), memory_space=pltpu.SMEM),
                  pl.BlockSpec(memory_space=pl.ANY),
                  pl.BlockSpec((TILE, D_MODEL), row),
                  pl.BlockSpec((TILE, LANES), row),
                  pl.BlockSpec((1, 1, N_MOD, D_MODEL), modrow),
                  pl.BlockSpec((1, D_MODEL), lambda i: (0, 0))],
        out_specs=pl.BlockSpec((TILE, D_MODEL), row),
        out_shape=jax.ShapeDtypeStruct((n_tiles * TILE, D_MODEL), F32),
        scratch_shapes=[pltpu.VMEM((TILE * ROW_SLABS, LANES), F32), pltpu.VMEM((TILE * ROW_SLABS, LANES), F32),
                        pltpu.SemaphoreType.DMA(())],
        compiler_params=_cparams(1, 40),
        name="moe_combine",
    )(pos_s, ys, xmid, rt, mod, g_final)


def _moe(xmid, tok, rt, cnt, mod, g_final, w_gate, w_up, w_down, n_batch, n_tiles, layer):
    base, te, n_used, tail = _moe_plan(cnt, n_tiles)
    n_max = _moe_max_tiles(n_tiles * TILE)
    pos = _positions(rt, base, n_tiles)
    pos_s = pos[:, :2].reshape(n_tiles, 1, 2 * TILE)
    xs = _dispatch(tail, n_used, pos_s, tok, n_tiles, n_max)
    ys = _experts(te, n_used, xs, w_gate, w_up, w_down, n_max, layer)
    return _combine(pos_s, ys, xmid, rt, mod, g_final, n_batch, n_tiles, layer)


def _rope_tables():
    t = jnp.arange(SEQ)
    rows_pos = (t // GRID_W).astype(F32)
    cols_pos = (t % GRID_W).astype(F32)
    half = HEAD_DIM // 2
    inv = 1.0 / (ROPE_BASE ** (jnp.arange(0, half, 2, dtype=F32) / half))
    ang_r = rows_pos[:, None] * inv[None, :]
    ang_c = cols_pos[:, None] * inv[None, :]
    ang = jnp.concatenate([ang_r, ang_r, ang_c, ang_c], axis=1)
    cos_h, sin_h = jnp.cos(ang), jnp.sin(ang)
    scale = jnp.concatenate([jnp.full((256,), HEAD_DIM ** -0.5, F32), jnp.ones((128,), F32)])
    cos_t = jnp.tile(cos_h, (1, 6)) * scale
    sin_t = jnp.tile(sin_h, (1, 6)) * scale
    cos_t = jnp.concatenate([cos_t, jnp.broadcast_to(scale, (TILE, 384))], axis=0)
    sin_t = jnp.concatenate([sin_t, jnp.zeros((TILE, 384), F32)], axis=0)
    return cos_t, sin_t


def _fused_in_weight(w_in):
    wq, wk, wv = w_in[..., 0:256], w_in[..., 256:384], w_in[..., 384:512]
    o = A_IN
    wz, wxbc, wdt = w_in[..., o:o + 512], w_in[..., o + 512:o + 1280], w_in[..., o + 1280:o + 1296]
    wna = w_in[..., A_IN + SSD_IN:]

    def rot(w):
        w4 = w.reshape(w.shape[:-1] + (w.shape[-1] // 32, 2, 16))
        return jnp.concatenate([-w4[..., 1:2, :], w4[..., 0:1, :]], axis=-2).reshape(w.shape)

    na_scale = jnp.concatenate([jnp.full((256,), HEAD_DIM ** -0.5, F32), jnp.ones((512,), F32)])
    pad = jnp.zeros(w_in.shape[:-1] + (128 - SSD_HEADS,), F32)
    cat = jnp.concatenate([wq, wk, rot(wq), rot(wk), wv, wz, wxbc,
                           wna * na_scale, wdt[..., :SSD_HEADS], pad, wdt[..., SSD_HEADS:], pad], axis=-1)
    return cat.astype(BF16)


def _router_weight(w_rg, b_rg, w_re, b_re):
    w = jnp.concatenate([w_re, w_rg, jnp.zeros((D_MODEL, LANES - N_EXPERTS - MOE_GROUPS), F32)], axis=1)
    b = jnp.concatenate([b_re, b_rg, jnp.zeros((LANES - N_EXPERTS - MOE_GROUPS,), F32)]).reshape(1, LANES)
    w1 = w.astype(BF16)
    w2 = (w - w1.astype(F32)).astype(BF16)
    return w1, w2, b


def kernel(x, c, ctx, c_ctx, w_mod, b_mod, g_mix, w_in, attn_sink, ssd_conv_w, ssd_conv_b, ssd_dt_bias, ssd_a_log, ssd_d, ssd_norm_g, na_rpb, w_out, g_ffn, w_router_group, b_router_group, w_router_expert, b_router_expert, w_exp_gate, w_exp_up, w_exp_down, g_final):
    n_batch, s, d = x.shape
    assert (s, d) == (SEQ, D_MODEL) and ctx.shape[1:] == (CTX_LEN, D_MODEL) and n_batch < 16
    n_lat = n_batch * SEQ
    n_ctx = n_batch * CTX_LEN
    assert n_ctx % TILE == 0
    n_lat_tiles = n_lat // TILE
    n_all_tiles = (n_lat + n_ctx) // TILE

    xc = jnp.concatenate([x.reshape(n_lat, d), ctx.reshape(n_ctx, d)], axis=0)
    cin = jnp.zeros((16, d), F32).at[:n_batch].set(c).at[n_batch].set(c_ctx)
    mod = _modulation(cin, w_mod, b_mod).reshape(DEPTH, 16, N_MOD, d)
    cos_t, sin_t = _rope_tables()
    w_cat = _fused_in_weight(w_in)
    w_out_b = w_out.astype(BF16)
    bias_t = _na_bias_table(na_rpb)
    g_mix3 = g_mix.reshape(DEPTH, 1, d)
    w_gate = w_exp_gate.reshape(DEPTH * N_EXPERTS, D_MODEL, D_EXPERT)
    w_up = w_exp_up.reshape(DEPTH * N_EXPERTS, D_MODEL, D_EXPERT)
    w_down = w_exp_down.reshape(DEPTH * N_EXPERTS, D_EXPERT, D_MODEL)

    for layer in range(DEPTH):
        need_ctx = layer < DEPTH - 1
        conv_w = jnp.zeros((8, SSD_CONV_DIM), F32).at[:SSD_CONV].set(ssd_conv_w[layer])
        qk, v, z, xbc_act, na, dt_raw = _inproj(xc, mod, g_mix3, w_cat, cos_t, sin_t, conv_w,
                                                ssd_conv_b[layer].reshape(1, SSD_CONV_DIM), n_batch, layer)
        sink = attn_sink[layer].astype(F32)
        oa = _window_attn(sink, qk, v, n_batch)
        y_f, y_b = _ssd_scan(xbc_act, dt_raw, ssd_dt_bias[layer], ssd_a_log[layer], n_batch)
        on = _neighborhood_attn(na, bias_t, n_batch, layer)
        oa_c, on_c = _ctx_attn(sink, qk, v, na, n_batch) if need_ctx else (oa, on)
        n_tiles = n_all_tiles if need_ctx else n_lat_tiles
        wr1, wr2, br = _router_weight(w_router_group[layer], b_router_group[layer],
                                      w_router_expert[layer], b_router_expert[layer])
        dskip = jnp.repeat(ssd_d[layer].astype(F32), SSD_INNER // SSD_HEADS).reshape(1, SSD_INNER)
        xmid, tok, rt, cnt = _outproj(xc, oa, oa_c, y_f, y_b, xbc_act, z, on, on_c, mod, dskip,
                                      ssd_norm_g[layer].reshape(1, SSD_INNER), w_out_b,
                                      g_ffn[layer].reshape(1, d), wr1, wr2, br, n_batch, n_tiles, layer)
        xc = _moe(xmid, tok, rt, cnt, mod, g_final.reshape(1, d), w_gate, w_up, w_down, n_batch, n_tiles, layer)

    return xc.reshape(n_batch, SEQ, d)
```

```python
import functools
import math

import jax
import jax.numpy as jnp
import numpy as np
from jax import lax
from jax.experimental import pallas as pl
from jax.experimental.pallas import tpu as pltpu

F32 = jnp.float32
BF16 = jnp.bfloat16

D_MODEL = 1024
SEQ = 2048
DEPTH = 4
GRID_W = 64
GRID_ROWS = SEQ // GRID_W
CTX_LEN = 256
HEAD_DIM = 64
A_HEADS = 4
A_KV_HEADS = 2
A_BLOCK = 128
ROPE_BASE = 10000.0
SSD_HEADS = 8
SSD_INNER = 512
SSD_STATE = 64
SSD_CONV = 5
SSD_CHUNK = 128
SSD_STEP_CHUNKS = 2
SSD_CONV_DIM = 768
NA_HEADS = 4
NA_WIN_ROWS = 8
NA_WIN_COLS = 16
A_IN = 512
SSD_IN = 1296
MOE_GROUPS = 4
MOE_EXPERTS = 8
N_EXPERTS = MOE_GROUPS * MOE_EXPERTS
D_EXPERT = 256
N_MOD = 6
RMS_EPS = 1e-6
NEG_INF = -1e30

TILE = 512
LANES = 128
C_QK, C_QKP, C_V, C_Z, C_XBC, C_NA, C_DT = 0, 384, 768, 896, 1408, 2176, 2944
N_COLS = 3200
NA_Q_ROWS = 4
NA_K_ROWS = NA_WIN_ROWS + NA_Q_ROWS
NA_BIAS_OFF = NA_Q_ROWS
NA_BIAS_N = NA_BIAS_OFF + (NA_K_ROWS - 2) + (NA_WIN_ROWS - 1) + 1


def _cparams(n_axes, vmem_mb):
    return pltpu.CompilerParams(dimension_semantics=("arbitrary",) * n_axes,
                                vmem_limit_bytes=vmem_mb << 20)


def _split3(x):
    h1 = x.astype(BF16)
    r1 = x - h1.astype(F32)
    h2 = r1.astype(BF16)
    h3 = (r1 - h2.astype(F32)).astype(BF16)
    return h1, h2, h3


def _dot(a, b):
    return jnp.dot(a, b, preferred_element_type=F32)


def _dot_nt(a, b):
    return lax.dot_general(a, b, (((1,), (1,)), ((), ())), preferred_element_type=F32)


def _dot_exact_lhs(lhs_bf16, x):
    return _dot(jnp.concatenate([lhs_bf16] * 3, axis=1), jnp.concatenate(_split3(x), axis=0))


def _dot_exact_rhs(x, rhs3_bf16):
    return _dot(jnp.concatenate(_split3(x), axis=1), rhs3_bf16)


def _silu(x):
    return x * jax.nn.sigmoid(x)


ROW_SLABS = D_MODEL // LANES


def _store_row_tiles(ref, x):
    n = x.shape[0]
    for s in range(ROW_SLABS):
        ref[pl.ds(s, n, stride=ROW_SLABS), :] = x[:, s * LANES:(s + 1) * LANES]


def _load_row_tiles(ref, n):
    return jnp.concatenate([ref[pl.ds(s, n, stride=ROW_SLABS), :] for s in range(ROW_SLABS)], axis=1)


def _iota(shape, dim):
    return lax.broadcasted_iota(jnp.int32, shape, dim)


def _mod_kernel(c_ref, w_ref, b_ref, o_ref):
    a = _silu(c_ref[...])
    a1, a2, _ = _split3(a)
    w = w_ref[0]
    w1 = w.astype(BF16)
    w2 = (w - w1.astype(F32)).astype(BF16)
    o_ref[0] = _dot(a1, w1) + _dot(a1, w2) + _dot(a2, w1) + b_ref[0]


def _modulation(cin, w_mod, b_mod):
    nt = 1024
    return pl.pallas_call(
        _mod_kernel,
        grid=(DEPTH, N_MOD * D_MODEL // nt),
        in_specs=[pl.BlockSpec((16, D_MODEL), lambda l, j: (0, 0)),
                  pl.BlockSpec((1, D_MODEL, nt), lambda l, j: (l, 0, j)),
                  pl.BlockSpec((1, 1, nt), lambda l, j: (l, 0, j))],
        out_specs=pl.BlockSpec((1, 16, nt), lambda l, j: (l, 0, j)),
        out_shape=jax.ShapeDtypeStruct((DEPTH, 16, N_MOD * D_MODEL), F32),
        compiler_params=_cparams(2, 40),
        name="modulation",
    )(cin, w_mod, b_mod.reshape(DEPTH, 1, N_MOD * D_MODEL))


def _inproj_kernel(prev_ref, x_ref, next_ref, mod_ref, g_ref, w_ref, cos_ref, sin_ref, cw_ref, cb_ref,
                   qk_ref, v_ref, z_ref, xbc_ref, na_ref, dt_ref, *, n_lat_tiles, tiles_per_batch):
    xe = jnp.concatenate([prev_ref[...], x_ref[...], next_ref[...]], axis=0)
    m = mod_ref[0, 0]
    he = xe * lax.rsqrt(jnp.mean(xe * xe, axis=-1, keepdims=True) + RMS_EPS) * g_ref[0]
    hbe = (he * (1.0 + m[1:2]) + m[0:1]).astype(BF16)
    hb = hbe[8:8 + TILE]

    def mm(lo, hi):
        return _dot(hb, w_ref[0, :, lo:hi])

    qk = mm(C_QK, C_QKP) * cos_ref[...] + mm(C_QKP, C_V) * sin_ref[...]
    qk_ref[...] = qk.astype(BF16)
    v_ref[...] = mm(C_V, C_Z).astype(BF16)
    z_ref[...] = mm(C_Z, C_XBC)
    na_ref[...] = mm(C_NA, C_DT).astype(BF16)
    dt_ref[...] = mm(C_DT, N_COLS)

    i = pl.program_id(0)
    is_lat = i < n_lat_tiles
    r = _iota((TILE, 1), 0)
    pos = jnp.where(is_lat, (i % tiles_per_batch) * TILE + r, r & (CTX_LEN - 1))
    seq_len = jnp.where(is_lat, SEQ, CTX_LEN)
    xbc_e = _dot(hbe, w_ref[0, :, C_XBC:C_NA])
    cw = cw_ref[...]
    acc = jnp.zeros((TILE, SSD_CONV_DIM), F32) + cb_ref[...]
    for k in range(SSD_CONV):
        off = k - SSD_CONV // 2
        tap = xbc_e[8 + off:8 + off + TILE, :]
        if off != 0:
            tap = jnp.where(jnp.logical_and(pos + off >= 0, pos + off < seq_len), tap, 0.0)
        acc = acc + tap * cw[k:k + 1, :]
    xbc_ref[...] = _silu(acc)


def _tile_mod_row(i, n_lat_tiles, tiles_per_batch, n_batch):
    return jnp.where(i < n_lat_tiles, i // tiles_per_batch, n_batch)


def _inproj(xc, mod, g_mix, w_cat, cos_t, sin_t, conv_w, conv_b, n_batch, layer):
    nt = xc.shape[0]
    n_lat_tiles = n_batch * SEQ // TILE
    tpb = SEQ // TILE
    t8 = TILE // 8
    row = lambda i: (i, 0)
    modrow = lambda i: (layer, _tile_mod_row(i, n_lat_tiles, tpb, n_batch), 0, 0)
    posrow = lambda i: (jnp.where(i < n_lat_tiles, i % tpb, tpb), 0)
    lay = lambda i: (layer, 0, 0)
    const = lambda i: (0, 0)
    outs = [(384, BF16), (128, BF16), (512, F32), (768, F32), (768, BF16), (256, F32)]
    kern = functools.partial(_inproj_kernel, n_lat_tiles=n_lat_tiles, tiles_per_batch=tpb)
    return pl.pallas_call(
        kern,
        grid=(nt // TILE,),
        in_specs=[pl.BlockSpec((8, D_MODEL), lambda i: (jnp.maximum(i * t8 - 1, 0), 0)),
                  pl.BlockSpec((TILE, D_MODEL), row),
                  pl.BlockSpec((8, D_MODEL), lambda i: (jnp.minimum(i * t8 + t8, nt // 8 - 1), 0)),
                  pl.BlockSpec((1, 1, N_MOD, D_MODEL), modrow),
                  pl.BlockSpec((1, 1, D_MODEL), lay),
                  pl.BlockSpec((1, D_MODEL, N_COLS), lay),
                  pl.BlockSpec((TILE, 384), posrow),
                  pl.BlockSpec((TILE, 384), posrow),
                  pl.BlockSpec((8, SSD_CONV_DIM), const),
                  pl.BlockSpec((1, SSD_CONV_DIM), const)],
        out_specs=[pl.BlockSpec((TILE, w), row) for w, _ in outs],
        out_shape=[jax.ShapeDtypeStruct((nt, w), dt) for w, dt in outs],
        compiler_params=_cparams(1, 56),
        name="inproj",
    )(xc, xc, xc, mod, g_mix, w_cat, cos_t, sin_t, conv_w, conv_b)


def _ssd_kernel(xf_ref, dtf_ref, xb_ref, dtb_ref, bias_ref, alog_ref, e512_ref, e1024_ref,
                yf_ref, yb_ref, stf_ref, stb_ref):
    @pl.when(pl.program_id(1) == 0)
    def _():
        stf_ref[...] = jnp.zeros_like(stf_ref)
        stb_ref[...] = jnp.zeros_like(stb_ref)

    for j in range(SSD_STEP_CHUNKS):
        rf = j * SSD_CHUNK
        rb = (SSD_STEP_CHUNKS - 1 - j) * SSD_CHUNK
        _ssd_chunk(0, rf, xf_ref, dtf_ref, bias_ref[0], alog_ref[0], e512_ref, e1024_ref, yf_ref, stf_ref)
        _ssd_chunk(1, rb, xb_ref, dtb_ref, bias_ref[1], alog_ref[1], e512_ref, e1024_ref, yb_ref, stb_ref)


def _ssd_chunk(d, r0, xbc_ref, dt_ref, dt_bias, a_log, e512_ref, e1024_ref, y_ref, st_ref):
    q = SSD_CHUNK
    xbc = xbc_ref[r0:r0 + q, :]
    xs = xbc[:, :SSD_INNER]
    bm = xbc[:, SSD_INNER:SSD_INNER + 128]
    cm = xbc[:, SSD_INNER + 128:]
    dtr = dt_ref[r0:r0 + q, :] + dt_bias
    dt = jnp.maximum(dtr, 0.0) + jnp.log(1.0 + jnp.exp(-jnp.abs(dtr)))
    a = -jnp.exp(a_log)
    da = dt * a

    ri = _iota((q, q), 0)
    ci = _iota((q, q), 1)
    tri = (ri >= ci) if d == 0 else (ri <= ci)
    trib = jnp.where(tri, 1.0, 0.0).astype(BF16)
    acs = _dot_exact_lhs(trib, da)
    acs_t = acs.T

    both_e = _dot_exact_rhs(jnp.concatenate([dt, acs], axis=0), e512_ref[...])
    dt_e = both_e[:q]
    acs_e = both_e[q:]
    acs_e2 = _dot_exact_rhs(acs, e1024_ref[...])
    tot_e = acs_e[q - 1:q, :] if d == 0 else acs_e[0:1, :]

    xdt = xs * dt_e
    xdec = (xdt * jnp.exp(tot_e - acs_e)).astype(BF16)
    btb = bm.T.astype(BF16)
    lane = _iota((q, 128), 1)
    cm0 = jnp.where(lane < SSD_STATE, cm, 0.0).astype(BF16)
    cm1 = jnp.where(lane >= SSD_STATE, cm, 0.0).astype(BF16)
    cbs = (_dot(cm0, btb), _dot(cm1, btb))

    st = st_ref[...]
    y_off = _dot(cm.astype(BF16), st.astype(BF16)) * jnp.exp(acs_e)
    s_all = _dot(btb, xdec)
    same = (_iota((q, SSD_INNER), 0) >> 6) == (_iota((q, SSD_INNER), 1) >> 8)
    st_ref[...] = jnp.where(same, st * jnp.exp(tot_e) + s_all, 0.0)

    for pair in range(SSD_HEADS // 2):
        cb = cbs[pair // 2]
        xp = xdt[:, pair * 128:(pair + 1) * 128]
        acc = None
        for k in range(2):
            h = 2 * pair + k
            seg = acs_e2[:, h * 128:(h + 1) * 128] - acs_t[h:h + 1, :]
            lmat = jnp.exp(jnp.where(tri, seg, NEG_INF))
            g = (cb * lmat).astype(BF16)
            rhs = jnp.where((lane < 64) if k == 0 else (lane >= 64), xp, 0.0).astype(BF16)
            t = _dot(g, rhs)
            acc = t if acc is None else acc + t
        y_ref[r0:r0 + q, pair * 128:(pair + 1) * 128] = acc + y_off[:, pair * 128:(pair + 1) * 128]


def _ssd_scan(xbc_act, dt_raw, dt_bias, a_log, n_batch):
    nt = xbc_act.shape[0]
    rows = SSD_STEP_CHUNKS * SSD_CHUNK
    n_lat_blk = n_batch * (SEQ // rows)
    lat_c = SEQ // rows
    ctx_c = CTX_LEN // rows
    n_steps = lat_c + ctx_c

    def blk_f(b, c):
        return jnp.where(c < ctx_c, n_lat_blk + b * ctx_c + c, b * lat_c + c - ctx_c)

    def blk_b(b, c):
        return jnp.where(c < ctx_c, n_lat_blk + b * ctx_c + ctx_c - 1 - c, b * lat_c + n_steps - 1 - c)

    heads = np.arange(128)
    e512 = (heads[:, None] == (np.arange(512)[None, :] // 64)).astype(np.float32)
    e1024 = (heads[:, None] == (np.arange(1024)[None, :] // 128)).astype(np.float32)
    dtb = jnp.zeros((2, 1, 128), F32).at[:, 0, :SSD_HEADS].set(dt_bias)
    alog = jnp.zeros((2, 1, 128), F32).at[:, 0, :SSD_HEADS].set(a_log)
    const3 = lambda b, c: (0, 0, 0)
    const2 = lambda b, c: (0, 0)
    return pl.pallas_call(
        _ssd_kernel,
        grid=(n_batch, n_steps),
        in_specs=[pl.BlockSpec((rows, SSD_CONV_DIM), lambda b, c: (blk_f(b, c), 0)),
                  pl.BlockSpec((rows, 128), lambda b, c: (blk_f(b, c), 0)),
                  pl.BlockSpec((rows, SSD_CONV_DIM), lambda b, c: (blk_b(b, c), 0)),
                  pl.BlockSpec((rows, 128), lambda b, c: (blk_b(b, c), 1)),
                  pl.BlockSpec((2, 1, 128), const3),
                  pl.BlockSpec((2, 1, 128), const3),
                  pl.BlockSpec((384, 512), const2),
                  pl.BlockSpec((384, 1024), const2)],
        out_specs=[pl.BlockSpec((rows, SSD_INNER), lambda b, c: (blk_f(b, c), 0)),
                   pl.BlockSpec((rows, SSD_INNER), lambda b, c: (blk_b(b, c), 0))],
        out_shape=[jax.ShapeDtypeStruct((nt, SSD_INNER), F32)] * 2,
        scratch_shapes=[pltpu.VMEM((128, SSD_INNER), F32)] * 2,
        compiler_params=_cparams(2, 32),
        name="ssd_scan",
    )(xbc_act, dt_raw, xbc_act, dt_raw, dtb, alog, jnp.asarray(np.tile(e512, (3, 1)), BF16),
      jnp.asarray(np.tile(e1024, (3, 1)), BF16))


def _softmax_pv(s_list, v_list, extra_logit=None):
    m = s_list[0].max(axis=-1, keepdims=True)
    for s in s_list[1:]:
        m = jnp.maximum(m, s.max(axis=-1, keepdims=True))
    if extra_logit is not None:
        m = jnp.maximum(m, extra_logit)
    den = None
    o = None
    for s, v in zip(s_list, v_list):
        p = jnp.exp(s - m)
        ps = p.sum(axis=-1, keepdims=True)
        den = ps if den is None else den + ps
        t = _dot(p.astype(BF16), v)
        o = t if o is None else o + t
    if extra_logit is not None:
        den = den + jnp.exp(extra_logit - m)
    return o / den


A_QB = 2
A_KB = A_QB + 2


def _wattn_kernel(sink_ref, q_ref, k_ref, v_ref, kc_ref, vc_ref, o_ref):
    n = pl.program_id(1) * A_QB
    nb = SEQ // A_BLOCK
    nq = A_QB * A_BLOCK
    nk = A_KB * A_BLOCK
    start = pl.multiple_of(jnp.clip(n - 1, 0, nb - A_KB) * A_BLOCK, A_BLOCK)
    q = q_ref[:, 0:256]
    kw = k_ref[pl.ds(start, nk), 256:384]
    vw = v_ref[pl.ds(start, nk), :]
    kc = kc_ref[:, 256:384]
    vc = vc_ref[...]
    qrow = _iota((2 * nq, nk), 0)
    qpos = n * A_BLOCK + jnp.where(qrow < nq, qrow, qrow - nq)
    kpos = start + _iota((2 * nq, nk), 1)
    valid = jnp.abs(qpos - kpos) <= A_BLOCK
    top = _iota((2 * nq, 1), 0) < nq
    outs = []
    for g in range(A_KV_HEADS):
        qg = jnp.concatenate([q[:, (2 * g) * 64:(2 * g + 1) * 64],
                              q[:, (2 * g + 1) * 64:(2 * g + 2) * 64]], axis=0)
        kg = kw[:, g * 64:(g + 1) * 64]
        vg = vw[:, g * 64:(g + 1) * 64]
        s_loc = jnp.where(valid, _dot_nt(qg, kg), NEG_INF)
        s_ctx = _dot_nt(qg, kc[:, g * 64:(g + 1) * 64])
        sink = jnp.where(top, sink_ref[2 * g], sink_ref[2 * g + 1])
        o = _softmax_pv([s_loc, s_ctx], [vg, vc[:, g * 64:(g + 1) * 64]], sink)
        outs += [o[:nq], o[nq:]]
    o_ref[...] = jnp.concatenate(outs, axis=1).astype(BF16)


def _window_attn(sink, qk, v, n_batch):
    steps = SEQ // (A_QB * A_BLOCK)
    nq = A_QB * A_BLOCK
    ctx0 = n_batch * SEQ // CTX_LEN
    return pl.pallas_call(
        _wattn_kernel,
        grid=(n_batch, steps),
        in_specs=[pl.BlockSpec(memory_space=pltpu.SMEM),
                  pl.BlockSpec((nq, 384), lambda b, n: (b * steps + n, 0)),
                  pl.BlockSpec((SEQ, 384), lambda b, n: (b, 0)),
                  pl.BlockSpec((SEQ, 128), lambda b, n: (b, 0)),
                  pl.BlockSpec((CTX_LEN, 384), lambda b, n: (ctx0 + b, 0)),
                  pl.BlockSpec((CTX_LEN, 128), lambda b, n: (ctx0 + b, 0))],
        out_specs=pl.BlockSpec((nq, 256), lambda b, n: (b * steps + n, 0)),
        out_shape=jax.ShapeDtypeStruct((n_batch * SEQ, 256), BF16),
        compiler_params=_cparams(2, 32),
        name="window_attn",
    )(sink, qk, qk, v, qk, v)


def _nattn_kernel(q_ref, kv_ref, c_ref, bias_ref, o_ref):
    i = pl.program_id(1)
    r0 = i * NA_Q_ROWS
    srow = jnp.clip(r0 - NA_WIN_ROWS // 2, 0, GRID_ROWS - NA_K_ROWS)
    start = pl.multiple_of(srow * GRID_W, GRID_W)
    nq = NA_Q_ROWS * GRID_W
    nk = NA_K_ROWS * GRID_W
    q = q_ref[:, 0:256]
    kw = kv_ref[pl.ds(start, nk), 256:512]
    vw = kv_ref[pl.ds(start, nk), 512:768]
    kc = c_ref[:, 256:512]
    vc = c_ref[:, 512:768]
    qrow = r0 + (_iota((nq, nk), 0) >> 6)
    krow = srow + (_iota((nq, nk), 1) >> 6)
    rs = jnp.clip(qrow - NA_WIN_ROWS // 2, 0, GRID_ROWS - NA_WIN_ROWS)
    valid = jnp.logical_and(krow >= rs, krow < rs + NA_WIN_ROWS)
    outs = []
    for h in range(NA_HEADS):
        sl = slice(h * 64, (h + 1) * 64)
        rows = []
        for qi in range(NA_Q_ROWS):
            blocks = []
            for p in range(NA_K_ROWS // 2):
                idx = srow + 2 * p - (r0 + qi) + (NA_WIN_ROWS - 1) + NA_BIAS_OFF
                blocks.append(bias_ref[0, h, idx])
            rows.append(jnp.concatenate(blocks, axis=1))
        bias = jnp.concatenate(rows, axis=0)
        s_loc = jnp.where(valid, _dot_nt(q[:, sl], kw[:, sl]) + bias, NEG_INF)
        s_ctx = _dot_nt(q[:, sl], kc[:, sl])
        outs.append(_softmax_pv([s_loc, s_ctx], [vw[:, sl], vc[:, sl]]))
    o_ref[...] = jnp.concatenate(outs, axis=1).astype(BF16)


def _na_bias_table(rpb):
    cq = np.arange(GRID_W)
    kcol = np.arange(GRID_W)
    cs = np.clip(cq - NA_WIN_COLS // 2, 0, GRID_W - NA_WIN_COLS)
    col_valid = (kcol[None, :] >= cs[:, None]) & (kcol[None, :] < cs[:, None] + NA_WIN_COLS)
    coff = np.clip(kcol[None, :] - cq[:, None], -(NA_WIN_COLS - 1), NA_WIN_COLS - 1) + (NA_WIN_COLS - 1)
    n_a = 2 * NA_WIN_ROWS - 1
    n_c = 2 * NA_WIN_COLS - 1
    pick = (np.arange(n_c)[:, None] == coff.reshape(1, -1)).astype(np.float32)
    tm = jnp.einsum("lhak,kn->lhan", rpb.astype(F32), jnp.asarray(pick), precision=lax.Precision.HIGHEST)
    tm = jnp.where(col_valid, tm.reshape(rpb.shape[:3] + (GRID_W, GRID_W)), NEG_INF)
    neg = jnp.full(rpb.shape[:2] + (1, GRID_W, GRID_W), NEG_INF, F32)
    pad_lo = NA_BIAS_OFF
    pad_hi = NA_BIAS_N + 1 - pad_lo - n_a
    ext = jnp.concatenate([neg] * pad_lo + [tm] + [neg] * pad_hi, axis=2)
    return jnp.concatenate([ext[:, :, :NA_BIAS_N], ext[:, :, 1:NA_BIAS_N + 1]], axis=-1)


def _neighborhood_attn(na, bias_t, n_batch, layer):
    steps = GRID_ROWS // NA_Q_ROWS
    nq = NA_Q_ROWS * GRID_W
    ctx0 = n_batch * SEQ // CTX_LEN
    return pl.pallas_call(
        _nattn_kernel,
        grid=(n_batch, steps),
        in_specs=[pl.BlockSpec((nq, 768), lambda b, i: (b * steps + i, 0)),
                  pl.BlockSpec((SEQ, 768), lambda b, i: (b, 0)),
                  pl.BlockSpec((CTX_LEN, 768), lambda b, i: (ctx0 + b, 0)),
                  pl.BlockSpec((1, NA_HEADS, NA_BIAS_N, GRID_W, 128), lambda b, i: (layer, 0, 0, 0, 0))],
        out_specs=pl.BlockSpec((nq, 256), lambda b, i: (b * steps + i, 0)),
        out_shape=jax.ShapeDtypeStruct((n_batch * SEQ, 256), BF16),
        compiler_params=_cparams(2, 40),
        name="neighborhood_attn",
    )(na, na, na, bias_t)


def _ctx_attn_kernel(sink_ref, qk_ref, v_ref, na_ref, oa_ref, on_ref):
    qk = qk_ref[...]
    v = v_ref[...]
    na = na_ref[...]
    outs = []
    for h in range(A_HEADS):
        g = h // (A_HEADS // A_KV_HEADS)
        s = _dot_nt(qk[:, h * 64:(h + 1) * 64], qk[:, 256 + g * 64:256 + (g + 1) * 64])
        sink = jnp.zeros((CTX_LEN, 1), F32) + sink_ref[h]
        outs.append(_softmax_pv([s], [v[:, g * 64:(g + 1) * 64]], sink))
    oa_ref[...] = jnp.concatenate(outs, axis=1).astype(BF16)
    outs = []
    for h in range(NA_HEADS):
        sl = slice(h * 64, (h + 1) * 64)
        s = _dot_nt(na[:, 0:256][:, sl], na[:, 256:512][:, sl])
        outs.append(_softmax_pv([s], [na[:, 512:768][:, sl]]))
    on_ref[...] = jnp.concatenate(outs, axis=1).astype(BF16)


def _ctx_attn(sink, qk, v, na, n_batch):
    ctx0 = n_batch * SEQ // CTX_LEN
    row = lambda b: (ctx0 + b, 0)
    return pl.pallas_call(
        _ctx_attn_kernel,
        grid=(n_batch,),
        in_specs=[pl.BlockSpec(memory_space=pltpu.SMEM),
                  pl.BlockSpec((CTX_LEN, 384), row),
                  pl.BlockSpec((CTX_LEN, 128), row),
                  pl.BlockSpec((CTX_LEN, 768), row)],
        out_specs=[pl.BlockSpec((CTX_LEN, 256), lambda b: (b, 0))] * 2,
        out_shape=[jax.ShapeDtypeStruct((n_batch * CTX_LEN, 256), BF16)] * 2,
        compiler_params=_cparams(1, 32),
        name="ctx_attn",
    )(sink, qk, v, na)


def _outproj_kernel(x_ref, oal_ref, oac_ref, yf_ref, yb_ref, xbc_ref, z_ref, onl_ref, onc_ref, mod_ref,
                    dskip_ref, ng_ref, w_ref, gf_ref, wr1_ref, wr2_ref, br_ref, xo_ref, tok_ref, rt_ref,
                    cnt_ref, *, n_lat_tiles):
    is_lat = pl.program_id(0) < n_lat_tiles
    m = mod_ref[0, 0]
    xs = xbc_ref[...]
    y = yf_ref[...] + yb_ref[...] + dskip_ref[...] * xs
    y = y * _silu(z_ref[...])
    ob = y * lax.rsqrt(jnp.mean(y * y, axis=-1, keepdims=True) + RMS_EPS) * ng_ref[...]
    oa = jnp.where(is_lat, oal_ref[...], oac_ref[...])
    on = jnp.where(is_lat, onl_ref[...], onc_ref[...])
    proj = (_dot(oa, w_ref[0, 0:256, :]) + _dot(ob.astype(BF16), w_ref[0, 256:768, :])
            + _dot(on, w_ref[0, 768:1024, :]))
    x = x_ref[...] + m[2:3] * proj
    xo_ref[...] = x
    t = x * lax.rsqrt(jnp.mean(x * x, axis=-1, keepdims=True) + RMS_EPS) * gf_ref[...]
    t = t * (1.0 + m[4:5]) + m[3:4]
    _store_row_tiles(tok_ref, t)
    t1 = t.astype(BF16)
    t2 = (t - t1.astype(F32)).astype(BF16)
    logits = _dot(t1, wr1_ref[...]) + _dot(t1, wr2_ref[...]) + _dot(t2, wr1_ref[...]) + br_ref[...]

    lane = _iota(logits.shape, 1)
    big = jnp.int32(1 << 20)
    is_g = jnp.logical_and(lane >= N_EXPERTS, lane < N_EXPERTS + MOE_GROUPS)
    gl = jnp.where(is_g, logits, NEG_INF)
    gmax = gl.max(axis=-1, keepdims=True)
    g_w = 1.0 / jnp.exp(gl - gmax).sum(axis=-1, keepdims=True)
    g_idx = jnp.where(gl == gmax, lane, big).min(axis=-1, keepdims=True) - N_EXPERTS
    in_grp = jnp.logical_and(lane < N_EXPERTS, (lane >> 3) == g_idx)
    el = jnp.where(in_grp, logits, NEG_INF)
    l1 = el.max(axis=-1, keepdims=True)
    i1 = jnp.where(el == l1, lane, big).min(axis=-1, keepdims=True)
    el2 = jnp.where(lane == i1, NEG_INF, el)
    l2 = el2.max(axis=-1, keepdims=True)
    i2 = jnp.where(el2 == l2, lane, big).min(axis=-1, keepdims=True)
    e2 = jnp.exp(l2 - l1)
    w1 = g_w / (1.0 + e2)
    w2 = w1 * e2
    rt_ref[...] = jnp.where(lane == 0, i1.astype(F32), jnp.where(lane == 1, i2.astype(F32),
                            jnp.where(lane == 2, w1, jnp.where(lane == 3, w2, 0.0))))
    hot = jnp.logical_or(lane == i1, lane == i2)
    cnt_ref[0] = jnp.where(hot, 1.0, 0.0).sum(axis=0, keepdims=True)


def _outproj(xc, oa_l, oa_c, y_f, y_b, xbc_act, z, on_l, on_c, mod, dskip, norm_g, w_out, g_ffn, wr1, wr2, br,
             n_batch, n_tiles, layer):
    n_lat_tiles = n_batch * SEQ // TILE
    tpb = SEQ // TILE
    row = lambda i: (i, 0)
    lat = lambda i: (jnp.minimum(i, n_lat_tiles - 1), 0)
    ctx = lambda i: (jnp.maximum(i - n_lat_tiles, 0), 0)
    modrow = lambda i: (layer, _tile_mod_row(i, n_lat_tiles, tpb, n_batch), 0, 0)
    const = lambda i: (0, 0)
    kern = functools.partial(_outproj_kernel, n_lat_tiles=n_lat_tiles)
    return pl.pallas_call(
        kern,
        grid=(n_tiles,),
        in_specs=[pl.BlockSpec((TILE, D_MODEL), row),
                  pl.BlockSpec((TILE, 256), lat),
                  pl.BlockSpec((TILE, 256), ctx),
                  pl.BlockSpec((TILE, SSD_INNER), row),
                  pl.BlockSpec((TILE, SSD_INNER), row),
                  pl.BlockSpec((TILE, SSD_INNER), row),
                  pl.BlockSpec((TILE, SSD_INNER), row),
                  pl.BlockSpec((TILE, 256), lat),
                  pl.BlockSpec((TILE, 256), ctx),
                  pl.BlockSpec((1, 1, N_MOD, D_MODEL), modrow),
                  pl.BlockSpec((1, SSD_INNER), const),
                  pl.BlockSpec((1, SSD_INNER), const),
                  pl.BlockSpec((1, D_MODEL, D_MODEL), lambda i: (layer, 0, 0)),
                  pl.BlockSpec((1, D_MODEL), const),
                  pl.BlockSpec((D_MODEL, LANES), const),
                  pl.BlockSpec((D_MODEL, LANES), const),
                  pl.BlockSpec((1, LANES), const)],
        out_specs=[pl.BlockSpec((TILE, D_MODEL), row),
                   pl.BlockSpec((TILE * ROW_SLABS, LANES), row),
                   pl.BlockSpec((TILE, LANES), row),
                   pl.BlockSpec((1, 1, LANES), lambda i: (i, 0, 0))],
        out_shape=[jax.ShapeDtypeStruct((n_tiles * TILE, D_MODEL), F32),
                   jax.ShapeDtypeStruct((n_tiles * TILE * ROW_SLABS, LANES), F32),
                   jax.ShapeDtypeStruct((n_tiles * TILE, LANES), F32),
                   jax.ShapeDtypeStruct((n_tiles, 1, LANES), F32)],
        compiler_params=_cparams(1, 56),
        name="outproj",
    )(xc, oa_l, oa_c, y_f, y_b, xbc_act, z, on_l, on_c, mod, dskip, norm_g, w_out, g_ffn, wr1, wr2, br)


MOE_TM = 512


def _moe_max_tiles(n_tokens):
    return (2 * n_tokens + N_EXPERTS * (MOE_TM - 1)) // MOE_TM


def _moe_plan(cnt, n_tiles):
    cnt = cnt[:, 0, :N_EXPERTS].astype(jnp.int32)
    tot = cnt.sum(axis=0)
    tiles_e = (tot + MOE_TM - 1) // MOE_TM
    t_end = jnp.cumsum(tiles_e)
    t_start = t_end - tiles_e
    base = (t_start * MOE_TM)[None, :] + jnp.cumsum(cnt, axis=0) - cnt
    n_used = t_end[-1]
    n_max = _moe_max_tiles(n_tiles * TILE)
    te = jnp.sum(jnp.arange(n_max)[:, None] >= t_end[None, :], axis=1)
    te = jnp.minimum(te, jnp.sum((n_used - 1) >= t_end)).astype(jnp.int32)
    tail = jnp.where(tiles_e > 0, t_end - 1, n_max).astype(jnp.int32)
    base_f = jnp.zeros((n_tiles, 1, LANES), F32).at[:, 0, :N_EXPERTS].set(base.astype(F32))
    return base_f, te, n_used.reshape(1).astype(jnp.int32), tail


def _pos_kernel(rt_ref, base_ref, pos_ref):
    rt = rt_ref[...]
    lane = _iota(rt.shape, 1)
    hot1 = lane == rt[:, 0:1].astype(jnp.int32)
    hot2 = lane == rt[:, 1:2].astype(jnp.int32)
    hot = jnp.where(jnp.logical_or(hot1, hot2), 1.0, 0.0).astype(BF16)
    n = rt.shape[0]
    strict = jnp.where(_iota((n, n), 0) > _iota((n, n), 1), 1.0, 0.0).astype(BF16)
    slot = base_ref[0] + _dot(strict, hot)
    p1 = jnp.where(hot1, slot, 0.0).sum(axis=-1, keepdims=True)
    p2 = jnp.where(hot2, slot, 0.0).sum(axis=-1, keepdims=True)
    pos_ref[...] = jnp.where(lane == 0, p1, jnp.where(lane == 1, p2, 0.0)).astype(jnp.int32)


def _positions(rt, base, n_rows, tok):
    return pl.pallas_call(
        _pos_kernel,
        grid=(n_rows // tok,),
        in_specs=[pl.BlockSpec((tok, LANES), lambda i: (i, 0)),
                  pl.BlockSpec((1, 1, LANES), lambda i: (i * (tok // TILE), 0, 0))],
        out_specs=pl.BlockSpec((tok, LANES), lambda i: (i, 0)),
        out_shape=jax.ShapeDtypeStruct((n_rows, LANES), jnp.int32),
        compiler_params=_cparams(1, 32),
        name="moe_positions",
    )(rt, base)


def _dispatch_kernel(tail_ref, nu_ref, pos_ref, tok_ref, xs_ref, zbuf, zsem, sem, *, n_max):
    i = pl.program_id(0)

    def zero_tile(j):
        start = pl.multiple_of(j * (MOE_TM * ROW_SLABS), MOE_TM * ROW_SLABS)
        return pltpu.make_async_copy(zbuf, xs_ref.at[pl.ds(start, MOE_TM * ROW_SLABS), :], zsem)

    def row(ref, r):
        return ref.at[pl.ds(pl.multiple_of(r * ROW_SLABS, ROW_SLABS), ROW_SLABS), :]

    @pl.when(i == 0)
    def _():
        zbuf[...] = jnp.zeros_like(zbuf)
        for e in range(N_EXPERTS):
            @pl.when(tail_ref[e] != n_max)
            def _():
                zero_tile(tail_ref[e]).start()
        lax.fori_loop(nu_ref[0], n_max + 1, lambda j, c: (zero_tile(j).start(), c)[1], 0)
        for e in range(N_EXPERTS):
            @pl.when(tail_ref[e] != n_max)
            def _():
                zero_tile(tail_ref[e]).wait()
        lax.fori_loop(nu_ref[0], n_max + 1, lambda j, c: (zero_tile(j).wait(), c)[1], 0)

    def body(r, carry):
        src = row(tok_ref, r)
        pltpu.make_async_copy(src, row(xs_ref, pos_ref[0, 0, 2 * r]), sem).start(priority=0)
        pltpu.make_async_copy(src, row(xs_ref, pos_ref[0, 0, 2 * r + 1]), sem).start(priority=1)
        return carry

    n_rows = tok_ref.shape[0] // ROW_SLABS
    lax.fori_loop(0, n_rows, body, 0, unroll=8)
    for _ in range(2):
        pltpu.make_async_copy(tok_ref, xs_ref.at[pl.ds(0, n_rows * ROW_SLABS), :], sem).wait()


def _dispatch(tail, n_used, pos_s, tok, n_max):
    kern = functools.partial(_dispatch_kernel, n_max=n_max)
    n_steps = pos_s.shape[0]
    step_rows = pos_s.shape[2] // 2
    return pl.pallas_call(
        kern,
        grid_spec=pltpu.PrefetchScalarGridSpec(
            num_scalar_prefetch=2,
            grid=(n_steps,),
            in_specs=[pl.BlockSpec((1, 1, 2 * step_rows), lambda i, tail, nu: (i, 0, 0), memory_space=pltpu.SMEM),
                      pl.BlockSpec((step_rows * ROW_SLABS, LANES), lambda i, tail, nu: (i, 0))],
            out_specs=pl.BlockSpec(memory_space=pl.ANY),
            scratch_shapes=[pltpu.VMEM((MOE_TM * ROW_SLABS, LANES), F32), pltpu.SemaphoreType.DMA(()),
                            pltpu.SemaphoreType.DMA(())]),
        out_shape=jax.ShapeDtypeStruct(((n_max + 1) * MOE_TM * ROW_SLABS, LANES), F32),
        compiler_params=_cparams(1, 32),
        name="moe_dispatch",
    )(tail, n_used, pos_s, tok)


def _experts_kernel(te_ref, nu_ref, xs_ref, wg_ref, wu_ref, wd_ref, ys_ref):
    used = pl.program_id(0) < nu_ref[0]

    @pl.when(used)
    def _():
        x = _load_row_tiles(xs_ref, MOE_TM).astype(BF16)
        gate = _dot(x, wg_ref[0].astype(BF16))
        up = _dot(x, wu_ref[0].astype(BF16))
        hid = (_silu(gate) * up).astype(BF16)
        _store_row_tiles(ys_ref, _dot(hid, wd_ref[0].astype(BF16)))

    @pl.when(jnp.logical_not(used))
    def _():
        ys_ref[...] = jnp.zeros_like(ys_ref)


def _experts(te, n_used, xs, w_gate, w_up, w_down, n_max, layer):
    rows = lambda j, te, nu: (jnp.minimum(j, nu[0] - 1), 0)
    wsel = lambda j, te, nu: (layer * N_EXPERTS + te[j], 0, 0)
    return pl.pallas_call(
        _experts_kernel,
        grid_spec=pltpu.PrefetchScalarGridSpec(
            num_scalar_prefetch=2,
            grid=(n_max,),
            in_specs=[pl.BlockSpec((MOE_TM * ROW_SLABS, LANES), rows),
                      pl.BlockSpec((1, D_MODEL, D_EXPERT), wsel),
                      pl.BlockSpec((1, D_MODEL, D_EXPERT), wsel),
                      pl.BlockSpec((1, D_EXPERT, D_MODEL), wsel)],
            out_specs=pl.BlockSpec((MOE_TM * ROW_SLABS, LANES), lambda j, te, nu: (j, 0))),
        out_shape=jax.ShapeDtypeStruct((n_max * MOE_TM * ROW_SLABS, LANES), F32),
        compiler_params=_cparams(1, 48),
        name="moe_experts",
    )(te, n_used, xs, w_gate, w_up, w_down)


def _combine_kernel(pos_ref, ys_ref, x_ref, rt_ref, mod_ref, gfin_ref, o_ref, ybuf0, ybuf1, sem, *, final):
    def row(ref, r):
        return ref.at[pl.ds(pl.multiple_of(r * ROW_SLABS, ROW_SLABS), ROW_SLABS), :]

    def body(r, carry):
        pltpu.make_async_copy(row(ys_ref, pos_ref[0, 0, 2 * r]), row(ybuf0, r), sem).start(priority=0)
        pltpu.make_async_copy(row(ys_ref, pos_ref[0, 0, 2 * r + 1]), row(ybuf1, r), sem).start(priority=1)
        return carry

    n = x_ref.shape[0]
    lax.fori_loop(0, n, body, 0, unroll=8)
    for buf in (ybuf0, ybuf1):
        pltpu.make_async_copy(ys_ref.at[pl.ds(0, n * ROW_SLABS), :], buf, sem).wait()
    rt = rt_ref[...]
    f = rt[:, 2:3] * _load_row_tiles(ybuf0, n) + rt[:, 3:4] * _load_row_tiles(ybuf1, n)
    x = x_ref[...] + mod_ref[0, 0][5:6] * f
    if final:
        x = x * lax.rsqrt(jnp.mean(x * x, axis=-1, keepdims=True) + RMS_EPS) * gfin_ref[...]
    o_ref[...] = x


def _combine(pos_s, ys, xmid, rt, mod, g_final, n_batch, layer):
    n_steps = pos_s.shape[0]
    tok = pos_s.shape[2] // 2
    n_lat_steps = n_batch * SEQ // tok
    spb = SEQ // tok
    row = lambda i: (i, 0)
    modrow = lambda i: (layer, _tile_mod_row(i, n_lat_steps, spb, n_batch), 0, 0)
    return pl.pallas_call(
        functools.partial(_combine_kernel, final=(layer == DEPTH - 1)),
        grid=(n_steps,),
        in_specs=[pl.BlockSpec((1, 1, 2 * tok), lambda i: (i, 0, 0), memory_space=pltpu.SMEM),
                  pl.BlockSpec(memory_space=pl.ANY),
                  pl.BlockSpec((tok, D_MODEL), row),
                  pl.BlockSpec((tok, LANES), row),
                  pl.BlockSpec((1, 1, N_MOD, D_MODEL), modrow),
                  pl.BlockSpec((1, D_MODEL), lambda i: (0, 0))],
        out_specs=pl.BlockSpec((tok, D_MODEL), row),
        out_shape=jax.ShapeDtypeStruct((n_steps * tok, D_MODEL), F32),
        scratch_shapes=[pltpu.VMEM((tok * ROW_SLABS, LANES), F32), pltpu.VMEM((tok * ROW_SLABS, LANES), F32),
                        pltpu.SemaphoreType.DMA(())],
        compiler_params=_cparams(1, 48),
        name="moe_combine",
    )(pos_s, ys, xmid, rt, mod, g_final)


MOE_TOK = 1024


def _moe(xmid, tok, rt, cnt, mod, g_final, w_gate, w_up, w_down, n_batch, n_tiles, layer):
    n_rows = n_tiles * TILE
    step = MOE_TOK if (n_rows % MOE_TOK == 0 and (n_batch * SEQ) % MOE_TOK == 0 and SEQ % MOE_TOK == 0) else TILE
    base, te, n_used, tail = _moe_plan(cnt, n_tiles)
    n_max = _moe_max_tiles(n_rows)
    pos = _positions(rt, base, n_rows, step)
    pos_s = pos[:, :2].reshape(n_rows // step, 1, 2 * step)
    xs = _dispatch(tail, n_used, pos_s, tok, n_max)
    ys = _experts(te, n_used, xs, w_gate, w_up, w_down, n_max, layer)
    return _combine(pos_s, ys, xmid, rt, mod, g_final, n_batch, layer)


def _rope_tables():
    t = jnp.arange(SEQ)
    rows_pos = (t // GRID_W).astype(F32)
    cols_pos = (t % GRID_W).astype(F32)
    half = HEAD_DIM // 2
    inv = 1.0 / (ROPE_BASE ** (jnp.arange(0, half, 2, dtype=F32) / half))
    ang_r = rows_pos[:, None] * inv[None, :]
    ang_c = cols_pos[:, None] * inv[None, :]
    ang = jnp.concatenate([ang_r, ang_r, ang_c, ang_c], axis=1)
    cos_h, sin_h = jnp.cos(ang), jnp.sin(ang)
    scale = jnp.concatenate([jnp.full((256,), HEAD_DIM ** -0.5, F32), jnp.ones((128,), F32)])
    cos_t = jnp.tile(cos_h, (1, 6)) * scale
    sin_t = jnp.tile(sin_h, (1, 6)) * scale
    cos_t = jnp.concatenate([cos_t, jnp.broadcast_to(scale, (TILE, 384))], axis=0)
    sin_t = jnp.concatenate([sin_t, jnp.zeros((TILE, 384), F32)], axis=0)
    return cos_t, sin_t


def _fused_in_weight(w_in):
    wq, wk, wv = w_in[..., 0:256], w_in[..., 256:384], w_in[..., 384:512]
    o = A_IN
    wz, wxbc, wdt = w_in[..., o:o + 512], w_in[..., o + 512:o + 1280], w_in[..., o + 1280:o + 1296]
    wna = w_in[..., A_IN + SSD_IN:]

    def rot(w):
        w4 = w.reshape(w.shape[:-1] + (w.shape[-1] // 32, 2, 16))
        return jnp.concatenate([-w4[..., 1:2, :], w4[..., 0:1, :]], axis=-2).reshape(w.shape)

    na_scale = jnp.concatenate([jnp.full((256,), HEAD_DIM ** -0.5, F32), jnp.ones((512,), F32)])
    pad = jnp.zeros(w_in.shape[:-1] + (128 - SSD_HEADS,), F32)
    cat = jnp.concatenate([wq, wk, rot(wq), rot(wk), wv, wz, wxbc,
                           wna * na_scale, wdt[..., :SSD_HEADS], pad, wdt[..., SSD_HEADS:], pad], axis=-1)
    return cat.astype(BF16)


def _router_weight(w_rg, b_rg, w_re, b_re):
    w = jnp.concatenate([w_re, w_rg, jnp.zeros((D_MODEL, LANES - N_EXPERTS - MOE_GROUPS), F32)], axis=1)
    b = jnp.concatenate([b_re, b_rg, jnp.zeros((LANES - N_EXPERTS - MOE_GROUPS,), F32)]).reshape(1, LANES)
    w1 = w.astype(BF16)
    w2 = (w - w1.astype(F32)).astype(BF16)
    return w1, w2, b


def kernel(x, c, ctx, c_ctx, w_mod, b_mod, g_mix, w_in, attn_sink, ssd_conv_w, ssd_conv_b, ssd_dt_bias, ssd_a_log, ssd_d, ssd_norm_g, na_rpb, w_out, g_ffn, w_router_group, b_router_group, w_router_expert, b_router_expert, w_exp_gate, w_exp_up, w_exp_down, g_final):
    n_batch, s, d = x.shape
    assert (s, d) == (SEQ, D_MODEL) and ctx.shape[1:] == (CTX_LEN, D_MODEL) and n_batch < 16
    n_lat = n_batch * SEQ
    n_ctx = n_batch * CTX_LEN
    assert n_ctx % TILE == 0
    n_lat_tiles = n_lat // TILE
    n_all_tiles = (n_lat + n_ctx) // TILE

    xc = jnp.concatenate([x.reshape(n_lat, d), ctx.reshape(n_ctx, d)], axis=0)
    cin = jnp.zeros((16, d), F32).at[:n_batch].set(c).at[n_batch].set(c_ctx)
    mod = _modulation(cin, w_mod, b_mod).reshape(DEPTH, 16, N_MOD, d)
    cos_t, sin_t = _rope_tables()
    w_cat = _fused_in_weight(w_in)
    w_out_b = w_out.astype(BF16)
    bias_t = _na_bias_table(na_rpb)
    g_mix3 = g_mix.reshape(DEPTH, 1, d)
    w_gate = w_exp_gate.reshape(DEPTH * N_EXPERTS, D_MODEL, D_EXPERT)
    w_up = w_exp_up.reshape(DEPTH * N_EXPERTS, D_MODEL, D_EXPERT)
    w_down = w_exp_down.reshape(DEPTH * N_EXPERTS, D_EXPERT, D_MODEL)

    for layer in range(DEPTH):
        need_ctx = layer < DEPTH - 1
        conv_w = jnp.zeros((8, SSD_CONV_DIM), F32).at[:SSD_CONV].set(ssd_conv_w[layer])
        qk, v, z, xbc_act, na, dt_raw = _inproj(xc, mod, g_mix3, w_cat, cos_t, sin_t, conv_w,
                                                ssd_conv_b[layer].reshape(1, SSD_CONV_DIM), n_batch, layer)
        sink = attn_sink[layer].astype(F32)
        oa = _window_attn(sink, qk, v, n_batch)
        y_f, y_b = _ssd_scan(xbc_act, dt_raw, ssd_dt_bias[layer], ssd_a_log[layer], n_batch)
        on = _neighborhood_attn(na, bias_t, n_batch, layer)
        oa_c, on_c = _ctx_attn(sink, qk, v, na, n_batch) if need_ctx else (oa, on)
        n_tiles = n_all_tiles if need_ctx else n_lat_tiles
        wr1, wr2, br = _router_weight(w_router_group[layer], b_router_group[layer],
                                      w_router_expert[layer], b_router_expert[layer])
        dskip = jnp.repeat(ssd_d[layer].astype(F32), SSD_INNER // SSD_HEADS).reshape(1, SSD_INNER)
        xmid, tok, rt, cnt = _outproj(xc, oa, oa_c, y_f, y_b, xbc_act, z, on, on_c, mod, dskip,
                                      ssd_norm_g[layer].reshape(1, SSD_INNER), w_out_b,
                                      g_ffn[layer].reshape(1, d), wr1, wr2, br, n_batch, n_tiles, layer)
        xc = _moe(xmid, tok, rt, cnt, mod, g_final.reshape(1, d), w_gate, w_up, w_down, n_batch, n_tiles, layer)

    return xc.reshape(n_batch, SEQ, d)
```

```python
import functools
import math

import jax
import jax.numpy as jnp
import numpy as np
from jax import lax
from jax.experimental import pallas as pl
from jax.experimental.pallas import tpu as pltpu

F32 = jnp.float32
BF16 = jnp.bfloat16

D_MODEL = 1024
SEQ = 2048
DEPTH = 4
GRID_W = 64
GRID_ROWS = SEQ // GRID_W
CTX_LEN = 256
HEAD_DIM = 64
A_HEADS = 4
A_KV_HEADS = 2
A_BLOCK = 128
ROPE_BASE = 10000.0
SSD_HEADS = 8
SSD_INNER = 512
SSD_STATE = 64
SSD_CONV = 5
SSD_CHUNK = 128
SSD_STEP_CHUNKS = 2
SSD_CONV_DIM = 768
NA_HEADS = 4
NA_WIN_ROWS = 8
NA_WIN_COLS = 16
A_IN = 512
SSD_IN = 1296
MOE_GROUPS = 4
MOE_EXPERTS = 8
N_EXPERTS = MOE_GROUPS * MOE_EXPERTS
D_EXPERT = 256
N_MOD = 6
RMS_EPS = 1e-6
NEG_INF = -1e30

TILE = 512
LANES = 128
C_QK, C_QKP, C_V, C_Z, C_XBC, C_NA, C_DT = 0, 384, 768, 896, 1408, 2176, 2944
N_COLS = 3200
NA_Q_ROWS = 4
NA_K_ROWS = NA_WIN_ROWS + NA_Q_ROWS
NA_BIAS_OFF = NA_Q_ROWS
NA_BIAS_N = NA_BIAS_OFF + (NA_K_ROWS - 2) + (NA_WIN_ROWS - 1) + 1


def _cparams(n_axes, vmem_mb):
    return pltpu.CompilerParams(dimension_semantics=("arbitrary",) * n_axes,
                                vmem_limit_bytes=vmem_mb << 20)


def _split3(x):
    h1 = x.astype(BF16)
    r1 = x - h1.astype(F32)
    h2 = r1.astype(BF16)
    h3 = (r1 - h2.astype(F32)).astype(BF16)
    return h1, h2, h3


def _dot(a, b):
    return jnp.dot(a, b, preferred_element_type=F32)


def _dot_nt(a, b):
    return lax.dot_general(a, b, (((1,), (1,)), ((), ())), preferred_element_type=F32)


def _dot_exact_lhs(lhs_bf16, x):
    return _dot(jnp.concatenate([lhs_bf16] * 3, axis=1), jnp.concatenate(_split3(x), axis=0))


def _dot_exact_rhs(x, rhs3_bf16):
    return _dot(jnp.concatenate(_split3(x), axis=1), rhs3_bf16)


def _silu(x):
    return x * jax.nn.sigmoid(x)


ROW_SLABS = D_MODEL // LANES


def _store_row_tiles(ref, x):
    n = x.shape[0]
    for s in range(ROW_SLABS):
        ref[pl.ds(s, n, stride=ROW_SLABS), :] = x[:, s * LANES:(s + 1) * LANES]


def _load_row_tiles(ref, n):
    return jnp.concatenate([ref[pl.ds(s, n, stride=ROW_SLABS), :] for s in range(ROW_SLABS)], axis=1)


def _iota(shape, dim):
    return lax.broadcasted_iota(jnp.int32, shape, dim)


def _mod_kernel(c_ref, w_ref, b_ref, o_ref):
    a = _silu(c_ref[...])
    a1, a2, _ = _split3(a)
    w = w_ref[0]
    w1 = w.astype(BF16)
    w2 = (w - w1.astype(F32)).astype(BF16)
    o_ref[0] = _dot(a1, w1) + _dot(a1, w2) + _dot(a2, w1) + b_ref[0]


def _modulation(cin, w_mod, b_mod):
    nt = 1024
    return pl.pallas_call(
        _mod_kernel,
        grid=(DEPTH, N_MOD * D_MODEL // nt),
        in_specs=[pl.BlockSpec((16, D_MODEL), lambda l, j: (0, 0)),
                  pl.BlockSpec((1, D_MODEL, nt), lambda l, j: (l, 0, j)),
                  pl.BlockSpec((1, 1, nt), lambda l, j: (l, 0, j))],
        out_specs=pl.BlockSpec((1, 16, nt), lambda l, j: (l, 0, j)),
        out_shape=jax.ShapeDtypeStruct((DEPTH, 16, N_MOD * D_MODEL), F32),
        compiler_params=_cparams(2, 40),
        name="modulation",
    )(cin, w_mod, b_mod.reshape(DEPTH, 1, N_MOD * D_MODEL))


def _inproj_kernel(prev_ref, x_ref, next_ref, mod_ref, g_ref, w_ref, cos_ref, sin_ref, cw_ref, cb_ref,
                   qk_ref, v_ref, z_ref, xbc_ref, na_ref, dt_ref, *, n_lat_tiles, tiles_per_batch):
    xe = jnp.concatenate([prev_ref[...], x_ref[...], next_ref[...]], axis=0)
    m = mod_ref[0, 0]
    he = xe * lax.rsqrt(jnp.mean(xe * xe, axis=-1, keepdims=True) + RMS_EPS) * g_ref[0]
    hbe = (he * (1.0 + m[1:2]) + m[0:1]).astype(BF16)
    hb = hbe[8:8 + TILE]

    def mm(lo, hi):
        return _dot(hb, w_ref[0, :, lo:hi])

    qk = mm(C_QK, C_QKP) * cos_ref[...] + mm(C_QKP, C_V) * sin_ref[...]
    qk_ref[...] = qk.astype(BF16)
    v_ref[...] = mm(C_V, C_Z).astype(BF16)
    z_ref[...] = mm(C_Z, C_XBC)
    na_ref[...] = mm(C_NA, C_DT).astype(BF16)
    dt_ref[...] = mm(C_DT, N_COLS)

    i = pl.program_id(0)
    is_lat = i < n_lat_tiles
    r = _iota((TILE, 1), 0)
    pos = jnp.where(is_lat, (i % tiles_per_batch) * TILE + r, r & (CTX_LEN - 1))
    seq_len = jnp.where(is_lat, SEQ, CTX_LEN)
    xbc_e = _dot(hbe, w_ref[0, :, C_XBC:C_NA])
    cw = cw_ref[...]
    acc = jnp.zeros((TILE, SSD_CONV_DIM), F32) + cb_ref[...]
    for k in range(SSD_CONV):
        off = k - SSD_CONV // 2
        tap = xbc_e[8 + off:8 + off + TILE, :]
        if off != 0:
            tap = jnp.where(jnp.logical_and(pos + off >= 0, pos + off < seq_len), tap, 0.0)
        acc = acc + tap * cw[k:k + 1, :]
    xbc_ref[...] = _silu(acc)


def _tile_mod_row(i, n_lat_tiles, tiles_per_batch, n_batch):
    return jnp.where(i < n_lat_tiles, i // tiles_per_batch, n_batch)


def _inproj(xc, mod, g_mix, w_cat, cos_t, sin_t, conv_w, conv_b, n_batch, layer):
    nt = xc.shape[0]
    n_lat_tiles = n_batch * SEQ // TILE
    tpb = SEQ // TILE
    t8 = TILE // 8
    row = lambda i: (i, 0)
    modrow = lambda i: (layer, _tile_mod_row(i, n_lat_tiles, tpb, n_batch), 0, 0)
    posrow = lambda i: (jnp.where(i < n_lat_tiles, i % tpb, tpb), 0)
    lay = lambda i: (layer, 0, 0)
    const = lambda i: (0, 0)
    outs = [(384, BF16), (128, BF16), (512, F32), (768, F32), (768, BF16), (256, F32)]
    kern = functools.partial(_inproj_kernel, n_lat_tiles=n_lat_tiles, tiles_per_batch=tpb)
    return pl.pallas_call(
        kern,
        grid=(nt // TILE,),
        in_specs=[pl.BlockSpec((8, D_MODEL), lambda i: (jnp.maximum(i * t8 - 1, 0), 0)),
                  pl.BlockSpec((TILE, D_MODEL), row),
                  pl.BlockSpec((8, D_MODEL), lambda i: (jnp.minimum(i * t8 + t8, nt // 8 - 1), 0)),
                  pl.BlockSpec((1, 1, N_MOD, D_MODEL), modrow),
                  pl.BlockSpec((1, 1, D_MODEL), lay),
                  pl.BlockSpec((1, D_MODEL, N_COLS), lay),
                  pl.BlockSpec((TILE, 384), posrow),
                  pl.BlockSpec((TILE, 384), posrow),
                  pl.BlockSpec((8, SSD_CONV_DIM), const),
                  pl.BlockSpec((1, SSD_CONV_DIM), const)],
        out_specs=[pl.BlockSpec((TILE, w), row) for w, _ in outs],
        out_shape=[jax.ShapeDtypeStruct((nt, w), dt) for w, dt in outs],
        compiler_params=_cparams(1, 56),
        name="inproj",
    )(xc, xc, xc, mod, g_mix, w_cat, cos_t, sin_t, conv_w, conv_b)


def _ssd_kernel(xf_ref, dtf_ref, xb_ref, dtb_ref, bias_ref, alog_ref, e512_ref, e1024_ref,
                yf_ref, yb_ref, stf_ref, stb_ref):
    @pl.when(pl.program_id(1) == 0)
    def _():
        stf_ref[...] = jnp.zeros_like(stf_ref)
        stb_ref[...] = jnp.zeros_like(stb_ref)

    for j in range(SSD_STEP_CHUNKS):
        rf = j * SSD_CHUNK
        rb = (SSD_STEP_CHUNKS - 1 - j) * SSD_CHUNK
        _ssd_chunk(0, rf, xf_ref, dtf_ref, bias_ref[0], alog_ref[0], e512_ref, e1024_ref, yf_ref, stf_ref)
        _ssd_chunk(1, rb, xb_ref, dtb_ref, bias_ref[1], alog_ref[1], e512_ref, e1024_ref, yb_ref, stb_ref)


def _ssd_chunk(d, r0, xbc_ref, dt_ref, dt_bias, a_log, e512_ref, e1024_ref, y_ref, st_ref):
    q = SSD_CHUNK
    xbc = xbc_ref[r0:r0 + q, :]
    xs = xbc[:, :SSD_INNER]
    bm = xbc[:, SSD_INNER:SSD_INNER + 128]
    cm = xbc[:, SSD_INNER + 128:]
    dtr = dt_ref[r0:r0 + q, :] + dt_bias
    dt = jnp.maximum(dtr, 0.0) + jnp.log(1.0 + jnp.exp(-jnp.abs(dtr)))
    a = -jnp.exp(a_log)
    da = dt * a

    ri = _iota((q, q), 0)
    ci = _iota((q, q), 1)
    tri = (ri >= ci) if d == 0 else (ri <= ci)
    trib = jnp.where(tri, 1.0, 0.0).astype(BF16)
    acs = _dot_exact_lhs(trib, da)
    acs_t = acs.T

    both_e = _dot_exact_rhs(jnp.concatenate([dt, acs], axis=0), e512_ref[...])
    dt_e = both_e[:q]
    acs_e = both_e[q:]
    acs_e2 = _dot_exact_rhs(acs, e1024_ref[...])
    tot_e = acs_e[q - 1:q, :] if d == 0 else acs_e[0:1, :]

    xdt = xs * dt_e
    xdec = (xdt * jnp.exp(tot_e - acs_e)).astype(BF16)
    btb = bm.T.astype(BF16)
    lane = _iota((q, 128), 1)
    cm0 = jnp.where(lane < SSD_STATE, cm, 0.0).astype(BF16)
    cm1 = jnp.where(lane >= SSD_STATE, cm, 0.0).astype(BF16)
    cbs = (_dot(cm0, btb), _dot(cm1, btb))

    st = st_ref[...]
    y_off = _dot(cm.astype(BF16), st.astype(BF16)) * jnp.exp(acs_e)
    s_all = _dot(btb, xdec)
    same = (_iota((q, SSD_INNER), 0) >> 6) == (_iota((q, SSD_INNER), 1) >> 8)
    st_ref[...] = jnp.where(same, st * jnp.exp(tot_e) + s_all, 0.0)

    for pair in range(SSD_HEADS // 2):
        cb = cbs[pair // 2]
        xp = xdt[:, pair * 128:(pair + 1) * 128]
        acc = None
        for k in range(2):
            h = 2 * pair + k
            seg = acs_e2[:, h * 128:(h + 1) * 128] - acs_t[h:h + 1, :]
            lmat = jnp.exp(jnp.where(tri, seg, NEG_INF))
            g = (cb * lmat).astype(BF16)
            rhs = jnp.where((lane < 64) if k == 0 else (lane >= 64), xp, 0.0).astype(BF16)
            t = _dot(g, rhs)
            acc = t if acc is None else acc + t
        y_ref[r0:r0 + q, pair * 128:(pair + 1) * 128] = acc + y_off[:, pair * 128:(pair + 1) * 128]


def _ssd_scan(xbc_act, dt_raw, dt_bias, a_log, n_batch):
    nt = xbc_act.shape[0]
    rows = SSD_STEP_CHUNKS * SSD_CHUNK
    n_lat_blk = n_batch * (SEQ // rows)
    lat_c = SEQ // rows
    ctx_c = CTX_LEN // rows
    n_steps = lat_c + ctx_c

    def blk_f(b, c):
        return jnp.where(c < ctx_c, n_lat_blk + b * ctx_c + c, b * lat_c + c - ctx_c)

    def blk_b(b, c):
        return jnp.where(c < ctx_c, n_lat_blk + b * ctx_c + ctx_c - 1 - c, b * lat_c + n_steps - 1 - c)

    heads = np.arange(128)
    e512 = (heads[:, None] == (np.arange(512)[None, :] // 64)).astype(np.float32)
    e1024 = (heads[:, None] == (np.arange(1024)[None, :] // 128)).astype(np.float32)
    dtb = jnp.zeros((2, 1, 128), F32).at[:, 0, :SSD_HEADS].set(dt_bias)
    alog = jnp.zeros((2, 1, 128), F32).at[:, 0, :SSD_HEADS].set(a_log)
    const3 = lambda b, c: (0, 0, 0)
    const2 = lambda b, c: (0, 0)
    return pl.pallas_call(
        _ssd_kernel,
        grid=(n_batch, n_steps),
        in_specs=[pl.BlockSpec((rows, SSD_CONV_DIM), lambda b, c: (blk_f(b, c), 0)),
                  pl.BlockSpec((rows, 128), lambda b, c: (blk_f(b, c), 0)),
                  pl.BlockSpec((rows, SSD_CONV_DIM), lambda b, c: (blk_b(b, c), 0)),
                  pl.BlockSpec((rows, 128), lambda b, c: (blk_b(b, c), 1)),
                  pl.BlockSpec((2, 1, 128), const3),
                  pl.BlockSpec((2, 1, 128), const3),
                  pl.BlockSpec((384, 512), const2),
                  pl.BlockSpec((384, 1024), const2)],
        out_specs=[pl.BlockSpec((rows, SSD_INNER), lambda b, c: (blk_f(b, c), 0)),
                   pl.BlockSpec((rows, SSD_INNER), lambda b, c: (blk_b(b, c), 0))],
        out_shape=[jax.ShapeDtypeStruct((nt, SSD_INNER), F32)] * 2,
        scratch_shapes=[pltpu.VMEM((128, SSD_INNER), F32)] * 2,
        compiler_params=_cparams(2, 32),
        name="ssd_scan",
    )(xbc_act, dt_raw, xbc_act, dt_raw, dtb, alog, jnp.asarray(np.tile(e512, (3, 1)), BF16),
      jnp.asarray(np.tile(e1024, (3, 1)), BF16))


def _softmax_pv(s_list, v_list, extra_logit=None):
    m = s_list[0].max(axis=-1, keepdims=True)
    for s in s_list[1:]:
        m = jnp.maximum(m, s.max(axis=-1, keepdims=True))
    if extra_logit is not None:
        m = jnp.maximum(m, extra_logit)
    den = None
    o = None
    for s, v in zip(s_list, v_list):
        p = jnp.exp(s - m)
        ps = p.sum(axis=-1, keepdims=True)
        den = ps if den is None else den + ps
        t = _dot(p.astype(BF16), v)
        o = t if o is None else o + t
    if extra_logit is not None:
        den = den + jnp.exp(extra_logit - m)
    return o / den


A_QB = 2
A_KB = A_QB + 2


def _wattn_kernel(sink_ref, q_ref, k_ref, v_ref, kc_ref, vc_ref, o_ref):
    n = pl.program_id(1) * A_QB
    nb = SEQ // A_BLOCK
    nq = A_QB * A_BLOCK
    nk = A_KB * A_BLOCK
    start = pl.multiple_of(jnp.clip(n - 1, 0, nb - A_KB) * A_BLOCK, A_BLOCK)
    q = q_ref[:, 0:256]
    kw = k_ref[pl.ds(start, nk), 256:384]
    vw = v_ref[pl.ds(start, nk), :]
    kc = kc_ref[:, 256:384]
    vc = vc_ref[...]
    qrow = _iota((2 * nq, nk), 0)
    qpos = n * A_BLOCK + jnp.where(qrow < nq, qrow, qrow - nq)
    kpos = start + _iota((2 * nq, nk), 1)
    valid = jnp.abs(qpos - kpos) <= A_BLOCK
    top = _iota((2 * nq, 1), 0) < nq
    outs = []
    for g in range(A_KV_HEADS):
        qg = jnp.concatenate([q[:, (2 * g) * 64:(2 * g + 1) * 64],
                              q[:, (2 * g + 1) * 64:(2 * g + 2) * 64]], axis=0)
        kg = kw[:, g * 64:(g + 1) * 64]
        vg = vw[:, g * 64:(g + 1) * 64]
        s_loc = jnp.where(valid, _dot_nt(qg, kg), NEG_INF)
        s_ctx = _dot_nt(qg, kc[:, g * 64:(g + 1) * 64])
        sink = jnp.where(top, sink_ref[2 * g], sink_ref[2 * g + 1])
        o = _softmax_pv([s_loc, s_ctx], [vg, vc[:, g * 64:(g + 1) * 64]], sink)
        outs += [o[:nq], o[nq:]]
    o_ref[...] = jnp.concatenate(outs, axis=1).astype(BF16)


def _nattn_kernel(q_ref, kv_ref, c_ref, bias_ref, o_ref):
    i = pl.program_id(1)
    r0 = i * NA_Q_ROWS
    srow = jnp.clip(r0 - NA_WIN_ROWS // 2, 0, GRID_ROWS - NA_K_ROWS)
    start = pl.multiple_of(srow * GRID_W, GRID_W)
    nq = NA_Q_ROWS * GRID_W
    nk = NA_K_ROWS * GRID_W
    q = q_ref[:, 0:256]
    kw = kv_ref[pl.ds(start, nk), 256:512]
    vw = kv_ref[pl.ds(start, nk), 512:768]
    kc = c_ref[:, 256:512]
    vc = c_ref[:, 512:768]
    qrow = r0 + (_iota((nq, nk), 0) >> 6)
    krow = srow + (_iota((nq, nk), 1) >> 6)
    rs = jnp.clip(qrow - NA_WIN_ROWS // 2, 0, GRID_ROWS - NA_WIN_ROWS)
    valid = jnp.logical_and(krow >= rs, krow < rs + NA_WIN_ROWS)
    outs = []
    for h in range(NA_HEADS):
        sl = slice(h * 64, (h + 1) * 64)
        rows = []
        for qi in range(NA_Q_ROWS):
            blocks = []
            for p in range(NA_K_ROWS // 2):
                idx = srow + 2 * p - (r0 + qi) + (NA_WIN_ROWS - 1) + NA_BIAS_OFF
                blocks.append(bias_ref[0, h, idx])
            rows.append(jnp.concatenate(blocks, axis=1))
        bias = jnp.concatenate(rows, axis=0)
        s_loc = jnp.where(valid, _dot_nt(q[:, sl], kw[:, sl]) + bias, NEG_INF)
        s_ctx = _dot_nt(q[:, sl], kc[:, sl])
        outs.append(_softmax_pv([s_loc, s_ctx], [vw[:, sl], vc[:, sl]]))
    o_ref[...] = jnp.concatenate(outs, axis=1).astype(BF16)


def _na_bias_table(rpb):
    cq = np.arange(GRID_W)
    kcol = np.arange(GRID_W)
    cs = np.clip(cq - NA_WIN_COLS // 2, 0, GRID_W - NA_WIN_COLS)
    col_valid = (kcol[None, :] >= cs[:, None]) & (kcol[None, :] < cs[:, None] + NA_WIN_COLS)
    coff = np.clip(kcol[None, :] - cq[:, None], -(NA_WIN_COLS - 1), NA_WIN_COLS - 1) + (NA_WIN_COLS - 1)
    n_a = 2 * NA_WIN_ROWS - 1
    n_c = 2 * NA_WIN_COLS - 1
    pick = (np.arange(n_c)[:, None] == coff.reshape(1, -1)).astype(np.float32)
    tm = jnp.einsum("lhak,kn->lhan", rpb.astype(F32), jnp.asarray(pick), precision=lax.Precision.HIGHEST)
    tm = jnp.where(col_valid, tm.reshape(rpb.shape[:3] + (GRID_W, GRID_W)), NEG_INF)
    neg = jnp.full(rpb.shape[:2] + (1, GRID_W, GRID_W), NEG_INF, F32)
    pad_lo = NA_BIAS_OFF
    pad_hi = NA_BIAS_N + 1 - pad_lo - n_a
    ext = jnp.concatenate([neg] * pad_lo + [tm] + [neg] * pad_hi, axis=2)
    return jnp.concatenate([ext[:, :, :NA_BIAS_N], ext[:, :, 1:NA_BIAS_N + 1]], axis=-1)


def _latent_attn_kernel(sink_ref, aq_ref, ak_ref, av_ref, akc_ref, avc_ref, nq_ref, nkv_ref, nc_ref, bias_ref,
                        oa_ref, on_ref):
    _wattn_kernel(sink_ref, aq_ref, ak_ref, av_ref, akc_ref, avc_ref, oa_ref)
    _nattn_kernel(nq_ref, nkv_ref, nc_ref, bias_ref, on_ref)


def _latent_attn(sink, qk, v, na, bias_t, n_batch, layer):
    nq = A_QB * A_BLOCK
    assert nq == NA_Q_ROWS * GRID_W
    steps = SEQ // nq
    ctx0 = n_batch * SEQ // CTX_LEN
    qrow = lambda b, n: (b * steps + n, 0)
    full = lambda b, n: (b, 0)
    ctx = lambda b, n: (ctx0 + b, 0)
    return pl.pallas_call(
        _latent_attn_kernel,
        grid=(n_batch, steps),
        in_specs=[pl.BlockSpec(memory_space=pltpu.SMEM),
                  pl.BlockSpec((nq, 384), qrow),
                  pl.BlockSpec((SEQ, 384), full),
                  pl.BlockSpec((SEQ, 128), full),
                  pl.BlockSpec((CTX_LEN, 384), ctx),
                  pl.BlockSpec((CTX_LEN, 128), ctx),
                  pl.BlockSpec((nq, 768), qrow),
                  pl.BlockSpec((SEQ, 768), full),
                  pl.BlockSpec((CTX_LEN, 768), ctx),
                  pl.BlockSpec((1, NA_HEADS, NA_BIAS_N, GRID_W, 128), lambda b, n: (layer, 0, 0, 0, 0))],
        out_specs=[pl.BlockSpec((nq, 256), qrow)] * 2,
        out_shape=[jax.ShapeDtypeStruct((n_batch * SEQ, 256), BF16)] * 2,
        compiler_params=_cparams(2, 48),
        name="latent_attn",
    )(sink, qk, qk, v, qk, v, na, na, na, bias_t)


def _ctx_attn_kernel(sink_ref, qk_ref, v_ref, na_ref, oa_ref, on_ref):
    qk = qk_ref[...]
    v = v_ref[...]
    na = na_ref[...]
    outs = []
    for h in range(A_HEADS):
        g = h // (A_HEADS // A_KV_HEADS)
        s = _dot_nt(qk[:, h * 64:(h + 1) * 64], qk[:, 256 + g * 64:256 + (g + 1) * 64])
        sink = jnp.zeros((CTX_LEN, 1), F32) + sink_ref[h]
        outs.append(_softmax_pv([s], [v[:, g * 64:(g + 1) * 64]], sink))
    oa_ref[...] = jnp.concatenate(outs, axis=1).astype(BF16)
    outs = []
    for h in range(NA_HEADS):
        sl = slice(h * 64, (h + 1) * 64)
        s = _dot_nt(na[:, 0:256][:, sl], na[:, 256:512][:, sl])
        outs.append(_softmax_pv([s], [na[:, 512:768][:, sl]]))
    on_ref[...] = jnp.concatenate(outs, axis=1).astype(BF16)


def _ctx_attn(sink, qk, v, na, n_batch):
    ctx0 = n_batch * SEQ // CTX_LEN
    row = lambda b: (ctx0 + b, 0)
    return pl.pallas_call(
        _ctx_attn_kernel,
        grid=(n_batch,),
        in_specs=[pl.BlockSpec(memory_space=pltpu.SMEM),
                  pl.BlockSpec((CTX_LEN, 384), row),
                  pl.BlockSpec((CTX_LEN, 128), row),
                  pl.BlockSpec((CTX_LEN, 768), row)],
        out_specs=[pl.BlockSpec((CTX_LEN, 256), lambda b: (b, 0))] * 2,
        out_shape=[jax.ShapeDtypeStruct((n_batch * CTX_LEN, 256), BF16)] * 2,
        compiler_params=_cparams(1, 32),
        name="ctx_attn",
    )(sink, qk, v, na)


def _outproj_kernel(x_ref, oal_ref, oac_ref, yf_ref, yb_ref, xbc_ref, z_ref, onl_ref, onc_ref, mod_ref,
                    dskip_ref, ng_ref, w_ref, gf_ref, wr1_ref, wr2_ref, br_ref, xo_ref, tok_ref, rt_ref,
                    cnt_ref, *, n_lat_tiles):
    is_lat = pl.program_id(0) < n_lat_tiles
    m = mod_ref[0, 0]
    xs = xbc_ref[...]
    y = yf_ref[...] + yb_ref[...] + dskip_ref[...] * xs
    y = y * _silu(z_ref[...])
    ob = y * lax.rsqrt(jnp.mean(y * y, axis=-1, keepdims=True) + RMS_EPS) * ng_ref[...]
    oa = jnp.where(is_lat, oal_ref[...], oac_ref[...])
    on = jnp.where(is_lat, onl_ref[...], onc_ref[...])
    proj = (_dot(oa, w_ref[0, 0:256, :]) + _dot(ob.astype(BF16), w_ref[0, 256:768, :])
            + _dot(on, w_ref[0, 768:1024, :]))
    x = x_ref[...] + m[2:3] * proj
    xo_ref[...] = x
    t = x * lax.rsqrt(jnp.mean(x * x, axis=-1, keepdims=True) + RMS_EPS) * gf_ref[...]
    t = t * (1.0 + m[4:5]) + m[3:4]
    _store_row_tiles(tok_ref, t)
    t1 = t.astype(BF16)
    t2 = (t - t1.astype(F32)).astype(BF16)
    logits = _dot(t1, wr1_ref[...]) + _dot(t1, wr2_ref[...]) + _dot(t2, wr1_ref[...]) + br_ref[...]

    lane = _iota(logits.shape, 1)
    big = jnp.int32(1 << 20)
    is_g = jnp.logical_and(lane >= N_EXPERTS, lane < N_EXPERTS + MOE_GROUPS)
    gl = jnp.where(is_g, logits, NEG_INF)
    gmax = gl.max(axis=-1, keepdims=True)
    g_w = 1.0 / jnp.exp(gl - gmax).sum(axis=-1, keepdims=True)
    g_idx = jnp.where(gl == gmax, lane, big).min(axis=-1, keepdims=True) - N_EXPERTS
    in_grp = jnp.logical_and(lane < N_EXPERTS, (lane >> 3) == g_idx)
    el = jnp.where(in_grp, logits, NEG_INF)
    l1 = el.max(axis=-1, keepdims=True)
    i1 = jnp.where(el == l1, lane, big).min(axis=-1, keepdims=True)
    el2 = jnp.where(lane == i1, NEG_INF, el)
    l2 = el2.max(axis=-1, keepdims=True)
    i2 = jnp.where(el2 == l2, lane, big).min(axis=-1, keepdims=True)
    e2 = jnp.exp(l2 - l1)
    w1 = g_w / (1.0 + e2)
    w2 = w1 * e2
    rt_ref[...] = jnp.where(lane == 0, i1.astype(F32), jnp.where(lane == 1, i2.astype(F32),
                            jnp.where(lane == 2, w1, jnp.where(lane == 3, w2, 0.0))))
    hot = jnp.logical_or(lane == i1, lane == i2)
    cnt_ref[0] = jnp.where(hot, 1.0, 0.0).sum(axis=0, keepdims=True)


def _outproj(xc, oa_l, oa_c, y_f, y_b, xbc_act, z, on_l, on_c, mod, dskip, norm_g, w_out, g_ffn, wr1, wr2, br,
             n_batch, n_tiles, layer):
    n_lat_tiles = n_batch * SEQ // TILE
    tpb = SEQ // TILE
    row = lambda i: (i, 0)
    lat = lambda i: (jnp.minimum(i, n_lat_tiles - 1), 0)
    ctx = lambda i: (jnp.maximum(i - n_lat_tiles, 0), 0)
    modrow = lambda i: (layer, _tile_mod_row(i, n_lat_tiles, tpb, n_batch), 0, 0)
    const = lambda i: (0, 0)
    kern = functools.partial(_outproj_kernel, n_lat_tiles=n_lat_tiles)
    return pl.pallas_call(
        kern,
        grid=(n_tiles,),
        in_specs=[pl.BlockSpec((TILE, D_MODEL), row),
                  pl.BlockSpec((TILE, 256), lat),
                  pl.BlockSpec((TILE, 256), ctx),
                  pl.BlockSpec((TILE, SSD_INNER), row),
                  pl.BlockSpec((TILE, SSD_INNER), row),
                  pl.BlockSpec((TILE, SSD_INNER), row),
                  pl.BlockSpec((TILE, SSD_INNER), row),
                  pl.BlockSpec((TILE, 256), lat),
                  pl.BlockSpec((TILE, 256), ctx),
                  pl.BlockSpec((1, 1, N_MOD, D_MODEL), modrow),
                  pl.BlockSpec((1, SSD_INNER), const),
                  pl.BlockSpec((1, SSD_INNER), const),
                  pl.BlockSpec((1, D_MODEL, D_MODEL), lambda i: (layer, 0, 0)),
                  pl.BlockSpec((1, D_MODEL), const),
                  pl.BlockSpec((D_MODEL, LANES), const),
                  pl.BlockSpec((D_MODEL, LANES), const),
                  pl.BlockSpec((1, LANES), const)],
        out_specs=[pl.BlockSpec((TILE, D_MODEL), row),
                   pl.BlockSpec((TILE * ROW_SLABS, LANES), row),
                   pl.BlockSpec((TILE, LANES), row),
                   pl.BlockSpec((1, 1, LANES), lambda i: (i, 0, 0))],
        out_shape=[jax.ShapeDtypeStruct((n_tiles * TILE, D_MODEL), F32),
                   jax.ShapeDtypeStruct((n_tiles * TILE * ROW_SLABS, LANES), F32),
                   jax.ShapeDtypeStruct((n_tiles * TILE, LANES), F32),
                   jax.ShapeDtypeStruct((n_tiles, 1, LANES), F32)],
        compiler_params=_cparams(1, 56),
        name="outproj",
    )(xc, oa_l, oa_c, y_f, y_b, xbc_act, z, on_l, on_c, mod, dskip, norm_g, w_out, g_ffn, wr1, wr2, br)


MOE_TM = 512


def _moe_max_tiles(n_tokens):
    return (2 * n_tokens + N_EXPERTS * (MOE_TM - 1)) // MOE_TM


def _moe_plan(cnt, n_tiles):
    cnt = cnt[:, 0, :N_EXPERTS].astype(jnp.int32)
    tot = cnt.sum(axis=0)
    tiles_e = (tot + MOE_TM - 1) // MOE_TM
    t_end = jnp.cumsum(tiles_e)
    t_start = t_end - tiles_e
    base = (t_start * MOE_TM)[None, :] + jnp.cumsum(cnt, axis=0) - cnt
    n_used = t_end[-1]
    n_max = _moe_max_tiles(n_tiles * TILE)
    te = jnp.sum(jnp.arange(n_max)[:, None] >= t_end[None, :], axis=1)
    te = jnp.minimum(te, jnp.sum((n_used - 1) >= t_end)).astype(jnp.int32)
    tail = jnp.where(tiles_e > 0, t_end - 1, n_max).astype(jnp.int32)
    base_f = jnp.zeros((n_tiles, 1, LANES), F32).at[:, 0, :N_EXPERTS].set(base.astype(F32))
    return base_f, te, n_used.reshape(1).astype(jnp.int32), tail


def _pos_kernel(rt_ref, base_ref, pos_ref):
    rt = rt_ref[...]
    lane = _iota(rt.shape, 1)
    hot1 = lane == rt[:, 0:1].astype(jnp.int32)
    hot2 = lane == rt[:, 1:2].astype(jnp.int32)
    hot = jnp.where(jnp.logical_or(hot1, hot2), 1.0, 0.0).astype(BF16)
    n = rt.shape[0]
    strict = jnp.where(_iota((n, n), 0) > _iota((n, n), 1), 1.0, 0.0).astype(BF16)
    slot = base_ref[0] + _dot(strict, hot)
    p1 = jnp.where(hot1, slot, 0.0).sum(axis=-1, keepdims=True)
    p2 = jnp.where(hot2, slot, 0.0).sum(axis=-1, keepdims=True)
    pos_ref[...] = jnp.where(lane == 0, p1, jnp.where(lane == 1, p2, 0.0)).astype(jnp.int32)


def _positions(rt, base, n_rows, tok):
    return pl.pallas_call(
        _pos_kernel,
        grid=(n_rows // tok,),
        in_specs=[pl.BlockSpec((tok, LANES), lambda i: (i, 0)),
                  pl.BlockSpec((1, 1, LANES), lambda i: (i * (tok // TILE), 0, 0))],
        out_specs=pl.BlockSpec((tok, LANES), lambda i: (i, 0)),
        out_shape=jax.ShapeDtypeStruct((n_rows, LANES), jnp.int32),
        compiler_params=_cparams(1, 32),
        name="moe_positions",
    )(rt, base)


def _dispatch_kernel(tail_ref, nu_ref, pos_ref, tok_ref, xs_ref, zbuf, zsem, sem, *, n_max):
    i = pl.program_id(0)

    def zero_tile(j):
        start = pl.multiple_of(j * (MOE_TM * ROW_SLABS), MOE_TM * ROW_SLABS)
        return pltpu.make_async_copy(zbuf, xs_ref.at[pl.ds(start, MOE_TM * ROW_SLABS), :], zsem)

    def row(ref, r):
        return ref.at[pl.ds(pl.multiple_of(r * ROW_SLABS, ROW_SLABS), ROW_SLABS), :]

    @pl.when(i == 0)
    def _():
        zbuf[...] = jnp.zeros_like(zbuf)
        for e in range(N_EXPERTS):
            @pl.when(tail_ref[e] != n_max)
            def _():
                zero_tile(tail_ref[e]).start()
        lax.fori_loop(nu_ref[0], n_max + 1, lambda j, c: (zero_tile(j).start(), c)[1], 0)
        for e in range(N_EXPERTS):
            @pl.when(tail_ref[e] != n_max)
            def _():
                zero_tile(tail_ref[e]).wait()
        lax.fori_loop(nu_ref[0], n_max + 1, lambda j, c: (zero_tile(j).wait(), c)[1], 0)

    def body(r, carry):
        src = row(tok_ref, r)
        pltpu.make_async_copy(src, row(xs_ref, pos_ref[0, 0, 2 * r]), sem).start(priority=0)
        pltpu.make_async_copy(src, row(xs_ref, pos_ref[0, 0, 2 * r + 1]), sem).start(priority=1)
        return carry

    n_rows = tok_ref.shape[0] // ROW_SLABS
    lax.fori_loop(0, n_rows, body, 0, unroll=8)
    for _ in range(2):
        pltpu.make_async_copy(tok_ref, xs_ref.at[pl.ds(0, n_rows * ROW_SLABS), :], sem).wait()


def _dispatch(tail, n_used, pos_s, tok, n_max):
    kern = functools.partial(_dispatch_kernel, n_max=n_max)
    n_steps = pos_s.shape[0]
    step_rows = pos_s.shape[2] // 2
    return pl.pallas_call(
        kern,
        grid_spec=pltpu.PrefetchScalarGridSpec(
            num_scalar_prefetch=2,
            grid=(n_steps,),
            in_specs=[pl.BlockSpec((1, 1, 2 * step_rows), lambda i, tail, nu: (i, 0, 0), memory_space=pltpu.SMEM),
                      pl.BlockSpec((step_rows * ROW_SLABS, LANES), lambda i, tail, nu: (i, 0))],
            out_specs=pl.BlockSpec(memory_space=pl.ANY),
            scratch_shapes=[pltpu.VMEM((MOE_TM * ROW_SLABS, LANES), F32), pltpu.SemaphoreType.DMA(()),
                            pltpu.SemaphoreType.DMA(())]),
        out_shape=jax.ShapeDtypeStruct(((n_max + 1) * MOE_TM * ROW_SLABS, LANES), F32),
        compiler_params=_cparams(1, 32),
        name="moe_dispatch",
    )(tail, n_used, pos_s, tok)


def _experts_kernel(te_ref, nu_ref, xs_ref, wg_ref, wu_ref, wd_ref, ys_ref):
    used = pl.program_id(0) < nu_ref[0]

    @pl.when(used)
    def _():
        x = _load_row_tiles(xs_ref, MOE_TM).astype(BF16)
        gate = _dot(x, wg_ref[0].astype(BF16))
        up = _dot(x, wu_ref[0].astype(BF16))
        hid = (_silu(gate) * up).astype(BF16)
        _store_row_tiles(ys_ref, _dot(hid, wd_ref[0].astype(BF16)))

    @pl.when(jnp.logical_not(used))
    def _():
        ys_ref[...] = jnp.zeros_like(ys_ref)


def _experts(te, n_used, xs, w_gate, w_up, w_down, n_max, layer):
    rows = lambda j, te, nu: (jnp.minimum(j, nu[0] - 1), 0)
    wsel = lambda j, te, nu: (layer * N_EXPERTS + te[j], 0, 0)
    return pl.pallas_call(
        _experts_kernel,
        grid_spec=pltpu.PrefetchScalarGridSpec(
            num_scalar_prefetch=2,
            grid=(n_max,),
            in_specs=[pl.BlockSpec((MOE_TM * ROW_SLABS, LANES), rows),
                      pl.BlockSpec((1, D_MODEL, D_EXPERT), wsel),
                      pl.BlockSpec((1, D_MODEL, D_EXPERT), wsel),
                      pl.BlockSpec((1, D_EXPERT, D_MODEL), wsel)],
            out_specs=pl.BlockSpec((MOE_TM * ROW_SLABS, LANES), lambda j, te, nu: (j, 0))),
        out_shape=jax.ShapeDtypeStruct((n_max * MOE_TM * ROW_SLABS, LANES), F32),
        compiler_params=_cparams(1, 48),
        name="moe_experts",
    )(te, n_used, xs, w_gate, w_up, w_down)


def _combine_kernel(pos_ref, ys_ref, x_ref, rt_ref, mod_ref, gfin_ref, o_ref, ybuf0, ybuf1, sem, *, final):
    def row(ref, r):
        return ref.at[pl.ds(pl.multiple_of(r * ROW_SLABS, ROW_SLABS), ROW_SLABS), :]

    def body(r, carry):
        pltpu.make_async_copy(row(ys_ref, pos_ref[0, 0, 2 * r]), row(ybuf0, r), sem).start(priority=0)
        pltpu.make_async_copy(row(ys_ref, pos_ref[0, 0, 2 * r + 1]), row(ybuf1, r), sem).start(priority=1)
        return carry

    n = x_ref.shape[0]
    lax.fori_loop(0, n, body, 0, unroll=8)
    for buf in (ybuf0, ybuf1):
        pltpu.make_async_copy(ys_ref.at[pl.ds(0, n * ROW_SLABS), :], buf, sem).wait()
    rt = rt_ref[...]
    f = rt[:, 2:3] * _load_row_tiles(ybuf0, n) + rt[:, 3:4] * _load_row_tiles(ybuf1, n)
    x = x_ref[...] + mod_ref[0, 0][5:6] * f
    if final:
        x = x * lax.rsqrt(jnp.mean(x * x, axis=-1, keepdims=True) + RMS_EPS) * gfin_ref[...]
    o_ref[...] = x


def _combine(pos_s, ys, xmid, rt, mod, g_final, n_batch, layer):
    n_steps = pos_s.shape[0]
    tok = pos_s.shape[2] // 2
    n_lat_steps = n_batch * SEQ // tok
    spb = SEQ // tok
    row = lambda i: (i, 0)
    modrow = lambda i: (layer, _tile_mod_row(i, n_lat_steps, spb, n_batch), 0, 0)
    return pl.pallas_call(
        functools.partial(_combine_kernel, final=(layer == DEPTH - 1)),
        grid=(n_steps,),
        in_specs=[pl.BlockSpec((1, 1, 2 * tok), lambda i: (i, 0, 0), memory_space=pltpu.SMEM),
                  pl.BlockSpec(memory_space=pl.ANY),
                  pl.BlockSpec((tok, D_MODEL), row),
                  pl.BlockSpec((tok, LANES), row),
                  pl.BlockSpec((1, 1, N_MOD, D_MODEL), modrow),
                  pl.BlockSpec((1, D_MODEL), lambda i: (0, 0))],
        out_specs=pl.BlockSpec((tok, D_MODEL), row),
        out_shape=jax.ShapeDtypeStruct((n_steps * tok, D_MODEL), F32),
        scratch_shapes=[pltpu.VMEM((tok * ROW_SLABS, LANES), F32), pltpu.VMEM((tok * ROW_SLABS, LANES), F32),
                        pltpu.SemaphoreType.DMA(())],
        compiler_params=_cparams(1, 48),
        name="moe_combine",
    )(pos_s, ys, xmid, rt, mod, g_final)


MOE_TOK = 1024


def _moe(xmid, tok, rt, cnt, mod, g_final, w_gate, w_up, w_down, n_batch, n_tiles, layer):
    n_rows = n_tiles * TILE
    step = MOE_TOK if (n_rows % MOE_TOK == 0 and (n_batch * SEQ) % MOE_TOK == 0 and SEQ % MOE_TOK == 0) else TILE
    base, te, n_used, tail = _moe_plan(cnt, n_tiles)
    n_max = _moe_max_tiles(n_rows)
    pos = _positions(rt, base, n_rows, step)
    pos_s = pos[:, :2].reshape(n_rows // step, 1, 2 * step)
    xs = _dispatch(tail, n_used, pos_s, tok, n_max)
    ys = _experts(te, n_used, xs, w_gate, w_up, w_down, n_max, layer)
    return _combine(pos_s, ys, xmid, rt, mod, g_final, n_batch, layer)


def _rope_tables():
    t = jnp.arange(SEQ)
    rows_pos = (t // GRID_W).astype(F32)
    cols_pos = (t % GRID_W).astype(F32)
    half = HEAD_DIM // 2
    inv = 1.0 / (ROPE_BASE ** (jnp.arange(0, half, 2, dtype=F32) / half))
    ang_r = rows_pos[:, None] * inv[None, :]
    ang_c = cols_pos[:, None] * inv[None, :]
    ang = jnp.concatenate([ang_r, ang_r, ang_c, ang_c], axis=1)
    cos_h, sin_h = jnp.cos(ang), jnp.sin(ang)
    scale = jnp.concatenate([jnp.full((256,), HEAD_DIM ** -0.5, F32), jnp.ones((128,), F32)])
    cos_t = jnp.tile(cos_h, (1, 6)) * scale
    sin_t = jnp.tile(sin_h, (1, 6)) * scale
    cos_t = jnp.concatenate([cos_t, jnp.broadcast_to(scale, (TILE, 384))], axis=0)
    sin_t = jnp.concatenate([sin_t, jnp.zeros((TILE, 384), F32)], axis=0)
    return cos_t, sin_t


def _fused_in_weight(w_in):
    wq, wk, wv = w_in[..., 0:256], w_in[..., 256:384], w_in[..., 384:512]
    o = A_IN
    wz, wxbc, wdt = w_in[..., o:o + 512], w_in[..., o + 512:o + 1280], w_in[..., o + 1280:o + 1296]
    wna = w_in[..., A_IN + SSD_IN:]

    def rot(w):
        w4 = w.reshape(w.shape[:-1] + (w.shape[-1] // 32, 2, 16))
        return jnp.concatenate([-w4[..., 1:2, :], w4[..., 0:1, :]], axis=-2).reshape(w.shape)

    na_scale = jnp.concatenate([jnp.full((256,), HEAD_DIM ** -0.5, F32), jnp.ones((512,), F32)])
    pad = jnp.zeros(w_in.shape[:-1] + (128 - SSD_HEADS,), F32)
    cat = jnp.concatenate([wq, wk, rot(wq), rot(wk), wv, wz, wxbc,
                           wna * na_scale, wdt[..., :SSD_HEADS], pad, wdt[..., SSD_HEADS:], pad], axis=-1)
    return cat.astype(BF16)


def _router_weight(w_rg, b_rg, w_re, b_re):
    w = jnp.concatenate([w_re, w_rg, jnp.zeros((D_MODEL, LANES - N_EXPERTS - MOE_GROUPS), F32)], axis=1)
    b = jnp.concatenate([b_re, b_rg, jnp.zeros((LANES - N_EXPERTS - MOE_GROUPS,), F32)]).reshape(1, LANES)
    w1 = w.astype(BF16)
    w2 = (w - w1.astype(F32)).astype(BF16)
    return w1, w2, b


def kernel(x, c, ctx, c_ctx, w_mod, b_mod, g_mix, w_in, attn_sink, ssd_conv_w, ssd_conv_b, ssd_dt_bias, ssd_a_log, ssd_d, ssd_norm_g, na_rpb, w_out, g_ffn, w_router_group, b_router_group, w_router_expert, b_router_expert, w_exp_gate, w_exp_up, w_exp_down, g_final):
    n_batch, s, d = x.shape
    assert (s, d) == (SEQ, D_MODEL) and ctx.shape[1:] == (CTX_LEN, D_MODEL) and n_batch < 16
    n_lat = n_batch * SEQ
    n_ctx = n_batch * CTX_LEN
    assert n_ctx % TILE == 0
    n_lat_tiles = n_lat // TILE
    n_all_tiles = (n_lat + n_ctx) // TILE

    xc = jnp.concatenate([x.reshape(n_lat, d), ctx.reshape(n_ctx, d)], axis=0)
    cin = jnp.zeros((16, d), F32).at[:n_batch].set(c).at[n_batch].set(c_ctx)
    mod = _modulation(cin, w_mod, b_mod).reshape(DEPTH, 16, N_MOD, d)
    cos_t, sin_t = _rope_tables()
    w_cat = _fused_in_weight(w_in)
    w_out_b = w_out.astype(BF16)
    bias_t = _na_bias_table(na_rpb)
    g_mix3 = g_mix.reshape(DEPTH, 1, d)
    w_gate = w_exp_gate.reshape(DEPTH * N_EXPERTS, D_MODEL, D_EXPERT)
    w_up = w_exp_up.reshape(DEPTH * N_EXPERTS, D_MODEL, D_EXPERT)
    w_down = w_exp_down.reshape(DEPTH * N_EXPERTS, D_EXPERT, D_MODEL)

    for layer in range(DEPTH):
        need_ctx = layer < DEPTH - 1
        conv_w = jnp.zeros((8, SSD_CONV_DIM), F32).at[:SSD_CONV].set(ssd_conv_w[layer])
        qk, v, z, xbc_act, na, dt_raw = _inproj(xc, mod, g_mix3, w_cat, cos_t, sin_t, conv_w,
                                                ssd_conv_b[layer].reshape(1, SSD_CONV_DIM), n_batch, layer)
        sink = attn_sink[layer].astype(F32)
        oa, on = _latent_attn(sink, qk, v, na, bias_t, n_batch, layer)
        y_f, y_b = _ssd_scan(xbc_act, dt_raw, ssd_dt_bias[layer], ssd_a_log[layer], n_batch)
        oa_c, on_c = _ctx_attn(sink, qk, v, na, n_batch) if need_ctx else (oa, on)
        n_tiles = n_all_tiles if need_ctx else n_lat_tiles
        wr1, wr2, br = _router_weight(w_router_group[layer], b_router_group[layer],
                                      w_router_expert[layer], b_router_expert[layer])
        dskip = jnp.repeat(ssd_d[layer].astype(F32), SSD_INNER // SSD_HEADS).reshape(1, SSD_INNER)
        xmid, tok, rt, cnt = _outproj(xc, oa, oa_c, y_f, y_b, xbc_act, z, on, on_c, mod, dskip,
                                      ssd_norm_g[layer].reshape(1, SSD_INNER), w_out_b,
                                      g_ffn[layer].reshape(1, d), wr1, wr2, br, n_batch, n_tiles, layer)
        xc = _moe(xmid, tok, rt, cnt, mod, g_final.reshape(1, d), w_gate, w_up, w_down, n_batch, n_tiles, layer)

    return xc.reshape(n_batch, SEQ, d)
```

```python
import functools
import math

import jax
import jax.numpy as jnp
import numpy as np
from jax import lax
from jax.experimental import pallas as pl
from jax.experimental.pallas import tpu as pltpu

F32 = jnp.float32
BF16 = jnp.bfloat16

D_MODEL = 1024
SEQ = 2048
DEPTH = 4
GRID_W = 64
GRID_ROWS = SEQ // GRID_W
CTX_LEN = 256
HEAD_DIM = 64
A_HEADS = 4
A_KV_HEADS = 2
A_BLOCK = 128
ROPE_BASE = 10000.0
SSD_HEADS = 8
SSD_INNER = 512
SSD_STATE = 64
SSD_CONV = 5
SSD_CHUNK = 128
SSD_STEP_CHUNKS = 2
SSD_CONV_DIM = 768
NA_HEADS = 4
NA_WIN_ROWS = 8
NA_WIN_COLS = 16
A_IN = 512
SSD_IN = 1296
MOE_GROUPS = 4
MOE_EXPERTS = 8
N_EXPERTS = MOE_GROUPS * MOE_EXPERTS
D_EXPERT = 256
N_MOD = 6
RMS_EPS = 1e-6
NEG_INF = -1e30
Q_SCALE = HEAD_DIM ** -0.5 * math.log2(math.e)

TILE = 512
LANES = 128
C_QK, C_QKP, C_V, C_Z, C_XBC, C_NA, C_DT = 0, 384, 768, 896, 1408, 2176, 2944
N_COLS = 3200
NA_Q_ROWS = 4
NA_K_ROWS = NA_WIN_ROWS + NA_Q_ROWS
NA_BIAS_OFF = NA_Q_ROWS
NA_BIAS_N = NA_BIAS_OFF + (NA_K_ROWS - 2) + (NA_WIN_ROWS - 1) + 1


def _cparams(n_axes, vmem_mb):
    return pltpu.CompilerParams(dimension_semantics=("arbitrary",) * n_axes,
                                vmem_limit_bytes=vmem_mb << 20)


def _split3(x):
    h1 = x.astype(BF16)
    r1 = x - h1.astype(F32)
    h2 = r1.astype(BF16)
    h3 = (r1 - h2.astype(F32)).astype(BF16)
    return h1, h2, h3


def _dot(a, b):
    return jnp.dot(a, b, preferred_element_type=F32)


def _dot_nt(a, b):
    return lax.dot_general(a, b, (((1,), (1,)), ((), ())), preferred_element_type=F32)


def _dot_exact_lhs(lhs_bf16, x):
    return _dot(jnp.concatenate([lhs_bf16] * 3, axis=1), jnp.concatenate(_split3(x), axis=0))


def _dot_exact_rhs(x, rhs3_bf16):
    return _dot(jnp.concatenate(_split3(x), axis=1), rhs3_bf16)


def _silu(x):
    return x * jax.nn.sigmoid(x)


ROW_SLABS = D_MODEL // LANES


def _store_row_tiles(ref, x):
    n = x.shape[0]
    for s in range(ROW_SLABS):
        ref[pl.ds(s, n, stride=ROW_SLABS), :] = x[:, s * LANES:(s + 1) * LANES]


def _load_row_tiles(ref, n):
    return jnp.concatenate([ref[pl.ds(s, n, stride=ROW_SLABS), :] for s in range(ROW_SLABS)], axis=1)


def _iota(shape, dim):
    return lax.broadcasted_iota(jnp.int32, shape, dim)


def _mod_kernel(c_ref, w_ref, b_ref, o_ref):
    a = _silu(c_ref[...])
    a1, a2, _ = _split3(a)
    w = w_ref[0]
    w1 = w.astype(BF16)
    w2 = (w - w1.astype(F32)).astype(BF16)
    o_ref[0] = _dot(a1, w1) + _dot(a1, w2) + _dot(a2, w1) + b_ref[0]


def _modulation(cin, w_mod, b_mod):
    nt = 1024
    return pl.pallas_call(
        _mod_kernel,
        grid=(DEPTH, N_MOD * D_MODEL // nt),
        in_specs=[pl.BlockSpec((16, D_MODEL), lambda l, j: (0, 0)),
                  pl.BlockSpec((1, D_MODEL, nt), lambda l, j: (l, 0, j)),
                  pl.BlockSpec((1, 1, nt), lambda l, j: (l, 0, j))],
        out_specs=pl.BlockSpec((1, 16, nt), lambda l, j: (l, 0, j)),
        out_shape=jax.ShapeDtypeStruct((DEPTH, 16, N_MOD * D_MODEL), F32),
        compiler_params=_cparams(2, 40),
        name="modulation",
    )(cin, w_mod, b_mod.reshape(DEPTH, 1, N_MOD * D_MODEL))


def _inproj_kernel(prev_ref, x_ref, next_ref, mod_ref, g_ref, w_ref, cos_ref, sin_ref, cw_ref, cb_ref,
                   qk_ref, v_ref, z_ref, xbc_ref, na_ref, dt_ref, *, n_lat_tiles, tiles_per_batch):
    xe = jnp.concatenate([prev_ref[...], x_ref[...], next_ref[...]], axis=0)
    m = mod_ref[0, 0]
    he = xe * lax.rsqrt(jnp.mean(xe * xe, axis=-1, keepdims=True) + RMS_EPS) * g_ref[0]
    hbe = (he * (1.0 + m[1:2]) + m[0:1]).astype(BF16)
    hb = hbe[8:8 + TILE]

    def mm(lo, hi):
        return _dot(hb, w_ref[0, :, lo:hi])

    qk = mm(C_QK, C_QKP) * cos_ref[...] + mm(C_QKP, C_V) * sin_ref[...]
    qk_ref[...] = qk.astype(BF16)
    v_ref[...] = mm(C_V, C_Z).astype(BF16)
    z_ref[...] = mm(C_Z, C_XBC)
    na_ref[...] = mm(C_NA, C_DT).astype(BF16)
    dt_ref[...] = mm(C_DT, N_COLS)

    i = pl.program_id(0)
    is_lat = i < n_lat_tiles
    r = _iota((TILE, 1), 0)
    pos = jnp.where(is_lat, (i % tiles_per_batch) * TILE + r, r & (CTX_LEN - 1))
    seq_len = jnp.where(is_lat, SEQ, CTX_LEN)
    xbc_e = _dot(hbe, w_ref[0, :, C_XBC:C_NA])
    cw = cw_ref[...]
    acc = jnp.zeros((TILE, SSD_CONV_DIM), F32) + cb_ref[...]
    for k in range(SSD_CONV):
        off = k - SSD_CONV // 2
        tap = xbc_e[8 + off:8 + off + TILE, :]
        if off != 0:
            tap = jnp.where(jnp.logical_and(pos + off >= 0, pos + off < seq_len), tap, 0.0)
        acc = acc + tap * cw[k:k + 1, :]
    xbc_ref[...] = _silu(acc)


def _tile_mod_row(i, n_lat_tiles, tiles_per_batch, n_batch):
    return jnp.where(i < n_lat_tiles, i // tiles_per_batch, n_batch)


def _inproj(xc, mod, g_mix, w_cat, cos_t, sin_t, conv_w, conv_b, n_batch, layer):
    nt = xc.shape[0]
    n_lat_tiles = n_batch * SEQ // TILE
    tpb = SEQ // TILE
    t8 = TILE // 8
    row = lambda i: (i, 0)
    modrow = lambda i: (layer, _tile_mod_row(i, n_lat_tiles, tpb, n_batch), 0, 0)
    posrow = lambda i: (jnp.where(i < n_lat_tiles, i % tpb, tpb), 0)
    lay = lambda i: (layer, 0, 0)
    const = lambda i: (0, 0)
    outs = [(384, BF16), (128, BF16), (512, F32), (768, F32), (768, BF16), (256, F32)]
    kern = functools.partial(_inproj_kernel, n_lat_tiles=n_lat_tiles, tiles_per_batch=tpb)
    return pl.pallas_call(
        kern,
        grid=(nt // TILE,),
        in_specs=[pl.BlockSpec((8, D_MODEL), lambda i: (jnp.maximum(i * t8 - 1, 0), 0)),
                  pl.BlockSpec((TILE, D_MODEL), row),
                  pl.BlockSpec((8, D_MODEL), lambda i: (jnp.minimum(i * t8 + t8, nt // 8 - 1), 0)),
                  pl.BlockSpec((1, 1, N_MOD, D_MODEL), modrow),
                  pl.BlockSpec((1, 1, D_MODEL), lay),
                  pl.BlockSpec((1, D_MODEL, N_COLS), lay),
                  pl.BlockSpec((TILE, 384), posrow),
                  pl.BlockSpec((TILE, 384), posrow),
                  pl.BlockSpec((8, SSD_CONV_DIM), const),
                  pl.BlockSpec((1, SSD_CONV_DIM), const)],
        out_specs=[pl.BlockSpec((TILE, w), row) for w, _ in outs],
        out_shape=[jax.ShapeDtypeStruct((nt, w), dt) for w, dt in outs],
        compiler_params=_cparams(1, 56),
        name="inproj",
    )(xc, xc, xc, mod, g_mix, w_cat, cos_t, sin_t, conv_w, conv_b)


def _ssd_kernel(xf_ref, dtf_ref, xb_ref, dtb_ref, bias_ref, alog_ref, e512_ref, e1024_ref,
                yf_ref, yb_ref, stf_ref, stb_ref):
    @pl.when(pl.program_id(1) == 0)
    def _():
        stf_ref[...] = jnp.zeros_like(stf_ref)
        stb_ref[...] = jnp.zeros_like(stb_ref)

    for j in range(SSD_STEP_CHUNKS):
        rf = j * SSD_CHUNK
        rb = (SSD_STEP_CHUNKS - 1 - j) * SSD_CHUNK
        _ssd_chunk(0, rf, xf_ref, dtf_ref, bias_ref[0], alog_ref[0], e512_ref, e1024_ref, yf_ref, stf_ref)
        _ssd_chunk(1, rb, xb_ref, dtb_ref, bias_ref[1], alog_ref[1], e512_ref, e1024_ref, yb_ref, stb_ref)


def _ssd_chunk(d, r0, xbc_ref, dt_ref, dt_bias, a_log, e512_ref, e1024_ref, y_ref, st_ref):
    q = SSD_CHUNK
    xbc = xbc_ref[r0:r0 + q, :]
    xs = xbc[:, :SSD_INNER]
    bm = xbc[:, SSD_INNER:SSD_INNER + 128]
    cm = xbc[:, SSD_INNER + 128:]
    dtr = dt_ref[r0:r0 + q, :] + dt_bias
    dt = jnp.maximum(dtr, 0.0) + jnp.log(1.0 + jnp.exp(-jnp.abs(dtr)))
    a = -jnp.exp(a_log)
    da = dt * a

    ri = _iota((q, q), 0)
    ci = _iota((q, q), 1)
    tri = (ri >= ci) if d == 0 else (ri <= ci)
    trib = jnp.where(tri, 1.0, 0.0).astype(BF16)
    acs = _dot_exact_lhs(trib, da)
    acs_t = acs.T

    both_e = _dot_exact_rhs(jnp.concatenate([dt, acs], axis=0), e512_ref[...])
    dt_e = both_e[:q]
    acs_e = both_e[q:]
    acs_e2 = _dot_exact_rhs(acs, e1024_ref[...])
    tot_e = acs_e[q - 1:q, :] if d == 0 else acs_e[0:1, :]

    xdt = xs * dt_e
    xdec = (xdt * jnp.exp(tot_e - acs_e)).astype(BF16)
    btb = bm.T.astype(BF16)
    lane = _iota((q, 128), 1)
    cm0 = jnp.where(lane < SSD_STATE, cm, 0.0).astype(BF16)
    cm1 = jnp.where(lane >= SSD_STATE, cm, 0.0).astype(BF16)
    cbs = (_dot(cm0, btb), _dot(cm1, btb))

    st = st_ref[...]
    y_off = _dot(cm.astype(BF16), st.astype(BF16)) * jnp.exp(acs_e)
    s_all = _dot(btb, xdec)
    same = (_iota((q, SSD_INNER), 0) >> 6) == (_iota((q, SSD_INNER), 1) >> 8)
    st_ref[...] = jnp.where(same, st * jnp.exp(tot_e) + s_all, 0.0)

    for pair in range(SSD_HEADS // 2):
        cb = cbs[pair // 2]
        xp = xdt[:, pair * 128:(pair + 1) * 128]
        acc = None
        for k in range(2):
            h = 2 * pair + k
            seg = acs_e2[:, h * 128:(h + 1) * 128] - acs_t[h:h + 1, :]
            lmat = jnp.exp(jnp.where(tri, seg, NEG_INF))
            g = (cb * lmat).astype(BF16)
            rhs = jnp.where((lane < 64) if k == 0 else (lane >= 64), xp, 0.0).astype(BF16)
            t = _dot(g, rhs)
            acc = t if acc is None else acc + t
        y_ref[r0:r0 + q, pair * 128:(pair + 1) * 128] = acc + y_off[:, pair * 128:(pair + 1) * 128]


def _ssd_scan(xbc_act, dt_raw, dt_bias, a_log, n_batch):
    nt = xbc_act.shape[0]
    rows = SSD_STEP_CHUNKS * SSD_CHUNK
    n_lat_blk = n_batch * (SEQ // rows)
    lat_c = SEQ // rows
    ctx_c = CTX_LEN // rows
    n_steps = lat_c + ctx_c

    def blk_f(b, c):
        return jnp.where(c < ctx_c, n_lat_blk + b * ctx_c + c, b * lat_c + c - ctx_c)

    def blk_b(b, c):
        return jnp.where(c < ctx_c, n_lat_blk + b * ctx_c + ctx_c - 1 - c, b * lat_c + n_steps - 1 - c)

    heads = np.arange(128)
    e512 = (heads[:, None] == (np.arange(512)[None, :] // 64)).astype(np.float32)
    e1024 = (heads[:, None] == (np.arange(1024)[None, :] // 128)).astype(np.float32)
    dtb = jnp.zeros((2, 1, 128), F32).at[:, 0, :SSD_HEADS].set(dt_bias)
    alog = jnp.zeros((2, 1, 128), F32).at[:, 0, :SSD_HEADS].set(a_log)
    const3 = lambda b, c: (0, 0, 0)
    const2 = lambda b, c: (0, 0)
    return pl.pallas_call(
        _ssd_kernel,
        grid=(n_batch, n_steps),
        in_specs=[pl.BlockSpec((rows, SSD_CONV_DIM), lambda b, c: (blk_f(b, c), 0)),
                  pl.BlockSpec((rows, 128), lambda b, c: (blk_f(b, c), 0)),
                  pl.BlockSpec((rows, SSD_CONV_DIM), lambda b, c: (blk_b(b, c), 0)),
                  pl.BlockSpec((rows, 128), lambda b, c: (blk_b(b, c), 1)),
                  pl.BlockSpec((2, 1, 128), const3),
                  pl.BlockSpec((2, 1, 128), const3),
                  pl.BlockSpec((384, 512), const2),
                  pl.BlockSpec((384, 1024), const2)],
        out_specs=[pl.BlockSpec((rows, SSD_INNER), lambda b, c: (blk_f(b, c), 0)),
                   pl.BlockSpec((rows, SSD_INNER), lambda b, c: (blk_b(b, c), 0))],
        out_shape=[jax.ShapeDtypeStruct((nt, SSD_INNER), F32)] * 2,
        scratch_shapes=[pltpu.VMEM((128, SSD_INNER), F32)] * 2,
        compiler_params=_cparams(2, 32),
        name="ssd_scan",
    )(xbc_act, dt_raw, xbc_act, dt_raw, dtb, alog, jnp.asarray(np.tile(e512, (3, 1)), BF16),
      jnp.asarray(np.tile(e1024, (3, 1)), BF16))


def _softmax_pv(s_list, v_list, extra_logit=None):
    m = s_list[0].max(axis=-1, keepdims=True)
    for s in s_list[1:]:
        m = jnp.maximum(m, s.max(axis=-1, keepdims=True))
    if extra_logit is not None:
        m = jnp.maximum(m, extra_logit)
    den = None
    o = None
    for s, v in zip(s_list, v_list):
        p = jnp.exp2(s - m)
        ps = p.sum(axis=-1, keepdims=True)
        den = ps if den is None else den + ps
        t = _dot(p.astype(BF16), v)
        o = t if o is None else o + t
    if extra_logit is not None:
        den = den + jnp.exp2(extra_logit - m)
    return o / den


A_QB = 2
A_KB = A_QB + 2


def _wattn_kernel(sink_ref, q_ref, k_ref, v_ref, kc_ref, vc_ref, o_ref):
    n = pl.program_id(1) * A_QB
    nb = SEQ // A_BLOCK
    nq = A_QB * A_BLOCK
    nk = A_KB * A_BLOCK
    start = pl.multiple_of(jnp.clip(n - 1, 0, nb - A_KB) * A_BLOCK, A_BLOCK)
    q = q_ref[:, 0:256]
    kw = k_ref[pl.ds(start, nk), 256:384]
    vw = v_ref[pl.ds(start, nk), :]
    kc = kc_ref[:, 256:384]
    vc = vc_ref[...]
    qrow = _iota((2 * nq, nk), 0)
    qpos = n * A_BLOCK + jnp.where(qrow < nq, qrow, qrow - nq)
    kpos = start + _iota((2 * nq, nk), 1)
    valid = jnp.abs(qpos - kpos) <= A_BLOCK
    top = _iota((2 * nq, 1), 0) < nq
    outs = []
    for g in range(A_KV_HEADS):
        qg = jnp.concatenate([q[:, (2 * g) * 64:(2 * g + 1) * 64],
                              q[:, (2 * g + 1) * 64:(2 * g + 2) * 64]], axis=0)
        kg = kw[:, g * 64:(g + 1) * 64]
        vg = vw[:, g * 64:(g + 1) * 64]
        s_loc = jnp.where(valid, _dot_nt(qg, kg), NEG_INF)
        s_ctx = _dot_nt(qg, kc[:, g * 64:(g + 1) * 64])
        sink = jnp.where(top, sink_ref[2 * g], sink_ref[2 * g + 1])
        o = _softmax_pv([s_loc, s_ctx], [vg, vc[:, g * 64:(g + 1) * 64]], sink)
        outs += [o[:nq], o[nq:]]
    o_ref[...] = jnp.concatenate(outs, axis=1).astype(BF16)


def _nattn_kernel(q_ref, kv_ref, c_ref, bias_ref, o_ref):
    i = pl.program_id(1)
    r0 = i * NA_Q_ROWS
    srow = jnp.clip(r0 - NA_WIN_ROWS // 2, 0, GRID_ROWS - NA_K_ROWS)
    start = pl.multiple_of(srow * GRID_W, GRID_W)
    nq = NA_Q_ROWS * GRID_W
    nk = NA_K_ROWS * GRID_W
    q = q_ref[:, 0:256]
    kw = kv_ref[pl.ds(start, nk), 256:512]
    vw = kv_ref[pl.ds(start, nk), 512:768]
    kc = c_ref[:, 256:512]
    vc = c_ref[:, 512:768]
    qrow = r0 + (_iota((nq, nk), 0) >> 6)
    krow = srow + (_iota((nq, nk), 1) >> 6)
    rs = jnp.clip(qrow - NA_WIN_ROWS // 2, 0, GRID_ROWS - NA_WIN_ROWS)
    valid = jnp.logical_and(krow >= rs, krow < rs + NA_WIN_ROWS)
    outs = []
    for h in range(NA_HEADS):
        sl = slice(h * 64, (h + 1) * 64)
        rows = []
        for qi in range(NA_Q_ROWS):
            blocks = []
            for p in range(NA_K_ROWS // 2):
                idx = srow + 2 * p - (r0 + qi) + (NA_WIN_ROWS - 1) + NA_BIAS_OFF
                blocks.append(bias_ref[0, h, idx])
            rows.append(jnp.concatenate(blocks, axis=1))
        bias = jnp.concatenate(rows, axis=0)
        s_loc = jnp.where(valid, _dot_nt(q[:, sl], kw[:, sl]) + bias, NEG_INF)
        s_ctx = _dot_nt(q[:, sl], kc[:, sl])
        outs.append(_softmax_pv([s_loc, s_ctx], [vw[:, sl], vc[:, sl]]))
    o_ref[...] = jnp.concatenate(outs, axis=1).astype(BF16)


def _na_bias_table(rpb):
    cq = np.arange(GRID_W)
    kcol = np.arange(GRID_W)
    cs = np.clip(cq - NA_WIN_COLS // 2, 0, GRID_W - NA_WIN_COLS)
    col_valid = (kcol[None, :] >= cs[:, None]) & (kcol[None, :] < cs[:, None] + NA_WIN_COLS)
    coff = np.clip(kcol[None, :] - cq[:, None], -(NA_WIN_COLS - 1), NA_WIN_COLS - 1) + (NA_WIN_COLS - 1)
    n_a = 2 * NA_WIN_ROWS - 1
    n_c = 2 * NA_WIN_COLS - 1
    pick = (np.arange(n_c)[:, None] == coff.reshape(1, -1)).astype(np.float32)
    tm = jnp.einsum("lhak,kn->lhan", rpb.astype(F32), jnp.asarray(pick), precision=lax.Precision.HIGHEST)
    tm = jnp.where(col_valid, tm.reshape(rpb.shape[:3] + (GRID_W, GRID_W)) * math.log2(math.e), NEG_INF)
    neg = jnp.full(rpb.shape[:2] + (1, GRID_W, GRID_W), NEG_INF, F32)
    pad_lo = NA_BIAS_OFF
    pad_hi = NA_BIAS_N + 1 - pad_lo - n_a
    ext = jnp.concatenate([neg] * pad_lo + [tm] + [neg] * pad_hi, axis=2)
    return jnp.concatenate([ext[:, :, :NA_BIAS_N], ext[:, :, 1:NA_BIAS_N + 1]], axis=-1)


def _latent_attn_kernel(sink_ref, aq_ref, ak_ref, av_ref, akc_ref, avc_ref, nq_ref, nkv_ref, nc_ref, bias_ref,
                        oa_ref, on_ref):
    _wattn_kernel(sink_ref, aq_ref, ak_ref, av_ref, akc_ref, avc_ref, oa_ref)
    _nattn_kernel(nq_ref, nkv_ref, nc_ref, bias_ref, on_ref)


def _latent_attn(sink, qk, v, na, bias_t, n_batch, layer):
    nq = A_QB * A_BLOCK
    assert nq == NA_Q_ROWS * GRID_W
    steps = SEQ // nq
    ctx0 = n_batch * SEQ // CTX_LEN
    qrow = lambda b, n: (b * steps + n, 0)
    full = lambda b, n: (b, 0)
    ctx = lambda b, n: (ctx0 + b, 0)
    return pl.pallas_call(
        _latent_attn_kernel,
        grid=(n_batch, steps),
        in_specs=[pl.BlockSpec(memory_space=pltpu.SMEM),
                  pl.BlockSpec((nq, 384), qrow),
                  pl.BlockSpec((SEQ, 384), full),
                  pl.BlockSpec((SEQ, 128), full),
                  pl.BlockSpec((CTX_LEN, 384), ctx),
                  pl.BlockSpec((CTX_LEN, 128), ctx),
                  pl.BlockSpec((nq, 768), qrow),
                  pl.BlockSpec((SEQ, 768), full),
                  pl.BlockSpec((CTX_LEN, 768), ctx),
                  pl.BlockSpec((1, NA_HEADS, NA_BIAS_N, GRID_W, 128), lambda b, n: (layer, 0, 0, 0, 0))],
        out_specs=[pl.BlockSpec((nq, 256), qrow)] * 2,
        out_shape=[jax.ShapeDtypeStruct((n_batch * SEQ, 256), BF16)] * 2,
        compiler_params=_cparams(2, 48),
        name="latent_attn",
    )(sink, qk, qk, v, qk, v, na, na, na, bias_t)


def _ctx_attn_kernel(sink_ref, qk_ref, v_ref, na_ref, oa_ref, on_ref):
    qk = qk_ref[...]
    v = v_ref[...]
    na = na_ref[...]
    outs = []
    for h in range(A_HEADS):
        g = h // (A_HEADS // A_KV_HEADS)
        s = _dot_nt(qk[:, h * 64:(h + 1) * 64], qk[:, 256 + g * 64:256 + (g + 1) * 64])
        sink = jnp.zeros((CTX_LEN, 1), F32) + sink_ref[h]
        outs.append(_softmax_pv([s], [v[:, g * 64:(g + 1) * 64]], sink))
    oa_ref[...] = jnp.concatenate(outs, axis=1).astype(BF16)
    outs = []
    for h in range(NA_HEADS):
        sl = slice(h * 64, (h + 1) * 64)
        s = _dot_nt(na[:, 0:256][:, sl], na[:, 256:512][:, sl])
        outs.append(_softmax_pv([s], [na[:, 512:768][:, sl]]))
    on_ref[...] = jnp.concatenate(outs, axis=1).astype(BF16)


def _ctx_attn(sink, qk, v, na, n_batch):
    ctx0 = n_batch * SEQ // CTX_LEN
    row = lambda b: (ctx0 + b, 0)
    return pl.pallas_call(
        _ctx_attn_kernel,
        grid=(n_batch,),
        in_specs=[pl.BlockSpec(memory_space=pltpu.SMEM),
                  pl.BlockSpec((CTX_LEN, 384), row),
                  pl.BlockSpec((CTX_LEN, 128), row),
                  pl.BlockSpec((CTX_LEN, 768), row)],
        out_specs=[pl.BlockSpec((CTX_LEN, 256), lambda b: (b, 0))] * 2,
        out_shape=[jax.ShapeDtypeStruct((n_batch * CTX_LEN, 256), BF16)] * 2,
        compiler_params=_cparams(1, 32),
        name="ctx_attn",
    )(sink, qk, v, na)


def _outproj_kernel(x_ref, oal_ref, oac_ref, yf_ref, yb_ref, xbc_ref, z_ref, onl_ref, onc_ref, mod_ref,
                    dskip_ref, ng_ref, w_ref, gf_ref, wr1_ref, wr2_ref, br_ref, xo_ref, tok_ref, rt_ref,
                    cnt_ref, *, n_lat_tiles):
    is_lat = pl.program_id(0) < n_lat_tiles
    m = mod_ref[0, 0]
    xs = xbc_ref[...]
    y = yf_ref[...] + yb_ref[...] + dskip_ref[...] * xs
    y = y * _silu(z_ref[...])
    ob = y * lax.rsqrt(jnp.mean(y * y, axis=-1, keepdims=True) + RMS_EPS) * ng_ref[...]
    oa = jnp.where(is_lat, oal_ref[...], oac_ref[...])
    on = jnp.where(is_lat, onl_ref[...], onc_ref[...])
    proj = (_dot(oa, w_ref[0, 0:256, :]) + _dot(ob.astype(BF16), w_ref[0, 256:768, :])
            + _dot(on, w_ref[0, 768:1024, :]))
    x = x_ref[...] + m[2:3] * proj
    xo_ref[...] = x
    t = x * lax.rsqrt(jnp.mean(x * x, axis=-1, keepdims=True) + RMS_EPS) * gf_ref[...]
    t = t * (1.0 + m[4:5]) + m[3:4]
    _store_row_tiles(tok_ref, t)
    t1 = t.astype(BF16)
    t2 = (t - t1.astype(F32)).astype(BF16)
    logits = _dot(t1, wr1_ref[...]) + _dot(t1, wr2_ref[...]) + _dot(t2, wr1_ref[...]) + br_ref[...]

    lane = _iota(logits.shape, 1)
    big = jnp.int32(1 << 20)
    is_g = jnp.logical_and(lane >= N_EXPERTS, lane < N_EXPERTS + MOE_GROUPS)
    gl = jnp.where(is_g, logits, NEG_INF)
    gmax = gl.max(axis=-1, keepdims=True)
    g_w = 1.0 / jnp.exp(gl - gmax).sum(axis=-1, keepdims=True)
    g_idx = jnp.where(gl == gmax, lane, big).min(axis=-1, keepdims=True) - N_EXPERTS
    in_grp = jnp.logical_and(lane < N_EXPERTS, (lane >> 3) == g_idx)
    el = jnp.where(in_grp, logits, NEG_INF)
    l1 = el.max(axis=-1, keepdims=True)
    i1 = jnp.where(el == l1, lane, big).min(axis=-1, keepdims=True)
    el2 = jnp.where(lane == i1, NEG_INF, el)
    l2 = el2.max(axis=-1, keepdims=True)
    i2 = jnp.where(el2 == l2, lane, big).min(axis=-1, keepdims=True)
    e2 = jnp.exp(l2 - l1)
    w1 = g_w / (1.0 + e2)
    w2 = w1 * e2
    rt_ref[...] = jnp.where(lane == 0, i1.astype(F32), jnp.where(lane == 1, i2.astype(F32),
                            jnp.where(lane == 2, w1, jnp.where(lane == 3, w2, 0.0))))
    hot = jnp.logical_or(lane == i1, lane == i2)
    cnt_ref[0] = jnp.where(hot, 1.0, 0.0).sum(axis=0, keepdims=True)


def _outproj(xc, oa_l, oa_c, y_f, y_b, xbc_act, z, on_l, on_c, mod, dskip, norm_g, w_out, g_ffn, wr1, wr2, br,
             n_batch, n_tiles, layer):
    n_lat_tiles = n_batch * SEQ // TILE
    tpb = SEQ // TILE
    row = lambda i: (i, 0)
    lat = lambda i: (jnp.minimum(i, n_lat_tiles - 1), 0)
    ctx = lambda i: (jnp.maximum(i - n_lat_tiles, 0), 0)
    modrow = lambda i: (layer, _tile_mod_row(i, n_lat_tiles, tpb, n_batch), 0, 0)
    const = lambda i: (0, 0)
    kern = functools.partial(_outproj_kernel, n_lat_tiles=n_lat_tiles)
    return pl.pallas_call(
        kern,
        grid=(n_tiles,),
        in_specs=[pl.BlockSpec((TILE, D_MODEL), row),
                  pl.BlockSpec((TILE, 256), lat),
                  pl.BlockSpec((TILE, 256), ctx),
                  pl.BlockSpec((TILE, SSD_INNER), row),
                  pl.BlockSpec((TILE, SSD_INNER), row),
                  pl.BlockSpec((TILE, SSD_INNER), row),
                  pl.BlockSpec((TILE, SSD_INNER), row),
                  pl.BlockSpec((TILE, 256), lat),
                  pl.BlockSpec((TILE, 256), ctx),
                  pl.BlockSpec((1, 1, N_MOD, D_MODEL), modrow),
                  pl.BlockSpec((1, SSD_INNER), const),
                  pl.BlockSpec((1, SSD_INNER), const),
                  pl.BlockSpec((1, D_MODEL, D_MODEL), lambda i: (layer, 0, 0)),
                  pl.BlockSpec((1, D_MODEL), const),
                  pl.BlockSpec((D_MODEL, LANES), const),
                  pl.BlockSpec((D_MODEL, LANES), const),
                  pl.BlockSpec((1, LANES), const)],
        out_specs=[pl.BlockSpec((TILE, D_MODEL), row),
                   pl.BlockSpec((TILE * ROW_SLABS, LANES), row),
                   pl.BlockSpec((TILE, LANES), row),
                   pl.BlockSpec((1, 1, LANES), lambda i: (i, 0, 0))],
        out_shape=[jax.ShapeDtypeStruct((n_tiles * TILE, D_MODEL), F32),
                   jax.ShapeDtypeStruct((n_tiles * TILE * ROW_SLABS, LANES), F32),
                   jax.ShapeDtypeStruct((n_tiles * TILE, LANES), F32),
                   jax.ShapeDtypeStruct((n_tiles, 1, LANES), F32)],
        compiler_params=_cparams(1, 56),
        name="outproj",
    )(xc, oa_l, oa_c, y_f, y_b, xbc_act, z, on_l, on_c, mod, dskip, norm_g, w_out, g_ffn, wr1, wr2, br)


MOE_TM = 512
COMBINE_PARTS = 4


def _moe_max_tiles(n_tokens):
    return (2 * n_tokens + N_EXPERTS * (MOE_TM - 1)) // MOE_TM


def _moe_plan(cnt, n_tiles):
    cnt = cnt[:, 0, :N_EXPERTS].astype(jnp.int32)
    tot = cnt.sum(axis=0)
    tiles_e = (tot + MOE_TM - 1) // MOE_TM
    t_end = jnp.cumsum(tiles_e)
    t_start = t_end - tiles_e
    base = (t_start * MOE_TM)[None, :] + jnp.cumsum(cnt, axis=0) - cnt
    n_used = t_end[-1]
    n_max = _moe_max_tiles(n_tiles * TILE)
    te = jnp.sum(jnp.arange(n_max)[:, None] >= t_end[None, :], axis=1)
    te = jnp.minimum(te, jnp.sum((n_used - 1) >= t_end)).astype(jnp.int32)
    tail = jnp.where(tiles_e > 0, t_end - 1, n_max).astype(jnp.int32)
    base_f = jnp.zeros((n_tiles, 1, LANES), F32).at[:, 0, :N_EXPERTS].set(base.astype(F32))
    return base_f, te, n_used.reshape(1).astype(jnp.int32), tail


def _pos_kernel(rt_ref, base_ref, pos_ref):
    rt = rt_ref[...]
    lane = _iota(rt.shape, 1)
    hot1 = lane == rt[:, 0:1].astype(jnp.int32)
    hot2 = lane == rt[:, 1:2].astype(jnp.int32)
    hot = jnp.where(jnp.logical_or(hot1, hot2), 1.0, 0.0).astype(BF16)
    n = rt.shape[0]
    strict = jnp.where(_iota((n, n), 0) > _iota((n, n), 1), 1.0, 0.0).astype(BF16)
    slot = base_ref[0] + _dot(strict, hot)
    p1 = jnp.where(hot1, slot, 0.0).sum(axis=-1, keepdims=True)
    p2 = jnp.where(hot2, slot, 0.0).sum(axis=-1, keepdims=True)
    pos_ref[...] = jnp.where(lane == 0, p1, jnp.where(lane == 1, p2, 0.0)).astype(jnp.int32)


def _positions(rt, base, n_rows, tok):
    return pl.pallas_call(
        _pos_kernel,
        grid=(n_rows // tok,),
        in_specs=[pl.BlockSpec((tok, LANES), lambda i: (i, 0)),
                  pl.BlockSpec((1, 1, LANES), lambda i: (i * (tok // TILE), 0, 0))],
        out_specs=pl.BlockSpec((tok, LANES), lambda i: (i, 0)),
        out_shape=jax.ShapeDtypeStruct((n_rows, LANES), jnp.int32),
        compiler_params=_cparams(1, 32),
        name="moe_positions",
    )(rt, base)


def _dispatch_kernel(tail_ref, nu_ref, pos_ref, tok_ref, xs_ref, zbuf, zsem, sem, *, n_max):
    i = pl.program_id(0)

    def zero_tile(j):
        start = pl.multiple_of(j * (MOE_TM * ROW_SLABS), MOE_TM * ROW_SLABS)
        return pltpu.make_async_copy(zbuf, xs_ref.at[pl.ds(start, MOE_TM * ROW_SLABS), :], zsem)

    def row(ref, r):
        return ref.at[pl.ds(pl.multiple_of(r * ROW_SLABS, ROW_SLABS), ROW_SLABS), :]

    @pl.when(i == 0)
    def _():
        zbuf[...] = jnp.zeros_like(zbuf)
        for e in range(N_EXPERTS):
            @pl.when(tail_ref[e] != n_max)
            def _():
                zero_tile(tail_ref[e]).start()
        lax.fori_loop(nu_ref[0], n_max + 1, lambda j, c: (zero_tile(j).start(), c)[1], 0)
        for e in range(N_EXPERTS):
            @pl.when(tail_ref[e] != n_max)
            def _():
                zero_tile(tail_ref[e]).wait()
        lax.fori_loop(nu_ref[0], n_max + 1, lambda j, c: (zero_tile(j).wait(), c)[1], 0)

    def body(r, carry):
        src = row(tok_ref, r)
        pltpu.make_async_copy(src, row(xs_ref, pos_ref[0, 0, 2 * r]), sem).start(priority=0)
        pltpu.make_async_copy(src, row(xs_ref, pos_ref[0, 0, 2 * r + 1]), sem).start(priority=1)
        return carry

    n_rows = tok_ref.shape[0] // ROW_SLABS
    lax.fori_loop(0, n_rows, body, 0, unroll=8)
    for _ in range(2):
        pltpu.make_async_copy(tok_ref, xs_ref.at[pl.ds(0, n_rows * ROW_SLABS), :], sem).wait()


def _dispatch(tail, n_used, pos_s, tok, n_max):
    kern = functools.partial(_dispatch_kernel, n_max=n_max)
    n_steps = pos_s.shape[0]
    step_rows = pos_s.shape[2] // 2
    return pl.pallas_call(
        kern,
        grid_spec=pltpu.PrefetchScalarGridSpec(
            num_scalar_prefetch=2,
            grid=(n_steps,),
            in_specs=[pl.BlockSpec((1, 1, 2 * step_rows), lambda i, tail, nu: (i, 0, 0), memory_space=pltpu.SMEM),
                      pl.BlockSpec((step_rows * ROW_SLABS, LANES), lambda i, tail, nu: (i, 0))],
            out_specs=pl.BlockSpec(memory_space=pl.ANY),
            scratch_shapes=[pltpu.VMEM((MOE_TM * ROW_SLABS, LANES), F32), pltpu.SemaphoreType.DMA(()),
                            pltpu.SemaphoreType.DMA(())]),
        out_shape=jax.ShapeDtypeStruct(((n_max + 1) * MOE_TM * ROW_SLABS, LANES), F32),
        compiler_params=_cparams(1, 32),
        name="moe_dispatch",
    )(tail, n_used, pos_s, tok)


def _experts_kernel(te_ref, nu_ref, xs_ref, wg_ref, wu_ref, wd_ref, ys_ref):
    used = pl.program_id(0) < nu_ref[0]

    @pl.when(used)
    def _():
        x = _load_row_tiles(xs_ref, MOE_TM).astype(BF16)
        gate = _dot(x, wg_ref[0].astype(BF16))
        up = _dot(x, wu_ref[0].astype(BF16))
        hid = (_silu(gate) * up).astype(BF16)
        _store_row_tiles(ys_ref, _dot(hid, wd_ref[0].astype(BF16)))

    @pl.when(jnp.logical_not(used))
    def _():
        ys_ref[...] = jnp.zeros_like(ys_ref)


def _experts(te, n_used, xs, w_gate, w_up, w_down, n_max, layer):
    rows = lambda j, te, nu: (jnp.minimum(j, nu[0] - 1), 0)
    wsel = lambda j, te, nu: (layer * N_EXPERTS + te[j], 0, 0)
    return pl.pallas_call(
        _experts_kernel,
        grid_spec=pltpu.PrefetchScalarGridSpec(
            num_scalar_prefetch=2,
            grid=(n_max,),
            in_specs=[pl.BlockSpec((MOE_TM * ROW_SLABS, LANES), rows),
                      pl.BlockSpec((1, D_MODEL, D_EXPERT), wsel),
                      pl.BlockSpec((1, D_MODEL, D_EXPERT), wsel),
                      pl.BlockSpec((1, D_EXPERT, D_MODEL), wsel)],
            out_specs=pl.BlockSpec((MOE_TM * ROW_SLABS, LANES), lambda j, te, nu: (j, 0))),
        out_shape=jax.ShapeDtypeStruct((n_max * MOE_TM * ROW_SLABS, LANES), F32),
        compiler_params=_cparams(1, 48),
        name="moe_experts",
    )(te, n_used, xs, w_gate, w_up, w_down)


def _combine_kernel(pos_ref, ys_ref, x_ref, rt_ref, mod_ref, gfin_ref, o_ref, ybuf0, ybuf1, sem, *, final):
    def row(ref, r):
        return ref.at[pl.ds(pl.multiple_of(r * ROW_SLABS, ROW_SLABS), ROW_SLABS), :]

    n = x_ref.shape[0]
    part = n // COMBINE_PARTS
    for k in range(COMBINE_PARTS):
        def body(r, carry, k=k):
            pltpu.make_async_copy(row(ys_ref, pos_ref[0, 0, 2 * r]), row(ybuf0, r), sem.at[k]).start(priority=0)
            pltpu.make_async_copy(row(ys_ref, pos_ref[0, 0, 2 * r + 1]), row(ybuf1, r), sem.at[k]).start(priority=1)
            return carry

        lax.fori_loop(k * part, (k + 1) * part, body, 0, unroll=8)
    gate = mod_ref[0, 0][5:6]
    for k in range(COMBINE_PARTS):
        rows = pl.ds(k * part * ROW_SLABS, part * ROW_SLABS)
        for buf in (ybuf0, ybuf1):
            pltpu.make_async_copy(ys_ref.at[pl.ds(0, part * ROW_SLABS), :], buf.at[rows, :], sem.at[k]).wait()
        rt = rt_ref[k * part:(k + 1) * part, :]
        f = (rt[:, 2:3] * _load_row_tiles(ybuf0.at[rows, :], part)
             + rt[:, 3:4] * _load_row_tiles(ybuf1.at[rows, :], part))
        x = x_ref[k * part:(k + 1) * part, :] + gate * f
        if final:
            x = x * lax.rsqrt(jnp.mean(x * x, axis=-1, keepdims=True) + RMS_EPS) * gfin_ref[...]
        o_ref[k * part:(k + 1) * part, :] = x


def _combine(pos_s, ys, xmid, rt, mod, g_final, n_batch, layer):
    n_steps = pos_s.shape[0]
    tok = pos_s.shape[2] // 2
    n_lat_steps = n_batch * SEQ // tok
    spb = SEQ // tok
    row = lambda i: (i, 0)
    modrow = lambda i: (layer, _tile_mod_row(i, n_lat_steps, spb, n_batch), 0, 0)
    return pl.pallas_call(
        functools.partial(_combine_kernel, final=(layer == DEPTH - 1)),
        grid=(n_steps,),
        in_specs=[pl.BlockSpec((1, 1, 2 * tok), lambda i: (i, 0, 0), memory_space=pltpu.SMEM),
                  pl.BlockSpec(memory_space=pl.ANY),
                  pl.BlockSpec((tok, D_MODEL), row),
                  pl.BlockSpec((tok, LANES), row),
                  pl.BlockSpec((1, 1, N_MOD, D_MODEL), modrow),
                  pl.BlockSpec((1, D_MODEL), lambda i: (0, 0))],
        out_specs=pl.BlockSpec((tok, D_MODEL), row),
        out_shape=jax.ShapeDtypeStruct((n_steps * tok, D_MODEL), F32),
        scratch_shapes=[pltpu.VMEM((tok * ROW_SLABS, LANES), F32), pltpu.VMEM((tok * ROW_SLABS, LANES), F32),
                        pltpu.SemaphoreType.DMA((COMBINE_PARTS,))],
        compiler_params=_cparams(1, 48),
        name="moe_combine",
    )(pos_s, ys, xmid, rt, mod, g_final)


MOE_TOK = 1024


def _moe(xmid, tok, rt, cnt, mod, g_final, w_gate, w_up, w_down, n_batch, n_tiles, layer):
    n_rows = n_tiles * TILE
    step = MOE_TOK if (n_rows % MOE_TOK == 0 and (n_batch * SEQ) % MOE_TOK == 0 and SEQ % MOE_TOK == 0) else TILE
    base, te, n_used, tail = _moe_plan(cnt, n_tiles)
    n_max = _moe_max_tiles(n_rows)
    pos = _positions(rt, base, n_rows, step)
    pos_s = pos[:, :2].reshape(n_rows // step, 1, 2 * step)
    xs = _dispatch(tail, n_used, pos_s, tok, n_max)
    ys = _experts(te, n_used, xs, w_gate, w_up, w_down, n_max, layer)
    return _combine(pos_s, ys, xmid, rt, mod, g_final, n_batch, layer)


def _rope_tables():
    t = jnp.arange(SEQ)
    rows_pos = (t // GRID_W).astype(F32)
    cols_pos = (t % GRID_W).astype(F32)
    half = HEAD_DIM // 2
    inv = 1.0 / (ROPE_BASE ** (jnp.arange(0, half, 2, dtype=F32) / half))
    ang_r = rows_pos[:, None] * inv[None, :]
    ang_c = cols_pos[:, None] * inv[None, :]
    ang = jnp.concatenate([ang_r, ang_r, ang_c, ang_c], axis=1)
    cos_h, sin_h = jnp.cos(ang), jnp.sin(ang)
    scale = jnp.concatenate([jnp.full((256,), Q_SCALE, F32), jnp.ones((128,), F32)])
    cos_t = jnp.tile(cos_h, (1, 6)) * scale
    sin_t = jnp.tile(sin_h, (1, 6)) * scale
    cos_t = jnp.concatenate([cos_t, jnp.broadcast_to(scale, (TILE, 384))], axis=0)
    sin_t = jnp.concatenate([sin_t, jnp.zeros((TILE, 384), F32)], axis=0)
    return cos_t, sin_t


def _fused_in_weight(w_in):
    wq, wk, wv = w_in[..., 0:256], w_in[..., 256:384], w_in[..., 384:512]
    o = A_IN
    wz, wxbc, wdt = w_in[..., o:o + 512], w_in[..., o + 512:o + 1280], w_in[..., o + 1280:o + 1296]
    wna = w_in[..., A_IN + SSD_IN:]

    def rot(w):
        w4 = w.reshape(w.shape[:-1] + (w.shape[-1] // 32, 2, 16))
        return jnp.concatenate([-w4[..., 1:2, :], w4[..., 0:1, :]], axis=-2).reshape(w.shape)

    na_scale = jnp.concatenate([jnp.full((256,), Q_SCALE, F32), jnp.ones((512,), F32)])
    pad = jnp.zeros(w_in.shape[:-1] + (128 - SSD_HEADS,), F32)
    cat = jnp.concatenate([wq, wk, rot(wq), rot(wk), wv, wz, wxbc,
                           wna * na_scale, wdt[..., :SSD_HEADS], pad, wdt[..., SSD_HEADS:], pad], axis=-1)
    return cat.astype(BF16)


def _router_weight(w_rg, b_rg, w_re, b_re):
    w = jnp.concatenate([w_re, w_rg, jnp.zeros((D_MODEL, LANES - N_EXPERTS - MOE_GROUPS), F32)], axis=1)
    b = jnp.concatenate([b_re, b_rg, jnp.zeros((LANES - N_EXPERTS - MOE_GROUPS,), F32)]).reshape(1, LANES)
    w1 = w.astype(BF16)
    w2 = (w - w1.astype(F32)).astype(BF16)
    return w1, w2, b


def kernel(x, c, ctx, c_ctx, w_mod, b_mod, g_mix, w_in, attn_sink, ssd_conv_w, ssd_conv_b, ssd_dt_bias, ssd_a_log, ssd_d, ssd_norm_g, na_rpb, w_out, g_ffn, w_router_group, b_router_group, w_router_expert, b_router_expert, w_exp_gate, w_exp_up, w_exp_down, g_final):
    n_batch, s, d = x.shape
    assert (s, d) == (SEQ, D_MODEL) and ctx.shape[1:] == (CTX_LEN, D_MODEL) and n_batch < 16
    n_lat = n_batch * SEQ
    n_ctx = n_batch * CTX_LEN
    assert n_ctx % TILE == 0
    n_lat_tiles = n_lat // TILE
    n_all_tiles = (n_lat + n_ctx) // TILE

    xc = jnp.concatenate([x.reshape(n_lat, d), ctx.reshape(n_ctx, d)], axis=0)
    cin = jnp.zeros((16, d), F32).at[:n_batch].set(c).at[n_batch].set(c_ctx)
    mod = _modulation(cin, w_mod, b_mod).reshape(DEPTH, 16, N_MOD, d)
    cos_t, sin_t = _rope_tables()
    w_cat = _fused_in_weight(w_in)
    w_out_b = w_out.astype(BF16)
    bias_t = _na_bias_table(na_rpb)
    g_mix3 = g_mix.reshape(DEPTH, 1, d)
    w_gate = w_exp_gate.reshape(DEPTH * N_EXPERTS, D_MODEL, D_EXPERT)
    w_up = w_exp_up.reshape(DEPTH * N_EXPERTS, D_MODEL, D_EXPERT)
    w_down = w_exp_down.reshape(DEPTH * N_EXPERTS, D_EXPERT, D_MODEL)

    for layer in range(DEPTH):
        need_ctx = layer < DEPTH - 1
        conv_w = jnp.zeros((8, SSD_CONV_DIM), F32).at[:SSD_CONV].set(ssd_conv_w[layer])
        qk, v, z, xbc_act, na, dt_raw = _inproj(xc, mod, g_mix3, w_cat, cos_t, sin_t, conv_w,
                                                ssd_conv_b[layer].reshape(1, SSD_CONV_DIM), n_batch, layer)
        sink = attn_sink[layer].astype(F32) * math.log2(math.e)
        oa, on = _latent_attn(sink, qk, v, na, bias_t, n_batch, layer)
        y_f, y_b = _ssd_scan(xbc_act, dt_raw, ssd_dt_bias[layer], ssd_a_log[layer], n_batch)
        oa_c, on_c = _ctx_attn(sink, qk, v, na, n_batch) if need_ctx else (oa, on)
        n_tiles = n_all_tiles if need_ctx else n_lat_tiles
        wr1, wr2, br = _router_weight(w_router_group[layer], b_router_group[layer],
                                      w_router_expert[layer], b_router_expert[layer])
        dskip = jnp.repeat(ssd_d[layer].astype(F32), SSD_INNER // SSD_HEADS).reshape(1, SSD_INNER)
        xmid, tok, rt, cnt = _outproj(xc, oa, oa_c, y_f, y_b, xbc_act, z, on, on_c, mod, dskip,
                                      ssd_norm_g[layer].reshape(1, SSD_INNER), w_out_b,
                                      g_ffn[layer].reshape(1, d), wr1, wr2, br, n_batch, n_tiles, layer)
        xc = _moe(xmid, tok, rt, cnt, mod, g_final.reshape(1, d), w_gate, w_up, w_down, n_batch, n_tiles, layer)

    return xc.reshape(n_batch, SEQ, d)
```

```python
import functools
import math

import jax
import jax.numpy as jnp
import numpy as np
from jax import lax
from jax.experimental import pallas as pl
from jax.experimental.pallas import tpu as pltpu

F32 = jnp.float32
BF16 = jnp.bfloat16

D_MODEL = 1024
SEQ = 2048
DEPTH = 4
GRID_W = 64
GRID_ROWS = SEQ // GRID_W
CTX_LEN = 256
HEAD_DIM = 64
A_HEADS = 4
A_KV_HEADS = 2
A_BLOCK = 128
ROPE_BASE = 10000.0
SSD_HEADS = 8
SSD_INNER = 512
SSD_STATE = 64
SSD_CONV = 5
SSD_CHUNK = 128
SSD_STEP_CHUNKS = 2
SSD_CONV_DIM = 768
NA_HEADS = 4
NA_WIN_ROWS = 8
NA_WIN_COLS = 16
MOE_GROUPS = 4
MOE_EXPERTS = 8
N_EXPERTS = MOE_GROUPS * MOE_EXPERTS
D_EXPERT = 256
N_MOD = 6
RMS_EPS = 1e-6
NEG_INF = -1e30
Q_SCALE = HEAD_DIM ** -0.5 * math.log2(math.e)

TILE = 512
LANES = 128
N_IN = 2576
W_QK, W_V, W_Z, W_XBC = (0, 384), (384, 512), (512, 1024), (1024, 1792)
W_DT = (1792, 1920)
W_NA = (1808, 2576)
NA_Q_ROWS = 4
NA_K_ROWS = NA_WIN_ROWS + NA_Q_ROWS
NA_BIAS_OFF = NA_Q_ROWS
NA_BIAS_N = NA_BIAS_OFF + (NA_K_ROWS - 2) + (NA_WIN_ROWS - 1) + 1


def _cparams(n_axes, vmem_mb):
    return pltpu.CompilerParams(dimension_semantics=("arbitrary",) * n_axes,
                                vmem_limit_bytes=vmem_mb << 20)


def _split3(x):
    h1 = x.astype(BF16)
    r1 = x - h1.astype(F32)
    h2 = r1.astype(BF16)
    h3 = (r1 - h2.astype(F32)).astype(BF16)
    return h1, h2, h3


def _dot(a, b):
    return jnp.dot(a, b, preferred_element_type=F32)


def _dot_nt(a, b):
    return lax.dot_general(a, b, (((1,), (1,)), ((), ())), preferred_element_type=F32)


def _dot_exact_lhs(lhs_bf16, x):
    return _dot(jnp.concatenate([lhs_bf16] * 3, axis=1), jnp.concatenate(_split3(x), axis=0))


def _dot_exact_rhs(x, rhs3_bf16):
    return _dot(jnp.concatenate(_split3(x), axis=1), rhs3_bf16)


def _silu(x):
    return x * jax.nn.sigmoid(x)


ROW_SLABS = D_MODEL // LANES


def _store_row_tiles(ref, x):
    n = x.shape[0]
    for s in range(ROW_SLABS):
        ref[pl.ds(s, n, stride=ROW_SLABS), :] = x[:, s * LANES:(s + 1) * LANES]


def _load_row_tiles(ref, n):
    return jnp.concatenate([ref[pl.ds(s, n, stride=ROW_SLABS), :] for s in range(ROW_SLABS)], axis=1)


def _iota(shape, dim):
    return lax.broadcasted_iota(jnp.int32, shape, dim)


def _mod_kernel(c_ref, w_ref, b_ref, o_ref):
    a = _silu(c_ref[...])
    a1, a2, _ = _split3(a)
    w = w_ref[0]
    w1 = w.astype(BF16)
    w2 = (w - w1.astype(F32)).astype(BF16)
    o_ref[0] = _dot(a1, w1) + _dot(a1, w2) + _dot(a2, w1) + b_ref[0]


def _modulation(cin, w_mod, b_mod):
    nt = 1024
    return pl.pallas_call(
        _mod_kernel,
        grid=(DEPTH, N_MOD * D_MODEL // nt),
        in_specs=[pl.BlockSpec((16, D_MODEL), lambda l, j: (0, 0)),
                  pl.BlockSpec((1, D_MODEL, nt), lambda l, j: (l, 0, j)),
                  pl.BlockSpec((1, 1, nt), lambda l, j: (l, 0, j))],
        out_specs=pl.BlockSpec((1, 16, nt), lambda l, j: (l, 0, j)),
        out_shape=jax.ShapeDtypeStruct((DEPTH, 16, N_MOD * D_MODEL), F32),
        compiler_params=_cparams(2, 40),
        name="modulation",
    )(cin, w_mod, b_mod.reshape(DEPTH, 1, N_MOD * D_MODEL))


def _inproj_kernel(prev_ref, x_ref, next_ref, mod_ref, g_ref, w_ref, wrot_ref, wna_ref, cos_ref, sin_ref,
                   cw_ref, cb_ref, qk_ref, v_ref, z_ref, xbc_ref, na_ref, dt_ref, *, n_lat_tiles, tiles_per_batch):
    xe = jnp.concatenate([prev_ref[...], x_ref[...], next_ref[...]], axis=0)
    m = mod_ref[0, 0]
    he = xe * lax.rsqrt(jnp.mean(xe * xe, axis=-1, keepdims=True) + RMS_EPS) * g_ref[0]
    hbe = (he * (1.0 + m[1:2]) + m[0:1]).astype(BF16)
    hb = hbe[8:8 + TILE]

    def mm(cols):
        return _dot(hb, w_ref[0, :, cols[0]:cols[1]])

    qk = mm(W_QK) * cos_ref[...] + _dot(hb, wrot_ref[0]) * sin_ref[...]
    qk_ref[...] = qk.astype(BF16)
    v_ref[...] = mm(W_V).astype(BF16)
    z_ref[...] = mm(W_Z)
    na_ref[...] = _dot(hb, wna_ref[0]).astype(BF16)
    dt_ref[...] = mm(W_DT)

    i = pl.program_id(0)
    is_lat = i < n_lat_tiles
    r = _iota((TILE, 1), 0)
    pos = jnp.where(is_lat, (i % tiles_per_batch) * TILE + r, r & (CTX_LEN - 1))
    seq_len = jnp.where(is_lat, SEQ, CTX_LEN)
    xbc_e = _dot(hbe, w_ref[0, :, W_XBC[0]:W_XBC[1]])
    cw = cw_ref[...]
    acc = jnp.zeros((TILE, SSD_CONV_DIM), F32) + cb_ref[...]
    for k in range(SSD_CONV):
        off = k - SSD_CONV // 2
        tap = xbc_e[8 + off:8 + off + TILE, :]
        if off != 0:
            tap = jnp.where(jnp.logical_and(pos + off >= 0, pos + off < seq_len), tap, 0.0)
        acc = acc + tap * cw[k:k + 1, :]
    xbc_ref[...] = _silu(acc)


def _tile_mod_row(i, n_lat_tiles, tiles_per_batch, n_batch):
    return jnp.where(i < n_lat_tiles, i // tiles_per_batch, n_batch)


def _inproj(xc, mod, g_mix, w_in_b, w_rot, w_na, cos_t, sin_t, conv_w, conv_b, n_batch, layer):
    nt = xc.shape[0]
    n_lat_tiles = n_batch * SEQ // TILE
    tpb = SEQ // TILE
    t8 = TILE // 8
    row = lambda i: (i, 0)
    modrow = lambda i: (layer, _tile_mod_row(i, n_lat_tiles, tpb, n_batch), 0, 0)
    posrow = lambda i: (jnp.where(i < n_lat_tiles, i % tpb, tpb), 0)
    lay = lambda i: (layer, 0, 0)
    const = lambda i: (0, 0)
    outs = [(384, BF16), (128, BF16), (512, F32), (768, F32), (768, BF16), (128, F32)]
    kern = functools.partial(_inproj_kernel, n_lat_tiles=n_lat_tiles, tiles_per_batch=tpb)
    return pl.pallas_call(
        kern,
        grid=(nt // TILE,),
        in_specs=[pl.BlockSpec((8, D_MODEL), lambda i: (jnp.maximum(i * t8 - 1, 0), 0)),
                  pl.BlockSpec((TILE, D_MODEL), row),
                  pl.BlockSpec((8, D_MODEL), lambda i: (jnp.minimum(i * t8 + t8, nt // 8 - 1), 0)),
                  pl.BlockSpec((1, 1, N_MOD, D_MODEL), modrow),
                  pl.BlockSpec((1, 1, D_MODEL), lay),
                  pl.BlockSpec((1, D_MODEL, N_IN), lay),
                  pl.BlockSpec((1, D_MODEL, 384), lay),
                  pl.BlockSpec((1, D_MODEL, 768), lay),
                  pl.BlockSpec((TILE, 384), posrow),
                  pl.BlockSpec((TILE, 384), posrow),
                  pl.BlockSpec((8, SSD_CONV_DIM), const),
                  pl.BlockSpec((1, SSD_CONV_DIM), const)],
        out_specs=[pl.BlockSpec((TILE, w), row) for w, _ in outs],
        out_shape=[jax.ShapeDtypeStruct((nt, w), dt) for w, dt in outs],
        compiler_params=_cparams(1, 56),
        name="inproj",
    )(xc, xc, xc, mod, g_mix, w_in_b, w_rot, w_na, cos_t, sin_t, conv_w, conv_b)


def _ssd_kernel(xf_ref, dtf_ref, xb_ref, dtb_ref, bias_ref, alog_ref, e512_ref, e1024_ref,
                yf_ref, yb_ref, stf_ref, stb_ref):
    @pl.when(pl.program_id(1) == 0)
    def _():
        stf_ref[...] = jnp.zeros_like(stf_ref)
        stb_ref[...] = jnp.zeros_like(stb_ref)

    for j in range(SSD_STEP_CHUNKS):
        rf = j * SSD_CHUNK
        rb = (SSD_STEP_CHUNKS - 1 - j) * SSD_CHUNK
        _ssd_chunk(0, rf, xf_ref, dtf_ref, bias_ref[0], alog_ref[0], e512_ref, e1024_ref, yf_ref, stf_ref)
        _ssd_chunk(1, rb, xb_ref, dtb_ref, bias_ref[1], alog_ref[1], e512_ref, e1024_ref, yb_ref, stb_ref)


def _ssd_chunk(d, r0, xbc_ref, dt_ref, dt_bias, a_log, e512_ref, e1024_ref, y_ref, st_ref):
    q = SSD_CHUNK
    xbc = xbc_ref[r0:r0 + q, :]
    xs = xbc[:, :SSD_INNER]
    bm = xbc[:, SSD_INNER:SSD_INNER + 128]
    cm = xbc[:, SSD_INNER + 128:]
    dtr = dt_ref[r0:r0 + q, :] + dt_bias
    dt = jnp.maximum(dtr, 0.0) + jnp.log(1.0 + jnp.exp(-jnp.abs(dtr)))
    a = -jnp.exp(a_log)
    da = dt * a

    ri = _iota((q, q), 0)
    ci = _iota((q, q), 1)
    tri = (ri >= ci) if d == 0 else (ri <= ci)
    trib = jnp.where(tri, 1.0, 0.0).astype(BF16)
    acs = _dot_exact_lhs(trib, da)
    acs_t = acs.T

    both_e = _dot_exact_rhs(jnp.concatenate([dt, acs], axis=0), e512_ref[d])
    dt_e = both_e[:q]
    acs_e = both_e[q:]
    acs_e2 = _dot_exact_rhs(acs, e1024_ref[d])
    tot_e = acs_e[q - 1:q, :] if d == 0 else acs_e[0:1, :]

    xdt = xs * dt_e
    xdec = (xdt * jnp.exp(tot_e - acs_e)).astype(BF16)
    btb = bm.T.astype(BF16)
    lane = _iota((q, 128), 1)
    cm0 = jnp.where(lane < SSD_STATE, cm, 0.0).astype(BF16)
    cm1 = jnp.where(lane >= SSD_STATE, cm, 0.0).astype(BF16)
    cbs = (_dot(cm0, btb), _dot(cm1, btb))

    st = st_ref[...]
    y_off = _dot(cm.astype(BF16), st.astype(BF16)) * jnp.exp(acs_e)
    s_all = _dot(btb, xdec)
    same = (_iota((q, SSD_INNER), 0) >> 6) == (_iota((q, SSD_INNER), 1) >> 8)
    st_ref[...] = jnp.where(same, st * jnp.exp(tot_e) + s_all, 0.0)

    for pair in range(SSD_HEADS // 2):
        cb = cbs[pair // 2]
        xp = xdt[:, pair * 128:(pair + 1) * 128]
        acc = None
        for k in range(2):
            h = 2 * pair + k
            seg = acs_e2[:, h * 128:(h + 1) * 128] - acs_t[d * SSD_HEADS + h:d * SSD_HEADS + h + 1, :]
            lmat = jnp.exp(jnp.where(tri, seg, NEG_INF))
            g = (cb * lmat).astype(BF16)
            rhs = jnp.where((lane < 64) if k == 0 else (lane >= 64), xp, 0.0).astype(BF16)
            t = _dot(g, rhs)
            acc = t if acc is None else acc + t
        y_ref[r0:r0 + q, pair * 128:(pair + 1) * 128] = acc + y_off[:, pair * 128:(pair + 1) * 128]


def _ssd_scan(xbc_act, dt_raw, dt_bias, a_log, n_batch):
    nt = xbc_act.shape[0]
    rows = SSD_STEP_CHUNKS * SSD_CHUNK
    n_lat_blk = n_batch * (SEQ // rows)
    lat_c = SEQ // rows
    ctx_c = CTX_LEN // rows
    n_steps = lat_c + ctx_c

    def blk_f(b, c):
        return jnp.where(c < ctx_c, n_lat_blk + b * ctx_c + c, b * lat_c + c - ctx_c)

    def blk_b(b, c):
        return jnp.where(c < ctx_c, n_lat_blk + b * ctx_c + ctx_c - 1 - c, b * lat_c + n_steps - 1 - c)

    lanes = np.arange(128)

    def expand(width, per_head):
        e = np.stack([(lanes[:, None] - d * SSD_HEADS == (np.arange(width)[None, :] // per_head))
                      for d in range(2)]).astype(np.float32)
        return jnp.asarray(np.tile(e, (1, 3, 1)), BF16)

    dtb = jnp.zeros((2, 1, 128), F32)
    alog = jnp.zeros((2, 1, 128), F32)
    for d in range(2):
        dtb = dtb.at[d, 0, d * SSD_HEADS:(d + 1) * SSD_HEADS].set(dt_bias[d])
        alog = alog.at[d, 0, d * SSD_HEADS:(d + 1) * SSD_HEADS].set(a_log[d])
    const3 = lambda b, c: (0, 0, 0)
    return pl.pallas_call(
        _ssd_kernel,
        grid=(n_batch, n_steps),
        in_specs=[pl.BlockSpec((rows, SSD_CONV_DIM), lambda b, c: (blk_f(b, c), 0)),
                  pl.BlockSpec((rows, 128), lambda b, c: (blk_f(b, c), 0)),
                  pl.BlockSpec((rows, SSD_CONV_DIM), lambda b, c: (blk_b(b, c), 0)),
                  pl.BlockSpec((rows, 128), lambda b, c: (blk_b(b, c), 0)),
                  pl.BlockSpec((2, 1, 128), const3),
                  pl.BlockSpec((2, 1, 128), const3),
                  pl.BlockSpec((2, 384, 512), const3),
                  pl.BlockSpec((2, 384, 1024), const3)],
        out_specs=[pl.BlockSpec((rows, SSD_INNER), lambda b, c: (blk_f(b, c), 0)),
                   pl.BlockSpec((rows, SSD_INNER), lambda b, c: (blk_b(b, c), 0))],
        out_shape=[jax.ShapeDtypeStruct((nt, SSD_INNER), F32)] * 2,
        scratch_shapes=[pltpu.VMEM((128, SSD_INNER), F32)] * 2,
        compiler_params=_cparams(2, 32),
        name="ssd_scan",
    )(xbc_act, dt_raw, xbc_act, dt_raw, dtb, alog, expand(SSD_INNER, SSD_INNER // SSD_HEADS),
      expand(SSD_HEADS * 128, 128))


def _softmax_pv(s_list, v_list, extra_logit=None):
    m = s_list[0].max(axis=-1, keepdims=True)
    for s in s_list[1:]:
        m = jnp.maximum(m, s.max(axis=-1, keepdims=True))
    if extra_logit is not None:
        m = jnp.maximum(m, extra_logit)
    den = None
    o = None
    for s, v in zip(s_list, v_list):
        p = jnp.exp2(s - m)
        ps = p.sum(axis=-1, keepdims=True)
        den = ps if den is None else den + ps
        t = _dot(p.astype(BF16), v)
        o = t if o is None else o + t
    if extra_logit is not None:
        den = den + jnp.exp2(extra_logit - m)
    return o / den


A_QB = 2
A_KB = A_QB + 2


def _wattn_kernel(sink_ref, q_ref, k_ref, v_ref, kc_ref, vc_ref, o_ref):
    n = pl.program_id(1) * A_QB
    nb = SEQ // A_BLOCK
    nq = A_QB * A_BLOCK
    nk = A_KB * A_BLOCK
    start = pl.multiple_of(jnp.clip(n - 1, 0, nb - A_KB) * A_BLOCK, A_BLOCK)
    q = q_ref[:, 0:256]
    kw = k_ref[pl.ds(start, nk), 256:384]
    vw = v_ref[pl.ds(start, nk), :]
    kc = kc_ref[:, 256:384]
    vc = vc_ref[...]
    qrow = _iota((2 * nq, nk), 0)
    qpos = n * A_BLOCK + jnp.where(qrow < nq, qrow, qrow - nq)
    kpos = start + _iota((2 * nq, nk), 1)
    valid = jnp.abs(qpos - kpos) <= A_BLOCK
    top = _iota((2 * nq, 1), 0) < nq
    outs = []
    for g in range(A_KV_HEADS):
        qg = jnp.concatenate([q[:, (2 * g) * 64:(2 * g + 1) * 64],
                              q[:, (2 * g + 1) * 64:(2 * g + 2) * 64]], axis=0)
        kg = kw[:, g * 64:(g + 1) * 64]
        vg = vw[:, g * 64:(g + 1) * 64]
        s_loc = jnp.where(valid, _dot_nt(qg, kg), NEG_INF)
        s_ctx = _dot_nt(qg, kc[:, g * 64:(g + 1) * 64])
        sink = jnp.where(top, sink_ref[2 * g], sink_ref[2 * g + 1])
        o = _softmax_pv([s_loc, s_ctx], [vg, vc[:, g * 64:(g + 1) * 64]], sink)
        outs += [o[:nq], o[nq:]]
    o_ref[...] = jnp.concatenate(outs, axis=1).astype(BF16)


def _nattn_kernel(q_ref, kv_ref, c_ref, bias_ref, o_ref):
    i = pl.program_id(1)
    r0 = i * NA_Q_ROWS
    srow = jnp.clip(r0 - NA_WIN_ROWS // 2, 0, GRID_ROWS - NA_K_ROWS)
    start = pl.multiple_of(srow * GRID_W, GRID_W)
    nq = NA_Q_ROWS * GRID_W
    nk = NA_K_ROWS * GRID_W
    q = q_ref[:, 0:256]
    kw = kv_ref[pl.ds(start, nk), 256:512]
    vw = kv_ref[pl.ds(start, nk), 512:768]
    kc = c_ref[:, 256:512]
    vc = c_ref[:, 512:768]
    qrow = r0 + (_iota((nq, nk), 0) >> 6)
    krow = srow + (_iota((nq, nk), 1) >> 6)
    rs = jnp.clip(qrow - NA_WIN_ROWS // 2, 0, GRID_ROWS - NA_WIN_ROWS)
    valid = jnp.logical_and(krow >= rs, krow < rs + NA_WIN_ROWS)
    outs = []
    for h in range(NA_HEADS):
        sl = slice(h * 64, (h + 1) * 64)
        rows = []
        for qi in range(NA_Q_ROWS):
            blocks = []
            for p in range(NA_K_ROWS // 2):
                idx = srow + 2 * p - (r0 + qi) + (NA_WIN_ROWS - 1) + NA_BIAS_OFF
                blocks.append(bias_ref[0, h, idx])
            rows.append(jnp.concatenate(blocks, axis=1))
        bias = jnp.concatenate(rows, axis=0)
        s_loc = jnp.where(valid, _dot_nt(q[:, sl], kw[:, sl]) + bias, NEG_INF)
        s_ctx = _dot_nt(q[:, sl], kc[:, sl])
        outs.append(_softmax_pv([s_loc, s_ctx], [vw[:, sl], vc[:, sl]]))
    o_ref[...] = jnp.concatenate(outs, axis=1).astype(BF16)


def _na_bias_table(rpb):
    cq = np.arange(GRID_W)
    kcol = np.arange(GRID_W)
    cs = np.clip(cq - NA_WIN_COLS // 2, 0, GRID_W - NA_WIN_COLS)
    col_valid = (kcol[None, :] >= cs[:, None]) & (kcol[None, :] < cs[:, None] + NA_WIN_COLS)
    coff = np.clip(kcol[None, :] - cq[:, None], -(NA_WIN_COLS - 1), NA_WIN_COLS - 1) + (NA_WIN_COLS - 1)
    n_a = 2 * NA_WIN_ROWS - 1
    n_c = 2 * NA_WIN_COLS - 1
    pick = (np.arange(n_c)[:, None] == coff.reshape(1, -1)).astype(np.float32)
    tm = jnp.einsum("lhak,kn->lhan", rpb.astype(F32), jnp.asarray(pick), precision=lax.Precision.HIGHEST)
    tm = jnp.where(col_valid, tm.reshape(rpb.shape[:3] + (GRID_W, GRID_W)) * math.log2(math.e), NEG_INF)
    neg = jnp.full(rpb.shape[:2] + (1, GRID_W, GRID_W), NEG_INF, F32)
    pad_lo = NA_BIAS_OFF
    pad_hi = NA_BIAS_N + 1 - pad_lo - n_a
    ext = jnp.concatenate([neg] * pad_lo + [tm] + [neg] * pad_hi, axis=2)
    return jnp.concatenate([ext[:, :, :NA_BIAS_N], ext[:, :, 1:NA_BIAS_N + 1]], axis=-1)


def _latent_attn_kernel(sink_ref, aq_ref, ak_ref, av_ref, akc_ref, avc_ref, nq_ref, nkv_ref, nc_ref, bias_ref,
                        oa_ref, on_ref):
    _wattn_kernel(sink_ref, aq_ref, ak_ref, av_ref, akc_ref, avc_ref, oa_ref)
    _nattn_kernel(nq_ref, nkv_ref, nc_ref, bias_ref, on_ref)


def _latent_attn(sink, qk, v, na, bias_t, n_batch, layer):
    nq = A_QB * A_BLOCK
    assert nq == NA_Q_ROWS * GRID_W
    steps = SEQ // nq
    ctx0 = n_batch * SEQ // CTX_LEN
    qrow = lambda b, n: (b * steps + n, 0)
    full = lambda b, n: (b, 0)
    ctx = lambda b, n: (ctx0 + b, 0)
    return pl.pallas_call(
        _latent_attn_kernel,
        grid=(n_batch, steps),
        in_specs=[pl.BlockSpec(memory_space=pltpu.SMEM),
                  pl.BlockSpec((nq, 384), qrow),
                  pl.BlockSpec((SEQ, 384), full),
                  pl.BlockSpec((SEQ, 128), full),
                  pl.BlockSpec((CTX_LEN, 384), ctx),
                  pl.BlockSpec((CTX_LEN, 128), ctx),
                  pl.BlockSpec((nq, 768), qrow),
                  pl.BlockSpec((SEQ, 768), full),
                  pl.BlockSpec((CTX_LEN, 768), ctx),
                  pl.BlockSpec((1, NA_HEADS, NA_BIAS_N, GRID_W, 128), lambda b, n: (layer, 0, 0, 0, 0))],
        out_specs=[pl.BlockSpec((nq, 256), qrow)] * 2,
        out_shape=[jax.ShapeDtypeStruct((n_batch * SEQ, 256), BF16)] * 2,
        compiler_params=_cparams(2, 48),
        name="latent_attn",
    )(sink, qk, qk, v, qk, v, na, na, na, bias_t)


def _ctx_attn_kernel(sink_ref, qk_ref, v_ref, na_ref, oa_ref, on_ref):
    qk = qk_ref[...]
    v = v_ref[...]
    na = na_ref[...]
    outs = []
    for h in range(A_HEADS):
        g = h // (A_HEADS // A_KV_HEADS)
        s = _dot_nt(qk[:, h * 64:(h + 1) * 64], qk[:, 256 + g * 64:256 + (g + 1) * 64])
        sink = jnp.zeros((CTX_LEN, 1), F32) + sink_ref[h]
        outs.append(_softmax_pv([s], [v[:, g * 64:(g + 1) * 64]], sink))
    oa_ref[...] = jnp.concatenate(outs, axis=1).astype(BF16)
    outs = []
    for h in range(NA_HEADS):
        sl = slice(h * 64, (h + 1) * 64)
        s = _dot_nt(na[:, 0:256][:, sl], na[:, 256:512][:, sl])
        outs.append(_softmax_pv([s], [na[:, 512:768][:, sl]]))
    on_ref[...] = jnp.concatenate(outs, axis=1).astype(BF16)


def _ctx_attn(sink, qk, v, na, n_batch):
    ctx0 = n_batch * SEQ // CTX_LEN
    row = lambda b: (ctx0 + b, 0)
    return pl.pallas_call(
        _ctx_attn_kernel,
        grid=(n_batch,),
        in_specs=[pl.BlockSpec(memory_space=pltpu.SMEM),
                  pl.BlockSpec((CTX_LEN, 384), row),
                  pl.BlockSpec((CTX_LEN, 128), row),
                  pl.BlockSpec((CTX_LEN, 768), row)],
        out_specs=[pl.BlockSpec((CTX_LEN, 256), lambda b: (b, 0))] * 2,
        out_shape=[jax.ShapeDtypeStruct((n_batch * CTX_LEN, 256), BF16)] * 2,
        compiler_params=_cparams(1, 32),
        name="ctx_attn",
    )(sink, qk, v, na)


def _outproj_kernel(x_ref, oal_ref, oac_ref, yf_ref, yb_ref, xbc_ref, z_ref, onl_ref, onc_ref, mod_ref,
                    dskip_ref, ng_ref, w_ref, gf_ref, wr1_ref, wr2_ref, br_ref, xo_ref, tok_ref, rt_ref,
                    cnt_ref, *, n_lat_tiles):
    is_lat = pl.program_id(0) < n_lat_tiles
    m = mod_ref[0, 0]
    xs = xbc_ref[...]
    y = yf_ref[...] + yb_ref[...] + dskip_ref[...] * xs
    y = y * _silu(z_ref[...])
    ob = y * lax.rsqrt(jnp.mean(y * y, axis=-1, keepdims=True) + RMS_EPS) * ng_ref[...]
    oa = jnp.where(is_lat, oal_ref[...], oac_ref[...])
    on = jnp.where(is_lat, onl_ref[...], onc_ref[...])
    proj = (_dot(oa, w_ref[0, 0:256, :]) + _dot(ob.astype(BF16), w_ref[0, 256:768, :])
            + _dot(on, w_ref[0, 768:1024, :]))
    x = x_ref[...] + m[2:3] * proj
    xo_ref[...] = x
    t = x * lax.rsqrt(jnp.mean(x * x, axis=-1, keepdims=True) + RMS_EPS) * gf_ref[...]
    t = t * (1.0 + m[4:5]) + m[3:4]
    _store_row_tiles(tok_ref, t)
    t1 = t.astype(BF16)
    t2 = (t - t1.astype(F32)).astype(BF16)
    logits = _dot(t1, wr1_ref[...]) + _dot(t1, wr2_ref[...]) + _dot(t2, wr1_ref[...]) + br_ref[...]

    lane = _iota(logits.shape, 1)
    big = jnp.int32(1 << 20)
    is_g = jnp.logical_and(lane >= N_EXPERTS, lane < N_EXPERTS + MOE_GROUPS)
    gl = jnp.where(is_g, logits, NEG_INF)
    gmax = gl.max(axis=-1, keepdims=True)
    g_w = 1.0 / jnp.exp(gl - gmax).sum(axis=-1, keepdims=True)
    g_idx = jnp.where(gl == gmax, lane, big).min(axis=-1, keepdims=True) - N_EXPERTS
    in_grp = jnp.logical_and(lane < N_EXPERTS, (lane >> 3) == g_idx)
    el = jnp.where(in_grp, logits, NEG_INF)
    l1 = el.max(axis=-1, keepdims=True)
    i1 = jnp.where(el == l1, lane, big).min(axis=-1, keepdims=True)
    el2 = jnp.where(lane == i1, NEG_INF, el)
    l2 = el2.max(axis=-1, keepdims=True)
    i2 = jnp.where(el2 == l2, lane, big).min(axis=-1, keepdims=True)
    e2 = jnp.exp(l2 - l1)
    w1 = g_w / (1.0 + e2)
    w2 = w1 * e2
    rt_ref[...] = jnp.where(lane == 0, i1.astype(F32), jnp.where(lane == 1, i2.astype(F32),
                            jnp.where(lane == 2, w1, jnp.where(lane == 3, w2, 0.0))))
    hot = jnp.logical_or(lane == i1, lane == i2)
    cnt_ref[0] = jnp.where(hot, 1.0, 0.0).sum(axis=0, keepdims=True)


def _outproj(xc, oa_l, oa_c, y_f, y_b, xbc_act, z, on_l, on_c, mod, dskip, norm_g, w_out, g_ffn, wr1, wr2, br,
             n_batch, n_tiles, layer):
    n_lat_tiles = n_batch * SEQ // TILE
    tpb = SEQ // TILE
    row = lambda i: (i, 0)
    lat = lambda i: (jnp.minimum(i, n_lat_tiles - 1), 0)
    ctx = lambda i: (jnp.maximum(i - n_lat_tiles, 0), 0)
    modrow = lambda i: (layer, _tile_mod_row(i, n_lat_tiles, tpb, n_batch), 0, 0)
    const = lambda i: (0, 0)
    kern = functools.partial(_outproj_kernel, n_lat_tiles=n_lat_tiles)
    return pl.pallas_call(
        kern,
        grid=(n_tiles,),
        in_specs=[pl.BlockSpec((TILE, D_MODEL), row),
                  pl.BlockSpec((TILE, 256), lat),
                  pl.BlockSpec((TILE, 256), ctx),
                  pl.BlockSpec((TILE, SSD_INNER), row),
                  pl.BlockSpec((TILE, SSD_INNER), row),
                  pl.BlockSpec((TILE, SSD_INNER), row),
                  pl.BlockSpec((TILE, SSD_INNER), row),
                  pl.BlockSpec((TILE, 256), lat),
                  pl.BlockSpec((TILE, 256), ctx),
                  pl.BlockSpec((1, 1, N_MOD, D_MODEL), modrow),
                  pl.BlockSpec((1, SSD_INNER), const),
                  pl.BlockSpec((1, SSD_INNER), const),
                  pl.BlockSpec((1, D_MODEL, D_MODEL), lambda i: (layer, 0, 0)),
                  pl.BlockSpec((1, D_MODEL), const),
                  pl.BlockSpec((D_MODEL, LANES), const),
                  pl.BlockSpec((D_MODEL, LANES), const),
                  pl.BlockSpec((1, LANES), const)],
        out_specs=[pl.BlockSpec((TILE, D_MODEL), row),
                   pl.BlockSpec((TILE * ROW_SLABS, LANES), row),
                   pl.BlockSpec((TILE, LANES), row),
                   pl.BlockSpec((1, 1, LANES), lambda i: (i, 0, 0))],
        out_shape=[jax.ShapeDtypeStruct((n_tiles * TILE, D_MODEL), F32),
                   jax.ShapeDtypeStruct((n_tiles * TILE * ROW_SLABS, LANES), F32),
                   jax.ShapeDtypeStruct((n_tiles * TILE, LANES), F32),
                   jax.ShapeDtypeStruct((n_tiles, 1, LANES), F32)],
        compiler_params=_cparams(1, 56),
        name="outproj",
    )(xc, oa_l, oa_c, y_f, y_b, xbc_act, z, on_l, on_c, mod, dskip, norm_g, w_out, g_ffn, wr1, wr2, br)


MOE_TM = 512
COMBINE_PARTS = 4


def _moe_max_tiles(n_tokens):
    return (2 * n_tokens + N_EXPERTS * (MOE_TM - 1)) // MOE_TM


def _moe_plan(cnt, n_tiles):
    cnt = cnt[:, 0, :N_EXPERTS].astype(jnp.int32)
    tot = cnt.sum(axis=0)
    tiles_e = (tot + MOE_TM - 1) // MOE_TM
    t_end = jnp.cumsum(tiles_e)
    t_start = t_end - tiles_e
    base = (t_start * MOE_TM)[None, :] + jnp.cumsum(cnt, axis=0) - cnt
    n_used = t_end[-1]
    n_max = _moe_max_tiles(n_tiles * TILE)
    te = jnp.sum(jnp.arange(n_max)[:, None] >= t_end[None, :], axis=1)
    te = jnp.minimum(te, jnp.sum((n_used - 1) >= t_end)).astype(jnp.int32)
    tail = jnp.where(tiles_e > 0, t_end - 1, n_max).astype(jnp.int32)
    base_f = jnp.zeros((n_tiles, 1, LANES), F32).at[:, 0, :N_EXPERTS].set(base.astype(F32))
    return base_f, te, n_used.reshape(1).astype(jnp.int32), tail


def _pos_kernel(rt_ref, base_ref, pos_ref):
    rt = rt_ref[...]
    lane = _iota(rt.shape, 1)
    hot1 = lane == rt[:, 0:1].astype(jnp.int32)
    hot2 = lane == rt[:, 1:2].astype(jnp.int32)
    hot = jnp.where(jnp.logical_or(hot1, hot2), 1.0, 0.0).astype(BF16)
    n = rt.shape[0]
    strict = jnp.where(_iota((n, n), 0) > _iota((n, n), 1), 1.0, 0.0).astype(BF16)
    slot = base_ref[0] + _dot(strict, hot)
    p1 = jnp.where(hot1, slot, 0.0).sum(axis=-1, keepdims=True)
    p2 = jnp.where(hot2, slot, 0.0).sum(axis=-1, keepdims=True)
    pos_ref[...] = jnp.where(lane == 0, p1, jnp.where(lane == 1, p2, 0.0)).astype(jnp.int32)


def _positions(rt, base, n_rows, tok):
    return pl.pallas_call(
        _pos_kernel,
        grid=(n_rows // tok,),
        in_specs=[pl.BlockSpec((tok, LANES), lambda i: (i, 0)),
                  pl.BlockSpec((1, 1, LANES), lambda i: (i * (tok // TILE), 0, 0))],
        out_specs=pl.BlockSpec((tok, LANES), lambda i: (i, 0)),
        out_shape=jax.ShapeDtypeStruct((n_rows, LANES), jnp.int32),
        compiler_params=_cparams(1, 32),
        name="moe_positions",
    )(rt, base)


def _dispatch_kernel(tail_ref, nu_ref, pos_ref, tok_ref, xs_ref, zbuf, zsem, sem, *, n_max):
    i = pl.program_id(0)

    def zero_tile(j):
        start = pl.multiple_of(j * (MOE_TM * ROW_SLABS), MOE_TM * ROW_SLABS)
        return pltpu.make_async_copy(zbuf, xs_ref.at[pl.ds(start, MOE_TM * ROW_SLABS), :], zsem)

    def row(ref, r):
        return ref.at[pl.ds(pl.multiple_of(r * ROW_SLABS, ROW_SLABS), ROW_SLABS), :]

    @pl.when(i == 0)
    def _():
        zbuf[...] = jnp.zeros_like(zbuf)
        for e in range(N_EXPERTS):
            @pl.when(tail_ref[e] != n_max)
            def _():
                zero_tile(tail_ref[e]).start()
        lax.fori_loop(nu_ref[0], n_max + 1, lambda j, c: (zero_tile(j).start(), c)[1], 0)
        for e in range(N_EXPERTS):
            @pl.when(tail_ref[e] != n_max)
            def _():
                zero_tile(tail_ref[e]).wait()
        lax.fori_loop(nu_ref[0], n_max + 1, lambda j, c: (zero_tile(j).wait(), c)[1], 0)

    def body(r, carry):
        src = row(tok_ref, r)
        pltpu.make_async_copy(src, row(xs_ref, pos_ref[0, 0, 2 * r]), sem).start(priority=0)
        pltpu.make_async_copy(src, row(xs_ref, pos_ref[0, 0, 2 * r + 1]), sem).start(priority=1)
        return carry

    n_rows = tok_ref.shape[0] // ROW_SLABS
    lax.fori_loop(0, n_rows, body, 0, unroll=8)
    for _ in range(2):
        pltpu.make_async_copy(tok_ref, xs_ref.at[pl.ds(0, n_rows * ROW_SLABS), :], sem).wait()


def _dispatch(tail, n_used, pos_s, tok, n_max):
    kern = functools.partial(_dispatch_kernel, n_max=n_max)
    n_steps = pos_s.shape[0]
    step_rows = pos_s.shape[2] // 2
    return pl.pallas_call(
        kern,
        grid_spec=pltpu.PrefetchScalarGridSpec(
            num_scalar_prefetch=2,
            grid=(n_steps,),
            in_specs=[pl.BlockSpec((1, 1, 2 * step_rows), lambda i, tail, nu: (i, 0, 0), memory_space=pltpu.SMEM),
                      pl.BlockSpec((step_rows * ROW_SLABS, LANES), lambda i, tail, nu: (i, 0))],
            out_specs=pl.BlockSpec(memory_space=pl.ANY),
            scratch_shapes=[pltpu.VMEM((MOE_TM * ROW_SLABS, LANES), F32), pltpu.SemaphoreType.DMA(()),
                            pltpu.SemaphoreType.DMA(())]),
        out_shape=jax.ShapeDtypeStruct(((n_max + 1) * MOE_TM * ROW_SLABS, LANES), F32),
        compiler_params=_cparams(1, 32),
        name="moe_dispatch",
    )(tail, n_used, pos_s, tok)


def _experts_kernel(te_ref, nu_ref, xs_ref, wg_ref, wu_ref, wd_ref, ys_ref):
    used = pl.program_id(0) < nu_ref[0]

    @pl.when(used)
    def _():
        x = _load_row_tiles(xs_ref, MOE_TM).astype(BF16)
        gate = _dot(x, wg_ref[0].astype(BF16))
        up = _dot(x, wu_ref[0].astype(BF16))
        hid = (_silu(gate) * up).astype(BF16)
        _store_row_tiles(ys_ref, _dot(hid, wd_ref[0].astype(BF16)))

    @pl.when(jnp.logical_not(used))
    def _():
        ys_ref[...] = jnp.zeros_like(ys_ref)


def _experts(te, n_used, xs, w_gate, w_up, w_down, n_max, layer):
    rows = lambda j, te, nu: (jnp.minimum(j, nu[0] - 1), 0)
    wsel = lambda j, te, nu: (layer * N_EXPERTS + te[j], 0, 0)
    return pl.pallas_call(
        _experts_kernel,
        grid_spec=pltpu.PrefetchScalarGridSpec(
            num_scalar_prefetch=2,
            grid=(n_max,),
            in_specs=[pl.BlockSpec((MOE_TM * ROW_SLABS, LANES), rows),
                      pl.BlockSpec((1, D_MODEL, D_EXPERT), wsel),
                      pl.BlockSpec((1, D_MODEL, D_EXPERT), wsel),
                      pl.BlockSpec((1, D_EXPERT, D_MODEL), wsel)],
            out_specs=pl.BlockSpec((MOE_TM * ROW_SLABS, LANES), lambda j, te, nu: (j, 0))),
        out_shape=jax.ShapeDtypeStruct((n_max * MOE_TM * ROW_SLABS, LANES), F32),
        compiler_params=_cparams(1, 48),
        name="moe_experts",
    )(te, n_used, xs, w_gate, w_up, w_down)


def _combine_kernel(pos_ref, ys_ref, x_ref, rt_ref, mod_ref, gfin_ref, o_ref, ybuf0, ybuf1, sem, *, final):
    def row(ref, r):
        return ref.at[pl.ds(pl.multiple_of(r * ROW_SLABS, ROW_SLABS), ROW_SLABS), :]

    n = x_ref.shape[0]
    part = n // COMBINE_PARTS
    for k in range(COMBINE_PARTS):
        def body(r, carry, k=k):
            pltpu.make_async_copy(row(ys_ref, pos_ref[0, 0, 2 * r]), row(ybuf0, r), sem.at[k]).start(priority=0)
            pltpu.make_async_copy(row(ys_ref, pos_ref[0, 0, 2 * r + 1]), row(ybuf1, r), sem.at[k]).start(priority=1)
            return carry

        lax.fori_loop(k * part, (k + 1) * part, body, 0, unroll=8)
    gate = mod_ref[0, 0][5:6]
    for k in range(COMBINE_PARTS):
        rows = pl.ds(k * part * ROW_SLABS, part * ROW_SLABS)
        for buf in (ybuf0, ybuf1):
            pltpu.make_async_copy(ys_ref.at[pl.ds(0, part * ROW_SLABS), :], buf.at[rows, :], sem.at[k]).wait()
        rt = rt_ref[k * part:(k + 1) * part, :]
        f = (rt[:, 2:3] * _load_row_tiles(ybuf0.at[rows, :], part)
             + rt[:, 3:4] * _load_row_tiles(ybuf1.at[rows, :], part))
        x = x_ref[k * part:(k + 1) * part, :] + gate * f
        if final:
            x = x * lax.rsqrt(jnp.mean(x * x, axis=-1, keepdims=True) + RMS_EPS) * gfin_ref[...]
        o_ref[k * part:(k + 1) * part, :] = x


def _combine(pos_s, ys, xmid, rt, mod, g_final, n_batch, layer):
    n_steps = pos_s.shape[0]
    tok = pos_s.shape[2] // 2
    n_lat_steps = n_batch * SEQ // tok
    spb = SEQ // tok
    row = lambda i: (i, 0)
    modrow = lambda i: (layer, _tile_mod_row(i, n_lat_steps, spb, n_batch), 0, 0)
    return pl.pallas_call(
        functools.partial(_combine_kernel, final=(layer == DEPTH - 1)),
        grid=(n_steps,),
        in_specs=[pl.BlockSpec((1, 1, 2 * tok), lambda i: (i, 0, 0), memory_space=pltpu.SMEM),
                  pl.BlockSpec(memory_space=pl.ANY),
                  pl.BlockSpec((tok, D_MODEL), row),
                  pl.BlockSpec((tok, LANES), row),
                  pl.BlockSpec((1, 1, N_MOD, D_MODEL), modrow),
                  pl.BlockSpec((1, D_MODEL), lambda i: (0, 0))],
        out_specs=pl.BlockSpec((tok, D_MODEL), row),
        out_shape=jax.ShapeDtypeStruct((n_steps * tok, D_MODEL), F32),
        scratch_shapes=[pltpu.VMEM((tok * ROW_SLABS, LANES), F32), pltpu.VMEM((tok * ROW_SLABS, LANES), F32),
                        pltpu.SemaphoreType.DMA((COMBINE_PARTS,))],
        compiler_params=_cparams(1, 48),
        name="moe_combine",
    )(pos_s, ys, xmid, rt, mod, g_final)


MOE_TOK = 1024


def _moe(xmid, tok, rt, cnt, mod, g_final, w_gate, w_up, w_down, n_batch, n_tiles, layer):
    n_rows = n_tiles * TILE
    step = MOE_TOK if (n_rows % MOE_TOK == 0 and (n_batch * SEQ) % MOE_TOK == 0 and SEQ % MOE_TOK == 0) else TILE
    base, te, n_used, tail = _moe_plan(cnt, n_tiles)
    n_max = _moe_max_tiles(n_rows)
    pos = _positions(rt, base, n_rows, step)
    pos_s = pos[:, :2].reshape(n_rows // step, 1, 2 * step)
    xs = _dispatch(tail, n_used, pos_s, tok, n_max)
    ys = _experts(te, n_used, xs, w_gate, w_up, w_down, n_max, layer)
    return _combine(pos_s, ys, xmid, rt, mod, g_final, n_batch, layer)


def _rope_tables():
    t = jnp.arange(SEQ)
    rows_pos = (t // GRID_W).astype(F32)
    cols_pos = (t % GRID_W).astype(F32)
    half = HEAD_DIM // 2
    inv = 1.0 / (ROPE_BASE ** (jnp.arange(0, half, 2, dtype=F32) / half))
    ang_r = rows_pos[:, None] * inv[None, :]
    ang_c = cols_pos[:, None] * inv[None, :]
    ang = jnp.concatenate([ang_r, ang_r, ang_c, ang_c], axis=1)
    cos_h, sin_h = jnp.cos(ang), jnp.sin(ang)
    scale = jnp.concatenate([jnp.full((256,), Q_SCALE, F32), jnp.ones((128,), F32)])
    cos_t = jnp.tile(cos_h, (1, 6)) * scale
    sin_t = jnp.tile(sin_h, (1, 6)) * scale
    cos_t = jnp.concatenate([cos_t, jnp.broadcast_to(scale, (TILE, 384))], axis=0)
    sin_t = jnp.concatenate([sin_t, jnp.zeros((TILE, 384), F32)], axis=0)
    return cos_t, sin_t


def _in_weights(w_in):
    wqk = w_in[..., W_QK[0]:W_QK[1]]
    w4 = wqk.reshape(wqk.shape[:-1] + (wqk.shape[-1] // 32, 2, 16))
    w_rot = jnp.concatenate([-w4[..., 1:2, :], w4[..., 0:1, :]], axis=-2).reshape(wqk.shape)
    na_scale = jnp.concatenate([jnp.full((256,), Q_SCALE, F32), jnp.ones((512,), F32)])
    w_na = w_in[..., W_NA[0]:W_NA[1]] * na_scale
    return w_in.astype(BF16), w_rot.astype(BF16), w_na.astype(BF16)


def _router_weight(w_rg, b_rg, w_re, b_re):
    w = jnp.concatenate([w_re, w_rg, jnp.zeros((D_MODEL, LANES - N_EXPERTS - MOE_GROUPS), F32)], axis=1)
    b = jnp.concatenate([b_re, b_rg, jnp.zeros((LANES - N_EXPERTS - MOE_GROUPS,), F32)]).reshape(1, LANES)
    w1 = w.astype(BF16)
    w2 = (w - w1.astype(F32)).astype(BF16)
    return w1, w2, b


def kernel(x, c, ctx, c_ctx, w_mod, b_mod, g_mix, w_in, attn_sink, ssd_conv_w, ssd_conv_b, ssd_dt_bias, ssd_a_log, ssd_d, ssd_norm_g, na_rpb, w_out, g_ffn, w_router_group, b_router_group, w_router_expert, b_router_expert, w_exp_gate, w_exp_up, w_exp_down, g_final):
    n_batch, s, d = x.shape
    assert (s, d) == (SEQ, D_MODEL) and ctx.shape[1:] == (CTX_LEN, D_MODEL) and n_batch < 16
    n_lat = n_batch * SEQ
    n_ctx = n_batch * CTX_LEN
    assert n_ctx % TILE == 0
    n_lat_tiles = n_lat // TILE
    n_all_tiles = (n_lat + n_ctx) // TILE

    xc = jnp.concatenate([x.reshape(n_lat, d), ctx.reshape(n_ctx, d)], axis=0)
    cin = jnp.zeros((16, d), F32).at[:n_batch].set(c).at[n_batch].set(c_ctx)
    mod = _modulation(cin, w_mod, b_mod).reshape(DEPTH, 16, N_MOD, d)
    cos_t, sin_t = _rope_tables()
    w_in_b, w_rot, w_na = _in_weights(w_in)
    w_out_b = w_out.astype(BF16)
    bias_t = _na_bias_table(na_rpb)
    g_mix3 = g_mix.reshape(DEPTH, 1, d)
    w_gate = w_exp_gate.reshape(DEPTH * N_EXPERTS, D_MODEL, D_EXPERT)
    w_up = w_exp_up.reshape(DEPTH * N_EXPERTS, D_MODEL, D_EXPERT)
    w_down = w_exp_down.reshape(DEPTH * N_EXPERTS, D_EXPERT, D_MODEL)

    for layer in range(DEPTH):
        need_ctx = layer < DEPTH - 1
        conv_w = jnp.zeros((8, SSD_CONV_DIM), F32).at[:SSD_CONV].set(ssd_conv_w[layer])
        qk, v, z, xbc_act, na, dt_raw = _inproj(xc, mod, g_mix3, w_in_b, w_rot, w_na, cos_t, sin_t, conv_w,
                                                ssd_conv_b[layer].reshape(1, SSD_CONV_DIM), n_batch, layer)
        sink = attn_sink[layer].astype(F32) * math.log2(math.e)
        oa, on = _latent_attn(sink, qk, v, na, bias_t, n_batch, layer)
        y_f, y_b = _ssd_scan(xbc_act, dt_raw, ssd_dt_bias[layer], ssd_a_log[layer], n_batch)
        oa_c, on_c = _ctx_attn(sink, qk, v, na, n_batch) if need_ctx else (oa, on)
        n_tiles = n_all_tiles if need_ctx else n_lat_tiles
        wr1, wr2, br = _router_weight(w_router_group[layer], b_router_group[layer],
                                      w_router_expert[layer], b_router_expert[layer])
        dskip = jnp.repeat(ssd_d[layer].astype(F32), SSD_INNER // SSD_HEADS).reshape(1, SSD_INNER)
        xmid, tok, rt, cnt = _outproj(xc, oa, oa_c, y_f, y_b, xbc_act, z, on, on_c, mod, dskip,
                                      ssd_norm_g[layer].reshape(1, SSD_INNER), w_out_b,
                                      g_ffn[layer].reshape(1, d), wr1, wr2, br, n_batch, n_tiles, layer)
        xc = _moe(xmid, tok, rt, cnt, mod, g_final.reshape(1, d), w_gate, w_up, w_down, n_batch, n_tiles, layer)

    return xc.reshape(n_batch, SEQ, d)
```

```python
import functools
import math

import jax
import jax.numpy as jnp
import numpy as np
from jax import lax
from jax.experimental import pallas as pl
from jax.experimental.pallas import tpu as pltpu

F32 = jnp.float32
BF16 = jnp.bfloat16

D_MODEL = 1024
SEQ = 2048
DEPTH = 4
GRID_W = 64
GRID_ROWS = SEQ // GRID_W
CTX_LEN = 256
HEAD_DIM = 64
A_HEADS = 4
A_KV_HEADS = 2
A_BLOCK = 128
ROPE_BASE = 10000.0
SSD_HEADS = 8
SSD_INNER = 512
SSD_STATE = 64
SSD_CONV = 5
SSD_CHUNK = 128
SSD_STEP_CHUNKS = 2
SSD_CONV_DIM = 768
NA_HEADS = 4
NA_WIN_ROWS = 8
NA_WIN_COLS = 16
MOE_GROUPS = 4
MOE_EXPERTS = 8
N_EXPERTS = MOE_GROUPS * MOE_EXPERTS
D_EXPERT = 256
N_MOD = 6
RMS_EPS = 1e-6
NEG_INF = -1e30
Q_SCALE = HEAD_DIM ** -0.5 * math.log2(math.e)

TILE = 512
LANES = 128
N_IN = 2576
W_QK, W_V, W_Z, W_XBC = (0, 384), (384, 512), (512, 1024), (1024, 1792)
W_DT = (1792, 1920)
W_NA = (1808, 2576)
NA_Q_ROWS = 4
NA_K_ROWS = NA_WIN_ROWS + NA_Q_ROWS
NA_BIAS_OFF = NA_Q_ROWS
NA_BIAS_N = NA_BIAS_OFF + (NA_K_ROWS - 2) + (NA_WIN_ROWS - 1) + 1


def _cparams(n_axes, vmem_mb):
    return pltpu.CompilerParams(dimension_semantics=("arbitrary",) * n_axes,
                                vmem_limit_bytes=vmem_mb << 20)


def _split3(x):
    h1 = x.astype(BF16)
    r1 = x - h1.astype(F32)
    h2 = r1.astype(BF16)
    h3 = (r1 - h2.astype(F32)).astype(BF16)
    return h1, h2, h3


def _dot(a, b):
    return jnp.dot(a, b, preferred_element_type=F32)


def _dot_nt(a, b):
    return lax.dot_general(a, b, (((1,), (1,)), ((), ())), preferred_element_type=F32)


def _dot_exact_lhs(lhs_bf16, x):
    return _dot(jnp.concatenate([lhs_bf16] * 3, axis=1), jnp.concatenate(_split3(x), axis=0))


def _dot_exact_rhs(x, rhs3_bf16):
    return _dot(jnp.concatenate(_split3(x), axis=1), rhs3_bf16)


def _silu(x):
    return x * jax.nn.sigmoid(x)


ROW_SLABS = D_MODEL // LANES


def _store_row_tiles(ref, x):
    n = x.shape[0]
    for s in range(ROW_SLABS):
        ref[pl.ds(s, n, stride=ROW_SLABS), :] = x[:, s * LANES:(s + 1) * LANES]


def _load_row_tiles(ref, n):
    return jnp.concatenate([ref[pl.ds(s, n, stride=ROW_SLABS), :] for s in range(ROW_SLABS)], axis=1)


def _iota(shape, dim):
    return lax.broadcasted_iota(jnp.int32, shape, dim)


def _mod_kernel(c_ref, w_ref, b_ref, o_ref):
    a = _silu(c_ref[...])
    a1, a2, _ = _split3(a)
    w = w_ref[0]
    w1 = w.astype(BF16)
    w2 = (w - w1.astype(F32)).astype(BF16)
    o_ref[0] = _dot(a1, w1) + _dot(a1, w2) + _dot(a2, w1) + b_ref[0]


def _modulation(cin, w_mod, b_mod):
    nt = 1024
    return pl.pallas_call(
        _mod_kernel,
        grid=(DEPTH, N_MOD * D_MODEL // nt),
        in_specs=[pl.BlockSpec((16, D_MODEL), lambda l, j: (0, 0)),
                  pl.BlockSpec((1, D_MODEL, nt), lambda l, j: (l, 0, j)),
                  pl.BlockSpec((1, 1, nt), lambda l, j: (l, 0, j))],
        out_specs=pl.BlockSpec((1, 16, nt), lambda l, j: (l, 0, j)),
        out_shape=jax.ShapeDtypeStruct((DEPTH, 16, N_MOD * D_MODEL), F32),
        compiler_params=_cparams(2, 40),
        name="modulation",
    )(cin, w_mod, b_mod.reshape(DEPTH, 1, N_MOD * D_MODEL))


def _inproj_kernel(prev_ref, x_ref, next_ref, mod_ref, g_ref, w_ref, wrot_ref, wna_ref, cos_ref, sin_ref,
                   cw_ref, cb_ref, qk_ref, v_ref, z_ref, xbc_ref, na_ref, dt_ref, *, n_lat_tiles, tiles_per_batch):
    xe = jnp.concatenate([prev_ref[...], x_ref[...], next_ref[...]], axis=0)
    m = mod_ref[0, 0]
    he = xe * lax.rsqrt(jnp.mean(xe * xe, axis=-1, keepdims=True) + RMS_EPS) * g_ref[0]
    hbe = (he * (1.0 + m[1:2]) + m[0:1]).astype(BF16)
    hb = hbe[8:8 + TILE]

    def mm(cols):
        return _dot(hb, w_ref[0, :, cols[0]:cols[1]])

    qk = mm(W_QK) * cos_ref[...] + _dot(hb, wrot_ref[0]) * sin_ref[...]
    qk_ref[...] = qk.astype(BF16)
    v_ref[...] = mm(W_V).astype(BF16)
    z_ref[...] = mm(W_Z)
    na_ref[...] = _dot(hb, wna_ref[0]).astype(BF16)
    dt_ref[...] = mm(W_DT)

    i = pl.program_id(0)
    is_lat = i < n_lat_tiles
    r = _iota((TILE, 1), 0)
    pos = jnp.where(is_lat, (i % tiles_per_batch) * TILE + r, r & (CTX_LEN - 1))
    seq_len = jnp.where(is_lat, SEQ, CTX_LEN)
    xbc_e = _dot(hbe, w_ref[0, :, W_XBC[0]:W_XBC[1]])
    cw = cw_ref[...]
    acc = jnp.zeros((TILE, SSD_CONV_DIM), F32) + cb_ref[...]
    for k in range(SSD_CONV):
        off = k - SSD_CONV // 2
        tap = xbc_e[8 + off:8 + off + TILE, :]
        if off != 0:
            tap = jnp.where(jnp.logical_and(pos + off >= 0, pos + off < seq_len), tap, 0.0)
        acc = acc + tap * cw[k:k + 1, :]
    xbc_ref[...] = _silu(acc)


def _tile_mod_row(i, n_lat_tiles, tiles_per_batch, n_batch):
    return jnp.where(i < n_lat_tiles, i // tiles_per_batch, n_batch)


def _inproj(xc, mod, g_mix, w_in_b, w_rot, w_na, cos_t, sin_t, conv_w, conv_b, n_batch, layer):
    nt = xc.shape[0]
    n_lat_tiles = n_batch * SEQ // TILE
    tpb = SEQ // TILE
    t8 = TILE // 8
    row = lambda i: (i, 0)
    modrow = lambda i: (layer, _tile_mod_row(i, n_lat_tiles, tpb, n_batch), 0, 0)
    posrow = lambda i: (jnp.where(i < n_lat_tiles, i % tpb, tpb), 0)
    lay = lambda i: (layer, 0, 0)
    const = lambda i: (0, 0)
    outs = [(384, BF16), (128, BF16), (512, F32), (768, F32), (768, BF16), (128, F32)]
    kern = functools.partial(_inproj_kernel, n_lat_tiles=n_lat_tiles, tiles_per_batch=tpb)
    return pl.pallas_call(
        kern,
        grid=(nt // TILE,),
        in_specs=[pl.BlockSpec((8, D_MODEL), lambda i: (jnp.maximum(i * t8 - 1, 0), 0)),
                  pl.BlockSpec((TILE, D_MODEL), row),
                  pl.BlockSpec((8, D_MODEL), lambda i: (jnp.minimum(i * t8 + t8, nt // 8 - 1), 0)),
                  pl.BlockSpec((1, 1, N_MOD, D_MODEL), modrow),
                  pl.BlockSpec((1, 1, D_MODEL), lay),
                  pl.BlockSpec((1, D_MODEL, N_IN), lay),
                  pl.BlockSpec((1, D_MODEL, 384), lay),
                  pl.BlockSpec((1, D_MODEL, 768), lay),
                  pl.BlockSpec((TILE, 384), posrow),
                  pl.BlockSpec((TILE, 384), posrow),
                  pl.BlockSpec((8, SSD_CONV_DIM), const),
                  pl.BlockSpec((1, SSD_CONV_DIM), const)],
        out_specs=[pl.BlockSpec((TILE, w), row) for w, _ in outs],
        out_shape=[jax.ShapeDtypeStruct((nt, w), dt) for w, dt in outs],
        compiler_params=_cparams(1, 56),
        name="inproj",
    )(xc, xc, xc, mod, g_mix, w_in_b, w_rot, w_na, cos_t, sin_t, conv_w, conv_b)


def _ssd_kernel(xf_ref, dtf_ref, xb_ref, dtb_ref, bias_ref, alog_ref, e512_ref, e1024_ref,
                yf_ref, yb_ref, stf_ref, stb_ref):
    @pl.when(pl.program_id(1) == 0)
    def _():
        stf_ref[...] = jnp.zeros_like(stf_ref)
        stb_ref[...] = jnp.zeros_like(stb_ref)

    for j in range(SSD_STEP_CHUNKS):
        rf = j * SSD_CHUNK
        rb = (SSD_STEP_CHUNKS - 1 - j) * SSD_CHUNK
        _ssd_chunk(0, rf, xf_ref, dtf_ref, bias_ref[0], alog_ref[0], e512_ref, e1024_ref, yf_ref, stf_ref)
        _ssd_chunk(1, rb, xb_ref, dtb_ref, bias_ref[1], alog_ref[1], e512_ref, e1024_ref, yb_ref, stb_ref)


def _ssd_chunk(d, r0, xbc_ref, dt_ref, dt_bias, a_log, e512_ref, e1024_ref, y_ref, st_ref):
    q = SSD_CHUNK
    xbc = xbc_ref[r0:r0 + q, :]
    xs = xbc[:, :SSD_INNER]
    bm = xbc[:, SSD_INNER:SSD_INNER + 128]
    cm = xbc[:, SSD_INNER + 128:]
    dtr = dt_ref[r0:r0 + q, :] + dt_bias
    dt = jnp.maximum(dtr, 0.0) + jnp.log(1.0 + jnp.exp(-jnp.abs(dtr)))
    a = -jnp.exp(a_log)
    da = dt * a

    ri = _iota((q, q), 0)
    ci = _iota((q, q), 1)
    tri = (ri >= ci) if d == 0 else (ri <= ci)
    trib = jnp.where(tri, 1.0, 0.0).astype(BF16)
    acs = _dot_exact_lhs(trib, da)
    acs_t = acs.T

    both_e = _dot_exact_rhs(jnp.concatenate([dt, acs], axis=0), e512_ref[d])
    dt_e = both_e[:q]
    acs_e = both_e[q:]
    acs_e2 = _dot_exact_rhs(acs, e1024_ref[d])
    tot_e = acs_e[q - 1:q, :] if d == 0 else acs_e[0:1, :]

    xdt = xs * dt_e
    xdec = (xdt * jnp.exp(tot_e - acs_e)).astype(BF16)
    btb = bm.T.astype(BF16)
    lane = _iota((q, 128), 1)
    cm0 = jnp.where(lane < SSD_STATE, cm, 0.0).astype(BF16)
    cm1 = jnp.where(lane >= SSD_STATE, cm, 0.0).astype(BF16)
    cbs = (_dot(cm0, btb), _dot(cm1, btb))

    st = st_ref[...]
    y_off = _dot(cm.astype(BF16), st.astype(BF16)) * jnp.exp(acs_e)
    s_all = _dot(btb, xdec)
    same = (_iota((q, SSD_INNER), 0) >> 6) == (_iota((q, SSD_INNER), 1) >> 8)
    st_ref[...] = jnp.where(same, st * jnp.exp(tot_e) + s_all, 0.0)

    for pair in range(SSD_HEADS // 2):
        cb = cbs[pair // 2]
        xp = xdt[:, pair * 128:(pair + 1) * 128]
        acc = None
        for k in range(2):
            h = 2 * pair + k
            seg = acs_e2[:, h * 128:(h + 1) * 128] - acs_t[d * SSD_HEADS + h:d * SSD_HEADS + h + 1, :]
            lmat = jnp.exp(jnp.where(tri, seg, NEG_INF))
            g = (cb * lmat).astype(BF16)
            rhs = jnp.where((lane < 64) if k == 0 else (lane >= 64), xp, 0.0).astype(BF16)
            t = _dot(g, rhs)
            acc = t if acc is None else acc + t
        y_ref[r0:r0 + q, pair * 128:(pair + 1) * 128] = acc + y_off[:, pair * 128:(pair + 1) * 128]


def _ssd_scan(xbc_act, dt_raw, dt_bias, a_log, n_batch):
    nt = xbc_act.shape[0]
    rows = SSD_STEP_CHUNKS * SSD_CHUNK
    n_lat_blk = n_batch * (SEQ // rows)
    lat_c = SEQ // rows
    ctx_c = CTX_LEN // rows
    n_steps = lat_c + ctx_c

    def blk_f(b, c):
        return jnp.where(c < ctx_c, n_lat_blk + b * ctx_c + c, b * lat_c + c - ctx_c)

    def blk_b(b, c):
        return jnp.where(c < ctx_c, n_lat_blk + b * ctx_c + ctx_c - 1 - c, b * lat_c + n_steps - 1 - c)

    lanes = np.arange(128)

    def expand(width, per_head):
        e = np.stack([(lanes[:, None] - d * SSD_HEADS == (np.arange(width)[None, :] // per_head))
                      for d in range(2)]).astype(np.float32)
        return jnp.asarray(np.tile(e, (1, 3, 1)), BF16)

    dtb = jnp.zeros((2, 1, 128), F32)
    alog = jnp.zeros((2, 1, 128), F32)
    for d in range(2):
        dtb = dtb.at[d, 0, d * SSD_HEADS:(d + 1) * SSD_HEADS].set(dt_bias[d])
        alog = alog.at[d, 0, d * SSD_HEADS:(d + 1) * SSD_HEADS].set(a_log[d])
    const3 = lambda b, c: (0, 0, 0)
    return pl.pallas_call(
        _ssd_kernel,
        grid=(n_batch, n_steps),
        in_specs=[pl.BlockSpec((rows, SSD_CONV_DIM), lambda b, c: (blk_f(b, c), 0)),
                  pl.BlockSpec((rows, 128), lambda b, c: (blk_f(b, c), 0)),
                  pl.BlockSpec((rows, SSD_CONV_DIM), lambda b, c: (blk_b(b, c), 0)),
                  pl.BlockSpec((rows, 128), lambda b, c: (blk_b(b, c), 0)),
                  pl.BlockSpec((2, 1, 128), const3),
                  pl.BlockSpec((2, 1, 128), const3),
                  pl.BlockSpec((2, 384, 512), const3),
                  pl.BlockSpec((2, 384, 1024), const3)],
        out_specs=[pl.BlockSpec((rows, SSD_INNER), lambda b, c: (blk_f(b, c), 0)),
                   pl.BlockSpec((rows, SSD_INNER), lambda b, c: (blk_b(b, c), 0))],
        out_shape=[jax.ShapeDtypeStruct((nt, SSD_INNER), F32)] * 2,
        scratch_shapes=[pltpu.VMEM((128, SSD_INNER), F32)] * 2,
        compiler_params=_cparams(2, 32),
        name="ssd_scan",
    )(xbc_act, dt_raw, xbc_act, dt_raw, dtb, alog, expand(SSD_INNER, SSD_INNER // SSD_HEADS),
      expand(SSD_HEADS * 128, 128))


def _softmax_pv(s_list, v_list, extra_logit=None):
    m = s_list[0].max(axis=-1, keepdims=True)
    for s in s_list[1:]:
        m = jnp.maximum(m, s.max(axis=-1, keepdims=True))
    if extra_logit is not None:
        m = jnp.maximum(m, extra_logit)
    den = None
    o = None
    for s, v in zip(s_list, v_list):
        p = jnp.exp2(s - m)
        ps = p.sum(axis=-1, keepdims=True)
        den = ps if den is None else den + ps
        t = _dot(p.astype(BF16), v)
        o = t if o is None else o + t
    if extra_logit is not None:
        den = den + jnp.exp2(extra_logit - m)
    return o / den


A_QB = 2
A_KB = A_QB + 2


def _wattn_kernel(sink_ref, q_ref, k_ref, v_ref, kc_ref, vc_ref, o_ref):
    n = pl.program_id(1) * A_QB
    nb = SEQ // A_BLOCK
    nq = A_QB * A_BLOCK
    nk = A_KB * A_BLOCK
    start = pl.multiple_of(jnp.clip(n - 1, 0, nb - A_KB) * A_BLOCK, A_BLOCK)
    q = q_ref[:, 0:256]
    kw = k_ref[pl.ds(start, nk), 256:384]
    vw = v_ref[pl.ds(start, nk), :]
    kc = kc_ref[:, 256:384]
    vc = vc_ref[...]
    qrow = _iota((2 * nq, nk), 0)
    qpos = n * A_BLOCK + jnp.where(qrow < nq, qrow, qrow - nq)
    kpos = start + _iota((2 * nq, nk), 1)
    valid = jnp.abs(qpos - kpos) <= A_BLOCK
    top = _iota((2 * nq, 1), 0) < nq
    outs = []
    for g in range(A_KV_HEADS):
        qg = jnp.concatenate([q[:, (2 * g) * 64:(2 * g + 1) * 64],
                              q[:, (2 * g + 1) * 64:(2 * g + 2) * 64]], axis=0)
        kg = kw[:, g * 64:(g + 1) * 64]
        vg = vw[:, g * 64:(g + 1) * 64]
        s_loc = jnp.where(valid, _dot_nt(qg, kg), NEG_INF)
        s_ctx = _dot_nt(qg, kc[:, g * 64:(g + 1) * 64])
        sink = jnp.where(top, sink_ref[2 * g], sink_ref[2 * g + 1])
        o = _softmax_pv([s_loc, s_ctx], [vg, vc[:, g * 64:(g + 1) * 64]], sink)
        outs += [o[:nq], o[nq:]]
    o_ref[...] = jnp.concatenate(outs, axis=1).astype(BF16)


def _nattn_kernel(q_ref, kv_ref, c_ref, bias_ref, o_ref):
    i = pl.program_id(1)
    r0 = i * NA_Q_ROWS
    srow = jnp.clip(r0 - NA_WIN_ROWS // 2, 0, GRID_ROWS - NA_K_ROWS)
    start = pl.multiple_of(srow * GRID_W, GRID_W)
    nq = NA_Q_ROWS * GRID_W
    nk = NA_K_ROWS * GRID_W
    q = q_ref[:, 0:256]
    kw = kv_ref[pl.ds(start, nk), 256:512]
    vw = kv_ref[pl.ds(start, nk), 512:768]
    kc = c_ref[:, 256:512]
    vc = c_ref[:, 512:768]
    qrow = r0 + (_iota((nq, nk), 0) >> 6)
    krow = srow + (_iota((nq, nk), 1) >> 6)
    rs = jnp.clip(qrow - NA_WIN_ROWS // 2, 0, GRID_ROWS - NA_WIN_ROWS)
    valid = jnp.logical_and(krow >= rs, krow < rs + NA_WIN_ROWS)
    outs = []
    for h in range(NA_HEADS):
        sl = slice(h * 64, (h + 1) * 64)
        rows = []
        for qi in range(NA_Q_ROWS):
            blocks = []
            for p in range(NA_K_ROWS // 2):
                idx = srow + 2 * p - (r0 + qi) + (NA_WIN_ROWS - 1) + NA_BIAS_OFF
                blocks.append(bias_ref[0, h, idx])
            rows.append(jnp.concatenate(blocks, axis=1))
        bias = jnp.concatenate(rows, axis=0)
        s_loc = jnp.where(valid, _dot_nt(q[:, sl], kw[:, sl]) + bias, NEG_INF)
        s_ctx = _dot_nt(q[:, sl], kc[:, sl])
        outs.append(_softmax_pv([s_loc, s_ctx], [vw[:, sl], vc[:, sl]]))
    o_ref[...] = jnp.concatenate(outs, axis=1).astype(BF16)


def _na_bias_table(rpb):
    cq = np.arange(GRID_W)
    kcol = np.arange(GRID_W)
    cs = np.clip(cq - NA_WIN_COLS // 2, 0, GRID_W - NA_WIN_COLS)
    col_valid = (kcol[None, :] >= cs[:, None]) & (kcol[None, :] < cs[:, None] + NA_WIN_COLS)
    coff = np.clip(kcol[None, :] - cq[:, None], -(NA_WIN_COLS - 1), NA_WIN_COLS - 1) + (NA_WIN_COLS - 1)
    n_a = 2 * NA_WIN_ROWS - 1
    n_c = 2 * NA_WIN_COLS - 1
    pick = (np.arange(n_c)[:, None] == coff.reshape(1, -1)).astype(np.float32)
    tm = jnp.einsum("lhak,kn->lhan", rpb.astype(F32), jnp.asarray(pick), precision=lax.Precision.HIGHEST)
    tm = jnp.where(col_valid, tm.reshape(rpb.shape[:3] + (GRID_W, GRID_W)) * math.log2(math.e), NEG_INF)
    neg = jnp.full(rpb.shape[:2] + (1, GRID_W, GRID_W), NEG_INF, F32)
    pad_lo = NA_BIAS_OFF
    pad_hi = NA_BIAS_N + 1 - pad_lo - n_a
    ext = jnp.concatenate([neg] * pad_lo + [tm] + [neg] * pad_hi, axis=2)
    return jnp.concatenate([ext[:, :, :NA_BIAS_N], ext[:, :, 1:NA_BIAS_N + 1]], axis=-1)


def _latent_attn_kernel(sink_ref, aq_ref, ak_ref, av_ref, akc_ref, avc_ref, nq_ref, nkv_ref, nc_ref, bias_ref,
                        oa_ref, on_ref):
    _wattn_kernel(sink_ref, aq_ref, ak_ref, av_ref, akc_ref, avc_ref, oa_ref)
    _nattn_kernel(nq_ref, nkv_ref, nc_ref, bias_ref, on_ref)


def _latent_attn(sink, qk, v, na, bias_t, n_batch, layer):
    nq = A_QB * A_BLOCK
    assert nq == NA_Q_ROWS * GRID_W
    steps = SEQ // nq
    ctx0 = n_batch * SEQ // CTX_LEN
    qrow = lambda b, n: (b * steps + n, 0)
    full = lambda b, n: (b, 0)
    ctx = lambda b, n: (ctx0 + b, 0)
    return pl.pallas_call(
        _latent_attn_kernel,
        grid=(n_batch, steps),
        in_specs=[pl.BlockSpec(memory_space=pltpu.SMEM),
                  pl.BlockSpec((nq, 384), qrow),
                  pl.BlockSpec((SEQ, 384), full),
                  pl.BlockSpec((SEQ, 128), full),
                  pl.BlockSpec((CTX_LEN, 384), ctx),
                  pl.BlockSpec((CTX_LEN, 128), ctx),
                  pl.BlockSpec((nq, 768), qrow),
                  pl.BlockSpec((SEQ, 768), full),
                  pl.BlockSpec((CTX_LEN, 768), ctx),
                  pl.BlockSpec((1, NA_HEADS, NA_BIAS_N, GRID_W, 128), lambda b, n: (layer, 0, 0, 0, 0))],
        out_specs=[pl.BlockSpec((nq, 256), qrow)] * 2,
        out_shape=[jax.ShapeDtypeStruct((n_batch * SEQ, 256), BF16)] * 2,
        compiler_params=_cparams(2, 48),
        name="latent_attn",
    )(sink, qk, qk, v, qk, v, na, na, na, bias_t)


def _ctx_attn_kernel(sink_ref, qk_ref, v_ref, na_ref, oa_ref, on_ref):
    qk = qk_ref[...]
    v = v_ref[...]
    na = na_ref[...]
    outs = []
    for h in range(A_HEADS):
        g = h // (A_HEADS // A_KV_HEADS)
        s = _dot_nt(qk[:, h * 64:(h + 1) * 64], qk[:, 256 + g * 64:256 + (g + 1) * 64])
        sink = jnp.zeros((CTX_LEN, 1), F32) + sink_ref[h]
        outs.append(_softmax_pv([s], [v[:, g * 64:(g + 1) * 64]], sink))
    oa_ref[...] = jnp.concatenate(outs, axis=1).astype(BF16)
    outs = []
    for h in range(NA_HEADS):
        sl = slice(h * 64, (h + 1) * 64)
        s = _dot_nt(na[:, 0:256][:, sl], na[:, 256:512][:, sl])
        outs.append(_softmax_pv([s], [na[:, 512:768][:, sl]]))
    on_ref[...] = jnp.concatenate(outs, axis=1).astype(BF16)


def _ctx_attn(sink, qk, v, na, n_batch):
    ctx0 = n_batch * SEQ // CTX_LEN
    row = lambda b: (ctx0 + b, 0)
    return pl.pallas_call(
        _ctx_attn_kernel,
        grid=(n_batch,),
        in_specs=[pl.BlockSpec(memory_space=pltpu.SMEM),
                  pl.BlockSpec((CTX_LEN, 384), row),
                  pl.BlockSpec((CTX_LEN, 128), row),
                  pl.BlockSpec((CTX_LEN, 768), row)],
        out_specs=[pl.BlockSpec((CTX_LEN, 256), lambda b: (b, 0))] * 2,
        out_shape=[jax.ShapeDtypeStruct((n_batch * CTX_LEN, 256), BF16)] * 2,
        compiler_params=_cparams(1, 32),
        name="ctx_attn",
    )(sink, qk, v, na)


def _outproj_kernel(x_ref, oal_ref, oac_ref, yf_ref, yb_ref, xbc_ref, z_ref, onl_ref, onc_ref, mod_ref,
                    dskip_ref, ng_ref, w_ref, gf_ref, wr1_ref, wr2_ref, br_ref, xo_ref, tok_ref, rt_ref,
                    cnt_ref, *, n_lat_tiles):
    is_lat = pl.program_id(0) < n_lat_tiles
    m = mod_ref[0, 0]
    xs = xbc_ref[...]
    y = yf_ref[...] + yb_ref[...] + dskip_ref[...] * xs
    y = y * _silu(z_ref[...])
    ob = y * lax.rsqrt(jnp.mean(y * y, axis=-1, keepdims=True) + RMS_EPS) * ng_ref[...]
    oa = jnp.where(is_lat, oal_ref[...], oac_ref[...])
    on = jnp.where(is_lat, onl_ref[...], onc_ref[...])
    proj = (_dot(oa, w_ref[0, 0:256, :]) + _dot(ob.astype(BF16), w_ref[0, 256:768, :])
            + _dot(on, w_ref[0, 768:1024, :]))
    x = x_ref[...] + m[2:3] * proj
    xo_ref[...] = x
    t = x * lax.rsqrt(jnp.mean(x * x, axis=-1, keepdims=True) + RMS_EPS) * gf_ref[...]
    t = t * (1.0 + m[4:5]) + m[3:4]
    _store_row_tiles(tok_ref, t)
    t1 = t.astype(BF16)
    t2 = (t - t1.astype(F32)).astype(BF16)
    logits = _dot(t1, wr1_ref[...]) + _dot(t1, wr2_ref[...]) + _dot(t2, wr1_ref[...]) + br_ref[...]

    lane = _iota(logits.shape, 1)
    big = jnp.int32(1 << 20)
    is_g = jnp.logical_and(lane >= N_EXPERTS, lane < N_EXPERTS + MOE_GROUPS)
    gl = jnp.where(is_g, logits, NEG_INF)
    gmax = gl.max(axis=-1, keepdims=True)
    g_w = 1.0 / jnp.exp(gl - gmax).sum(axis=-1, keepdims=True)
    g_idx = jnp.where(gl == gmax, lane, big).min(axis=-1, keepdims=True) - N_EXPERTS
    in_grp = jnp.logical_and(lane < N_EXPERTS, (lane >> 3) == g_idx)
    el = jnp.where(in_grp, logits, NEG_INF)
    l1 = el.max(axis=-1, keepdims=True)
    i1 = jnp.where(el == l1, lane, big).min(axis=-1, keepdims=True)
    el2 = jnp.where(lane == i1, NEG_INF, el)
    l2 = el2.max(axis=-1, keepdims=True)
    i2 = jnp.where(el2 == l2, lane, big).min(axis=-1, keepdims=True)
    e2 = jnp.exp(l2 - l1)
    w1 = g_w / (1.0 + e2)
    w2 = w1 * e2
    rt_ref[...] = jnp.where(lane == 0, i1.astype(F32), jnp.where(lane == 1, i2.astype(F32),
                            jnp.where(lane == 2, w1, jnp.where(lane == 3, w2, 0.0))))
    hot = jnp.logical_or(lane == i1, lane == i2)
    cnt_ref[0] = jnp.where(hot, 1.0, 0.0).sum(axis=0, keepdims=True)


def _outproj(xc, oa_l, oa_c, y_f, y_b, xbc_act, z, on_l, on_c, mod, dskip, norm_g, w_out, g_ffn, wr1, wr2, br,
             n_batch, n_tiles, layer):
    n_lat_tiles = n_batch * SEQ // TILE
    tpb = SEQ // TILE
    row = lambda i: (i, 0)
    lat = lambda i: (jnp.minimum(i, n_lat_tiles - 1), 0)
    ctx = lambda i: (jnp.maximum(i - n_lat_tiles, 0), 0)
    modrow = lambda i: (layer, _tile_mod_row(i, n_lat_tiles, tpb, n_batch), 0, 0)
    const = lambda i: (0, 0)
    kern = functools.partial(_outproj_kernel, n_lat_tiles=n_lat_tiles)
    return pl.pallas_call(
        kern,
        grid=(n_tiles,),
        in_specs=[pl.BlockSpec((TILE, D_MODEL), row),
                  pl.BlockSpec((TILE, 256), lat),
                  pl.BlockSpec((TILE, 256), ctx),
                  pl.BlockSpec((TILE, SSD_INNER), row),
                  pl.BlockSpec((TILE, SSD_INNER), row),
                  pl.BlockSpec((TILE, SSD_INNER), row),
                  pl.BlockSpec((TILE, SSD_INNER), row),
                  pl.BlockSpec((TILE, 256), lat),
                  pl.BlockSpec((TILE, 256), ctx),
                  pl.BlockSpec((1, 1, N_MOD, D_MODEL), modrow),
                  pl.BlockSpec((1, SSD_INNER), const),
                  pl.BlockSpec((1, SSD_INNER), const),
                  pl.BlockSpec((1, D_MODEL, D_MODEL), lambda i: (layer, 0, 0)),
                  pl.BlockSpec((1, D_MODEL), const),
                  pl.BlockSpec((D_MODEL, LANES), const),
                  pl.BlockSpec((D_MODEL, LANES), const),
                  pl.BlockSpec((1, LANES), const)],
        out_specs=[pl.BlockSpec((TILE, D_MODEL), row),
                   pl.BlockSpec((TILE * ROW_SLABS, LANES), row),
                   pl.BlockSpec((TILE, LANES), row),
                   pl.BlockSpec((1, 1, LANES), lambda i: (i, 0, 0))],
        out_shape=[jax.ShapeDtypeStruct((n_tiles * TILE, D_MODEL), F32),
                   jax.ShapeDtypeStruct((n_tiles * TILE * ROW_SLABS, LANES), F32),
                   jax.ShapeDtypeStruct((n_tiles * TILE, LANES), F32),
                   jax.ShapeDtypeStruct((n_tiles, 1, LANES), F32)],
        compiler_params=_cparams(1, 56),
        name="outproj",
    )(xc, oa_l, oa_c, y_f, y_b, xbc_act, z, on_l, on_c, mod, dskip, norm_g, w_out, g_ffn, wr1, wr2, br)


MOE_TM = 512
COMBINE_PARTS = 4


def _moe_max_tiles(n_tokens):
    return (2 * n_tokens + N_EXPERTS * (MOE_TM - 1)) // MOE_TM


def _moe_plan(cnt, n_tiles):
    cnt = cnt[:, 0, :N_EXPERTS].astype(jnp.int32)
    tot = cnt.sum(axis=0)
    tiles_e = (tot + MOE_TM - 1) // MOE_TM
    t_end = jnp.cumsum(tiles_e)
    t_start = t_end - tiles_e
    base = (t_start * MOE_TM)[None, :] + jnp.cumsum(cnt, axis=0) - cnt
    n_used = t_end[-1]
    n_max = _moe_max_tiles(n_tiles * TILE)
    te = jnp.sum(jnp.arange(n_max)[:, None] >= t_end[None, :], axis=1)
    te = jnp.minimum(te, jnp.sum((n_used - 1) >= t_end)).astype(jnp.int32)
    tail = jnp.where(tiles_e > 0, t_end - 1, n_max).astype(jnp.int32)
    base_f = jnp.zeros((n_tiles, 1, LANES), F32).at[:, 0, :N_EXPERTS].set(base.astype(F32))
    return base_f, te, n_used.reshape(1).astype(jnp.int32), tail


def _pos_kernel(rt_ref, base_ref, pos_ref):
    rt = rt_ref[...]
    lane = _iota(rt.shape, 1)
    hot1 = lane == rt[:, 0:1].astype(jnp.int32)
    hot2 = lane == rt[:, 1:2].astype(jnp.int32)
    hot = jnp.where(jnp.logical_or(hot1, hot2), 1.0, 0.0).astype(BF16)
    n = rt.shape[0]
    strict = jnp.where(_iota((n, n), 0) > _iota((n, n), 1), 1.0, 0.0).astype(BF16)
    slot = base_ref[0] + _dot(strict, hot)
    p1 = jnp.where(hot1, slot, 0.0).sum(axis=-1, keepdims=True)
    p2 = jnp.where(hot2, slot, 0.0).sum(axis=-1, keepdims=True)
    pos_ref[...] = jnp.where(lane == 0, p1, jnp.where(lane == 1, p2, 0.0)).astype(jnp.int32)


def _positions(rt, base, n_rows, tok):
    return pl.pallas_call(
        _pos_kernel,
        grid=(n_rows // tok,),
        in_specs=[pl.BlockSpec((tok, LANES), lambda i: (i, 0)),
                  pl.BlockSpec((1, 1, LANES), lambda i: (i * (tok // TILE), 0, 0))],
        out_specs=pl.BlockSpec((tok, LANES), lambda i: (i, 0)),
        out_shape=jax.ShapeDtypeStruct((n_rows, LANES), jnp.int32),
        compiler_params=_cparams(1, 32),
        name="moe_positions",
    )(rt, base)


def _dispatch_kernel(tail_ref, nu_ref, pos_ref, tok_ref, xs_ref, zbuf, zsem, usem, sem, *, n_max):
    i = pl.program_id(0)

    def zero_tile(j, s):
        start = pl.multiple_of(j * (MOE_TM * ROW_SLABS), MOE_TM * ROW_SLABS)
        return pltpu.make_async_copy(zbuf, xs_ref.at[pl.ds(start, MOE_TM * ROW_SLABS), :], s)

    def row(ref, r):
        return ref.at[pl.ds(pl.multiple_of(r * ROW_SLABS, ROW_SLABS), ROW_SLABS), :]

    @pl.when(i == 0)
    def _():
        zbuf[...] = jnp.zeros_like(zbuf)
        for e in range(N_EXPERTS):
            @pl.when(tail_ref[e] != n_max)
            def _():
                zero_tile(tail_ref[e], zsem).start()
        lax.fori_loop(nu_ref[0], n_max + 1, lambda j, c: (zero_tile(j, usem).start(), c)[1], 0)
        for e in range(N_EXPERTS):
            @pl.when(tail_ref[e] != n_max)
            def _():
                zero_tile(tail_ref[e], zsem).wait()

    def body(r, carry):
        src = row(tok_ref, r)
        pltpu.make_async_copy(src, row(xs_ref, pos_ref[0, 0, 2 * r]), sem).start(priority=0)
        pltpu.make_async_copy(src, row(xs_ref, pos_ref[0, 0, 2 * r + 1]), sem).start(priority=1)
        return carry

    n_rows = tok_ref.shape[0] // ROW_SLABS
    lax.fori_loop(0, n_rows, body, 0, unroll=8)
    for _ in range(2):
        pltpu.make_async_copy(tok_ref, xs_ref.at[pl.ds(0, n_rows * ROW_SLABS), :], sem).wait()

    @pl.when(i == pl.num_programs(0) - 1)
    def _():
        lax.fori_loop(nu_ref[0], n_max + 1, lambda j, c: (zero_tile(j, usem).wait(), c)[1], 0)


def _dispatch(tail, n_used, pos_s, tok, n_max):
    kern = functools.partial(_dispatch_kernel, n_max=n_max)
    n_steps = pos_s.shape[0]
    step_rows = pos_s.shape[2] // 2
    return pl.pallas_call(
        kern,
        grid_spec=pltpu.PrefetchScalarGridSpec(
            num_scalar_prefetch=2,
            grid=(n_steps,),
            in_specs=[pl.BlockSpec((1, 1, 2 * step_rows), lambda i, tail, nu: (i, 0, 0), memory_space=pltpu.SMEM),
                      pl.BlockSpec((step_rows * ROW_SLABS, LANES), lambda i, tail, nu: (i, 0))],
            out_specs=pl.BlockSpec(memory_space=pl.ANY),
            scratch_shapes=[pltpu.VMEM((MOE_TM * ROW_SLABS, LANES), F32), pltpu.SemaphoreType.DMA(()),
                            pltpu.SemaphoreType.DMA(()), pltpu.SemaphoreType.DMA(())]),
        out_shape=jax.ShapeDtypeStruct(((n_max + 1) * MOE_TM * ROW_SLABS, LANES), F32),
        compiler_params=_cparams(1, 32),
        name="moe_dispatch",
    )(tail, n_used, pos_s, tok)


def _experts_kernel(te_ref, nu_ref, xs_ref, wg_ref, wu_ref, wd_ref, ys_ref):
    used = pl.program_id(0) < nu_ref[0]

    @pl.when(used)
    def _():
        x = _load_row_tiles(xs_ref, MOE_TM).astype(BF16)
        gate = _dot(x, wg_ref[0].astype(BF16))
        up = _dot(x, wu_ref[0].astype(BF16))
        hid = (_silu(gate) * up).astype(BF16)
        _store_row_tiles(ys_ref, _dot(hid, wd_ref[0].astype(BF16)))

    @pl.when(jnp.logical_not(used))
    def _():
        ys_ref[...] = jnp.zeros_like(ys_ref)


def _experts(te, n_used, xs, w_gate, w_up, w_down, n_max, layer):
    rows = lambda j, te, nu: (jnp.minimum(j, nu[0] - 1), 0)
    wsel = lambda j, te, nu: (layer * N_EXPERTS + te[j], 0, 0)
    return pl.pallas_call(
        _experts_kernel,
        grid_spec=pltpu.PrefetchScalarGridSpec(
            num_scalar_prefetch=2,
            grid=(n_max,),
            in_specs=[pl.BlockSpec((MOE_TM * ROW_SLABS, LANES), rows),
                      pl.BlockSpec((1, D_MODEL, D_EXPERT), wsel),
                      pl.BlockSpec((1, D_MODEL, D_EXPERT), wsel),
                      pl.BlockSpec((1, D_EXPERT, D_MODEL), wsel)],
            out_specs=pl.BlockSpec((MOE_TM * ROW_SLABS, LANES), lambda j, te, nu: (j, 0))),
        out_shape=jax.ShapeDtypeStruct((n_max * MOE_TM * ROW_SLABS, LANES), F32),
        compiler_params=_cparams(1, 48),
        name="moe_experts",
    )(te, n_used, xs, w_gate, w_up, w_down)


def _combine_kernel(pos_ref, ys_ref, x_ref, rt_ref, mod_ref, gfin_ref, o_ref, ybuf0, ybuf1, sem, *, final):
    def row(ref, r):
        return ref.at[pl.ds(pl.multiple_of(r * ROW_SLABS, ROW_SLABS), ROW_SLABS), :]

    n = x_ref.shape[0]
    part = n // COMBINE_PARTS
    for k in range(COMBINE_PARTS):
        def body(r, carry, k=k):
            pltpu.make_async_copy(row(ys_ref, pos_ref[0, 0, 2 * r]), row(ybuf0, r), sem.at[k]).start(priority=0)
            pltpu.make_async_copy(row(ys_ref, pos_ref[0, 0, 2 * r + 1]), row(ybuf1, r), sem.at[k]).start(priority=1)
            return carry

        lax.fori_loop(k * part, (k + 1) * part, body, 0, unroll=8)
    gate = mod_ref[0, 0][5:6]
    for k in range(COMBINE_PARTS):
        rows = pl.ds(k * part * ROW_SLABS, part * ROW_SLABS)
        for buf in (ybuf0, ybuf1):
            pltpu.make_async_copy(ys_ref.at[pl.ds(0, part * ROW_SLABS), :], buf.at[rows, :], sem.at[k]).wait()
        rt = rt_ref[k * part:(k + 1) * part, :]
        f = (rt[:, 2:3] * _load_row_tiles(ybuf0.at[rows, :], part)
             + rt[:, 3:4] * _load_row_tiles(ybuf1.at[rows, :], part))
        x = x_ref[k * part:(k + 1) * part, :] + gate * f
        if final:
            x = x * lax.rsqrt(jnp.mean(x * x, axis=-1, keepdims=True) + RMS_EPS) * gfin_ref[...]
        o_ref[k * part:(k + 1) * part, :] = x


def _combine(pos_s, ys, xmid, rt, mod, g_final, n_batch, layer):
    n_steps = pos_s.shape[0]
    tok = pos_s.shape[2] // 2
    n_lat_steps = n_batch * SEQ // tok
    spb = SEQ // tok
    row = lambda i: (i, 0)
    modrow = lambda i: (layer, _tile_mod_row(i, n_lat_steps, spb, n_batch), 0, 0)
    return pl.pallas_call(
        functools.partial(_combine_kernel, final=(layer == DEPTH - 1)),
        grid=(n_steps,),
        in_specs=[pl.BlockSpec((1, 1, 2 * tok), lambda i: (i, 0, 0), memory_space=pltpu.SMEM),
                  pl.BlockSpec(memory_space=pl.ANY),
                  pl.BlockSpec((tok, D_MODEL), row),
                  pl.BlockSpec((tok, LANES), row),
                  pl.BlockSpec((1, 1, N_MOD, D_MODEL), modrow),
                  pl.BlockSpec((1, D_MODEL), lambda i: (0, 0))],
        out_specs=pl.BlockSpec((tok, D_MODEL), row),
        out_shape=jax.ShapeDtypeStruct((n_steps * tok, D_MODEL), F32),
        scratch_shapes=[pltpu.VMEM((tok * ROW_SLABS, LANES), F32), pltpu.VMEM((tok * ROW_SLABS, LANES), F32),
                        pltpu.SemaphoreType.DMA((COMBINE_PARTS,))],
        compiler_params=_cparams(1, 48),
        name="moe_combine",
    )(pos_s, ys, xmid, rt, mod, g_final)


MOE_TOK = 1024


def _moe(xmid, tok, rt, cnt, mod, g_final, w_gate, w_up, w_down, n_batch, n_tiles, layer):
    n_rows = n_tiles * TILE
    step = MOE_TOK if (n_rows % MOE_TOK == 0 and (n_batch * SEQ) % MOE_TOK == 0 and SEQ % MOE_TOK == 0) else TILE
    base, te, n_used, tail = _moe_plan(cnt, n_tiles)
    n_max = _moe_max_tiles(n_rows)
    pos = _positions(rt, base, n_rows, step)
    pos_s = pos[:, :2].reshape(n_rows // step, 1, 2 * step)
    xs = _dispatch(tail, n_used, pos_s, tok, n_max)
    ys = _experts(te, n_used, xs, w_gate, w_up, w_down, n_max, layer)
    return _combine(pos_s, ys, xmid, rt, mod, g_final, n_batch, layer)


def _rope_tables():
    t = jnp.arange(SEQ)
    rows_pos = (t // GRID_W).astype(F32)
    cols_pos = (t % GRID_W).astype(F32)
    half = HEAD_DIM // 2
    inv = 1.0 / (ROPE_BASE ** (jnp.arange(0, half, 2, dtype=F32) / half))
    ang_r = rows_pos[:, None] * inv[None, :]
    ang_c = cols_pos[:, None] * inv[None, :]
    ang = jnp.concatenate([ang_r, ang_r, ang_c, ang_c], axis=1)
    cos_h, sin_h = jnp.cos(ang), jnp.sin(ang)
    scale = jnp.concatenate([jnp.full((256,), Q_SCALE, F32), jnp.ones((128,), F32)])
    cos_t = jnp.tile(cos_h, (1, 6)) * scale
    sin_t = jnp.tile(sin_h, (1, 6)) * scale
    cos_t = jnp.concatenate([cos_t, jnp.broadcast_to(scale, (TILE, 384))], axis=0)
    sin_t = jnp.concatenate([sin_t, jnp.zeros((TILE, 384), F32)], axis=0)
    return cos_t, sin_t


def _in_weights(w_in):
    wqk = w_in[..., W_QK[0]:W_QK[1]]
    w4 = wqk.reshape(wqk.shape[:-1] + (wqk.shape[-1] // 32, 2, 16))
    w_rot = jnp.concatenate([-w4[..., 1:2, :], w4[..., 0:1, :]], axis=-2).reshape(wqk.shape)
    na_scale = jnp.concatenate([jnp.full((256,), Q_SCALE, F32), jnp.ones((512,), F32)])
    w_na = w_in[..., W_NA[0]:W_NA[1]] * na_scale
    return w_in.astype(BF16), w_rot.astype(BF16), w_na.astype(BF16)


def _router_weight(w_rg, b_rg, w_re, b_re):
    w = jnp.concatenate([w_re, w_rg, jnp.zeros((D_MODEL, LANES - N_EXPERTS - MOE_GROUPS), F32)], axis=1)
    b = jnp.concatenate([b_re, b_rg, jnp.zeros((LANES - N_EXPERTS - MOE_GROUPS,), F32)]).reshape(1, LANES)
    w1 = w.astype(BF16)
    w2 = (w - w1.astype(F32)).astype(BF16)
    return w1, w2, b


def kernel(x, c, ctx, c_ctx, w_mod, b_mod, g_mix, w_in, attn_sink, ssd_conv_w, ssd_conv_b, ssd_dt_bias, ssd_a_log, ssd_d, ssd_norm_g, na_rpb, w_out, g_ffn, w_router_group, b_router_group, w_router_expert, b_router_expert, w_exp_gate, w_exp_up, w_exp_down, g_final):
    n_batch, s, d = x.shape
    assert (s, d) == (SEQ, D_MODEL) and ctx.shape[1:] == (CTX_LEN, D_MODEL) and n_batch < 16
    n_lat = n_batch * SEQ
    n_ctx = n_batch * CTX_LEN
    assert n_ctx % TILE == 0
    n_lat_tiles = n_lat // TILE
    n_all_tiles = (n_lat + n_ctx) // TILE

    xc = jnp.concatenate([x.reshape(n_lat, d), ctx.reshape(n_ctx, d)], axis=0)
    cin = jnp.zeros((16, d), F32).at[:n_batch].set(c).at[n_batch].set(c_ctx)
    mod = _modulation(cin, w_mod, b_mod).reshape(DEPTH, 16, N_MOD, d)
    cos_t, sin_t = _rope_tables()
    w_in_b, w_rot, w_na = _in_weights(w_in)
    w_out_b = w_out.astype(BF16)
    bias_t = _na_bias_table(na_rpb)
    g_mix3 = g_mix.reshape(DEPTH, 1, d)
    w_gate = w_exp_gate.reshape(DEPTH * N_EXPERTS, D_MODEL, D_EXPERT)
    w_up = w_exp_up.reshape(DEPTH * N_EXPERTS, D_MODEL, D_EXPERT)
    w_down = w_exp_down.reshape(DEPTH * N_EXPERTS, D_EXPERT, D_MODEL)

    for layer in range(DEPTH):
        need_ctx = layer < DEPTH - 1
        conv_w = jnp.zeros((8, SSD_CONV_DIM), F32).at[:SSD_CONV].set(ssd_conv_w[layer])
        qk, v, z, xbc_act, na, dt_raw = _inproj(xc, mod, g_mix3, w_in_b, w_rot, w_na, cos_t, sin_t, conv_w,
                                                ssd_conv_b[layer].reshape(1, SSD_CONV_DIM), n_batch, layer)
        sink = attn_sink[layer].astype(F32) * math.log2(math.e)
        oa, on = _latent_attn(sink, qk, v, na, bias_t, n_batch, layer)
        y_f, y_b = _ssd_scan(xbc_act, dt_raw, ssd_dt_bias[layer], ssd_a_log[layer], n_batch)
        oa_c, on_c = _ctx_attn(sink, qk, v, na, n_batch) if need_ctx else (oa, on)
        n_tiles = n_all_tiles if need_ctx else n_lat_tiles
        wr1, wr2, br = _router_weight(w_router_group[layer], b_router_group[layer],
                                      w_router_expert[layer], b_router_expert[layer])
        dskip = jnp.repeat(ssd_d[layer].astype(F32), SSD_INNER // SSD_HEADS).reshape(1, SSD_INNER)
        xmid, tok, rt, cnt = _outproj(xc, oa, oa_c, y_f, y_b, xbc_act, z, on, on_c, mod, dskip,
                                      ssd_norm_g[layer].reshape(1, SSD_INNER), w_out_b,
                                      g_ffn[layer].reshape(1, d), wr1, wr2, br, n_batch, n_tiles, layer)
        xc = _moe(xmid, tok, rt, cnt, mod, g_final.reshape(1, d), w_gate, w_up, w_down, n_batch, n_tiles, layer)

    return xc.reshape(n_batch, SEQ, d)
```

```python
import functools
import math

import jax
import jax.numpy as jnp
import numpy as np
from jax import lax
from jax.experimental import pallas as pl
from jax.experimental.pallas import tpu as pltpu

F32 = jnp.float32
BF16 = jnp.bfloat16

D_MODEL = 1024
SEQ = 2048
DEPTH = 4
GRID_W = 64
GRID_ROWS = SEQ // GRID_W
CTX_LEN = 256
HEAD_DIM = 64
A_HEADS = 4
A_KV_HEADS = 2
A_BLOCK = 128
ROPE_BASE = 10000.0
SSD_HEADS = 8
SSD_INNER = 512
SSD_STATE = 64
SSD_CONV = 5
SSD_CHUNK = 128
SSD_STEP_CHUNKS = 2
SSD_CONV_DIM = 768
NA_HEADS = 4
NA_WIN_ROWS = 8
NA_WIN_COLS = 16
MOE_GROUPS = 4
MOE_EXPERTS = 8
N_EXPERTS = MOE_GROUPS * MOE_EXPERTS
D_EXPERT = 256
N_MOD = 6
RMS_EPS = 1e-6
NEG_INF = -1e30
Q_SCALE = HEAD_DIM ** -0.5 * math.log2(math.e)

TILE = 512
LANES = 128
N_IN = 2576
W_QK, W_V, W_Z, W_XBC = (0, 384), (384, 512), (512, 1024), (1024, 1792)
W_DT = (1792, 1920)
W_NA = (1808, 2576)
NA_Q_ROWS = 4
NA_K_ROWS = NA_WIN_ROWS + NA_Q_ROWS
NA_BIAS_OFF = NA_Q_ROWS
NA_BIAS_N = NA_BIAS_OFF + (NA_K_ROWS - 2) + (NA_WIN_ROWS - 1) + 1


def _cparams(n_axes, vmem_mb):
    return pltpu.CompilerParams(dimension_semantics=("arbitrary",) * n_axes,
                                vmem_limit_bytes=vmem_mb << 20)


def _split3(x):
    h1 = x.astype(BF16)
    r1 = x - h1.astype(F32)
    h2 = r1.astype(BF16)
    h3 = (r1 - h2.astype(F32)).astype(BF16)
    return h1, h2, h3


def _dot(a, b):
    return jnp.dot(a, b, preferred_element_type=F32)


def _dot_nt(a, b):
    return lax.dot_general(a, b, (((1,), (1,)), ((), ())), preferred_element_type=F32)


def _dot_exact_lhs(lhs_bf16, x):
    return _dot(jnp.concatenate([lhs_bf16] * 3, axis=1), jnp.concatenate(_split3(x), axis=0))


def _dot_exact_rhs(x, rhs3_bf16):
    return _dot(jnp.concatenate(_split3(x), axis=1), rhs3_bf16)


def _silu(x):
    return x * jax.nn.sigmoid(x)


ROW_SLABS = D_MODEL // LANES


def _store_row_tiles(ref, x):
    n = x.shape[0]
    for s in range(ROW_SLABS):
        ref[pl.ds(s, n, stride=ROW_SLABS), :] = x[:, s * LANES:(s + 1) * LANES]


def _load_row_tiles(ref, n):
    return jnp.concatenate([ref[pl.ds(s, n, stride=ROW_SLABS), :] for s in range(ROW_SLABS)], axis=1)


def _iota(shape, dim):
    return lax.broadcasted_iota(jnp.int32, shape, dim)


def _mod_kernel(c_ref, w_ref, b_ref, o_ref):
    a = _silu(c_ref[...])
    a1, a2, _ = _split3(a)
    w = w_ref[0]
    w1 = w.astype(BF16)
    w2 = (w - w1.astype(F32)).astype(BF16)
    o_ref[0] = _dot(a1, w1) + _dot(a1, w2) + _dot(a2, w1) + b_ref[0]


def _modulation(cin, w_mod, b_mod):
    nt = 1024
    return pl.pallas_call(
        _mod_kernel,
        grid=(DEPTH, N_MOD * D_MODEL // nt),
        in_specs=[pl.BlockSpec((16, D_MODEL), lambda l, j: (0, 0)),
                  pl.BlockSpec((1, D_MODEL, nt), lambda l, j: (l, 0, j)),
                  pl.BlockSpec((1, 1, nt), lambda l, j: (l, 0, j))],
        out_specs=pl.BlockSpec((1, 16, nt), lambda l, j: (l, 0, j)),
        out_shape=jax.ShapeDtypeStruct((DEPTH, 16, N_MOD * D_MODEL), F32),
        compiler_params=_cparams(2, 16),
        name="modulation",
    )(cin, w_mod, b_mod.reshape(DEPTH, 1, N_MOD * D_MODEL))


def _inproj_kernel(prev_ref, x_ref, next_ref, mod_ref, g_ref, w_ref, wrot_ref, wna_ref, cos_ref, sin_ref,
                   cw_ref, cb_ref, qk_ref, v_ref, z_ref, xbc_ref, na_ref, dt_ref, *, n_lat_tiles, tiles_per_batch):
    xe = jnp.concatenate([prev_ref[...], x_ref[...], next_ref[...]], axis=0)
    m = mod_ref[0, 0]
    he = xe * lax.rsqrt(jnp.mean(xe * xe, axis=-1, keepdims=True) + RMS_EPS) * g_ref[0]
    hbe = (he * (1.0 + m[1:2]) + m[0:1]).astype(BF16)
    hb = hbe[8:8 + TILE]

    def mm(cols):
        return _dot(hb, w_ref[0, :, cols[0]:cols[1]])

    qk = mm(W_QK) * cos_ref[...] + _dot(hb, wrot_ref[0]) * sin_ref[...]
    qk_ref[...] = qk.astype(BF16)
    v_ref[...] = mm(W_V).astype(BF16)
    z_ref[...] = mm(W_Z)
    na_ref[...] = _dot(hb, wna_ref[0]).astype(BF16)
    dt_ref[...] = mm(W_DT)

    i = pl.program_id(0)
    is_lat = i < n_lat_tiles
    r = _iota((TILE, 1), 0)
    pos = jnp.where(is_lat, (i % tiles_per_batch) * TILE + r, r & (CTX_LEN - 1))
    seq_len = jnp.where(is_lat, SEQ, CTX_LEN)
    xbc_e = _dot(hbe, w_ref[0, :, W_XBC[0]:W_XBC[1]])
    cw = cw_ref[...]
    acc = jnp.zeros((TILE, SSD_CONV_DIM), F32) + cb_ref[...]
    for k in range(SSD_CONV):
        off = k - SSD_CONV // 2
        tap = xbc_e[8 + off:8 + off + TILE, :]
        if off != 0:
            tap = jnp.where(jnp.logical_and(pos + off >= 0, pos + off < seq_len), tap, 0.0)
        acc = acc + tap * cw[k:k + 1, :]
    xbc_ref[...] = _silu(acc)


def _tile_mod_row(i, n_lat_tiles, tiles_per_batch, n_batch):
    return jnp.where(i < n_lat_tiles, i // tiles_per_batch, n_batch)


def _inproj(xc, mod, g_mix, w_in_b, w_rot, w_na, cos_t, sin_t, conv_w, conv_b, n_batch, layer):
    nt = xc.shape[0]
    n_lat_tiles = n_batch * SEQ // TILE
    tpb = SEQ // TILE
    t8 = TILE // 8
    row = lambda i: (i, 0)
    modrow = lambda i: (layer, _tile_mod_row(i, n_lat_tiles, tpb, n_batch), 0, 0)
    posrow = lambda i: (jnp.where(i < n_lat_tiles, i % tpb, tpb), 0)
    lay = lambda i: (layer, 0, 0)
    const = lambda i: (0, 0)
    outs = [(384, BF16), (128, BF16), (512, F32), (768, F32), (768, BF16), (128, F32)]
    kern = functools.partial(_inproj_kernel, n_lat_tiles=n_lat_tiles, tiles_per_batch=tpb)
    return pl.pallas_call(
        kern,
        grid=(nt // TILE,),
        in_specs=[pl.BlockSpec((8, D_MODEL), lambda i: (jnp.maximum(i * t8 - 1, 0), 0)),
                  pl.BlockSpec((TILE, D_MODEL), row),
                  pl.BlockSpec((8, D_MODEL), lambda i: (jnp.minimum(i * t8 + t8, nt // 8 - 1), 0)),
                  pl.BlockSpec((1, 1, N_MOD, D_MODEL), modrow),
                  pl.BlockSpec((1, 1, D_MODEL), lay),
                  pl.BlockSpec((1, D_MODEL, N_IN), lay),
                  pl.BlockSpec((1, D_MODEL, 384), lay),
                  pl.BlockSpec((1, D_MODEL, 768), lay),
                  pl.BlockSpec((TILE, 384), posrow),
                  pl.BlockSpec((TILE, 384), posrow),
                  pl.BlockSpec((8, SSD_CONV_DIM), const),
                  pl.BlockSpec((1, SSD_CONV_DIM), const)],
        out_specs=[pl.BlockSpec((TILE, w), row) for w, _ in outs],
        out_shape=[jax.ShapeDtypeStruct((nt, w), dt) for w, dt in outs],
        compiler_params=_cparams(1, 32),
        name="inproj",
    )(xc, xc, xc, mod, g_mix, w_in_b, w_rot, w_na, cos_t, sin_t, conv_w, conv_b)


def _ssd_kernel(xf_ref, dtf_ref, xb_ref, dtb_ref, bias_ref, alog_ref, e512_ref, e1024_ref,
                yf_ref, yb_ref, stf_ref, stb_ref):
    @pl.when(pl.program_id(1) == 0)
    def _():
        stf_ref[...] = jnp.zeros_like(stf_ref)
        stb_ref[...] = jnp.zeros_like(stb_ref)

    for j in range(SSD_STEP_CHUNKS):
        rf = j * SSD_CHUNK
        rb = (SSD_STEP_CHUNKS - 1 - j) * SSD_CHUNK
        _ssd_chunk(0, rf, xf_ref, dtf_ref, bias_ref[0], alog_ref[0], e512_ref, e1024_ref, yf_ref, stf_ref)
        _ssd_chunk(1, rb, xb_ref, dtb_ref, bias_ref[1], alog_ref[1], e512_ref, e1024_ref, yb_ref, stb_ref)


def _ssd_chunk(d, r0, xbc_ref, dt_ref, dt_bias, a_log, e512_ref, e1024_ref, y_ref, st_ref):
    q = SSD_CHUNK
    xbc = xbc_ref[r0:r0 + q, :]
    xs = xbc[:, :SSD_INNER]
    bm = xbc[:, SSD_INNER:SSD_INNER + 128]
    cm = xbc[:, SSD_INNER + 128:]
    dtr = dt_ref[r0:r0 + q, :] + dt_bias
    dt = jnp.maximum(dtr, 0.0) + jnp.log(1.0 + jnp.exp(-jnp.abs(dtr)))
    a = -jnp.exp(a_log)
    da = dt * a

    ri = _iota((q, q), 0)
    ci = _iota((q, q), 1)
    tri = (ri >= ci) if d == 0 else (ri <= ci)
    trib = jnp.where(tri, 1.0, 0.0).astype(BF16)
    acs = _dot_exact_lhs(trib, da)
    acs_t = acs.T

    both_e = _dot_exact_rhs(jnp.concatenate([dt, acs], axis=0), e512_ref[d])
    dt_e = both_e[:q]
    acs_e = both_e[q:]
    acs_e2 = _dot_exact_rhs(acs, e1024_ref[d])
    tot_e = acs_e[q - 1:q, :] if d == 0 else acs_e[0:1, :]

    xdt = xs * dt_e
    xdec = (xdt * jnp.exp(tot_e - acs_e)).astype(BF16)
    btb = bm.T.astype(BF16)
    lane = _iota((q, 128), 1)
    cm0 = jnp.where(lane < SSD_STATE, cm, 0.0).astype(BF16)
    cm1 = jnp.where(lane >= SSD_STATE, cm, 0.0).astype(BF16)
    cbs = (_dot(cm0, btb), _dot(cm1, btb))

    st = st_ref[...]
    y_off = _dot(cm.astype(BF16), st.astype(BF16)) * jnp.exp(acs_e)
    s_all = _dot(btb, xdec)
    same = (_iota((q, SSD_INNER), 0) >> 6) == (_iota((q, SSD_INNER), 1) >> 8)
    st_ref[...] = jnp.where(same, st * jnp.exp(tot_e) + s_all, 0.0)

    for pair in range(SSD_HEADS // 2):
        cb = cbs[pair // 2]
        xp = xdt[:, pair * 128:(pair + 1) * 128]
        acc = None
        for k in range(2):
            h = 2 * pair + k
            seg = acs_e2[:, h * 128:(h + 1) * 128] - acs_t[d * SSD_HEADS + h:d * SSD_HEADS + h + 1, :]
            lmat = jnp.exp(jnp.where(tri, seg, NEG_INF))
            g = (cb * lmat).astype(BF16)
            rhs = jnp.where((lane < 64) if k == 0 else (lane >= 64), xp, 0.0).astype(BF16)
            t = _dot(g, rhs)
            acc = t if acc is None else acc + t
        y_ref[r0:r0 + q, pair * 128:(pair + 1) * 128] = acc + y_off[:, pair * 128:(pair + 1) * 128]


def _ssd_scan(xbc_act, dt_raw, dt_bias, a_log, n_batch):
    nt = xbc_act.shape[0]
    rows = SSD_STEP_CHUNKS * SSD_CHUNK
    n_lat_blk = n_batch * (SEQ // rows)
    lat_c = SEQ // rows
    ctx_c = CTX_LEN // rows
    n_steps = lat_c + ctx_c

    def blk_f(b, c):
        return jnp.where(c < ctx_c, n_lat_blk + b * ctx_c + c, b * lat_c + c - ctx_c)

    def blk_b(b, c):
        return jnp.where(c < ctx_c, n_lat_blk + b * ctx_c + ctx_c - 1 - c, b * lat_c + n_steps - 1 - c)

    lanes = np.arange(128)

    def expand(width, per_head):
        e = np.stack([(lanes[:, None] - d * SSD_HEADS == (np.arange(width)[None, :] // per_head))
                      for d in range(2)]).astype(np.float32)
        return jnp.asarray(np.tile(e, (1, 3, 1)), BF16)

    dtb = jnp.zeros((2, 1, 128), F32)
    alog = jnp.zeros((2, 1, 128), F32)
    for d in range(2):
        dtb = dtb.at[d, 0, d * SSD_HEADS:(d + 1) * SSD_HEADS].set(dt_bias[d])
        alog = alog.at[d, 0, d * SSD_HEADS:(d + 1) * SSD_HEADS].set(a_log[d])
    const3 = lambda b, c: (0, 0, 0)
    return pl.pallas_call(
        _ssd_kernel,
        grid=(n_batch, n_steps),
        in_specs=[pl.BlockSpec((rows, SSD_CONV_DIM), lambda b, c: (blk_f(b, c), 0)),
                  pl.BlockSpec((rows, 128), lambda b, c: (blk_f(b, c), 0)),
                  pl.BlockSpec((rows, SSD_CONV_DIM), lambda b, c: (blk_b(b, c), 0)),
                  pl.BlockSpec((rows, 128), lambda b, c: (blk_b(b, c), 0)),
                  pl.BlockSpec((2, 1, 128), const3),
                  pl.BlockSpec((2, 1, 128), const3),
                  pl.BlockSpec((2, 384, 512), const3),
                  pl.BlockSpec((2, 384, 1024), const3)],
        out_specs=[pl.BlockSpec((rows, SSD_INNER), lambda b, c: (blk_f(b, c), 0)),
                   pl.BlockSpec((rows, SSD_INNER), lambda b, c: (blk_b(b, c), 0))],
        out_shape=[jax.ShapeDtypeStruct((nt, SSD_INNER), F32)] * 2,
        scratch_shapes=[pltpu.VMEM((128, SSD_INNER), F32)] * 2,
        compiler_params=_cparams(2, 12),
        name="ssd_scan",
    )(xbc_act, dt_raw, xbc_act, dt_raw, dtb, alog, expand(SSD_INNER, SSD_INNER // SSD_HEADS),
      expand(SSD_HEADS * 128, 128))


def _softmax_pv(s_list, v_list, extra_logit=None):
    m = s_list[0].max(axis=-1, keepdims=True)
    for s in s_list[1:]:
        m = jnp.maximum(m, s.max(axis=-1, keepdims=True))
    if extra_logit is not None:
        m = jnp.maximum(m, extra_logit)
    den = None
    o = None
    for s, v in zip(s_list, v_list):
        p = jnp.exp2(s - m)
        ps = p.sum(axis=-1, keepdims=True)
        den = ps if den is None else den + ps
        t = _dot(p.astype(BF16), v)
        o = t if o is None else o + t
    if extra_logit is not None:
        den = den + jnp.exp2(extra_logit - m)
    return o / den


A_QB = 2
A_KB = A_QB + 2


def _wattn_kernel(sink_ref, q_ref, k_ref, v_ref, kc_ref, vc_ref, o_ref):
    n = pl.program_id(1) * A_QB
    nb = SEQ // A_BLOCK
    nq = A_QB * A_BLOCK
    nk = A_KB * A_BLOCK
    start = pl.multiple_of(jnp.clip(n - 1, 0, nb - A_KB) * A_BLOCK, A_BLOCK)
    q = q_ref[:, 0:256]
    kw = k_ref[pl.ds(start, nk), 256:384]
    vw = v_ref[pl.ds(start, nk), :]
    kc = kc_ref[:, 256:384]
    vc = vc_ref[...]
    qrow = _iota((2 * nq, nk), 0)
    qpos = n * A_BLOCK + jnp.where(qrow < nq, qrow, qrow - nq)
    kpos = start + _iota((2 * nq, nk), 1)
    valid = jnp.abs(qpos - kpos) <= A_BLOCK
    top = _iota((2 * nq, 1), 0) < nq
    outs = []
    for g in range(A_KV_HEADS):
        qg = jnp.concatenate([q[:, (2 * g) * 64:(2 * g + 1) * 64],
                              q[:, (2 * g + 1) * 64:(2 * g + 2) * 64]], axis=0)
        kg = kw[:, g * 64:(g + 1) * 64]
        vg = vw[:, g * 64:(g + 1) * 64]
        s_loc = jnp.where(valid, _dot_nt(qg, kg), NEG_INF)
        s_ctx = _dot_nt(qg, kc[:, g * 64:(g + 1) * 64])
        sink = jnp.where(top, sink_ref[2 * g], sink_ref[2 * g + 1])
        o = _softmax_pv([s_loc, s_ctx], [vg, vc[:, g * 64:(g + 1) * 64]], sink)
        outs += [o[:nq], o[nq:]]
    o_ref[...] = jnp.concatenate(outs, axis=1).astype(BF16)


def _nattn_kernel(q_ref, kv_ref, c_ref, bias_ref, o_ref):
    i = pl.program_id(1)
    r0 = i * NA_Q_ROWS
    srow = jnp.clip(r0 - NA_WIN_ROWS // 2, 0, GRID_ROWS - NA_K_ROWS)
    start = pl.multiple_of(srow * GRID_W, GRID_W)
    nq = NA_Q_ROWS * GRID_W
    nk = NA_K_ROWS * GRID_W
    q = q_ref[:, 0:256]
    kw = kv_ref[pl.ds(start, nk), 256:512]
    vw = kv_ref[pl.ds(start, nk), 512:768]
    kc = c_ref[:, 256:512]
    vc = c_ref[:, 512:768]
    qrow = r0 + (_iota((nq, nk), 0) >> 6)
    krow = srow + (_iota((nq, nk), 1) >> 6)
    rs = jnp.clip(qrow - NA_WIN_ROWS // 2, 0, GRID_ROWS - NA_WIN_ROWS)
    valid = jnp.logical_and(krow >= rs, krow < rs + NA_WIN_ROWS)
    outs = []
    for h in range(NA_HEADS):
        sl = slice(h * 64, (h + 1) * 64)
        rows = []
        for qi in range(NA_Q_ROWS):
            blocks = []
            for p in range(NA_K_ROWS // 2):
                idx = srow + 2 * p - (r0 + qi) + (NA_WIN_ROWS - 1) + NA_BIAS_OFF
                blocks.append(bias_ref[0, h, idx])
            rows.append(jnp.concatenate(blocks, axis=1))
        bias = jnp.concatenate(rows, axis=0)
        s_loc = jnp.where(valid, _dot_nt(q[:, sl], kw[:, sl]) + bias, NEG_INF)
        s_ctx = _dot_nt(q[:, sl], kc[:, sl])
        outs.append(_softmax_pv([s_loc, s_ctx], [vw[:, sl], vc[:, sl]]))
    o_ref[...] = jnp.concatenate(outs, axis=1).astype(BF16)


def _na_bias_table(rpb):
    cq = np.arange(GRID_W)
    kcol = np.arange(GRID_W)
    cs = np.clip(cq - NA_WIN_COLS // 2, 0, GRID_W - NA_WIN_COLS)
    col_valid = (kcol[None, :] >= cs[:, None]) & (kcol[None, :] < cs[:, None] + NA_WIN_COLS)
    coff = np.clip(kcol[None, :] - cq[:, None], -(NA_WIN_COLS - 1), NA_WIN_COLS - 1) + (NA_WIN_COLS - 1)
    n_a = 2 * NA_WIN_ROWS - 1
    n_c = 2 * NA_WIN_COLS - 1
    pick = (np.arange(n_c)[:, None] == coff.reshape(1, -1)).astype(np.float32)
    tm = jnp.einsum("lhak,kn->lhan", rpb.astype(F32), jnp.asarray(pick), precision=lax.Precision.HIGHEST)
    tm = jnp.where(col_valid, tm.reshape(rpb.shape[:3] + (GRID_W, GRID_W)) * math.log2(math.e), NEG_INF)
    neg = jnp.full(rpb.shape[:2] + (1, GRID_W, GRID_W), NEG_INF, F32)
    pad_lo = NA_BIAS_OFF
    pad_hi = NA_BIAS_N + 1 - pad_lo - n_a
    ext = jnp.concatenate([neg] * pad_lo + [tm] + [neg] * pad_hi, axis=2)
    return jnp.concatenate([ext[:, :, :NA_BIAS_N], ext[:, :, 1:NA_BIAS_N + 1]], axis=-1)


def _latent_attn_kernel(sink_ref, aq_ref, ak_ref, av_ref, akc_ref, avc_ref, nq_ref, nkv_ref, nc_ref, bias_ref,
                        oa_ref, on_ref):
    _wattn_kernel(sink_ref, aq_ref, ak_ref, av_ref, akc_ref, avc_ref, oa_ref)
    _nattn_kernel(nq_ref, nkv_ref, nc_ref, bias_ref, on_ref)


def _latent_attn(sink, qk, v, na, bias_t, n_batch, layer):
    nq = A_QB * A_BLOCK
    assert nq == NA_Q_ROWS * GRID_W
    steps = SEQ // nq
    ctx0 = n_batch * SEQ // CTX_LEN
    qrow = lambda b, n: (b * steps + n, 0)
    full = lambda b, n: (b, 0)
    ctx = lambda b, n: (ctx0 + b, 0)
    return pl.pallas_call(
        _latent_attn_kernel,
        grid=(n_batch, steps),
        in_specs=[pl.BlockSpec(memory_space=pltpu.SMEM),
                  pl.BlockSpec((nq, 384), qrow),
                  pl.BlockSpec((SEQ, 384), full),
                  pl.BlockSpec((SEQ, 128), full),
                  pl.BlockSpec((CTX_LEN, 384), ctx),
                  pl.BlockSpec((CTX_LEN, 128), ctx),
                  pl.BlockSpec((nq, 768), qrow),
                  pl.BlockSpec((SEQ, 768), full),
                  pl.BlockSpec((CTX_LEN, 768), ctx),
                  pl.BlockSpec((1, NA_HEADS, NA_BIAS_N, GRID_W, 128), lambda b, n: (layer, 0, 0, 0, 0))],
        out_specs=[pl.BlockSpec((nq, 256), qrow)] * 2,
        out_shape=[jax.ShapeDtypeStruct((n_batch * SEQ, 256), BF16)] * 2,
        compiler_params=_cparams(2, 28),
        name="latent_attn",
    )(sink, qk, qk, v, qk, v, na, na, na, bias_t)


def _ctx_attn_kernel(sink_ref, qk_ref, v_ref, na_ref, oa_ref, on_ref):
    qk = qk_ref[...]
    v = v_ref[...]
    na = na_ref[...]
    outs = []
    for h in range(A_HEADS):
        g = h // (A_HEADS // A_KV_HEADS)
        s = _dot_nt(qk[:, h * 64:(h + 1) * 64], qk[:, 256 + g * 64:256 + (g + 1) * 64])
        sink = jnp.zeros((CTX_LEN, 1), F32) + sink_ref[h]
        outs.append(_softmax_pv([s], [v[:, g * 64:(g + 1) * 64]], sink))
    oa_ref[...] = jnp.concatenate(outs, axis=1).astype(BF16)
    outs = []
    for h in range(NA_HEADS):
        sl = slice(h * 64, (h + 1) * 64)
        s = _dot_nt(na[:, 0:256][:, sl], na[:, 256:512][:, sl])
        outs.append(_softmax_pv([s], [na[:, 512:768][:, sl]]))
    on_ref[...] = jnp.concatenate(outs, axis=1).astype(BF16)


def _ctx_attn(sink, qk, v, na, n_batch):
    ctx0 = n_batch * SEQ // CTX_LEN
    row = lambda b: (ctx0 + b, 0)
    return pl.pallas_call(
        _ctx_attn_kernel,
        grid=(n_batch,),
        in_specs=[pl.BlockSpec(memory_space=pltpu.SMEM),
                  pl.BlockSpec((CTX_LEN, 384), row),
                  pl.BlockSpec((CTX_LEN, 128), row),
                  pl.BlockSpec((CTX_LEN, 768), row)],
        out_specs=[pl.BlockSpec((CTX_LEN, 256), lambda b: (b, 0))] * 2,
        out_shape=[jax.ShapeDtypeStruct((n_batch * CTX_LEN, 256), BF16)] * 2,
        compiler_params=_cparams(1, 8),
        name="ctx_attn",
    )(sink, qk, v, na)


def _outproj_kernel(x_ref, oal_ref, oac_ref, yf_ref, yb_ref, xbc_ref, z_ref, onl_ref, onc_ref, mod_ref,
                    dskip_ref, ng_ref, w_ref, gf_ref, wr1_ref, wr2_ref, br_ref, xo_ref, tok_ref, rt_ref,
                    cnt_ref, *, n_lat_tiles):
    is_lat = pl.program_id(0) < n_lat_tiles
    m = mod_ref[0, 0]
    xs = xbc_ref[...]
    y = yf_ref[...] + yb_ref[...] + dskip_ref[...] * xs
    y = y * _silu(z_ref[...])
    ob = y * lax.rsqrt(jnp.mean(y * y, axis=-1, keepdims=True) + RMS_EPS) * ng_ref[...]
    oa = jnp.where(is_lat, oal_ref[...], oac_ref[...])
    on = jnp.where(is_lat, onl_ref[...], onc_ref[...])
    proj = (_dot(oa, w_ref[0, 0:256, :]) + _dot(ob.astype(BF16), w_ref[0, 256:768, :])
            + _dot(on, w_ref[0, 768:1024, :]))
    x = x_ref[...] + m[2:3] * proj
    xo_ref[...] = x
    t = x * lax.rsqrt(jnp.mean(x * x, axis=-1, keepdims=True) + RMS_EPS) * gf_ref[...]
    t = t * (1.0 + m[4:5]) + m[3:4]
    _store_row_tiles(tok_ref, t)
    t1 = t.astype(BF16)
    t2 = (t - t1.astype(F32)).astype(BF16)
    logits = _dot(t1, wr1_ref[...]) + _dot(t1, wr2_ref[...]) + _dot(t2, wr1_ref[...]) + br_ref[...]

    lane = _iota(logits.shape, 1)
    big = jnp.int32(1 << 20)
    is_g = jnp.logical_and(lane >= N_EXPERTS, lane < N_EXPERTS + MOE_GROUPS)
    gl = jnp.where(is_g, logits, NEG_INF)
    gmax = gl.max(axis=-1, keepdims=True)
    g_w = 1.0 / jnp.exp(gl - gmax).sum(axis=-1, keepdims=True)
    g_idx = jnp.where(gl == gmax, lane, big).min(axis=-1, keepdims=True) - N_EXPERTS
    in_grp = jnp.logical_and(lane < N_EXPERTS, (lane >> 3) == g_idx)
    el = jnp.where(in_grp, logits, NEG_INF)
    l1 = el.max(axis=-1, keepdims=True)
    i1 = jnp.where(el == l1, lane, big).min(axis=-1, keepdims=True)
    el2 = jnp.where(lane == i1, NEG_INF, el)
    l2 = el2.max(axis=-1, keepdims=True)
    i2 = jnp.where(el2 == l2, lane, big).min(axis=-1, keepdims=True)
    e2 = jnp.exp(l2 - l1)
    w1 = g_w / (1.0 + e2)
    w2 = w1 * e2
    rt_ref[...] = jnp.where(lane == 0, i1.astype(F32), jnp.where(lane == 1, i2.astype(F32),
                            jnp.where(lane == 2, w1, jnp.where(lane == 3, w2, 0.0))))
    hot = jnp.logical_or(lane == i1, lane == i2)
    cnt_ref[0] = jnp.where(hot, 1.0, 0.0).sum(axis=0, keepdims=True)


def _outproj(xc, oa_l, oa_c, y_f, y_b, xbc_act, z, on_l, on_c, mod, dskip, norm_g, w_out, g_ffn, wr1, wr2, br,
             n_batch, n_tiles, layer):
    n_lat_tiles = n_batch * SEQ // TILE
    tpb = SEQ // TILE
    row = lambda i: (i, 0)
    lat = lambda i: (jnp.minimum(i, n_lat_tiles - 1), 0)
    ctx = lambda i: (jnp.maximum(i - n_lat_tiles, 0), 0)
    modrow = lambda i: (layer, _tile_mod_row(i, n_lat_tiles, tpb, n_batch), 0, 0)
    const = lambda i: (0, 0)
    kern = functools.partial(_outproj_kernel, n_lat_tiles=n_lat_tiles)
    return pl.pallas_call(
        kern,
        grid=(n_tiles,),
        in_specs=[pl.BlockSpec((TILE, D_MODEL), row),
                  pl.BlockSpec((TILE, 256), lat),
                  pl.BlockSpec((TILE, 256), ctx),
                  pl.BlockSpec((TILE, SSD_INNER), row),
                  pl.BlockSpec((TILE, SSD_INNER), row),
                  pl.BlockSpec((TILE, SSD_INNER), row),
                  pl.BlockSpec((TILE, SSD_INNER), row),
                  pl.BlockSpec((TILE, 256), lat),
                  pl.BlockSpec((TILE, 256), ctx),
                  pl.BlockSpec((1, 1, N_MOD, D_MODEL), modrow),
                  pl.BlockSpec((1, SSD_INNER), const),
                  pl.BlockSpec((1, SSD_INNER), const),
                  pl.BlockSpec((1, D_MODEL, D_MODEL), lambda i: (layer, 0, 0)),
                  pl.BlockSpec((1, D_MODEL), const),
                  pl.BlockSpec((D_MODEL, LANES), const),
                  pl.BlockSpec((D_MODEL, LANES), const),
                  pl.BlockSpec((1, LANES), const)],
        out_specs=[pl.BlockSpec((TILE, D_MODEL), row),
                   pl.BlockSpec((TILE * ROW_SLABS, LANES), row),
                   pl.BlockSpec((TILE, LANES), row),
                   pl.BlockSpec((1, 1, LANES), lambda i: (i, 0, 0))],
        out_shape=[jax.ShapeDtypeStruct((n_tiles * TILE, D_MODEL), F32),
                   jax.ShapeDtypeStruct((n_tiles * TILE * ROW_SLABS, LANES), F32),
                   jax.ShapeDtypeStruct((n_tiles * TILE, LANES), F32),
                   jax.ShapeDtypeStruct((n_tiles, 1, LANES), F32)],
        compiler_params=_cparams(1, 34),
        name="outproj",
    )(xc, oa_l, oa_c, y_f, y_b, xbc_act, z, on_l, on_c, mod, dskip, norm_g, w_out, g_ffn, wr1, wr2, br)


MOE_TM = 512
COMBINE_PARTS = 4


def _moe_max_tiles(n_tokens):
    return (2 * n_tokens + N_EXPERTS * (MOE_TM - 1)) // MOE_TM


def _moe_plan(cnt, n_tiles):
    cnt = cnt[:, 0, :N_EXPERTS].astype(jnp.int32)
    tot = cnt.sum(axis=0)
    tiles_e = (tot + MOE_TM - 1) // MOE_TM
    t_end = jnp.cumsum(tiles_e)
    t_start = t_end - tiles_e
    base = (t_start * MOE_TM)[None, :] + jnp.cumsum(cnt, axis=0) - cnt
    n_used = t_end[-1]
    n_max = _moe_max_tiles(n_tiles * TILE)
    te = jnp.sum(jnp.arange(n_max)[:, None] >= t_end[None, :], axis=1)
    te = jnp.minimum(te, jnp.sum((n_used - 1) >= t_end)).astype(jnp.int32)
    tail = jnp.where(tiles_e > 0, t_end - 1, n_max).astype(jnp.int32)
    base_f = jnp.zeros((n_tiles, 1, LANES), F32).at[:, 0, :N_EXPERTS].set(base.astype(F32))
    return base_f, te, n_used.reshape(1).astype(jnp.int32), tail


def _pos_kernel(rt_ref, base_ref, pos_ref):
    rt = rt_ref[...]
    lane = _iota(rt.shape, 1)
    hot1 = lane == rt[:, 0:1].astype(jnp.int32)
    hot2 = lane == rt[:, 1:2].astype(jnp.int32)
    hot = jnp.where(jnp.logical_or(hot1, hot2), 1.0, 0.0).astype(BF16)
    n = rt.shape[0]
    strict = jnp.where(_iota((n, n), 0) > _iota((n, n), 1), 1.0, 0.0).astype(BF16)
    slot = base_ref[0] + _dot(strict, hot)
    p1 = jnp.where(hot1, slot, 0.0).sum(axis=-1, keepdims=True)
    p2 = jnp.where(hot2, slot, 0.0).sum(axis=-1, keepdims=True)
    pos_ref[...] = jnp.where(lane == 0, p1, jnp.where(lane == 1, p2, 0.0)).astype(jnp.int32)


def _positions(rt, base, n_rows, tok):
    return pl.pallas_call(
        _pos_kernel,
        grid=(n_rows // tok,),
        in_specs=[pl.BlockSpec((tok, LANES), lambda i: (i, 0)),
                  pl.BlockSpec((1, 1, LANES), lambda i: (i * (tok // TILE), 0, 0))],
        out_specs=pl.BlockSpec((tok, LANES), lambda i: (i, 0)),
        out_shape=jax.ShapeDtypeStruct((n_rows, LANES), jnp.int32),
        compiler_params=_cparams(1, 8),
        name="moe_positions",
    )(rt, base)


def _dispatch_kernel(tail_ref, nu_ref, pos_ref, tok_ref, xs_ref, zbuf, zsem, usem, sem, *, n_max):
    i = pl.program_id(0)

    def zero_tile(j, s):
        start = pl.multiple_of(j * (MOE_TM * ROW_SLABS), MOE_TM * ROW_SLABS)
        return pltpu.make_async_copy(zbuf, xs_ref.at[pl.ds(start, MOE_TM * ROW_SLABS), :], s)

    def row(ref, r):
        return ref.at[pl.ds(pl.multiple_of(r * ROW_SLABS, ROW_SLABS), ROW_SLABS), :]

    @pl.when(i == 0)
    def _():
        zbuf[...] = jnp.zeros_like(zbuf)
        for e in range(N_EXPERTS):
            @pl.when(tail_ref[e] != n_max)
            def _():
                zero_tile(tail_ref[e], zsem).start()
        lax.fori_loop(nu_ref[0], n_max + 1, lambda j, c: (zero_tile(j, usem).start(), c)[1], 0)
        for e in range(N_EXPERTS):
            @pl.when(tail_ref[e] != n_max)
            def _():
                zero_tile(tail_ref[e], zsem).wait()

    def body(r, carry):
        src = row(tok_ref, r)
        pltpu.make_async_copy(src, row(xs_ref, pos_ref[0, 0, 2 * r]), sem).start(priority=0)
        pltpu.make_async_copy(src, row(xs_ref, pos_ref[0, 0, 2 * r + 1]), sem).start(priority=1)
        return carry

    n_rows = tok_ref.shape[0] // ROW_SLABS
    lax.fori_loop(0, n_rows, body, 0, unroll=8)
    for _ in range(2):
        pltpu.make_async_copy(tok_ref, xs_ref.at[pl.ds(0, n_rows * ROW_SLABS), :], sem).wait()

    @pl.when(i == pl.num_programs(0) - 1)
    def _():
        lax.fori_loop(nu_ref[0], n_max + 1, lambda j, c: (zero_tile(j, usem).wait(), c)[1], 0)


def _dispatch(tail, n_used, pos_s, tok, n_max):
    kern = functools.partial(_dispatch_kernel, n_max=n_max)
    n_steps = pos_s.shape[0]
    step_rows = pos_s.shape[2] // 2
    return pl.pallas_call(
        kern,
        grid_spec=pltpu.PrefetchScalarGridSpec(
            num_scalar_prefetch=2,
            grid=(n_steps,),
            in_specs=[pl.BlockSpec((1, 1, 2 * step_rows), lambda i, tail, nu: (i, 0, 0), memory_space=pltpu.SMEM),
                      pl.BlockSpec((step_rows * ROW_SLABS, LANES), lambda i, tail, nu: (i, 0))],
            out_specs=pl.BlockSpec(memory_space=pl.ANY),
            scratch_shapes=[pltpu.VMEM((MOE_TM * ROW_SLABS, LANES), F32), pltpu.SemaphoreType.DMA(()),
                            pltpu.SemaphoreType.DMA(()), pltpu.SemaphoreType.DMA(())]),
        out_shape=jax.ShapeDtypeStruct(((n_max + 1) * MOE_TM * ROW_SLABS, LANES), F32),
        compiler_params=_cparams(1, 14),
        name="moe_dispatch",
    )(tail, n_used, pos_s, tok)


def _experts_kernel(te_ref, nu_ref, xs_ref, wg_ref, wu_ref, wd_ref, ys_ref):
    used = pl.program_id(0) < nu_ref[0]

    @pl.when(used)
    def _():
        x = _load_row_tiles(xs_ref, MOE_TM).astype(BF16)
        gate = _dot(x, wg_ref[0].astype(BF16))
        up = _dot(x, wu_ref[0].astype(BF16))
        hid = (_silu(gate) * up).astype(BF16)
        _store_row_tiles(ys_ref, _dot(hid, wd_ref[0].astype(BF16)))

    @pl.when(jnp.logical_not(used))
    def _():
        ys_ref[...] = jnp.zeros_like(ys_ref)


def _experts(te, n_used, xs, w_gate, w_up, w_down, n_max, layer):
    rows = lambda j, te, nu: (jnp.minimum(j, nu[0] - 1), 0)
    wsel = lambda j, te, nu: (layer * N_EXPERTS + te[j], 0, 0)
    return pl.pallas_call(
        _experts_kernel,
        grid_spec=pltpu.PrefetchScalarGridSpec(
            num_scalar_prefetch=2,
            grid=(n_max,),
            in_specs=[pl.BlockSpec((MOE_TM * ROW_SLABS, LANES), rows),
                      pl.BlockSpec((1, D_MODEL, D_EXPERT), wsel),
                      pl.BlockSpec((1, D_MODEL, D_EXPERT), wsel),
                      pl.BlockSpec((1, D_EXPERT, D_MODEL), wsel)],
            out_specs=pl.BlockSpec((MOE_TM * ROW_SLABS, LANES), lambda j, te, nu: (j, 0))),
        out_shape=jax.ShapeDtypeStruct((n_max * MOE_TM * ROW_SLABS, LANES), F32),
        compiler_params=_cparams(1, 20),
        name="moe_experts",
    )(te, n_used, xs, w_gate, w_up, w_down)


def _combine_kernel(pos_ref, ys_ref, x_ref, rt_ref, mod_ref, gfin_ref, o_ref, ybuf0, ybuf1, sem, *, final):
    def row(ref, r):
        return ref.at[pl.ds(pl.multiple_of(r * ROW_SLABS, ROW_SLABS), ROW_SLABS), :]

    n = x_ref.shape[0]
    part = n // COMBINE_PARTS
    for k in range(COMBINE_PARTS):
        def body(r, carry, k=k):
            pltpu.make_async_copy(row(ys_ref, pos_ref[0, 0, 2 * r]), row(ybuf0, r), sem.at[k]).start(priority=0)
            pltpu.make_async_copy(row(ys_ref, pos_ref[0, 0, 2 * r + 1]), row(ybuf1, r), sem.at[k]).start(priority=1)
            return carry

        lax.fori_loop(k * part, (k + 1) * part, body, 0, unroll=8)
    gate = mod_ref[0, 0][5:6]
    for k in range(COMBINE_PARTS):
        rows = pl.ds(k * part * ROW_SLABS, part * ROW_SLABS)
        for buf in (ybuf0, ybuf1):
            pltpu.make_async_copy(ys_ref.at[pl.ds(0, part * ROW_SLABS), :], buf.at[rows, :], sem.at[k]).wait()
        rt = rt_ref[k * part:(k + 1) * part, :]
        f = (rt[:, 2:3] * _load_row_tiles(ybuf0.at[rows, :], part)
             + rt[:, 3:4] * _load_row_tiles(ybuf1.at[rows, :], part))
        x = x_ref[k * part:(k + 1) * part, :] + gate * f
        if final:
            x = x * lax.rsqrt(jnp.mean(x * x, axis=-1, keepdims=True) + RMS_EPS) * gfin_ref[...]
        o_ref[k * part:(k + 1) * part, :] = x


def _combine(pos_s, ys, xmid, rt, mod, g_final, n_batch, layer):
    n_steps = pos_s.shape[0]
    tok = pos_s.shape[2] // 2
    n_lat_steps = n_batch * SEQ // tok
    spb = SEQ // tok
    row = lambda i: (i, 0)
    modrow = lambda i: (layer, _tile_mod_row(i, n_lat_steps, spb, n_batch), 0, 0)
    return pl.pallas_call(
        functools.partial(_combine_kernel, final=(layer == DEPTH - 1)),
        grid=(n_steps,),
        in_specs=[pl.BlockSpec((1, 1, 2 * tok), lambda i: (i, 0, 0), memory_space=pltpu.SMEM),
                  pl.BlockSpec(memory_space=pl.ANY),
                  pl.BlockSpec((tok, D_MODEL), row),
                  pl.BlockSpec((tok, LANES), row),
                  pl.BlockSpec((1, 1, N_MOD, D_MODEL), modrow),
                  pl.BlockSpec((1, D_MODEL), lambda i: (0, 0))],
        out_specs=pl.BlockSpec((tok, D_MODEL), row),
        out_shape=jax.ShapeDtypeStruct((n_steps * tok, D_MODEL), F32),
        scratch_shapes=[pltpu.VMEM((tok * ROW_SLABS, LANES), F32), pltpu.VMEM((tok * ROW_SLABS, LANES), F32),
                        pltpu.SemaphoreType.DMA((COMBINE_PARTS,))],
        compiler_params=_cparams(1, 30),
        name="moe_combine",
    )(pos_s, ys, xmid, rt, mod, g_final)


MOE_TOK = 1024


def _moe(xmid, tok, rt, cnt, mod, g_final, w_gate, w_up, w_down, n_batch, n_tiles, layer):
    n_rows = n_tiles * TILE
    step = MOE_TOK if (n_rows % MOE_TOK == 0 and (n_batch * SEQ) % MOE_TOK == 0 and SEQ % MOE_TOK == 0) else TILE
    base, te, n_used, tail = _moe_plan(cnt, n_tiles)
    n_max = _moe_max_tiles(n_rows)
    pos = _positions(rt, base, n_rows, step)
    pos_s = pos[:, :2].reshape(n_rows // step, 1, 2 * step)
    xs = _dispatch(tail, n_used, pos_s, tok, n_max)
    ys = _experts(te, n_used, xs, w_gate, w_up, w_down, n_max, layer)
    return _combine(pos_s, ys, xmid, rt, mod, g_final, n_batch, layer)


def _rope_tables():
    t = jnp.arange(SEQ)
    rows_pos = (t // GRID_W).astype(F32)
    cols_pos = (t % GRID_W).astype(F32)
    half = HEAD_DIM // 2
    inv = 1.0 / (ROPE_BASE ** (jnp.arange(0, half, 2, dtype=F32) / half))
    ang_r = rows_pos[:, None] * inv[None, :]
    ang_c = cols_pos[:, None] * inv[None, :]
    ang = jnp.concatenate([ang_r, ang_r, ang_c, ang_c], axis=1)
    cos_h, sin_h = jnp.cos(ang), jnp.sin(ang)
    scale = jnp.concatenate([jnp.full((256,), Q_SCALE, F32), jnp.ones((128,), F32)])
    cos_t = jnp.tile(cos_h, (1, 6)) * scale
    sin_t = jnp.tile(sin_h, (1, 6)) * scale
    cos_t = jnp.concatenate([cos_t, jnp.broadcast_to(scale, (TILE, 384))], axis=0)
    sin_t = jnp.concatenate([sin_t, jnp.zeros((TILE, 384), F32)], axis=0)
    return cos_t, sin_t


def _in_weights(w_in):
    wqk = w_in[..., W_QK[0]:W_QK[1]]
    w4 = wqk.reshape(wqk.shape[:-1] + (wqk.shape[-1] // 32, 2, 16))
    w_rot = jnp.concatenate([-w4[..., 1:2, :], w4[..., 0:1, :]], axis=-2).reshape(wqk.shape)
    na_scale = jnp.concatenate([jnp.full((256,), Q_SCALE, F32), jnp.ones((512,), F32)])
    w_na = w_in[..., W_NA[0]:W_NA[1]] * na_scale
    return w_in.astype(BF16), w_rot.astype(BF16), w_na.astype(BF16)


def _router_weight(w_rg, b_rg, w_re, b_re):
    w = jnp.concatenate([w_re, w_rg, jnp.zeros((D_MODEL, LANES - N_EXPERTS - MOE_GROUPS), F32)], axis=1)
    b = jnp.concatenate([b_re, b_rg, jnp.zeros((LANES - N_EXPERTS - MOE_GROUPS,), F32)]).reshape(1, LANES)
    w1 = w.astype(BF16)
    w2 = (w - w1.astype(F32)).astype(BF16)
    return w1, w2, b


def kernel(x, c, ctx, c_ctx, w_mod, b_mod, g_mix, w_in, attn_sink, ssd_conv_w, ssd_conv_b, ssd_dt_bias, ssd_a_log, ssd_d, ssd_norm_g, na_rpb, w_out, g_ffn, w_router_group, b_router_group, w_router_expert, b_router_expert, w_exp_gate, w_exp_up, w_exp_down, g_final):
    n_batch, s, d = x.shape
    assert (s, d) == (SEQ, D_MODEL) and ctx.shape[1:] == (CTX_LEN, D_MODEL) and n_batch < 16
    n_lat = n_batch * SEQ
    n_ctx = n_batch * CTX_LEN
    assert n_ctx % TILE == 0
    n_lat_tiles = n_lat // TILE
    n_all_tiles = (n_lat + n_ctx) // TILE

    xc = jnp.concatenate([x.reshape(n_lat, d), ctx.reshape(n_ctx, d)], axis=0)
    cin = jnp.zeros((16, d), F32).at[:n_batch].set(c).at[n_batch].set(c_ctx)
    mod = _modulation(cin, w_mod, b_mod).reshape(DEPTH, 16, N_MOD, d)
    cos_t, sin_t = _rope_tables()
    w_in_b, w_rot, w_na = _in_weights(w_in)
    w_out_b = w_out.astype(BF16)
    bias_t = _na_bias_table(na_rpb)
    g_mix3 = g_mix.reshape(DEPTH, 1, d)
    w_gate = w_exp_gate.reshape(DEPTH * N_EXPERTS, D_MODEL, D_EXPERT)
    w_up = w_exp_up.reshape(DEPTH * N_EXPERTS, D_MODEL, D_EXPERT)
    w_down = w_exp_down.reshape(DEPTH * N_EXPERTS, D_EXPERT, D_MODEL)

    for layer in range(DEPTH):
        need_ctx = layer < DEPTH - 1
        conv_w = jnp.zeros((8, SSD_CONV_DIM), F32).at[:SSD_CONV].set(ssd_conv_w[layer])
        qk, v, z, xbc_act, na, dt_raw = _inproj(xc, mod, g_mix3, w_in_b, w_rot, w_na, cos_t, sin_t, conv_w,
                                                ssd_conv_b[layer].reshape(1, SSD_CONV_DIM), n_batch, layer)
        sink = attn_sink[layer].astype(F32) * math.log2(math.e)
        oa, on = _latent_attn(sink, qk, v, na, bias_t, n_batch, layer)
        y_f, y_b = _ssd_scan(xbc_act, dt_raw, ssd_dt_bias[layer], ssd_a_log[layer], n_batch)
        oa_c, on_c = _ctx_attn(sink, qk, v, na, n_batch) if need_ctx else (oa, on)
        n_tiles = n_all_tiles if need_ctx else n_lat_tiles
        wr1, wr2, br = _router_weight(w_router_group[layer], b_router_group[layer],
                                      w_router_expert[layer], b_router_expert[layer])
        dskip = jnp.repeat(ssd_d[layer].astype(F32), SSD_INNER // SSD_HEADS).reshape(1, SSD_INNER)
        xmid, tok, rt, cnt = _outproj(xc, oa, oa_c, y_f, y_b, xbc_act, z, on, on_c, mod, dskip,
                                      ssd_norm_g[layer].reshape(1, SSD_INNER), w_out_b,
                                      g_ffn[layer].reshape(1, d), wr1, wr2, br, n_batch, n_tiles, layer)
        xc = _moe(xmid, tok, rt, cnt, mod, g_final.reshape(1, d), w_gate, w_up, w_down, n_batch, n_tiles, layer)

    return xc.reshape(n_batch, SEQ, d)
```
